```python
import math
import jax, jax.numpy as jnp
from jax import lax
import numpy as np

D_MODEL = 2048
BATCH = 4
SEQ = 2048
DEPTH = 2

CHUNK = 64
Q_BLOCK = 128
HEAD_DIM = 128
EPS = 1e-6
NEG_INF = -1e30

A_HEADS = 4
IDX_HEADS = 16
IDX_DIM = 64
TOPK_MAX = 256
NUM_BUCKETS = 32
MAX_DISTANCE = 128
B_HEADS = 8
C_HEADS = 4
Q_LORA = 448
KV_LORA = 128
NOPE_DIM = 128
ROPE_DIM = 64
V_DIM = 128
ROPE_THETA = 10000.0
N_BRANCH = 3
D_FF = 5632
CONV_WIDTH = 3

IN_SIZES = (
    A_HEADS * HEAD_DIM, HEAD_DIM, HEAD_DIM,
    IDX_HEADS * IDX_DIM, IDX_DIM, IDX_HEADS,
    B_HEADS * HEAD_DIM, B_HEADS * HEAD_DIM, B_HEADS * HEAD_DIM, B_HEADS,
    Q_LORA, KV_LORA, ROPE_DIM,
    N_BRANCH * D_MODEL,
)
N_IN = sum(IN_SIZES)

kernel_name = "hybrid_dsa_fox_mla_convffn"


def rmsnorm(x, g):
    xf = x.astype(jnp.float32)
    y = xf * lax.rsqrt(jnp.mean(xf * xf, axis=-1, keepdims=True) + EPS)
    return (y * g.astype(jnp.float32)).astype(x.dtype)


def to_blocks(a):
    b, s = a.shape[:2]
    return jnp.moveaxis(a.reshape(b, s // Q_BLOCK, Q_BLOCK, *a.shape[2:]), 1, 0)


def from_blocks(a):
    nb, b, qb = a.shape[:3]
    return jnp.moveaxis(a, 0, 1).reshape(b, nb * qb, *a.shape[3:])


def t5_bucket(rel):
    nb = NUM_BUCKETS // 2
    max_exact = nb // 2
    base = jnp.where(rel > 0, nb, 0)
    n = jnp.abs(rel)
    nf = jnp.maximum(n, 1).astype(jnp.float32)
    large = max_exact + (jnp.log(nf / max_exact) / math.log(MAX_DISTANCE / max_exact)
                         * (nb - max_exact)).astype(jnp.int32)
    large = jnp.minimum(large, nb - 1)
    return base + jnp.where(n < max_exact, n, large)


def rope(x, pos):
    half = x.shape[-1] // 2
    inv = ROPE_THETA ** (-jnp.arange(half, dtype=jnp.float32) / half)
    ang = pos.astype(jnp.float32)[:, None] * inv[None, :]
    cos = jnp.cos(ang)[None, :, None, :]
    sin = jnp.sin(ang)[None, :, None, :]
    xf = x.astype(jnp.float32)
    x1, x2 = xf[..., :half], xf[..., half:]
    return jnp.concatenate([x1 * cos - x2 * sin, x1 * sin + x2 * cos], axis=-1).astype(x.dtype)


def dsa_attention(q, k, v, qi, ki, wi, t5_bias, pos):
    s_len = q.shape[1]
    topk = min(TOPK_MAX, s_len // 4)
    chunk_k = pos // CHUNK
    scale = HEAD_DIM ** -0.5
    gather = jax.vmap(lambda a, i: a[i])

    def block(args):
        qb, qib, wib, tb = args
        tc = tb // CHUNK
        s = jnp.einsum('bthd,bsd->bths', qib, ki).astype(jnp.float32) * (IDX_DIM ** -0.5)
        score = jnp.einsum('bth,bths->bts', wib.astype(jnp.float32), jax.nn.relu(s))
        adm = chunk_k[None, :] <= tc[:, None]
        score = jnp.where(adm[None], score, NEG_INF)
        _, idx = lax.top_k(score, topk)
        ks = gather(k, idx)
        vs = gather(v, idx)
        logits = jnp.einsum('bthd,btkd->bhtk', qb, ks).astype(jnp.float32) * scale
        bias = t5_bias[t5_bucket(idx - tb[None, :, None])]
        logits = logits + jnp.moveaxis(bias.astype(jnp.float32), -1, 1)
        valid = (idx // CHUNK) <= tc[None, :, None]
        logits = jnp.where(valid[:, None], logits, NEG_INF)
        p = jax.nn.softmax(logits, axis=-1).astype(vs.dtype)
        return jnp.einsum('bhtk,btkd->bthd', p, vs)

    out = lax.map(block, (to_blocks(q), to_blocks(qi), to_blocks(wi), pos.reshape(-1, Q_BLOCK)))
    return from_blocks(out)


def fox_attention(q, k, v, log_f, pos):
    scale = HEAD_DIM ** -0.5
    c = jnp.cumsum(log_f, axis=1)
    ck = jnp.moveaxis(c, 1, 2)

    def block(args):
        qb, cqb, tb = args
        logits = jnp.einsum('bthd,bshd->bhts', qb, k).astype(jnp.float32) * scale
        logits = logits + jnp.moveaxis(cqb, 1, 2)[..., None] - ck[:, :, None, :]
        mask = pos[None, :] <= tb[:, None]
        logits = jnp.where(mask[None, None], logits, NEG_INF)
        p = jax.nn.softmax(logits, axis=-1).astype(v.dtype)
        return jnp.einsum('bhts,bshd->bthd', p, v)

    out = lax.map(block, (to_blocks(q), to_blocks(c), pos.reshape(-1, Q_BLOCK)))
    return from_blocks(out)


def chunk_causal_attention(q, k, v, pos, scale):
    chunk_k = pos // CHUNK

    def block(args):
        qb, tb = args
        logits = jnp.einsum('bthd,bshd->bhts', qb, k).astype(jnp.float32) * scale
        mask = chunk_k[None, :] <= (tb // CHUNK)[:, None]
        logits = jnp.where(mask[None, None], logits, NEG_INF)
        p = jax.nn.softmax(logits, axis=-1).astype(v.dtype)
        return jnp.einsum('bhts,bshd->bthd', p, v)

    out = lax.map(block, (to_blocks(q), pos.reshape(-1, Q_BLOCK)))
    return from_blocks(out)


def hybrid_mixer(h, w_in, b_forget, g_cq, g_ckv, w_uq, w_ukv,
                 w_branch_a, w_branch_b, w_branch_c, w_o, t5_bias, pos):
    b, s, _ = h.shape
    offsets = np.cumsum(IN_SIZES)[:-1].tolist()
    (qa, ka, va, qi, ki, wi, qb, kb, vb, fl, cq, ckv, kr, gl) = jnp.split(h @ w_in, offsets, axis=-1)

    ya = dsa_attention(qa.reshape(b, s, A_HEADS, HEAD_DIM), ka, va,
                       qi.reshape(b, s, IDX_HEADS, IDX_DIM), ki,
                       wi * (IDX_HEADS ** -0.5), t5_bias, pos)
    ya = ya.reshape(b, s, A_HEADS * HEAD_DIM) @ w_branch_a

    log_f = jax.nn.log_sigmoid((fl + b_forget).astype(jnp.float32))
    yb = fox_attention(qb.reshape(b, s, B_HEADS, HEAD_DIM), kb.reshape(b, s, B_HEADS, HEAD_DIM),
                       vb.reshape(b, s, B_HEADS, HEAD_DIM), log_f, pos)
    yb = yb.reshape(b, s, B_HEADS * HEAD_DIM) @ w_branch_b

    qc = (rmsnorm(cq, g_cq) @ w_uq).reshape(b, s, C_HEADS, NOPE_DIM + ROPE_DIM)
    qc = jnp.concatenate([qc[..., :NOPE_DIM], rope(qc[..., NOPE_DIM:], pos)], axis=-1)
    kv = (rmsnorm(ckv, g_ckv) @ w_ukv).reshape(b, s, C_HEADS, NOPE_DIM + V_DIM)
    k_rope = rope(kr.reshape(b, s, 1, ROPE_DIM), pos)
    kc = jnp.concatenate([kv[..., :NOPE_DIM],
                          jnp.broadcast_to(k_rope, (b, s, C_HEADS, ROPE_DIM))], axis=-1)
    yc = chunk_causal_attention(qc, kc, kv[..., NOPE_DIM:], pos, (NOPE_DIM + ROPE_DIM) ** -0.5)
    yc = yc.reshape(b, s, C_HEADS * V_DIM) @ w_branch_c

    g = jax.nn.sigmoid(gl.reshape(b, s, N_BRANCH, D_MODEL))
    merged = g[:, :, 0] * ya + g[:, :, 1] * yb + g[:, :, 2] * yc
    return merged @ w_o


def conv_ffn(h, w_up, conv_w, conv_b, w_down):
    u, v = jnp.split(h @ w_up, 2, axis=-1)
    u = lax.conv_general_dilated(u, conv_w[:, None, :], window_strides=(1,),
                                 padding=[(CONV_WIDTH - 1, 0)],
                                 dimension_numbers=('NWC', 'WIO', 'NWC'),
                                 feature_group_count=D_FF) + conv_b
    return (jax.nn.gelu(u) * v) @ w_down


def setup_inputs(seed: int = 0) -> dict:
    key = jax.random.key(seed)
    ks = jax.random.split(key, 20)
    f32 = jnp.float32
    L = DEPTH

    def nrm(k, shape, fan_in):
        return jax.random.normal(k, shape, f32) * (fan_in ** -0.5)

    def gain(k, shape):
        return 1.0 + 0.1 * jax.random.normal(k, shape, f32)

    return {
        "x": jax.random.normal(ks[0], (BATCH, SEQ, D_MODEL), f32),
        "norm_mix_g": gain(ks[1], (L, D_MODEL)),
        "w_in": nrm(ks[2], (L, D_MODEL, N_IN), D_MODEL),
        "b_forget": 0.1 * jax.random.normal(ks[3], (L, B_HEADS), f32),
        "g_cq": gain(ks[4], (L, Q_LORA)),
        "g_ckv": gain(ks[5], (L, KV_LORA)),
        "w_uq": nrm(ks[6], (L, Q_LORA, C_HEADS * (NOPE_DIM + ROPE_DIM)), Q_LORA),
        "w_ukv": nrm(ks[7], (L, KV_LORA, C_HEADS * (NOPE_DIM + V_DIM)), KV_LORA),
        "w_branch_a": nrm(ks[8], (L, A_HEADS * HEAD_DIM, D_MODEL), A_HEADS * HEAD_DIM),
        "w_branch_b": nrm(ks[9], (L, B_HEADS * HEAD_DIM, D_MODEL), B_HEADS * HEAD_DIM),
        "w_branch_c": nrm(ks[10], (L, C_HEADS * V_DIM, D_MODEL), C_HEADS * V_DIM),
        "w_o": nrm(ks[11], (L, D_MODEL, D_MODEL), D_MODEL),
        "norm_ffn_g": gain(ks[12], (L, D_MODEL)),
        "w_up": nrm(ks[13], (L, D_MODEL, 2 * D_FF), D_MODEL),
        "conv_w": nrm(ks[14], (L, CONV_WIDTH, D_FF), CONV_WIDTH),
        "conv_b": 0.01 * jax.random.normal(ks[15], (L, D_FF), f32),
        "w_down": nrm(ks[16], (L, D_FF, D_MODEL), D_FF),
        "t5_bias": 0.5 * jax.random.normal(ks[17], (NUM_BUCKETS, A_HEADS), f32),
        "final_g": gain(ks[18], (D_MODEL,)),
    }


def reference(x, norm_mix_g, w_in, b_forget, g_cq, g_ckv, w_uq, w_ukv, w_branch_a, w_branch_b,
              w_branch_c, w_o, norm_ffn_g, w_up, conv_w, conv_b, w_down, t5_bias, final_g):
    pos = jnp.arange(x.shape[1], dtype=jnp.int32)
    for l in range(DEPTH):
        h = rmsnorm(x, norm_mix_g[l])
        x = x + hybrid_mixer(h, w_in[l], b_forget[l], g_cq[l], g_ckv[l], w_uq[l], w_ukv[l],
                             w_branch_a[l], w_branch_b[l], w_branch_c[l], w_o[l], t5_bias, pos)
        h = rmsnorm(x, norm_ffn_g[l])
        x = x + conv_ffn(h, w_up[l], conv_w[l], conv_b[l], w_down[l])
    return rmsnorm(x, final_g)
```

```python
import functools
import math

import jax
import jax.numpy as jnp
import numpy as np
from jax import lax
from jax.experimental import pallas as pl
from jax.experimental.pallas import tpu as pltpu

F32 = jnp.float32
BF16 = jnp.bfloat16

D_MODEL = 2048
SEQ = 2048
DEPTH = 2
CHUNK = 64
HEAD_DIM = 128
EPS = 1e-6
NEG_INF = -1e30

A_HEADS = 4
IDX_HEADS = 16
IDX_DIM = 64
TOPK_MAX = 256
NUM_BUCKETS = 32
MAX_DISTANCE = 128
B_HEADS = 8
C_HEADS = 4
Q_LORA = 448
KV_LORA = 128
NOPE_DIM = 128
ROPE_DIM = 64
V_DIM = 128
ROPE_THETA = 10000.0
D_FF = 5632

IN_SIZES = (
    A_HEADS * HEAD_DIM, HEAD_DIM, HEAD_DIM,
    IDX_HEADS * IDX_DIM, IDX_DIM, IDX_HEADS,
    B_HEADS * HEAD_DIM, B_HEADS * HEAD_DIM, B_HEADS * HEAD_DIM, B_HEADS,
    Q_LORA, KV_LORA, ROPE_DIM,
    3 * D_MODEL,
)

LANE = 128
V7X_VMEM_BYTES = 64 * 1024 * 1024

OFF_QI = 0
OFF_QB = 1024
OFF_KB = 2048
OFF_VB = 3072
OFF_GL = 4096
OFF_QA = 10240
OFF_CQ = 10752
OFF_KA = 11264
OFF_VA = 11392
OFF_KI_LO = 11520
OFF_CKV = 11648
OFF_KR = 11776
OFF_KRS = 11904
OFF_MISC = 12032
OFF_KI_HI = 12160
N_PACK = 12288
CQ_PAD = 512
MISC_WI = 0
MISC_FL = IDX_HEADS

ATT_T = 256
SELECT_MIN = -1e29
BISECT_MAX_ITERS = 512


def _vmem_limit(estimate_bytes):
    return int(min(max(estimate_bytes * 5 // 4 + (4 << 20), 32 << 20), V7X_VMEM_BYTES - (6 << 20)))


def _params(semantics, vmem_estimate):
    return pltpu.CompilerParams(dimension_semantics=semantics,
                                vmem_limit_bytes=_vmem_limit(vmem_estimate))


def _rmsnorm_rows(x_ref, g_ref, out_ref, rows):
    def body(c, carry):
        r = pl.multiple_of(c * 128, 128)
        x = x_ref[pl.ds(r, 128), :]
        ms = jnp.mean(x * x, axis=-1, keepdims=True)
        out_ref[pl.ds(r, 128), :] = (x * lax.rsqrt(ms + EPS) * g_ref[...]).astype(out_ref.dtype)
        return carry
    lax.fori_loop(0, rows // 128, body, 0)


def _inproj_kernel(x_ref, g_ref, w_ref, o_ref, misc_ref, h_ref, *, tm, tn):
    j = pl.program_id(1)

    @pl.when(j == 0)
    def _():
        _rmsnorm_rows(x_ref, g_ref, h_ref, tm)

    acc = jnp.dot(h_ref[...], w_ref[...], preferred_element_type=F32)
    o_ref[...] = acc.astype(o_ref.dtype)

    @pl.when(j == OFF_MISC // tn)
    def _():
        lo = OFF_MISC % tn
        misc_ref[...] = acc[:, lo:lo + LANE]


def _inproj(x2, g, w_pack, *, tm=512, tn=1024):
    m = x2.shape[0]
    est = 2 * tm * D_MODEL * 4 + tm * D_MODEL * 2 + 2 * D_MODEL * tn * 2 + 2 * tm * tn * 2 + tm * tn * 4
    return pl.pallas_call(
        functools.partial(_inproj_kernel, tm=tm, tn=tn),
        grid=(m // tm, N_PACK // tn),
        in_specs=[
            pl.BlockSpec((tm, D_MODEL), lambda i, j: (i, 0)),
            pl.BlockSpec((1, D_MODEL), lambda i, j: (0, 0)),
            pl.BlockSpec((D_MODEL, tn), lambda i, j: (0, j)),
        ],
        out_specs=[
            pl.BlockSpec((tm, tn), lambda i, j: (i, j)),
            pl.BlockSpec((tm, LANE), lambda i, j: (i, 0)),
        ],
        out_shape=[
            jax.ShapeDtypeStruct((m, N_PACK), BF16),
            jax.ShapeDtypeStruct((m, LANE), F32),
        ],
        scratch_shapes=[pltpu.VMEM((tm, D_MODEL), BF16)],
        compiler_params=_params(("arbitrary", "arbitrary"), est),
        name="inproj",
    )(x2, g, w_pack)


def _forget_cumsum_kernel(misc_ref, bias_ref, ccol_ref, crow_ref, *, blk):
    rows = lax.broadcasted_iota(jnp.int32, (blk, blk), 0)
    cols = lax.broadcasted_iota(jnp.int32, (blk, blk), 1)
    tri = jnp.where(rows >= cols, 1.0, 0.0).astype(BF16)
    carry = jnp.zeros((1, LANE), F32)
    for c in range(SEQ // blk):
        z = misc_ref[c * blk:(c + 1) * blk, :] + bias_ref[...]
        lf = jnp.minimum(z, 0.0) - jnp.log1p(jnp.exp(-jnp.abs(z)))
        p0 = lf.astype(BF16)
        r1 = lf - p0.astype(F32)
        p1 = r1.astype(BF16)
        p2 = (r1 - p1.astype(F32)).astype(BF16)
        cs = (jnp.dot(tri, p0, preferred_element_type=F32)
              + jnp.dot(tri, p1, preferred_element_type=F32)
              + jnp.dot(tri, p2, preferred_element_type=F32)) + carry
        ccol_ref[c * blk:(c + 1) * blk, :] = cs
        crow_ref[:, c * blk:(c + 1) * blk] = cs.T[MISC_FL:MISC_FL + B_HEADS, :]
        carry = cs[blk - 1:blk, :]


def _forget_cumsum(misc, bias_row, batch, *, blk=256):
    est = 4 * SEQ * LANE * 4 + 2 * 8 * SEQ * 4
    return pl.pallas_call(
        functools.partial(_forget_cumsum_kernel, blk=blk),
        grid=(batch,),
        in_specs=[
            pl.BlockSpec((SEQ, LANE), lambda b: (b, 0)),
            pl.BlockSpec((1, LANE), lambda b: (0, 0)),
        ],
        out_specs=[
            pl.BlockSpec((SEQ, LANE), lambda b: (b, 0)),
            pl.BlockSpec((None, B_HEADS, SEQ), lambda b: (b, 0, 0)),
        ],
        out_shape=[
            jax.ShapeDtypeStruct((batch * SEQ, LANE), F32),
            jax.ShapeDtypeStruct((batch, B_HEADS, SEQ), F32),
        ],
        compiler_params=_params(("arbitrary",), est),
        name="forget_cumsum",
    )(misc, bias_row)


def _mla_prep_kernel(cq_ref, ckv_ref, kr_ref, krs_ref, gq_ref, gkv_ref, wq1_ref, wq2_ref, wkv_ref,
                     cos_ref, sin_ref, qc_ref, kc_ref, vc_ref):
    cq = cq_ref[...].astype(F32)
    ms = jnp.sum(cq * cq, axis=-1, keepdims=True) * (1.0 / Q_LORA)
    cqn = (cq * lax.rsqrt(ms + EPS) * gq_ref[...]).astype(BF16)
    ckv = ckv_ref[...].astype(F32)
    ms2 = jnp.mean(ckv * ckv, axis=-1, keepdims=True)
    ckvn = (ckv * lax.rsqrt(ms2 + EPS) * gkv_ref[...]).astype(BF16)

    q1 = jnp.dot(cqn, wq1_ref[...], preferred_element_type=F32)
    q2 = jnp.dot(cqn, wq2_ref[...], preferred_element_type=F32)
    kv = jnp.dot(ckvn, wkv_ref[...], preferred_element_type=F32)
    cos = cos_ref[...]
    sin = sin_ref[...]
    k_rope = (kr_ref[...].astype(F32) * cos + krs_ref[...].astype(F32) * sin).astype(BF16)
    for h in range(C_HEADS):
        qw = NOPE_DIM + LANE
        qc_ref[:, h * qw:h * qw + NOPE_DIM] = q1[:, h * qw:h * qw + NOPE_DIM].astype(BF16)
        q_rope = q1[:, h * qw + NOPE_DIM:(h + 1) * qw] * cos + q2[:, h * LANE:(h + 1) * LANE] * sin
        qc_ref[:, h * qw + NOPE_DIM:(h + 1) * qw] = q_rope.astype(BF16)
        kw = NOPE_DIM + V_DIM
        kc_ref[:, h * qw:h * qw + NOPE_DIM] = kv[:, h * kw:h * kw + NOPE_DIM].astype(BF16)
        kc_ref[:, h * qw + NOPE_DIM:(h + 1) * qw] = k_rope
        vc_ref[:, h * V_DIM:(h + 1) * V_DIM] = kv[:, h * kw + NOPE_DIM:(h + 1) * kw].astype(BF16)


def _mla_prep(proj, gq, gkv, wq1, wq2, wkv, cos_t, sin_t, *, tm=512):
    m = proj.shape[0]
    nseq = SEQ // tm
    qw = C_HEADS * (NOPE_DIM + LANE)
    est = 2 * (tm * 1024 * 2) + 2 * (CQ_PAD * qw * 2 + CQ_PAD * 512 * 2 + 128 * 1024 * 2) \
        + 2 * (2 * tm * qw * 2 + tm * 512 * 2) + 3 * tm * qw * 4
    const = lambda i: (0, 0)
    return pl.pallas_call(
        _mla_prep_kernel,
        grid=(m // tm,),
        in_specs=[
            pl.BlockSpec((tm, CQ_PAD), lambda i: (i, OFF_CQ // CQ_PAD)),
            pl.BlockSpec((tm, LANE), lambda i: (i, OFF_CKV // LANE)),
            pl.BlockSpec((tm, LANE), lambda i: (i, OFF_KR // LANE)),
            pl.BlockSpec((tm, LANE), lambda i: (i, OFF_KRS // LANE)),
            pl.BlockSpec((1, CQ_PAD), const),
            pl.BlockSpec((1, KV_LORA), const),
            pl.BlockSpec((CQ_PAD, qw), const),
            pl.BlockSpec((CQ_PAD, C_HEADS * LANE), const),
            pl.BlockSpec((KV_LORA, C_HEADS * (NOPE_DIM + V_DIM)), const),
            pl.BlockSpec((tm, LANE), lambda i: (i % nseq, 0)),
            pl.BlockSpec((tm, LANE), lambda i: (i % nseq, 0)),
        ],
        out_specs=[
            pl.BlockSpec((tm, qw), lambda i: (i, 0)),
            pl.BlockSpec((tm, qw), lambda i: (i, 0)),
            pl.BlockSpec((tm, C_HEADS * V_DIM), lambda i: (i, 0)),
        ],
        out_shape=[
            jax.ShapeDtypeStruct((m, qw), BF16),
            jax.ShapeDtypeStruct((m, qw), BF16),
            jax.ShapeDtypeStruct((m, C_HEADS * V_DIM), BF16),
        ],
        compiler_params=_params(("arbitrary",), est),
        name="mla_prep",
    )(proj, proj, proj, proj, gq, gkv, wq1, wq2, wkv, cos_t, sin_t)


def _softmax_step(carry, s, v, sel=None):
    m, l, acc = carry
    m_new = jnp.maximum(m, jnp.max(s, axis=-1, keepdims=True))
    alpha = jnp.exp(m - m_new)
    p = jnp.exp(s - m_new)
    if sel is not None:
        p = jnp.where(sel, p, 0.0)
    l = alpha * l + jnp.sum(p, axis=-1, keepdims=True)
    acc = alpha * acc + jnp.dot(p.astype(v.dtype), v, preferred_element_type=F32)
    return m_new, l, acc


def _softmax_init(t, dv):
    return (jnp.full((t, 1), NEG_INF, F32), jnp.zeros((t, 1), F32), jnp.zeros((t, dv), F32))


def _nt_dot(a, b):
    return lax.dot_general(a, b, (((1,), (1,)), ((), ())), preferred_element_type=F32)


def _fox_kernel(q_ref, k_ref, v_ref, ccol_ref, crow_ref, o_ref, *, t):
    i = pl.program_id(1)
    scale = HEAD_DIM ** -0.5
    rows = lax.broadcasted_iota(jnp.int32, (t, t), 0)
    cols = lax.broadcasted_iota(jnp.int32, (t, t), 1)
    causal = cols <= rows
    for h in range(B_HEADS):
        hs = slice(h * HEAD_DIM, (h + 1) * HEAD_DIM)
        q = q_ref[:, hs]
        cq = ccol_ref[:, MISC_FL + h:MISC_FL + h + 1]

        def logits(kb, q=q, cq=cq, hs=hs, h=h):
            ks = pl.multiple_of(kb * t, t)
            s = _nt_dot(q, k_ref[pl.ds(ks, t), hs]) * scale
            return s + cq - crow_ref[h:h + 1, pl.ds(ks, t)], v_ref[pl.ds(ks, t), hs]

        def body(kb, carry, logits=logits):
            s, v = logits(kb)
            return _softmax_step(carry, s, v)

        carry = lax.fori_loop(0, i, body, _softmax_init(t, HEAD_DIM))
        s, v = logits(i)
        _, l, acc = _softmax_step(carry, jnp.where(causal, s, NEG_INF), v)
        o_ref[:, hs] = (acc / l).astype(o_ref.dtype)


def _fox_attention(proj, ccol, crow, batch, *, t=ATT_T):
    m = proj.shape[0]
    nq = SEQ // t
    w = B_HEADS * HEAD_DIM
    est = 2 * (t * w * 2) * 2 + 2 * 2 * (SEQ * w * 2) + 2 * t * LANE * 4 + 2 * 8 * SEQ * 4 + 8 * t * t * 4
    return pl.pallas_call(
        functools.partial(_fox_kernel, t=t),
        grid=(batch, nq),
        in_specs=[
            pl.BlockSpec((t, w), lambda b, i: (b * nq + i, OFF_QB // w)),
            pl.BlockSpec((SEQ, w), lambda b, i: (b, OFF_KB // w)),
            pl.BlockSpec((SEQ, w), lambda b, i: (b, OFF_VB // w)),
            pl.BlockSpec((t, LANE), lambda b, i: (b * nq + i, 0)),
            pl.BlockSpec((None, B_HEADS, SEQ), lambda b, i: (b, 0, 0)),
        ],
        out_specs=pl.BlockSpec((t, w), lambda b, i: (b * nq + i, 0)),
        out_shape=jax.ShapeDtypeStruct((m, w), BF16),
        compiler_params=_params(("arbitrary", "arbitrary"), est),
        name="fox_attention",
    )(proj, proj, proj, ccol, crow)


def _mla_kernel(q_ref, k_ref, v_ref, o_ref, *, t):
    i = pl.program_id(1)
    scale = (NOPE_DIM + ROPE_DIM) ** -0.5
    qw = NOPE_DIM + LANE
    rows = lax.broadcasted_iota(jnp.int32, (t, t), 0)
    cols = lax.broadcasted_iota(jnp.int32, (t, t), 1)
    chunk_causal = (cols // CHUNK) <= (rows // CHUNK)
    for h in range(C_HEADS):
        qs = slice(h * qw, (h + 1) * qw)
        vs = slice(h * V_DIM, (h + 1) * V_DIM)
        q = q_ref[:, qs]

        def logits(kb, q=q, qs=qs, vs=vs):
            ks = pl.multiple_of(kb * t, t)
            return _nt_dot(q, k_ref[pl.ds(ks, t), qs]) * scale, v_ref[pl.ds(ks, t), vs]

        def body(kb, carry, logits=logits):
            s, v = logits(kb)
            return _softmax_step(carry, s, v)

        carry = lax.fori_loop(0, i, body, _softmax_init(t, V_DIM))
        s, v = logits(i)
        _, l, acc = _softmax_step(carry, jnp.where(chunk_causal, s, NEG_INF), v)
        o_ref[:, vs] = (acc / l).astype(o_ref.dtype)


def _mla_attention(qc, kc, vc, batch, *, t=ATT_T):
    m = qc.shape[0]
    nq = SEQ // t
    qw = C_HEADS * (NOPE_DIM + LANE)
    vw = C_HEADS * V_DIM
    est = 2 * t * qw * 2 + 2 * SEQ * qw * 2 + 2 * SEQ * vw * 2 + 2 * t * vw * 2 + 8 * t * t * 4
    return pl.pallas_call(
        functools.partial(_mla_kernel, t=t),
        grid=(batch, nq),
        in_specs=[
            pl.BlockSpec((t, qw), lambda b, i: (b * nq + i, 0)),
            pl.BlockSpec((SEQ, qw), lambda b, i: (b, 0)),
            pl.BlockSpec((SEQ, vw), lambda b, i: (b, 0)),
        ],
        out_specs=pl.BlockSpec((t, vw), lambda b, i: (b * nq + i, 0)),
        out_shape=jax.ShapeDtypeStruct((m, vw), BF16),
        compiler_params=_params(("arbitrary", "arbitrary"), est),
        name="mla_attention",
    )(qc, kc, vc)


def _t5_tiles_kernel(bucket_ref, table_ref, o_ref):
    for d in range(3):
        bucket = bucket_ref[d]
        for h in range(A_HEADS):
            acc = jnp.zeros(bucket.shape, F32)
            for nb in range(NUM_BUCKETS):
                acc = jnp.where(bucket == nb, table_ref[nb, h], acc)
            o_ref[d, h] = acc


def _t5_tiles(bucket_tiles, t5_bias, *, t=ATT_T):
    return pl.pallas_call(
        _t5_tiles_kernel,
        in_specs=[
            pl.BlockSpec(memory_space=pltpu.VMEM),
            pl.BlockSpec(memory_space=pltpu.SMEM),
        ],
        out_specs=pl.BlockSpec(memory_space=pltpu.VMEM),
        out_shape=jax.ShapeDtypeStruct((3, A_HEADS, t, t), F32),
        name="t5_tiles",
    )(bucket_tiles, t5_bias)


def _dsa_kernel(qi_ref, klo_ref, khi_ref, misc_ref, qa_ref, ka_ref, va_ref, bias_ref, o_ref,
                sc_ref, wb_ref, mn_ref, mx_ref, *, t):
    i = pl.program_id(1)
    half = t // LANE
    rows = lax.broadcasted_iota(jnp.int32, (t, t), 0)
    cols = lax.broadcasted_iota(jnp.int32, (t, t), 1)
    admissible = (cols // CHUNK) <= (rows // CHUNK)

    w = misc_ref[:, MISC_WI:MISC_WI + IDX_HEADS] * (IDX_HEADS ** -0.5 * IDX_DIM ** -0.5)
    for h in range(IDX_HEADS):
        wb_ref[h] = jnp.broadcast_to(w[:, h:h + 1], (t, LANE))

    def block_scores(kb):
        ks = pl.multiple_of(kb * t, t)
        klo = klo_ref[pl.ds(ks, t), :]
        khi = khi_ref[pl.ds(ks, t), :]
        acc = jnp.zeros((t, t), F32)
        for pair in range(IDX_HEADS // 2):
            qp = qi_ref[:, pair * LANE:(pair + 1) * LANE]
            for sub, kk in enumerate((klo, khi)):
                wb = wb_ref[2 * pair + sub]
                s = jnp.maximum(_nt_dot(qp, kk), 0.0)
                acc = acc + s * jnp.concatenate([wb] * half, axis=1)
        return ks, acc

    def fold_minmax(blk):
        lo, hi = blk[:, :LANE], blk[:, :LANE]
        for c in range(1, half):
            lo = jnp.minimum(lo, blk[:, c * LANE:(c + 1) * LANE])
            hi = jnp.maximum(hi, blk[:, c * LANE:(c + 1) * LANE])
        return lo, hi

    mn_ref[...] = jnp.full((t, LANE), -SELECT_MIN, F32)
    mx_ref[...] = jnp.full((t, LANE), SELECT_MIN, F32)

    def score_body(kb, carry):
        ks, acc = block_scores(kb)
        sc_ref[:, pl.ds(ks, t)] = acc
        lo, hi = fold_minmax(acc)
        mn_ref[...] = jnp.minimum(mn_ref[...], lo)
        mx_ref[...] = jnp.maximum(mx_ref[...], hi)
        return carry

    lax.fori_loop(0, i, score_body, 0)
    ks, acc = block_scores(i)
    sc_ref[:, pl.ds(ks, t)] = jnp.where(admissible, acc, NEG_INF)
    lo, _ = fold_minmax(jnp.where(admissible, acc, -SELECT_MIN))
    _, hi = fold_minmax(jnp.where(admissible, acc, SELECT_MIN))
    row_min = jnp.min(jnp.minimum(mn_ref[...], lo), axis=-1, keepdims=True)
    row_max = jnp.max(jnp.maximum(mx_ref[...], hi), axis=-1, keepdims=True)

    ones = jnp.ones((LANE, LANE), BF16)

    def count_ge(thr):
        def body(kb, cnt):
            ks = pl.multiple_of(kb * t, t)
            for c in range(half):
                blk = sc_ref[:, pl.ds(ks + c * LANE, LANE)]
                cnt = cnt + jnp.where(blk >= thr, 1.0, 0.0)
            return cnt
        cnt = lax.fori_loop(0, i + 1, body, jnp.zeros((t, LANE), F32))
        return jnp.dot(cnt.astype(BF16), ones, preferred_element_type=F32)

    k_sel = float(TOPK_MAX)
    search = i * t >= TOPK_MAX
    lo0 = jnp.where(search, jnp.broadcast_to(row_min, (t, LANE)), SELECT_MIN)
    hi0 = jnp.where(search, jnp.broadcast_to(row_max, (t, LANE)), SELECT_MIN)

    def any_active(lo, hi):
        mid = 0.5 * lo + 0.5 * hi
        active = jnp.where((mid > lo) & (mid < hi), 1.0, 0.0)
        return jnp.max(active) > 0.5

    def bisect_cond(state):
        it, go, _, _ = state
        return jnp.logical_and(it < BISECT_MAX_ITERS, go)

    def bisect_body(state):
        it, _, lo, hi = state
        mid = 0.5 * lo + 0.5 * hi
        cnt = count_ge(mid)
        ge = cnt >= k_sel
        lo = jnp.where(ge, mid, lo)
        hi = jnp.where(ge, jnp.where(cnt == k_sel, mid, hi), mid)
        return it + 1, any_active(lo, hi), lo, hi

    _, _, lo, hi = lax.while_loop(bisect_cond, bisect_body,
                                  (jnp.int32(0), any_active(lo0, hi0), lo0, hi0))
    thr = jnp.where(count_ge(hi) >= k_sel, hi, lo)

    scale = HEAD_DIM ** -0.5
    for h in range(A_HEADS):
        hs = slice(h * HEAD_DIM, (h + 1) * HEAD_DIM)
        q = qa_ref[:, hs]

        def body(kb, carry, q=q, h=h):
            ks = pl.multiple_of(kb * t, t)
            dist = jnp.minimum(i - kb, 2)
            s = _nt_dot(q, ka_ref[pl.ds(ks, t), :]) * scale + bias_ref[dist, h]
            sel = jnp.concatenate(
                [sc_ref[:, pl.ds(ks + c * LANE, LANE)] >= thr for c in range(half)], axis=1)
            s = jnp.where(sel, s, NEG_INF)
            return _softmax_step(carry, s, va_ref[pl.ds(ks, t), :], sel=sel)

        _, l, acc = lax.fori_loop(0, i + 1, body, _softmax_init(t, HEAD_DIM))
        o_ref[:, hs] = (acc / l).astype(o_ref.dtype)


def _dsa_attention(proj, misc, bias_tiles, batch, *, t=ATT_T):
    m = proj.shape[0]
    nq = SEQ // t
    qiw = IDX_HEADS * IDX_DIM
    qaw = A_HEADS * HEAD_DIM
    est = (2 * t * qiw * 2 + 2 * 4 * SEQ * LANE * 2 + 2 * t * LANE * 4 + 2 * t * qaw * 2
           + 2 * 3 * A_HEADS * t * t * 4 + 2 * t * qaw * 2
           + t * SEQ * 4 + IDX_HEADS * t * LANE * 4 + 2 * t * LANE * 4 + 8 * t * t * 4)
    kblock = lambda off: pl.BlockSpec((SEQ, LANE), lambda b, i: (b, off // LANE))
    return pl.pallas_call(
        functools.partial(_dsa_kernel, t=t),
        grid=(batch, nq),
        in_specs=[
            pl.BlockSpec((t, qiw), lambda b, i: (b * nq + i, OFF_QI // qiw)),
            kblock(OFF_KI_LO),
            kblock(OFF_KI_HI),
            pl.BlockSpec((t, LANE), lambda b, i: (b * nq + i, 0)),
            pl.BlockSpec((t, qaw), lambda b, i: (b * nq + i, OFF_QA // qaw)),
            kblock(OFF_KA),
            kblock(OFF_VA),
            pl.BlockSpec((3, A_HEADS, t, t), lambda b, i: (0, 0, 0, 0)),
        ],
        out_specs=pl.BlockSpec((t, qaw), lambda b, i: (b * nq + i, 0)),
        out_shape=jax.ShapeDtypeStruct((m, qaw), BF16),
        scratch_shapes=[
            pltpu.VMEM((t, SEQ), F32),
            pltpu.VMEM((IDX_HEADS, t, LANE), F32),
            pltpu.VMEM((t, LANE), F32),
            pltpu.VMEM((t, LANE), F32),
        ],
        compiler_params=_params(("arbitrary", "arbitrary"), est),
        name="dsa_attention",
    )(proj, proj, proj, misc, proj, proj, proj, bias_tiles)


def _merge_kernel(x_ref, ya_ref, yb_ref, yc_ref, ga_ref, gb_ref, gc_ref,
                  wa_ref, wb_ref, wc_ref, wo_ref, o_ref):
    def branch(y_ref, w_ref, g_ref):
        y = jnp.dot(y_ref[...], w_ref[...], preferred_element_type=F32)
        return jax.nn.sigmoid(g_ref[...].astype(F32)) * y

    merged = branch(ya_ref, wa_ref, ga_ref) + branch(yb_ref, wb_ref, gb_ref) + branch(yc_ref, wc_ref, gc_ref)
    o_ref[...] = x_ref[...] + jnp.dot(merged.astype(BF16), wo_ref[...], preferred_element_type=F32)


def _merge(x2, ya, yb, yc, proj, wa, wb, wc, wo, *, tm=256):
    m = x2.shape[0]
    wbytes = (wa.size + wb.size + wc.size + wo.size) * 2
    est = 2 * wbytes + 4 * tm * D_MODEL * 4 + 2 * tm * 2048 * 2 + 6 * tm * D_MODEL * 2 + 4 * tm * D_MODEL * 4
    const = lambda i: (0, 0)
    gate = lambda k: pl.BlockSpec((tm, D_MODEL), lambda i: (i, OFF_GL // D_MODEL + k))
    return pl.pallas_call(
        _merge_kernel,
        grid=(m // tm,),
        in_specs=[
            pl.BlockSpec((tm, D_MODEL), lambda i: (i, 0)),
            pl.BlockSpec((tm, ya.shape[1]), lambda i: (i, 0)),
            pl.BlockSpec((tm, yb.shape[1]), lambda i: (i, 0)),
            pl.BlockSpec((tm, yc.shape[1]), lambda i: (i, 0)),
            gate(0), gate(1), gate(2),
            pl.BlockSpec(wa.shape, const),
            pl.BlockSpec(wb.shape, const),
            pl.BlockSpec(wc.shape, const),
            pl.BlockSpec(wo.shape, const),
        ],
        out_specs=pl.BlockSpec((tm, D_MODEL), lambda i: (i, 0)),
        out_shape=jax.ShapeDtypeStruct((m, D_MODEL), F32),
        compiler_params=_params(("arbitrary",), est),
        name="merge",
    )(x2, ya, yb, yc, proj, proj, proj, wa, wb, wc, wo)


def _ffn_kernel(x_ref, g_ref, wu_ref, wv_ref, cw_ref, wd_ref, fg_ref, o_ref,
                h_ref, ubuf_ref, halo_ref, *, tm, tf, final_norm):
    i = pl.program_id(0)
    j = pl.program_id(1)
    nj = pl.num_programs(1)

    @pl.when(j == 0)
    def _():
        _rmsnorm_rows(x_ref, g_ref, h_ref, tm)

    h = h_ref[...]
    u = jnp.dot(h, wu_ref[...], preferred_element_type=F32)
    v = jnp.dot(h, wv_ref[...], preferred_element_type=F32)

    seq_start = (i % (SEQ // tm)) == 0

    @pl.when(seq_start)
    def _():
        ubuf_ref[0:8, :] = jnp.zeros((8, tf), F32)

    @pl.when(jnp.logical_not(seq_start))
    def _():
        ubuf_ref[0:8, :] = halo_ref[j]

    ubuf_ref[8:, :] = u
    halo_ref[j] = u[tm - 8:, :]
    conv = (cw_ref[0:1, :] * ubuf_ref[6:6 + tm, :] + cw_ref[1:2, :] * ubuf_ref[7:7 + tm, :]
            + cw_ref[2:3, :] * u + cw_ref[3:4, :])
    act = (jax.nn.gelu(conv) * v).astype(BF16)
    down = jnp.dot(act, wd_ref[...], preferred_element_type=F32)

    @pl.when(j == 0)
    def _():
        o_ref[...] = x_ref[...] + down

    @pl.when(j > 0)
    def _():
        o_ref[...] += down

    if final_norm:
        @pl.when(j == nj - 1)
        def _():
            _rmsnorm_rows(o_ref, fg_ref, o_ref, tm)


def _ffn(x2, g, w_up, cw, w_down, final_g, *, final_norm, tm=512, tf=512):
    m = x2.shape[0]
    nj = D_FF // tf
    est = (2 * tm * D_MODEL * 4 * 2 + tm * D_MODEL * 2 + 2 * 2 * D_MODEL * tf * 2 + 2 * tf * D_MODEL * 2
           + (tm + 8) * tf * 4 + nj * 8 * tf * 4 + 6 * tm * tf * 4 + tm * D_MODEL * 4)
    return pl.pallas_call(
        functools.partial(_ffn_kernel, tm=tm, tf=tf, final_norm=final_norm),
        grid=(m // tm, nj),
        in_specs=[
            pl.BlockSpec((tm, D_MODEL), lambda i, j: (i, 0)),
            pl.BlockSpec((1, D_MODEL), lambda i, j: (0, 0)),
            pl.BlockSpec((D_MODEL, tf), lambda i, j: (0, j)),
            pl.BlockSpec((D_MODEL, tf), lambda i, j: (0, nj + j)),
            pl.BlockSpec((8, tf), lambda i, j: (0, j)),
            pl.BlockSpec((tf, D_MODEL), lambda i, j: (j, 0)),
            pl.BlockSpec((1, D_MODEL), lambda i, j: (0, 0)),
        ],
        out_specs=pl.BlockSpec((tm, D_MODEL), lambda i, j: (i, 0)),
        out_shape=jax.ShapeDtypeStruct((m, D_MODEL), F32),
        scratch_shapes=[
            pltpu.VMEM((tm, D_MODEL), BF16),
            pltpu.VMEM((tm + 8, tf), F32),
            pltpu.VMEM((nj, 8, tf), F32),
        ],
        compiler_params=_params(("arbitrary", "arbitrary"), est),
        name="conv_ffn",
    )(x2, g, w_up, w_up, cw, w_down, final_g)


def _t5_bucket(rel):
    nb = NUM_BUCKETS // 2
    max_exact = nb // 2
    base = jnp.where(rel > 0, nb, 0)
    n = jnp.abs(rel)
    nf = jnp.maximum(n, 1).astype(F32)
    large = max_exact + (jnp.log(nf / max_exact) / math.log(MAX_DISTANCE / max_exact)
                         * (nb - max_exact)).astype(jnp.int32)
    large = jnp.minimum(large, nb - 1)
    return base + jnp.where(n < max_exact, n, large)


def _bucket_tiles(t):
    r = jnp.arange(t, dtype=jnp.int32)[:, None]
    c = jnp.arange(t, dtype=jnp.int32)[None, :]
    return jnp.stack([_t5_bucket(c - r - d * t) for d in range(3)])


def _rope_tables():
    half = ROPE_DIM // 2
    inv = ROPE_THETA ** (-jnp.arange(half, dtype=F32) / half)
    ang = jnp.arange(SEQ, dtype=jnp.int32).astype(F32)[:, None] * inv[None, :]
    cos, sin = jnp.cos(ang), jnp.sin(ang)
    pad = jnp.zeros((SEQ, LANE - ROPE_DIM), F32)
    return jnp.concatenate([cos, cos, pad], axis=1), jnp.concatenate([-sin, sin, pad], axis=1)


def _swap_halves(w):
    half = w.shape[-1] // 2
    return jnp.concatenate([w[..., half:], w[..., :half]], axis=-1)


def _pack_w_in(w):
    off = np.cumsum((0,) + IN_SIZES)
    qa, ka, va, qi, ki, wi, qb, kb, vb, fl, cq, ckv, kr, gl = [
        w[:, off[k]:off[k + 1]] for k in range(len(IN_SIZES))]
    z = lambda n: jnp.zeros((w.shape[0], n), w.dtype)
    cols = [qi, qb, kb, vb, gl, qa, cq, z(CQ_PAD - Q_LORA), ka, va, ki, z(64), ckv,
            kr, z(64), _swap_halves(kr), z(64), wi, fl, z(LANE - IDX_HEADS - B_HEADS), z(64), ki]
    packed = jnp.concatenate(cols, axis=1).astype(BF16)
    assert packed.shape[1] == N_PACK
    return packed


def _pack_w_uq(w):
    w = jnp.pad(w, ((0, CQ_PAD - Q_LORA), (0, 0))).reshape(CQ_PAD, C_HEADS, NOPE_DIM + ROPE_DIM)
    z = jnp.zeros((CQ_PAD, C_HEADS, LANE - ROPE_DIM), w.dtype)
    rope = w[..., NOPE_DIM:]
    wq1 = jnp.concatenate([w, z], axis=-1).reshape(CQ_PAD, -1)
    wq2 = jnp.concatenate([_swap_halves(rope), z], axis=-1).reshape(CQ_PAD, -1)
    return wq1.astype(BF16), wq2.astype(BF16)


def kernel(x, norm_mix_g, w_in, b_forget, g_cq, g_ckv, w_uq, w_ukv, w_branch_a, w_branch_b, w_branch_c,
           w_o, norm_ffn_g, w_up, conv_w, conv_b, w_down, t5_bias, final_g):
    batch, seq, d = x.shape
    assert (seq, d) == (SEQ, D_MODEL)
    x2 = x.reshape(batch * seq, d)

    bias_tiles = _t5_tiles(_bucket_tiles(ATT_T), t5_bias)
    cos_t, sin_t = _rope_tables()
    final_row = final_g.reshape(1, D_MODEL)

    for l in range(DEPTH):
        proj, misc = _inproj(x2, norm_mix_g[l].reshape(1, D_MODEL), _pack_w_in(w_in[l]))

        fbias = jnp.zeros((1, LANE), F32).at[0, MISC_FL:MISC_FL + B_HEADS].set(b_forget[l])
        ccol, crow = _forget_cumsum(misc, fbias, batch)

        wq1, wq2 = _pack_w_uq(w_uq[l])
        gq = jnp.pad(g_cq[l], (0, CQ_PAD - Q_LORA)).reshape(1, CQ_PAD)
        qc, kc, vc = _mla_prep(proj, gq, g_ckv[l].reshape(1, KV_LORA), wq1, wq2,
                               w_ukv[l].astype(BF16), cos_t, sin_t)

        ya = _dsa_attention(proj, misc, bias_tiles, batch)
        yb = _fox_attention(proj, ccol, crow, batch)
        yc = _mla_attention(qc, kc, vc, batch)

        x2 = _merge(x2, ya, yb, yc, proj, w_branch_a[l].astype(BF16), w_branch_b[l].astype(BF16),
                    w_branch_c[l].astype(BF16), w_o[l].astype(BF16))

        cw = jnp.concatenate([conv_w[l], conv_b[l][None, :], jnp.zeros((4, D_FF), F32)], axis=0)
        x2 = _ffn(x2, norm_ffn_g[l].reshape(1, D_MODEL), w_up[l].astype(BF16), cw,
                  w_down[l].astype(BF16), final_row, final_norm=(l == DEPTH - 1))

    return x2.reshape(batch, seq, d)
```

```python
import functools
import math

import jax
import jax.numpy as jnp
import numpy as np
from jax import lax
from jax.experimental import pallas as pl
from jax.experimental.pallas import tpu as pltpu

F32 = jnp.float32
BF16 = jnp.bfloat16

D_MODEL = 2048
SEQ = 2048
DEPTH = 2
CHUNK = 64
HEAD_DIM = 128
EPS = 1e-6
NEG_INF = -1e30

A_HEADS = 4
IDX_HEADS = 16
IDX_DIM = 64
TOPK_MAX = 256
NUM_BUCKETS = 32
MAX_DISTANCE = 128
B_HEADS = 8
C_HEADS = 4
Q_LORA = 448
KV_LORA = 128
NOPE_DIM = 128
ROPE_DIM = 64
V_DIM = 128
ROPE_THETA = 10000.0
D_FF = 5632

IN_SIZES = (
    A_HEADS * HEAD_DIM, HEAD_DIM, HEAD_DIM,
    IDX_HEADS * IDX_DIM, IDX_DIM, IDX_HEADS,
    B_HEADS * HEAD_DIM, B_HEADS * HEAD_DIM, B_HEADS * HEAD_DIM, B_HEADS,
    Q_LORA, KV_LORA, ROPE_DIM,
    3 * D_MODEL,
)

LANE = 128
V7X_VMEM_BYTES = 64 * 1024 * 1024

OFF_QI = 0
OFF_QB = 1024
OFF_KB = 2048
OFF_VB = 3072
OFF_GL = 4096
OFF_QA = 10240
OFF_CQ = 10752
OFF_KA = 11264
OFF_VA = 11392
OFF_KI_LO = 11520
OFF_CKV = 11648
OFF_KR = 11776
OFF_KRS = 11904
OFF_MISC = 12032
OFF_KI_HI = 12160
N_PACK = 12288
CQ_PAD = 512
MISC_WI = 0
MISC_FL = IDX_HEADS

ATT_T = 256
SELECT_MIN = -1e29
BISECT_MAX_ITERS = 512


def _vmem_limit(estimate_bytes):
    return int(min(max(estimate_bytes * 5 // 4 + (4 << 20), 32 << 20), V7X_VMEM_BYTES - (6 << 20)))


def _params(semantics, vmem_estimate):
    return pltpu.CompilerParams(dimension_semantics=semantics,
                                vmem_limit_bytes=_vmem_limit(vmem_estimate))


def _rmsnorm_rows(x_ref, g_ref, out_ref, rows):
    def body(c, carry):
        r = pl.multiple_of(c * 128, 128)
        x = x_ref[pl.ds(r, 128), :]
        ms = jnp.mean(x * x, axis=-1, keepdims=True)
        out_ref[pl.ds(r, 128), :] = (x * lax.rsqrt(ms + EPS) * g_ref[...]).astype(out_ref.dtype)
        return carry
    lax.fori_loop(0, rows // 128, body, 0)


def _inproj_kernel(x_ref, g_ref, w_ref, o_ref, misc_ref, h_ref, *, tm, tn):
    j = pl.program_id(1)

    @pl.when(j == 0)
    def _():
        _rmsnorm_rows(x_ref, g_ref, h_ref, tm)

    acc = jnp.dot(h_ref[...], w_ref[...], preferred_element_type=F32)
    o_ref[...] = acc.astype(o_ref.dtype)

    @pl.when(j == OFF_MISC // tn)
    def _():
        lo = OFF_MISC % tn
        misc_ref[...] = acc[:, lo:lo + LANE]


def _inproj(x2, g, w_pack, *, tm=512, tn=1024):
    m = x2.shape[0]
    est = 2 * tm * D_MODEL * 4 + tm * D_MODEL * 2 + 2 * D_MODEL * tn * 2 + 2 * tm * tn * 2 + tm * tn * 4
    return pl.pallas_call(
        functools.partial(_inproj_kernel, tm=tm, tn=tn),
        grid=(m // tm, N_PACK // tn),
        in_specs=[
            pl.BlockSpec((tm, D_MODEL), lambda i, j: (i, 0)),
            pl.BlockSpec((1, D_MODEL), lambda i, j: (0, 0)),
            pl.BlockSpec((D_MODEL, tn), lambda i, j: (0, j)),
        ],
        out_specs=[
            pl.BlockSpec((tm, tn), lambda i, j: (i, j)),
            pl.BlockSpec((tm, LANE), lambda i, j: (i, 0)),
        ],
        out_shape=[
            jax.ShapeDtypeStruct((m, N_PACK), BF16),
            jax.ShapeDtypeStruct((m, LANE), F32),
        ],
        scratch_shapes=[pltpu.VMEM((tm, D_MODEL), BF16)],
        compiler_params=_params(("arbitrary", "arbitrary"), est),
        name="inproj",
    )(x2, g, w_pack)


def _forget_cumsum_kernel(misc_ref, bias_ref, ccol_ref, crow_ref, *, blk):
    rows = lax.broadcasted_iota(jnp.int32, (blk, blk), 0)
    cols = lax.broadcasted_iota(jnp.int32, (blk, blk), 1)
    tri = jnp.where(rows >= cols, 1.0, 0.0).astype(BF16)
    carry = jnp.zeros((1, LANE), F32)
    for c in range(SEQ // blk):
        z = misc_ref[c * blk:(c + 1) * blk, :] + bias_ref[...]
        lf = jnp.minimum(z, 0.0) - jnp.log1p(jnp.exp(-jnp.abs(z)))
        p0 = lf.astype(BF16)
        r1 = lf - p0.astype(F32)
        p1 = r1.astype(BF16)
        p2 = (r1 - p1.astype(F32)).astype(BF16)
        cs = (jnp.dot(tri, p0, preferred_element_type=F32)
              + jnp.dot(tri, p1, preferred_element_type=F32)
              + jnp.dot(tri, p2, preferred_element_type=F32)) + carry
        ccol_ref[c * blk:(c + 1) * blk, :] = cs
        crow_ref[:, c * blk:(c + 1) * blk] = cs.T[MISC_FL:MISC_FL + B_HEADS, :]
        carry = cs[blk - 1:blk, :]


def _forget_cumsum(misc, bias_row, batch, *, blk=256):
    est = 4 * SEQ * LANE * 4 + 2 * 8 * SEQ * 4
    return pl.pallas_call(
        functools.partial(_forget_cumsum_kernel, blk=blk),
        grid=(batch,),
        in_specs=[
            pl.BlockSpec((SEQ, LANE), lambda b: (b, 0)),
            pl.BlockSpec((1, LANE), lambda b: (0, 0)),
        ],
        out_specs=[
            pl.BlockSpec((SEQ, LANE), lambda b: (b, 0)),
            pl.BlockSpec((None, B_HEADS, SEQ), lambda b: (b, 0, 0)),
        ],
        out_shape=[
            jax.ShapeDtypeStruct((batch * SEQ, LANE), F32),
            jax.ShapeDtypeStruct((batch, B_HEADS, SEQ), F32),
        ],
        compiler_params=_params(("arbitrary",), est),
        name="forget_cumsum",
    )(misc, bias_row)


def _mla_prep_kernel(cq_ref, ckv_ref, kr_ref, krs_ref, gq_ref, gkv_ref, wq1_ref, wq2_ref, wkv_ref,
                     cos_ref, sin_ref, qc_ref, kc_ref, vc_ref):
    cq = cq_ref[...].astype(F32)
    ms = jnp.sum(cq * cq, axis=-1, keepdims=True) * (1.0 / Q_LORA)
    cqn = (cq * lax.rsqrt(ms + EPS) * gq_ref[...]).astype(BF16)
    ckv = ckv_ref[...].astype(F32)
    ms2 = jnp.mean(ckv * ckv, axis=-1, keepdims=True)
    ckvn = (ckv * lax.rsqrt(ms2 + EPS) * gkv_ref[...]).astype(BF16)

    q1 = jnp.dot(cqn, wq1_ref[...], preferred_element_type=F32)
    q2 = jnp.dot(cqn, wq2_ref[...], preferred_element_type=F32)
    kv = jnp.dot(ckvn, wkv_ref[...], preferred_element_type=F32)
    cos = cos_ref[...]
    sin = sin_ref[...]
    k_rope = (kr_ref[...].astype(F32) * cos + krs_ref[...].astype(F32) * sin).astype(BF16)
    for h in range(C_HEADS):
        qw = NOPE_DIM + LANE
        qc_ref[:, h * qw:h * qw + NOPE_DIM] = q1[:, h * qw:h * qw + NOPE_DIM].astype(BF16)
        q_rope = q1[:, h * qw + NOPE_DIM:(h + 1) * qw] * cos + q2[:, h * LANE:(h + 1) * LANE] * sin
        qc_ref[:, h * qw + NOPE_DIM:(h + 1) * qw] = q_rope.astype(BF16)
        kw = NOPE_DIM + V_DIM
        kc_ref[:, h * qw:h * qw + NOPE_DIM] = kv[:, h * kw:h * kw + NOPE_DIM].astype(BF16)
        kc_ref[:, h * qw + NOPE_DIM:(h + 1) * qw] = k_rope
        vc_ref[:, h * V_DIM:(h + 1) * V_DIM] = kv[:, h * kw + NOPE_DIM:(h + 1) * kw].astype(BF16)


def _mla_prep(proj, gq, gkv, wq1, wq2, wkv, cos_t, sin_t, *, tm=512):
    m = proj.shape[0]
    nseq = SEQ // tm
    qw = C_HEADS * (NOPE_DIM + LANE)
    est = 2 * (tm * 1024 * 2) + 2 * (CQ_PAD * qw * 2 + CQ_PAD * 512 * 2 + 128 * 1024 * 2) \
        + 2 * (2 * tm * qw * 2 + tm * 512 * 2) + 3 * tm * qw * 4
    const = lambda i: (0, 0)
    return pl.pallas_call(
        _mla_prep_kernel,
        grid=(m // tm,),
        in_specs=[
            pl.BlockSpec((tm, CQ_PAD), lambda i: (i, OFF_CQ // CQ_PAD)),
            pl.BlockSpec((tm, LANE), lambda i: (i, OFF_CKV // LANE)),
            pl.BlockSpec((tm, LANE), lambda i: (i, OFF_KR // LANE)),
            pl.BlockSpec((tm, LANE), lambda i: (i, OFF_KRS // LANE)),
            pl.BlockSpec((1, CQ_PAD), const),
            pl.BlockSpec((1, KV_LORA), const),
            pl.BlockSpec((CQ_PAD, qw), const),
            pl.BlockSpec((CQ_PAD, C_HEADS * LANE), const),
            pl.BlockSpec((KV_LORA, C_HEADS * (NOPE_DIM + V_DIM)), const),
            pl.BlockSpec((tm, LANE), lambda i: (i % nseq, 0)),
            pl.BlockSpec((tm, LANE), lambda i: (i % nseq, 0)),
        ],
        out_specs=[
            pl.BlockSpec((tm, qw), lambda i: (i, 0)),
            pl.BlockSpec((tm, qw), lambda i: (i, 0)),
            pl.BlockSpec((tm, C_HEADS * V_DIM), lambda i: (i, 0)),
        ],
        out_shape=[
            jax.ShapeDtypeStruct((m, qw), BF16),
            jax.ShapeDtypeStruct((m, qw), BF16),
            jax.ShapeDtypeStruct((m, C_HEADS * V_DIM), BF16),
        ],
        compiler_params=_params(("arbitrary",), est),
        name="mla_prep",
    )(proj, proj, proj, proj, gq, gkv, wq1, wq2, wkv, cos_t, sin_t)


def _softmax_step(carry, s, v, sel=None):
    m, l, acc = carry
    m_new = jnp.maximum(m, jnp.max(s, axis=-1, keepdims=True))
    alpha = jnp.exp(m - m_new)
    p = jnp.exp(s - m_new)
    if sel is not None:
        p = jnp.where(sel, p, 0.0)
    l = alpha * l + jnp.sum(p, axis=-1, keepdims=True)
    acc = alpha * acc + jnp.dot(p.astype(v.dtype), v, preferred_element_type=F32)
    return m_new, l, acc


def _softmax_init(t, dv):
    return (jnp.full((t, 1), NEG_INF, F32), jnp.zeros((t, 1), F32), jnp.zeros((t, dv), F32))


def _nt_dot(a, b):
    return lax.dot_general(a, b, (((1,), (1,)), ((), ())), preferred_element_type=F32)


def _two_pass_attention(i, *, n_heads, tq, tk, logits_fn, mask_fn, v_fn, store_fn,
                        s_ref, mx_ref, ls_ref, acc_ref):
    nc = tk // LANE
    n_full = (i * tq) // tk
    nkb = ((i + 1) * tq + tk - 1) // tk

    mx_ref[...] = jnp.full(mx_ref.shape, NEG_INF, F32)

    def pass1(kb, masked):
        ks = pl.multiple_of(kb * tk, tk)
        mask = mask_fn(ks) if masked else None
        for h in range(n_heads):
            s = logits_fn(h, ks)
            if masked:
                s = jnp.where(mask, s, NEG_INF)
            s_ref[h, :, pl.ds(ks, tk)] = s
            mx = mx_ref[h]
            for c in range(nc):
                mx = jnp.maximum(mx, s[:, c * LANE:(c + 1) * LANE])
            mx_ref[h] = mx

    def pass1_full(kb, carry):
        pass1(kb, False)
        return carry

    def pass1_masked(kb, carry):
        pass1(kb, True)
        return carry

    lax.fori_loop(0, n_full, pass1_full, 0)
    lax.fori_loop(n_full, nkb, pass1_masked, 0)

    for h in range(n_heads):
        mx_ref[h] = jnp.broadcast_to(jnp.max(mx_ref[h], axis=-1, keepdims=True), (tq, LANE))
    ls_ref[...] = jnp.zeros(ls_ref.shape, F32)
    acc_ref[...] = jnp.zeros(acc_ref.shape, F32)

    def pass2(kb, carry):
        ks = pl.multiple_of(kb * tk, tk)
        for h in range(n_heads):
            m = mx_ref[h]
            ls = ls_ref[h]
            ps = []
            for c in range(nc):
                p = jnp.exp(s_ref[h, :, pl.ds(ks + c * LANE, LANE)] - m)
                ls = ls + p
                ps.append(p.astype(BF16))
            ls_ref[h] = ls
            acc_ref[h] += jnp.dot(jnp.concatenate(ps, axis=1), v_fn(h, ks), preferred_element_type=F32)
        return carry

    lax.fori_loop(0, nkb, pass2, 0)

    for h in range(n_heads):
        l = jnp.sum(ls_ref[h], axis=-1, keepdims=True)
        store_fn(h, acc_ref[h] / l)


def _tile_iotas(tq, tk):
    return (lax.broadcasted_iota(jnp.int32, (tq, tk), 0), lax.broadcasted_iota(jnp.int32, (tq, tk), 1))


def _fox_kernel(q_ref, k_ref, v_ref, ccol_ref, crow_ref, o_ref,
                s_ref, mx_ref, ls_ref, acc_ref, cq_ref, *, tq, tk):
    i = pl.program_id(1)
    scale = HEAD_DIM ** -0.5
    nc = tk // LANE
    head = lambda h: slice(h * HEAD_DIM, (h + 1) * HEAD_DIM)
    for h in range(B_HEADS):
        cq_ref[h] = jnp.broadcast_to(ccol_ref[:, MISC_FL + h:MISC_FL + h + 1], (tq, LANE))

    def logits_fn(h, ks):
        s = _nt_dot(q_ref[:, head(h)], k_ref[pl.ds(ks, tk), head(h)]) * scale
        return s + jnp.concatenate([cq_ref[h]] * nc, axis=1) - crow_ref[h:h + 1, pl.ds(ks, tk)]

    def mask_fn(ks):
        rows, cols = _tile_iotas(tq, tk)
        return cols + (ks - i * tq) <= rows

    def store_fn(h, y):
        o_ref[:, head(h)] = y.astype(o_ref.dtype)

    _two_pass_attention(i, n_heads=B_HEADS, tq=tq, tk=tk, logits_fn=logits_fn, mask_fn=mask_fn,
                        v_fn=lambda h, ks: v_ref[pl.ds(ks, tk), head(h)], store_fn=store_fn,
                        s_ref=s_ref, mx_ref=mx_ref, ls_ref=ls_ref, acc_ref=acc_ref)


def _fox_attention(proj, ccol, crow, batch, *, tq=256, tk=512):
    m = proj.shape[0]
    nq = SEQ // tq
    w = B_HEADS * HEAD_DIM
    state = B_HEADS * tq * LANE * 4
    est = (4 * tq * w * 2 + 4 * SEQ * w * 2 + 2 * tq * LANE * 4 + 2 * 8 * SEQ * 4
           + B_HEADS * tq * SEQ * 4 + 4 * state + 6 * tq * tk * 4)
    return pl.pallas_call(
        functools.partial(_fox_kernel, tq=tq, tk=tk),
        grid=(batch, nq),
        in_specs=[
            pl.BlockSpec((tq, w), lambda b, i: (b * nq + i, OFF_QB // w)),
            pl.BlockSpec((SEQ, w), lambda b, i: (b, OFF_KB // w)),
            pl.BlockSpec((SEQ, w), lambda b, i: (b, OFF_VB // w)),
            pl.BlockSpec((tq, LANE), lambda b, i: (b * nq + i, 0)),
            pl.BlockSpec((None, B_HEADS, SEQ), lambda b, i: (b, 0, 0)),
        ],
        out_specs=pl.BlockSpec((tq, w), lambda b, i: (b * nq + i, 0)),
        out_shape=jax.ShapeDtypeStruct((m, w), BF16),
        scratch_shapes=[
            pltpu.VMEM((B_HEADS, tq, SEQ), F32),
            pltpu.VMEM((B_HEADS, tq, LANE), F32),
            pltpu.VMEM((B_HEADS, tq, LANE), F32),
            pltpu.VMEM((B_HEADS, tq, HEAD_DIM), F32),
            pltpu.VMEM((B_HEADS, tq, LANE), F32),
        ],
        compiler_params=_params(("arbitrary", "arbitrary"), est),
        name="fox_attention",
    )(proj, proj, proj, ccol, crow)


def _mla_kernel(q_ref, k_ref, v_ref, o_ref, s_ref, mx_ref, ls_ref, acc_ref, *, tq, tk):
    i = pl.program_id(1)
    scale = (NOPE_DIM + ROPE_DIM) ** -0.5
    qw = NOPE_DIM + LANE
    qhead = lambda h: slice(h * qw, (h + 1) * qw)
    vhead = lambda h: slice(h * V_DIM, (h + 1) * V_DIM)

    def logits_fn(h, ks):
        return _nt_dot(q_ref[:, qhead(h)], k_ref[pl.ds(ks, tk), qhead(h)]) * scale

    def mask_fn(ks):
        rows, cols = _tile_iotas(tq, tk)
        return (cols + ks) // CHUNK <= (rows + i * tq) // CHUNK

    def store_fn(h, y):
        o_ref[:, vhead(h)] = y.astype(o_ref.dtype)

    _two_pass_attention(i, n_heads=C_HEADS, tq=tq, tk=tk, logits_fn=logits_fn, mask_fn=mask_fn,
                        v_fn=lambda h, ks: v_ref[pl.ds(ks, tk), vhead(h)], store_fn=store_fn,
                        s_ref=s_ref, mx_ref=mx_ref, ls_ref=ls_ref, acc_ref=acc_ref)


def _mla_attention(qc, kc, vc, batch, *, tq=256, tk=512):
    m = qc.shape[0]
    nq = SEQ // tq
    qw = C_HEADS * (NOPE_DIM + LANE)
    vw = C_HEADS * V_DIM
    state = C_HEADS * tq * LANE * 4
    est = (2 * tq * qw * 2 + 2 * SEQ * qw * 2 + 2 * SEQ * vw * 2 + 2 * tq * vw * 2
           + C_HEADS * tq * SEQ * 4 + 3 * state + 6 * tq * tk * 4)
    return pl.pallas_call(
        functools.partial(_mla_kernel, tq=tq, tk=tk),
        grid=(batch, nq),
        in_specs=[
            pl.BlockSpec((tq, qw), lambda b, i: (b * nq + i, 0)),
            pl.BlockSpec((SEQ, qw), lambda b, i: (b, 0)),
            pl.BlockSpec((SEQ, vw), lambda b, i: (b, 0)),
        ],
        out_specs=pl.BlockSpec((tq, vw), lambda b, i: (b * nq + i, 0)),
        out_shape=jax.ShapeDtypeStruct((m, vw), BF16),
        scratch_shapes=[
            pltpu.VMEM((C_HEADS, tq, SEQ), F32),
            pltpu.VMEM((C_HEADS, tq, LANE), F32),
            pltpu.VMEM((C_HEADS, tq, LANE), F32),
            pltpu.VMEM((C_HEADS, tq, V_DIM), F32),
        ],
        compiler_params=_params(("arbitrary", "arbitrary"), est),
        name="mla_attention",
    )(qc, kc, vc)


def _t5_tiles_kernel(bucket_ref, table_ref, o_ref):
    for d in range(3):
        bucket = bucket_ref[d]
        for h in range(A_HEADS):
            acc = jnp.zeros(bucket.shape, F32)
            for nb in range(NUM_BUCKETS):
                acc = jnp.where(bucket == nb, table_ref[nb, h], acc)
            o_ref[d, h] = acc


def _t5_tiles(bucket_tiles, t5_bias, *, t=ATT_T):
    return pl.pallas_call(
        _t5_tiles_kernel,
        in_specs=[
            pl.BlockSpec(memory_space=pltpu.VMEM),
            pl.BlockSpec(memory_space=pltpu.SMEM),
        ],
        out_specs=pl.BlockSpec(memory_space=pltpu.VMEM),
        out_shape=jax.ShapeDtypeStruct((3, A_HEADS, t, t), F32),
        name="t5_tiles",
    )(bucket_tiles, t5_bias)


def _dsa_kernel(qi_ref, klo_ref, khi_ref, misc_ref, qa_ref, ka_ref, va_ref, bias_ref, o_ref,
                sc_ref, wb_ref, mn_ref, mx_ref, *, t):
    i = pl.program_id(1)
    half = t // LANE
    rows = lax.broadcasted_iota(jnp.int32, (t, t), 0)
    cols = lax.broadcasted_iota(jnp.int32, (t, t), 1)
    admissible = (cols // CHUNK) <= (rows // CHUNK)

    w = misc_ref[:, MISC_WI:MISC_WI + IDX_HEADS] * (IDX_HEADS ** -0.5 * IDX_DIM ** -0.5)
    for h in range(IDX_HEADS):
        wb_ref[h] = jnp.broadcast_to(w[:, h:h + 1], (t, LANE))

    def block_scores(kb):
        ks = pl.multiple_of(kb * t, t)
        klo = klo_ref[pl.ds(ks, t), :]
        khi = khi_ref[pl.ds(ks, t), :]
        acc = jnp.zeros((t, t), F32)
        for pair in range(IDX_HEADS // 2):
            qp = qi_ref[:, pair * LANE:(pair + 1) * LANE]
            for sub, kk in enumerate((klo, khi)):
                wb = wb_ref[2 * pair + sub]
                s = jnp.maximum(_nt_dot(qp, kk), 0.0)
                acc = acc + s * jnp.concatenate([wb] * half, axis=1)
        return ks, acc

    def fold_minmax(blk):
        lo, hi = blk[:, :LANE], blk[:, :LANE]
        for c in range(1, half):
            lo = jnp.minimum(lo, blk[:, c * LANE:(c + 1) * LANE])
            hi = jnp.maximum(hi, blk[:, c * LANE:(c + 1) * LANE])
        return lo, hi

    mn_ref[...] = jnp.full((t, LANE), -SELECT_MIN, F32)
    mx_ref[...] = jnp.full((t, LANE), SELECT_MIN, F32)

    def score_body(kb, carry):
        ks, acc = block_scores(kb)
        sc_ref[:, pl.ds(ks, t)] = acc
        lo, hi = fold_minmax(acc)
        mn_ref[...] = jnp.minimum(mn_ref[...], lo)
        mx_ref[...] = jnp.maximum(mx_ref[...], hi)
        return carry

    lax.fori_loop(0, i, score_body, 0)
    ks, acc = block_scores(i)
    sc_ref[:, pl.ds(ks, t)] = jnp.where(admissible, acc, NEG_INF)
    lo, _ = fold_minmax(jnp.where(admissible, acc, -SELECT_MIN))
    _, hi = fold_minmax(jnp.where(admissible, acc, SELECT_MIN))
    row_min = jnp.min(jnp.minimum(mn_ref[...], lo), axis=-1, keepdims=True)
    row_max = jnp.max(jnp.maximum(mx_ref[...], hi), axis=-1, keepdims=True)

    ones = jnp.ones((LANE, LANE), BF16)

    def count_ge(thr):
        def body(kb, cnt):
            ks = pl.multiple_of(kb * t, t)
            for c in range(half):
                blk = sc_ref[:, pl.ds(ks + c * LANE, LANE)]
                cnt = cnt + jnp.where(blk >= thr, 1.0, 0.0)
            return cnt
        cnt = lax.fori_loop(0, i + 1, body, jnp.zeros((t, LANE), F32))
        return jnp.dot(cnt.astype(BF16), ones, preferred_element_type=F32)

    k_sel = float(TOPK_MAX)
    search = i * t >= TOPK_MAX
    lo0 = jnp.where(search, jnp.broadcast_to(row_min, (t, LANE)), SELECT_MIN)
    hi0 = jnp.where(search, jnp.broadcast_to(row_max, (t, LANE)), SELECT_MIN)

    def any_active(lo, hi):
        mid = 0.5 * lo + 0.5 * hi
        active = jnp.where((mid > lo) & (mid < hi), 1.0, 0.0)
        return jnp.max(active) > 0.5

    def bisect_cond(state):
        it, go, _, _ = state
        return jnp.logical_and(it < BISECT_MAX_ITERS, go)

    def bisect_body(state):
        it, _, lo, hi = state
        mid = 0.5 * lo + 0.5 * hi
        cnt = count_ge(mid)
        ge = cnt >= k_sel
        lo = jnp.where(ge, mid, lo)
        hi = jnp.where(ge, jnp.where(cnt == k_sel, mid, hi), mid)
        return it + 1, any_active(lo, hi), lo, hi

    _, _, lo, hi = lax.while_loop(bisect_cond, bisect_body,
                                  (jnp.int32(0), any_active(lo0, hi0), lo0, hi0))
    thr = jnp.where(count_ge(hi) >= k_sel, hi, lo)

    scale = HEAD_DIM ** -0.5
    for h in range(A_HEADS):
        hs = slice(h * HEAD_DIM, (h + 1) * HEAD_DIM)
        q = qa_ref[:, hs]

        def body(kb, carry, q=q, h=h):
            ks = pl.multiple_of(kb * t, t)
            dist = jnp.minimum(i - kb, 2)
            s = _nt_dot(q, ka_ref[pl.ds(ks, t), :]) * scale + bias_ref[dist, h]
            sel = jnp.concatenate(
                [sc_ref[:, pl.ds(ks + c * LANE, LANE)] >= thr for c in range(half)], axis=1)
            s = jnp.where(sel, s, NEG_INF)
            return _softmax_step(carry, s, va_ref[pl.ds(ks, t), :], sel=sel)

        _, l, acc = lax.fori_loop(0, i + 1, body, _softmax_init(t, HEAD_DIM))
        o_ref[:, hs] = (acc / l).astype(o_ref.dtype)


def _dsa_attention(proj, misc, bias_tiles, batch, *, t=ATT_T):
    m = proj.shape[0]
    nq = SEQ // t
    qiw = IDX_HEADS * IDX_DIM
    qaw = A_HEADS * HEAD_DIM
    est = (2 * t * qiw * 2 + 2 * 4 * SEQ * LANE * 2 + 2 * t * LANE * 4 + 2 * t * qaw * 2
           + 2 * 3 * A_HEADS * t * t * 4 + 2 * t * qaw * 2
           + t * SEQ * 4 + IDX_HEADS * t * LANE * 4 + 2 * t * LANE * 4 + 8 * t * t * 4)
    kblock = lambda off: pl.BlockSpec((SEQ, LANE), lambda b, i: (b, off // LANE))
    return pl.pallas_call(
        functools.partial(_dsa_kernel, t=t),
        grid=(batch, nq),
        in_specs=[
            pl.BlockSpec((t, qiw), lambda b, i: (b * nq + i, OFF_QI // qiw)),
            kblock(OFF_KI_LO),
            kblock(OFF_KI_HI),
            pl.BlockSpec((t, LANE), lambda b, i: (b * nq + i, 0)),
            pl.BlockSpec((t, qaw), lambda b, i: (b * nq + i, OFF_QA // qaw)),
            kblock(OFF_KA),
            kblock(OFF_VA),
            pl.BlockSpec((3, A_HEADS, t, t), lambda b, i: (0, 0, 0, 0)),
        ],
        out_specs=pl.BlockSpec((t, qaw), lambda b, i: (b * nq + i, 0)),
        out_shape=jax.ShapeDtypeStruct((m, qaw), BF16),
        scratch_shapes=[
            pltpu.VMEM((t, SEQ), F32),
            pltpu.VMEM((IDX_HEADS, t, LANE), F32),
            pltpu.VMEM((t, LANE), F32),
            pltpu.VMEM((t, LANE), F32),
        ],
        compiler_params=_params(("arbitrary", "arbitrary"), est),
        name="dsa_attention",
    )(proj, proj, proj, misc, proj, proj, proj, bias_tiles)


def _merge_kernel(x_ref, ya_ref, yb_ref, yc_ref, ga_ref, gb_ref, gc_ref,
                  wa_ref, wb_ref, wc_ref, wo_ref, o_ref):
    def branch(y_ref, w_ref, g_ref):
        y = jnp.dot(y_ref[...], w_ref[...], preferred_element_type=F32)
        return jax.nn.sigmoid(g_ref[...].astype(F32)) * y

    merged = branch(ya_ref, wa_ref, ga_ref) + branch(yb_ref, wb_ref, gb_ref) + branch(yc_ref, wc_ref, gc_ref)
    o_ref[...] = x_ref[...] + jnp.dot(merged.astype(BF16), wo_ref[...], preferred_element_type=F32)


def _merge(x2, ya, yb, yc, proj, wa, wb, wc, wo, *, tm=256):
    m = x2.shape[0]
    wbytes = (wa.size + wb.size + wc.size + wo.size) * 2
    est = 2 * wbytes + 4 * tm * D_MODEL * 4 + 2 * tm * 2048 * 2 + 6 * tm * D_MODEL * 2 + 4 * tm * D_MODEL * 4
    const = lambda i: (0, 0)
    gate = lambda k: pl.BlockSpec((tm, D_MODEL), lambda i: (i, OFF_GL // D_MODEL + k))
    return pl.pallas_call(
        _merge_kernel,
        grid=(m // tm,),
        in_specs=[
            pl.BlockSpec((tm, D_MODEL), lambda i: (i, 0)),
            pl.BlockSpec((tm, ya.shape[1]), lambda i: (i, 0)),
            pl.BlockSpec((tm, yb.shape[1]), lambda i: (i, 0)),
            pl.BlockSpec((tm, yc.shape[1]), lambda i: (i, 0)),
            gate(0), gate(1), gate(2),
            pl.BlockSpec(wa.shape, const),
            pl.BlockSpec(wb.shape, const),
            pl.BlockSpec(wc.shape, const),
            pl.BlockSpec(wo.shape, const),
        ],
        out_specs=pl.BlockSpec((tm, D_MODEL), lambda i: (i, 0)),
        out_shape=jax.ShapeDtypeStruct((m, D_MODEL), F32),
        compiler_params=_params(("arbitrary",), est),
        name="merge",
    )(x2, ya, yb, yc, proj, proj, proj, wa, wb, wc, wo)


def _ffn_kernel(x_ref, g_ref, wu_ref, wv_ref, cw_ref, wd_ref, fg_ref, o_ref,
                h_ref, ubuf_ref, halo_ref, *, tm, tf, final_norm):
    i = pl.program_id(0)
    j = pl.program_id(1)
    nj = pl.num_programs(1)

    @pl.when(j == 0)
    def _():
        _rmsnorm_rows(x_ref, g_ref, h_ref, tm)

    h = h_ref[...]
    u = jnp.dot(h, wu_ref[...], preferred_element_type=F32)
    v = jnp.dot(h, wv_ref[...], preferred_element_type=F32)

    seq_start = (i % (SEQ // tm)) == 0

    @pl.when(seq_start)
    def _():
        ubuf_ref[0:8, :] = jnp.zeros((8, tf), F32)

    @pl.when(jnp.logical_not(seq_start))
    def _():
        ubuf_ref[0:8, :] = halo_ref[j]

    ubuf_ref[8:, :] = u
    halo_ref[j] = u[tm - 8:, :]
    conv = (cw_ref[0:1, :] * ubuf_ref[6:6 + tm, :] + cw_ref[1:2, :] * ubuf_ref[7:7 + tm, :]
            + cw_ref[2:3, :] * u + cw_ref[3:4, :])
    act = (jax.nn.gelu(conv) * v).astype(BF16)
    down = jnp.dot(act, wd_ref[...], preferred_element_type=F32)

    @pl.when(j == 0)
    def _():
        o_ref[...] = x_ref[...] + down

    @pl.when(j > 0)
    def _():
        o_ref[...] += down

    if final_norm:
        @pl.when(j == nj - 1)
        def _():
            _rmsnorm_rows(o_ref, fg_ref, o_ref, tm)


def _ffn(x2, g, w_up, cw, w_down, final_g, *, final_norm, tm=512, tf=512):
    m = x2.shape[0]
    nj = D_FF // tf
    est = (2 * tm * D_MODEL * 4 * 2 + tm * D_MODEL * 2 + 2 * 2 * D_MODEL * tf * 2 + 2 * tf * D_MODEL * 2
           + (tm + 8) * tf * 4 + nj * 8 * tf * 4 + 6 * tm * tf * 4 + tm * D_MODEL * 4)
    return pl.pallas_call(
        functools.partial(_ffn_kernel, tm=tm, tf=tf, final_norm=final_norm),
        grid=(m // tm, nj),
        in_specs=[
            pl.BlockSpec((tm, D_MODEL), lambda i, j: (i, 0)),
            pl.BlockSpec((1, D_MODEL), lambda i, j: (0, 0)),
            pl.BlockSpec((D_MODEL, tf), lambda i, j: (0, j)),
            pl.BlockSpec((D_MODEL, tf), lambda i, j: (0, nj + j)),
            pl.BlockSpec((8, tf), lambda i, j: (0, j)),
            pl.BlockSpec((tf, D_MODEL), lambda i, j: (j, 0)),
            pl.BlockSpec((1, D_MODEL), lambda i, j: (0, 0)),
        ],
        out_specs=pl.BlockSpec((tm, D_MODEL), lambda i, j: (i, 0)),
        out_shape=jax.ShapeDtypeStruct((m, D_MODEL), F32),
        scratch_shapes=[
            pltpu.VMEM((tm, D_MODEL), BF16),
            pltpu.VMEM((tm + 8, tf), F32),
            pltpu.VMEM((nj, 8, tf), F32),
        ],
        compiler_params=_params(("arbitrary", "arbitrary"), est),
        name="conv_ffn",
    )(x2, g, w_up, w_up, cw, w_down, final_g)


def _t5_bucket(rel):
    nb = NUM_BUCKETS // 2
    max_exact = nb // 2
    base = jnp.where(rel > 0, nb, 0)
    n = jnp.abs(rel)
    nf = jnp.maximum(n, 1).astype(F32)
    large = max_exact + (jnp.log(nf / max_exact) / math.log(MAX_DISTANCE / max_exact)
                         * (nb - max_exact)).astype(jnp.int32)
    large = jnp.minimum(large, nb - 1)
    return base + jnp.where(n < max_exact, n, large)


def _bucket_tiles(t):
    r = jnp.arange(t, dtype=jnp.int32)[:, None]
    c = jnp.arange(t, dtype=jnp.int32)[None, :]
    return jnp.stack([_t5_bucket(c - r - d * t) for d in range(3)])


def _rope_tables():
    half = ROPE_DIM // 2
    inv = ROPE_THETA ** (-jnp.arange(half, dtype=F32) / half)
    ang = jnp.arange(SEQ, dtype=jnp.int32).astype(F32)[:, None] * inv[None, :]
    cos, sin = jnp.cos(ang), jnp.sin(ang)
    pad = jnp.zeros((SEQ, LANE - ROPE_DIM), F32)
    return jnp.concatenate([cos, cos, pad], axis=1), jnp.concatenate([-sin, sin, pad], axis=1)


def _swap_halves(w):
    half = w.shape[-1] // 2
    return jnp.concatenate([w[..., half:], w[..., :half]], axis=-1)


def _pack_w_in(w):
    off = np.cumsum((0,) + IN_SIZES)
    qa, ka, va, qi, ki, wi, qb, kb, vb, fl, cq, ckv, kr, gl = [
        w[:, off[k]:off[k + 1]] for k in range(len(IN_SIZES))]
    z = lambda n: jnp.zeros((w.shape[0], n), w.dtype)
    cols = [qi, qb, kb, vb, gl, qa, cq, z(CQ_PAD - Q_LORA), ka, va, ki, z(64), ckv,
            kr, z(64), _swap_halves(kr), z(64), wi, fl, z(LANE - IDX_HEADS - B_HEADS), z(64), ki]
    packed = jnp.concatenate(cols, axis=1).astype(BF16)
    assert packed.shape[1] == N_PACK
    return packed


def _pack_w_uq(w):
    w = jnp.pad(w, ((0, CQ_PAD - Q_LORA), (0, 0))).reshape(CQ_PAD, C_HEADS, NOPE_DIM + ROPE_DIM)
    z = jnp.zeros((CQ_PAD, C_HEADS, LANE - ROPE_DIM), w.dtype)
    rope = w[..., NOPE_DIM:]
    wq1 = jnp.concatenate([w, z], axis=-1).reshape(CQ_PAD, -1)
    wq2 = jnp.concatenate([_swap_halves(rope), z], axis=-1).reshape(CQ_PAD, -1)
    return wq1.astype(BF16), wq2.astype(BF16)


def kernel(x, norm_mix_g, w_in, b_forget, g_cq, g_ckv, w_uq, w_ukv, w_branch_a, w_branch_b, w_branch_c,
           w_o, norm_ffn_g, w_up, conv_w, conv_b, w_down, t5_bias, final_g):
    batch, seq, d = x.shape
    assert (seq, d) == (SEQ, D_MODEL)
    x2 = x.reshape(batch * seq, d)

    bias_tiles = _t5_tiles(_bucket_tiles(ATT_T), t5_bias)
    cos_t, sin_t = _rope_tables()
    final_row = final_g.reshape(1, D_MODEL)

    for l in range(DEPTH):
        proj, misc = _inproj(x2, norm_mix_g[l].reshape(1, D_MODEL), _pack_w_in(w_in[l]))

        fbias = jnp.zeros((1, LANE), F32).at[0, MISC_FL:MISC_FL + B_HEADS].set(b_forget[l])
        ccol, crow = _forget_cumsum(misc, fbias, batch)

        wq1, wq2 = _pack_w_uq(w_uq[l])
        gq = jnp.pad(g_cq[l], (0, CQ_PAD - Q_LORA)).reshape(1, CQ_PAD)
        qc, kc, vc = _mla_prep(proj, gq, g_ckv[l].reshape(1, KV_LORA), wq1, wq2,
                               w_ukv[l].astype(BF16), cos_t, sin_t)

        ya = _dsa_attention(proj, misc, bias_tiles, batch)
        yb = _fox_attention(proj, ccol, crow, batch)
        yc = _mla_attention(qc, kc, vc, batch)

        x2 = _merge(x2, ya, yb, yc, proj, w_branch_a[l].astype(BF16), w_branch_b[l].astype(BF16),
                    w_branch_c[l].astype(BF16), w_o[l].astype(BF16))

        cw = jnp.concatenate([conv_w[l], conv_b[l][None, :], jnp.zeros((4, D_FF), F32)], axis=0)
        x2 = _ffn(x2, norm_ffn_g[l].reshape(1, D_MODEL), w_up[l].astype(BF16), cw,
                  w_down[l].astype(BF16), final_row, final_norm=(l == DEPTH - 1))

    return x2.reshape(batch, seq, d)
```

```python
import functools
import math

import jax
import jax.numpy as jnp
import numpy as np
from jax import lax
from jax.experimental import pallas as pl
from jax.experimental.pallas import tpu as pltpu

F32 = jnp.float32
BF16 = jnp.bfloat16

D_MODEL = 2048
SEQ = 2048
DEPTH = 2
CHUNK = 64
HEAD_DIM = 128
EPS = 1e-6
NEG_INF = -1e30

A_HEADS = 4
IDX_HEADS = 16
IDX_DIM = 64
TOPK_MAX = 256
NUM_BUCKETS = 32
MAX_DISTANCE = 128
B_HEADS = 8
C_HEADS = 4
Q_LORA = 448
KV_LORA = 128
NOPE_DIM = 128
ROPE_DIM = 64
V_DIM = 128
ROPE_THETA = 10000.0
D_FF = 5632

IN_SIZES = (
    A_HEADS * HEAD_DIM, HEAD_DIM, HEAD_DIM,
    IDX_HEADS * IDX_DIM, IDX_DIM, IDX_HEADS,
    B_HEADS * HEAD_DIM, B_HEADS * HEAD_DIM, B_HEADS * HEAD_DIM, B_HEADS,
    Q_LORA, KV_LORA, ROPE_DIM,
    3 * D_MODEL,
)

LANE = 128
V7X_VMEM_BYTES = 64 * 1024 * 1024

OFF_QI = 0
OFF_QB = 1024
OFF_KB = 2048
OFF_VB = 3072
OFF_GL = 4096
OFF_QA = 10240
OFF_CQ = 10752
OFF_KA = 11264
OFF_VA = 11392
OFF_KI_LO = 11520
OFF_CKV = 11648
OFF_KR = 11776
OFF_KRS = 11904
OFF_MISC = 12032
OFF_KI_HI = 12160
N_PACK = 12288
CQ_PAD = 512
MISC_WI = 0
MISC_FL = IDX_HEADS

ATT_T = 256
SELECT_MIN = -1e29
BISECT_MAX_ITERS = 512


def _vmem_limit(estimate_bytes):
    return int(min(max(estimate_bytes * 5 // 4 + (4 << 20), 32 << 20), V7X_VMEM_BYTES - (6 << 20)))


def _params(semantics, vmem_estimate):
    return pltpu.CompilerParams(dimension_semantics=semantics,
                                vmem_limit_bytes=_vmem_limit(vmem_estimate))


def _rmsnorm_rows(x_ref, g_ref, out_ref, rows):
    def body(c, carry):
        r = pl.multiple_of(c * 128, 128)
        x = x_ref[pl.ds(r, 128), :]
        ms = jnp.mean(x * x, axis=-1, keepdims=True)
        out_ref[pl.ds(r, 128), :] = (x * lax.rsqrt(ms + EPS) * g_ref[...]).astype(out_ref.dtype)
        return carry
    lax.fori_loop(0, rows // 128, body, 0)


def _inproj_kernel(x_ref, g_ref, w_ref, o_ref, misc_ref, h_ref, *, tm, tn):
    j = pl.program_id(1)

    @pl.when(j == 0)
    def _():
        _rmsnorm_rows(x_ref, g_ref, h_ref, tm)

    acc = jnp.dot(h_ref[...], w_ref[...], preferred_element_type=F32)
    o_ref[...] = acc.astype(o_ref.dtype)

    @pl.when(j == OFF_MISC // tn)
    def _():
        lo = OFF_MISC % tn
        misc_ref[...] = acc[:, lo:lo + LANE]


def _inproj(x2, g, w_pack, *, tm=1024, tn=1024):
    m = x2.shape[0]
    est = 2 * tm * D_MODEL * 4 + tm * D_MODEL * 2 + 2 * D_MODEL * tn * 2 + 2 * tm * tn * 2 + tm * tn * 4
    return pl.pallas_call(
        functools.partial(_inproj_kernel, tm=tm, tn=tn),
        grid=(m // tm, N_PACK // tn),
        in_specs=[
            pl.BlockSpec((tm, D_MODEL), lambda i, j: (i, 0)),
            pl.BlockSpec((1, D_MODEL), lambda i, j: (0, 0)),
            pl.BlockSpec((D_MODEL, tn), lambda i, j: (0, j)),
        ],
        out_specs=[
            pl.BlockSpec((tm, tn), lambda i, j: (i, j)),
            pl.BlockSpec((tm, LANE), lambda i, j: (i, 0)),
        ],
        out_shape=[
            jax.ShapeDtypeStruct((m, N_PACK), BF16),
            jax.ShapeDtypeStruct((m, LANE), F32),
        ],
        scratch_shapes=[pltpu.VMEM((tm, D_MODEL), BF16)],
        compiler_params=_params(("arbitrary", "arbitrary"), est),
        name="inproj",
    )(x2, g, w_pack)


def _forget_cumsum_kernel(misc_ref, bias_ref, ccol_ref, crow_ref, *, blk):
    rows = lax.broadcasted_iota(jnp.int32, (blk, blk), 0)
    cols = lax.broadcasted_iota(jnp.int32, (blk, blk), 1)
    tri = jnp.where(rows >= cols, 1.0, 0.0).astype(BF16)
    carry = jnp.zeros((1, LANE), F32)
    for c in range(SEQ // blk):
        z = misc_ref[c * blk:(c + 1) * blk, :] + bias_ref[...]
        lf = jnp.minimum(z, 0.0) - jnp.log1p(jnp.exp(-jnp.abs(z)))
        p0 = lf.astype(BF16)
        r1 = lf - p0.astype(F32)
        p1 = r1.astype(BF16)
        p2 = (r1 - p1.astype(F32)).astype(BF16)
        cs = (jnp.dot(tri, p0, preferred_element_type=F32)
              + jnp.dot(tri, p1, preferred_element_type=F32)
              + jnp.dot(tri, p2, preferred_element_type=F32)) + carry
        ccol_ref[c * blk:(c + 1) * blk, :] = cs
        crow_ref[:, c * blk:(c + 1) * blk] = cs.T[MISC_FL:MISC_FL + B_HEADS, :]
        carry = cs[blk - 1:blk, :]


def _forget_cumsum(misc, bias_row, batch, *, blk=256):
    est = 4 * SEQ * LANE * 4 + 2 * 8 * SEQ * 4
    return pl.pallas_call(
        functools.partial(_forget_cumsum_kernel, blk=blk),
        grid=(batch,),
        in_specs=[
            pl.BlockSpec((SEQ, LANE), lambda b: (b, 0)),
            pl.BlockSpec((1, LANE), lambda b: (0, 0)),
        ],
        out_specs=[
            pl.BlockSpec((SEQ, LANE), lambda b: (b, 0)),
            pl.BlockSpec((None, B_HEADS, SEQ), lambda b: (b, 0, 0)),
        ],
        out_shape=[
            jax.ShapeDtypeStruct((batch * SEQ, LANE), F32),
            jax.ShapeDtypeStruct((batch, B_HEADS, SEQ), F32),
        ],
        compiler_params=_params(("arbitrary",), est),
        name="forget_cumsum",
    )(misc, bias_row)


def _mla_prep_kernel(cq_ref, ckv_ref, kr_ref, krs_ref, gq_ref, gkv_ref, wq1_ref, wq2_ref, wkv_ref,
                     cos_ref, sin_ref, qc_ref, kc_ref, vc_ref):
    cq = cq_ref[...].astype(F32)
    ms = jnp.sum(cq * cq, axis=-1, keepdims=True) * (1.0 / Q_LORA)
    cqn = (cq * lax.rsqrt(ms + EPS) * gq_ref[...]).astype(BF16)
    ckv = ckv_ref[...].astype(F32)
    ms2 = jnp.mean(ckv * ckv, axis=-1, keepdims=True)
    ckvn = (ckv * lax.rsqrt(ms2 + EPS) * gkv_ref[...]).astype(BF16)

    q1 = jnp.dot(cqn, wq1_ref[...], preferred_element_type=F32)
    q2 = jnp.dot(cqn, wq2_ref[...], preferred_element_type=F32)
    kv = jnp.dot(ckvn, wkv_ref[...], preferred_element_type=F32)
    cos = cos_ref[...]
    sin = sin_ref[...]
    k_rope = (kr_ref[...].astype(F32) * cos + krs_ref[...].astype(F32) * sin).astype(BF16)
    for h in range(C_HEADS):
        qw = NOPE_DIM + LANE
        qc_ref[:, h * qw:h * qw + NOPE_DIM] = q1[:, h * qw:h * qw + NOPE_DIM].astype(BF16)
        q_rope = q1[:, h * qw + NOPE_DIM:(h + 1) * qw] * cos + q2[:, h * LANE:(h + 1) * LANE] * sin
        qc_ref[:, h * qw + NOPE_DIM:(h + 1) * qw] = q_rope.astype(BF16)
        kw = NOPE_DIM + V_DIM
        kc_ref[:, h * qw:h * qw + NOPE_DIM] = kv[:, h * kw:h * kw + NOPE_DIM].astype(BF16)
        kc_ref[:, h * qw + NOPE_DIM:(h + 1) * qw] = k_rope
        vc_ref[:, h * V_DIM:(h + 1) * V_DIM] = kv[:, h * kw + NOPE_DIM:(h + 1) * kw].astype(BF16)


def _mla_prep(proj, gq, gkv, wq1, wq2, wkv, cos_t, sin_t, *, tm=512):
    m = proj.shape[0]
    nseq = SEQ // tm
    qw = C_HEADS * (NOPE_DIM + LANE)
    est = 2 * (tm * 1024 * 2) + 2 * (CQ_PAD * qw * 2 + CQ_PAD * 512 * 2 + 128 * 1024 * 2) \
        + 2 * (2 * tm * qw * 2 + tm * 512 * 2) + 3 * tm * qw * 4
    const = lambda i: (0, 0)
    return pl.pallas_call(
        _mla_prep_kernel,
        grid=(m // tm,),
        in_specs=[
            pl.BlockSpec((tm, CQ_PAD), lambda i: (i, OFF_CQ // CQ_PAD)),
            pl.BlockSpec((tm, LANE), lambda i: (i, OFF_CKV // LANE)),
            pl.BlockSpec((tm, LANE), lambda i: (i, OFF_KR // LANE)),
            pl.BlockSpec((tm, LANE), lambda i: (i, OFF_KRS // LANE)),
            pl.BlockSpec((1, CQ_PAD), const),
            pl.BlockSpec((1, KV_LORA), const),
            pl.BlockSpec((CQ_PAD, qw), const),
            pl.BlockSpec((CQ_PAD, C_HEADS * LANE), const),
            pl.BlockSpec((KV_LORA, C_HEADS * (NOPE_DIM + V_DIM)), const),
            pl.BlockSpec((tm, LANE), lambda i: (i % nseq, 0)),
            pl.BlockSpec((tm, LANE), lambda i: (i % nseq, 0)),
        ],
        out_specs=[
            pl.BlockSpec((tm, qw), lambda i: (i, 0)),
            pl.BlockSpec((tm, qw), lambda i: (i, 0)),
            pl.BlockSpec((tm, C_HEADS * V_DIM), lambda i: (i, 0)),
        ],
        out_shape=[
            jax.ShapeDtypeStruct((m, qw), BF16),
            jax.ShapeDtypeStruct((m, qw), BF16),
            jax.ShapeDtypeStruct((m, C_HEADS * V_DIM), BF16),
        ],
        compiler_params=_params(("arbitrary",), est),
        name="mla_prep",
    )(proj, proj, proj, proj, gq, gkv, wq1, wq2, wkv, cos_t, sin_t)


def _nt_dot(a, b):
    return lax.dot_general(a, b, (((1,), (1,)), ((), ())), preferred_element_type=F32)


def _two_pass_attention(i, *, n_heads, tq, tk, logits_fn, mask_fn, v_fn, store_fn,
                        s_ref, mx_ref, ls_ref, acc_ref):
    nc = tk // LANE
    n_full = (i * tq) // tk
    nkb = ((i + 1) * tq + tk - 1) // tk

    mx_ref[...] = jnp.full(mx_ref.shape, NEG_INF, F32)

    def pass1(kb, masked):
        ks = pl.multiple_of(kb * tk, tk)
        mask = mask_fn(ks) if masked else None
        for h in range(n_heads):
            s = logits_fn(h, ks)
            if masked:
                s = jnp.where(mask, s, NEG_INF)
            s_ref[h, :, pl.ds(ks, tk)] = s
            mx = mx_ref[h]
            for c in range(nc):
                mx = jnp.maximum(mx, s[:, c * LANE:(c + 1) * LANE])
            mx_ref[h] = mx

    def pass1_full(kb, carry):
        pass1(kb, False)
        return carry

    def pass1_masked(kb, carry):
        pass1(kb, True)
        return carry

    lax.fori_loop(0, n_full, pass1_full, 0)
    lax.fori_loop(n_full, nkb, pass1_masked, 0)

    for h in range(n_heads):
        mx_ref[h] = jnp.broadcast_to(jnp.max(mx_ref[h], axis=-1, keepdims=True), (tq, LANE))
    ls_ref[...] = jnp.zeros(ls_ref.shape, F32)
    acc_ref[...] = jnp.zeros(acc_ref.shape, F32)

    def pass2(kb, carry):
        ks = pl.multiple_of(kb * tk, tk)
        for h in range(n_heads):
            m = mx_ref[h]
            ls = ls_ref[h]
            ps = []
            for c in range(nc):
                p = jnp.exp(s_ref[h, :, pl.ds(ks + c * LANE, LANE)] - m)
                ls = ls + p
                ps.append(p.astype(BF16))
            ls_ref[h] = ls
            acc_ref[h] += jnp.dot(jnp.concatenate(ps, axis=1), v_fn(h, ks), preferred_element_type=F32)
        return carry

    lax.fori_loop(0, nkb, pass2, 0)

    for h in range(n_heads):
        l = jnp.sum(ls_ref[h], axis=-1, keepdims=True)
        store_fn(h, acc_ref[h] / l)


def _tile_iotas(tq, tk):
    return (lax.broadcasted_iota(jnp.int32, (tq, tk), 0), lax.broadcasted_iota(jnp.int32, (tq, tk), 1))


def _fox_kernel(q_ref, k_ref, v_ref, ccol_ref, crow_ref, o_ref,
                s_ref, mx_ref, ls_ref, acc_ref, cq_ref, *, tq, tk):
    i = pl.program_id(1)
    scale = HEAD_DIM ** -0.5
    nc = tk // LANE
    head = lambda h: slice(h * HEAD_DIM, (h + 1) * HEAD_DIM)
    for h in range(B_HEADS):
        cq_ref[h] = jnp.broadcast_to(ccol_ref[:, MISC_FL + h:MISC_FL + h + 1], (tq, LANE))

    def logits_fn(h, ks):
        s = _nt_dot(q_ref[:, head(h)], k_ref[pl.ds(ks, tk), head(h)]) * scale
        return s + jnp.concatenate([cq_ref[h]] * nc, axis=1) - crow_ref[h:h + 1, pl.ds(ks, tk)]

    def mask_fn(ks):
        rows, cols = _tile_iotas(tq, tk)
        return cols + (ks - i * tq) <= rows

    def store_fn(h, y):
        o_ref[:, head(h)] = y.astype(o_ref.dtype)

    _two_pass_attention(i, n_heads=B_HEADS, tq=tq, tk=tk, logits_fn=logits_fn, mask_fn=mask_fn,
                        v_fn=lambda h, ks: v_ref[pl.ds(ks, tk), head(h)], store_fn=store_fn,
                        s_ref=s_ref, mx_ref=mx_ref, ls_ref=ls_ref, acc_ref=acc_ref)


def _fox_attention(proj, ccol, crow, batch, *, tq=256, tk=512):
    m = proj.shape[0]
    nq = SEQ // tq
    w = B_HEADS * HEAD_DIM
    state = B_HEADS * tq * LANE * 4
    est = (4 * tq * w * 2 + 4 * SEQ * w * 2 + 2 * tq * LANE * 4 + 2 * 8 * SEQ * 4
           + B_HEADS * tq * SEQ * 4 + 4 * state + 6 * tq * tk * 4)
    return pl.pallas_call(
        functools.partial(_fox_kernel, tq=tq, tk=tk),
        grid=(batch, nq),
        in_specs=[
            pl.BlockSpec((tq, w), lambda b, i: (b * nq + i, OFF_QB // w)),
            pl.BlockSpec((SEQ, w), lambda b, i: (b, OFF_KB // w)),
            pl.BlockSpec((SEQ, w), lambda b, i: (b, OFF_VB // w)),
            pl.BlockSpec((tq, LANE), lambda b, i: (b * nq + i, 0)),
            pl.BlockSpec((None, B_HEADS, SEQ), lambda b, i: (b, 0, 0)),
        ],
        out_specs=pl.BlockSpec((tq, w), lambda b, i: (b * nq + i, 0)),
        out_shape=jax.ShapeDtypeStruct((m, w), BF16),
        scratch_shapes=[
            pltpu.VMEM((B_HEADS, tq, SEQ), F32),
            pltpu.VMEM((B_HEADS, tq, LANE), F32),
            pltpu.VMEM((B_HEADS, tq, LANE), F32),
            pltpu.VMEM((B_HEADS, tq, HEAD_DIM), F32),
            pltpu.VMEM((B_HEADS, tq, LANE), F32),
        ],
        compiler_params=_params(("arbitrary", "arbitrary"), est),
        name="fox_attention",
    )(proj, proj, proj, ccol, crow)


def _mla_kernel(q_ref, k_ref, v_ref, o_ref, s_ref, mx_ref, ls_ref, acc_ref, *, tq, tk):
    i = pl.program_id(1)
    scale = (NOPE_DIM + ROPE_DIM) ** -0.5
    qw = NOPE_DIM + LANE
    qhead = lambda h: slice(h * qw, (h + 1) * qw)
    vhead = lambda h: slice(h * V_DIM, (h + 1) * V_DIM)

    def logits_fn(h, ks):
        return _nt_dot(q_ref[:, qhead(h)], k_ref[pl.ds(ks, tk), qhead(h)]) * scale

    def mask_fn(ks):
        rows, cols = _tile_iotas(tq, tk)
        return (cols + ks) // CHUNK <= (rows + i * tq) // CHUNK

    def store_fn(h, y):
        o_ref[:, vhead(h)] = y.astype(o_ref.dtype)

    _two_pass_attention(i, n_heads=C_HEADS, tq=tq, tk=tk, logits_fn=logits_fn, mask_fn=mask_fn,
                        v_fn=lambda h, ks: v_ref[pl.ds(ks, tk), vhead(h)], store_fn=store_fn,
                        s_ref=s_ref, mx_ref=mx_ref, ls_ref=ls_ref, acc_ref=acc_ref)


def _mla_attention(qc, kc, vc, batch, *, tq=256, tk=512):
    m = qc.shape[0]
    nq = SEQ // tq
    qw = C_HEADS * (NOPE_DIM + LANE)
    vw = C_HEADS * V_DIM
    state = C_HEADS * tq * LANE * 4
    est = (2 * tq * qw * 2 + 2 * SEQ * qw * 2 + 2 * SEQ * vw * 2 + 2 * tq * vw * 2
           + C_HEADS * tq * SEQ * 4 + 3 * state + 6 * tq * tk * 4)
    return pl.pallas_call(
        functools.partial(_mla_kernel, tq=tq, tk=tk),
        grid=(batch, nq),
        in_specs=[
            pl.BlockSpec((tq, qw), lambda b, i: (b * nq + i, 0)),
            pl.BlockSpec((SEQ, qw), lambda b, i: (b, 0)),
            pl.BlockSpec((SEQ, vw), lambda b, i: (b, 0)),
        ],
        out_specs=pl.BlockSpec((tq, vw), lambda b, i: (b * nq + i, 0)),
        out_shape=jax.ShapeDtypeStruct((m, vw), BF16),
        scratch_shapes=[
            pltpu.VMEM((C_HEADS, tq, SEQ), F32),
            pltpu.VMEM((C_HEADS, tq, LANE), F32),
            pltpu.VMEM((C_HEADS, tq, LANE), F32),
            pltpu.VMEM((C_HEADS, tq, V_DIM), F32),
        ],
        compiler_params=_params(("arbitrary", "arbitrary"), est),
        name="mla_attention",
    )(qc, kc, vc)


def _t5_tiles_kernel(bucket_ref, table_ref, o_ref):
    for d in range(3):
        bucket = bucket_ref[d]
        for h in range(A_HEADS):
            acc = jnp.zeros(bucket.shape, F32)
            for nb in range(NUM_BUCKETS):
                acc = jnp.where(bucket == nb, table_ref[nb, h], acc)
            o_ref[d, h] = acc


def _t5_tiles(bucket_tiles, t5_bias, *, t=ATT_T):
    return pl.pallas_call(
        _t5_tiles_kernel,
        in_specs=[
            pl.BlockSpec(memory_space=pltpu.VMEM),
            pl.BlockSpec(memory_space=pltpu.SMEM),
        ],
        out_specs=pl.BlockSpec(memory_space=pltpu.VMEM),
        out_shape=jax.ShapeDtypeStruct((3, A_HEADS, t, t), F32),
        name="t5_tiles",
    )(bucket_tiles, t5_bias)


def _dsa_kernel(qi_ref, klo_ref, khi_ref, misc_ref, qa_ref, ka_ref, va_ref, bias_ref, o_ref,
                sc_ref, lg_ref, acc_ref, *, t):
    i = pl.program_id(1)
    groups = t // 8
    key = lax.broadcasted_iota(jnp.int32, (t, t), 0)
    qry = lax.broadcasted_iota(jnp.int32, (t, t), 1)
    admissible = (key // CHUNK) <= (qry // CHUNK)

    def fold(x, op):
        return op(x.reshape(groups, 8, t), axis=0)

    def over_keys(x8, op):
        return jnp.broadcast_to(op(x8, axis=0, keepdims=True), (8, t))

    w_t = misc_ref[...].T[MISC_WI:MISC_WI + IDX_HEADS, :] * (IDX_HEADS ** -0.5 * IDX_DIM ** -0.5)

    def block_scores(kb):
        ks = pl.multiple_of(kb * t, t)
        klo = klo_ref[pl.ds(ks, t), :]
        khi = khi_ref[pl.ds(ks, t), :]
        acc = jnp.zeros((t, t), F32)
        for pair in range(IDX_HEADS // 2):
            qp = qi_ref[:, pair * LANE:(pair + 1) * LANE]
            for sub, kk in enumerate((klo, khi)):
                h = 2 * pair + sub
                acc = acc + jnp.maximum(_nt_dot(kk, qp), 0.0) * w_t[h:h + 1, :]
        return ks, acc

    def score_body(kb, carry):
        mn, mx = carry
        ks, acc = block_scores(kb)
        sc_ref[pl.ds(ks, t), :] = acc
        return jnp.minimum(mn, fold(acc, jnp.min)), jnp.maximum(mx, fold(acc, jnp.max))

    mn, mx = lax.fori_loop(0, i, score_body,
                           (jnp.full((8, t), -SELECT_MIN, F32), jnp.full((8, t), SELECT_MIN, F32)))
    ks, acc = block_scores(i)
    sc_ref[pl.ds(ks, t), :] = jnp.where(admissible, acc, NEG_INF)
    mn = jnp.minimum(mn, fold(jnp.where(admissible, acc, -SELECT_MIN), jnp.min))
    mx = jnp.maximum(mx, fold(jnp.where(admissible, acc, SELECT_MIN), jnp.max))

    def count_ge(thr):
        def body(kb, cnt):
            ks = pl.multiple_of(kb * t, t)
            hit = jnp.where(sc_ref[pl.ds(ks, t), :].reshape(groups, 8, t) >= thr[None], 1.0, 0.0)
            return cnt + jnp.sum(hit, axis=0)
        return over_keys(lax.fori_loop(0, i + 1, body, jnp.zeros((8, t), F32)), jnp.sum)

    k_sel = float(TOPK_MAX)
    search = i * t >= TOPK_MAX
    lo0 = jnp.where(search, over_keys(mn, jnp.min), SELECT_MIN)
    hi0 = jnp.where(search, over_keys(mx, jnp.max), SELECT_MIN)

    def midpoint(lo, hi):
        return 0.5 * lo + 0.5 * hi

    def any_active(lo, hi):
        mid = midpoint(lo, hi)
        return jnp.max(jnp.where((mid > lo) & (mid < hi), 1.0, 0.0)) > 0.5

    def search_cond(state):
        return jnp.logical_and(state[0] < BISECT_MAX_ITERS, state[1])

    def search_body(state):
        it, _, lo, hi, c_lo, c_hi = state
        mid = midpoint(lo, hi)
        guess = lo + (hi - lo) * ((c_lo - k_sel + 0.5) / (c_lo - c_hi))
        use_guess = jnp.logical_and(it % 2 == 0, (guess > lo) & (guess < hi))
        probe = jnp.where(use_guess, guess, mid)
        cnt = count_ge(probe)
        ge = cnt >= k_sel
        hit = cnt == k_sel
        lo = jnp.where(ge, probe, lo)
        c_lo = jnp.where(ge, cnt, c_lo)
        hi = jnp.where(ge & ~hit, hi, probe)
        c_hi = jnp.where(ge & ~hit, c_hi, jnp.minimum(cnt, k_sel - 1.0))
        return it + 1, any_active(lo, hi), lo, hi, c_lo, c_hi

    n_adm = ((i * t + qry[:8]) // CHUNK + 1) * CHUNK
    state = (jnp.int32(0), any_active(lo0, hi0), lo0, hi0,
             jnp.maximum(n_adm.astype(F32), k_sel), jnp.zeros((8, t), F32))
    _, _, lo, hi, _, _ = lax.while_loop(search_cond, search_body, state)
    thr = jnp.where(count_ge(hi) >= k_sel, hi, lo)

    scale = HEAD_DIM ** -0.5
    head = lambda h: slice(h * HEAD_DIM, (h + 1) * HEAD_DIM)
    acc_ref[...] = jnp.zeros(acc_ref.shape, F32)

    def logits_pass(kb, mx):
        ks = pl.multiple_of(kb * t, t)
        dist = jnp.minimum(i - kb, 2)
        k_blk = ka_ref[pl.ds(ks, t), :]
        sel = sc_ref[pl.ds(ks, t), :] >= thr[0:1, :]
        out = []
        for h in range(A_HEADS):
            s = _nt_dot(k_blk, qa_ref[:, head(h)]) * scale + bias_ref[dist, h]
            s = jnp.where(sel, s, NEG_INF)
            lg_ref[h, pl.ds(ks, t), :] = s
            out.append(jnp.maximum(mx[h], fold(s, jnp.max)))
        return tuple(out)

    mx = lax.fori_loop(0, i + 1, logits_pass,
                       tuple(jnp.full((8, t), NEG_INF, F32) for _ in range(A_HEADS)))
    m = [over_keys(mx[h], jnp.max)[0:1, :] for h in range(A_HEADS)]

    def value_pass(kb, ls):
        ks = pl.multiple_of(kb * t, t)
        v_t = va_ref[pl.ds(ks, t), :].astype(F32).T.astype(BF16)
        out = []
        for h in range(A_HEADS):
            p = jnp.exp(lg_ref[h, pl.ds(ks, t), :] - m[h])
            out.append(ls[h] + fold(p, jnp.sum))
            acc_ref[h] += jnp.dot(v_t, p.astype(BF16), preferred_element_type=F32)
        return tuple(out)

    ls = lax.fori_loop(0, i + 1, value_pass, tuple(jnp.zeros((8, t), F32) for _ in range(A_HEADS)))
    for h in range(A_HEADS):
        l = over_keys(ls[h], jnp.sum)[0:1, :]
        o_ref[:, head(h)] = (acc_ref[h] / l).T.astype(o_ref.dtype)


def _dsa_attention(proj, misc, bias_tiles, batch, *, t=ATT_T):
    m = proj.shape[0]
    nq = SEQ // t
    qiw = IDX_HEADS * IDX_DIM
    qaw = A_HEADS * HEAD_DIM
    est = (2 * t * qiw * 2 + 2 * 4 * SEQ * LANE * 2 + 2 * t * LANE * 4 + 2 * t * qaw * 2
           + 2 * 3 * A_HEADS * t * t * 4 + 2 * t * qaw * 2
           + (1 + A_HEADS) * t * SEQ * 4 + A_HEADS * HEAD_DIM * t * 4 + 12 * t * t * 4)
    kblock = lambda off: pl.BlockSpec((SEQ, LANE), lambda b, i: (b, off // LANE))
    return pl.pallas_call(
        functools.partial(_dsa_kernel, t=t),
        grid=(batch, nq),
        in_specs=[
            pl.BlockSpec((t, qiw), lambda b, i: (b * nq + i, OFF_QI // qiw)),
            kblock(OFF_KI_LO),
            kblock(OFF_KI_HI),
            pl.BlockSpec((t, LANE), lambda b, i: (b * nq + i, 0)),
            pl.BlockSpec((t, qaw), lambda b, i: (b * nq + i, OFF_QA // qaw)),
            kblock(OFF_KA),
            kblock(OFF_VA),
            pl.BlockSpec((3, A_HEADS, t, t), lambda b, i: (0, 0, 0, 0)),
        ],
        out_specs=pl.BlockSpec((t, qaw), lambda b, i: (b * nq + i, 0)),
        out_shape=jax.ShapeDtypeStruct((m, qaw), BF16),
        scratch_shapes=[
            pltpu.VMEM((SEQ, t), F32),
            pltpu.VMEM((A_HEADS, SEQ, t), F32),
            pltpu.VMEM((A_HEADS, HEAD_DIM, t), F32),
        ],
        compiler_params=_params(("arbitrary", "arbitrary"), est),
        name="dsa_attention",
    )(proj, proj, proj, misc, proj, proj, proj, bias_tiles)


def _merge_kernel(x_ref, ya_ref, yb_ref, yc_ref, ga_ref, gb_ref, gc_ref,
                  wa_ref, wb_ref, wc_ref, wo_ref, o_ref):
    def branch(y_ref, w_ref, g_ref):
        y = jnp.dot(y_ref[...], w_ref[...], preferred_element_type=F32)
        return jax.nn.sigmoid(g_ref[...].astype(F32)) * y

    merged = branch(ya_ref, wa_ref, ga_ref) + branch(yb_ref, wb_ref, gb_ref) + branch(yc_ref, wc_ref, gc_ref)
    o_ref[...] = x_ref[...] + jnp.dot(merged.astype(BF16), wo_ref[...], preferred_element_type=F32)


def _merge(x2, ya, yb, yc, proj, wa, wb, wc, wo, *, tm=256):
    m = x2.shape[0]
    wbytes = (wa.size + wb.size + wc.size + wo.size) * 2
    est = 2 * wbytes + 4 * tm * D_MODEL * 4 + 2 * tm * 2048 * 2 + 6 * tm * D_MODEL * 2 + 4 * tm * D_MODEL * 4
    const = lambda i: (0, 0)
    gate = lambda k: pl.BlockSpec((tm, D_MODEL), lambda i: (i, OFF_GL // D_MODEL + k))
    return pl.pallas_call(
        _merge_kernel,
        grid=(m // tm,),
        in_specs=[
            pl.BlockSpec((tm, D_MODEL), lambda i: (i, 0)),
            pl.BlockSpec((tm, ya.shape[1]), lambda i: (i, 0)),
            pl.BlockSpec((tm, yb.shape[1]), lambda i: (i, 0)),
            pl.BlockSpec((tm, yc.shape[1]), lambda i: (i, 0)),
            gate(0), gate(1), gate(2),
            pl.BlockSpec(wa.shape, const),
            pl.BlockSpec(wb.shape, const),
            pl.BlockSpec(wc.shape, const),
            pl.BlockSpec(wo.shape, const),
        ],
        out_specs=pl.BlockSpec((tm, D_MODEL), lambda i: (i, 0)),
        out_shape=jax.ShapeDtypeStruct((m, D_MODEL), F32),
        compiler_params=_params(("arbitrary",), est),
        name="merge",
    )(x2, ya, yb, yc, proj, proj, proj, wa, wb, wc, wo)


def _ffn_kernel(x_ref, g_ref, wu_ref, wv_ref, cw_ref, wd_ref, fg_ref, o_ref,
                h_ref, ubuf_ref, halo_ref, *, tm, tf, final_norm):
    i = pl.program_id(0)
    j = pl.program_id(1)
    nj = pl.num_programs(1)

    @pl.when(j == 0)
    def _():
        _rmsnorm_rows(x_ref, g_ref, h_ref, tm)
        o_ref[...] = x_ref[...]

        @pl.when((i % (SEQ // tm)) == 0)
        def _():
            halo_ref[...] = jnp.zeros(halo_ref.shape, F32)

    h = h_ref[...]
    u = jnp.dot(h, wu_ref[...], preferred_element_type=F32)
    v = jnp.dot(h, wv_ref[...], preferred_element_type=F32)

    ubuf_ref[0:8, :] = halo_ref[j]
    ubuf_ref[8:, :] = u
    halo_ref[j] = u[tm - 8:, :]
    conv = (cw_ref[0:1, :] * ubuf_ref[6:6 + tm, :] + cw_ref[1:2, :] * ubuf_ref[7:7 + tm, :]
            + cw_ref[2:3, :] * u + cw_ref[3:4, :])
    act = (jax.nn.gelu(conv) * v).astype(BF16)
    o_ref[...] += jnp.dot(act, wd_ref[...], preferred_element_type=F32)

    if final_norm:
        @pl.when(j == nj - 1)
        def _():
            _rmsnorm_rows(o_ref, fg_ref, o_ref, tm)


def _ffn(x2, g, w_up, cw, w_down, final_g, *, final_norm, tm=512, tf=512):
    m = x2.shape[0]
    nj = D_FF // tf
    est = (2 * tm * D_MODEL * 4 * 2 + tm * D_MODEL * 2 + 2 * 2 * D_MODEL * tf * 2 + 2 * tf * D_MODEL * 2
           + (tm + 8) * tf * 4 + nj * 8 * tf * 4 + 6 * tm * tf * 4 + tm * D_MODEL * 4)
    return pl.pallas_call(
        functools.partial(_ffn_kernel, tm=tm, tf=tf, final_norm=final_norm),
        grid=(m // tm, nj),
        in_specs=[
            pl.BlockSpec((tm, D_MODEL), lambda i, j: (i, 0)),
            pl.BlockSpec((1, D_MODEL), lambda i, j: (0, 0)),
            pl.BlockSpec((D_MODEL, tf), lambda i, j: (0, j)),
            pl.BlockSpec((D_MODEL, tf), lambda i, j: (0, nj + j)),
            pl.BlockSpec((8, tf), lambda i, j: (0, j)),
            pl.BlockSpec((tf, D_MODEL), lambda i, j: (j, 0)),
            pl.BlockSpec((1, D_MODEL), lambda i, j: (0, 0)),
        ],
        out_specs=pl.BlockSpec((tm, D_MODEL), lambda i, j: (i, 0)),
        out_shape=jax.ShapeDtypeStruct((m, D_MODEL), F32),
        scratch_shapes=[
            pltpu.VMEM((tm, D_MODEL), BF16),
            pltpu.VMEM((tm + 8, tf), F32),
            pltpu.VMEM((nj, 8, tf), F32),
        ],
        compiler_params=_params(("arbitrary", "arbitrary"), est),
        name="conv_ffn",
    )(x2, g, w_up, w_up, cw, w_down, final_g)


def _t5_bucket(rel):
    nb = NUM_BUCKETS // 2
    max_exact = nb // 2
    base = jnp.where(rel > 0, nb, 0)
    n = jnp.abs(rel)
    nf = jnp.maximum(n, 1).astype(F32)
    large = max_exact + (jnp.log(nf / max_exact) / math.log(MAX_DISTANCE / max_exact)
                         * (nb - max_exact)).astype(jnp.int32)
    large = jnp.minimum(large, nb - 1)
    return base + jnp.where(n < max_exact, n, large)


def _bucket_tiles(t):
    key = jnp.arange(t, dtype=jnp.int32)[:, None]
    qry = jnp.arange(t, dtype=jnp.int32)[None, :]
    return jnp.stack([_t5_bucket(key - qry - d * t) for d in range(3)])


def _rope_tables():
    half = ROPE_DIM // 2
    inv = ROPE_THETA ** (-jnp.arange(half, dtype=F32) / half)
    ang = jnp.arange(SEQ, dtype=jnp.int32).astype(F32)[:, None] * inv[None, :]
    cos, sin = jnp.cos(ang), jnp.sin(ang)
    pad = jnp.zeros((SEQ, LANE - ROPE_DIM), F32)
    return jnp.concatenate([cos, cos, pad], axis=1), jnp.concatenate([-sin, sin, pad], axis=1)


def _swap_halves(w):
    half = w.shape[-1] // 2
    return jnp.concatenate([w[..., half:], w[..., :half]], axis=-1)


def _pack_w_in(w):
    off = np.cumsum((0,) + IN_SIZES)
    qa, ka, va, qi, ki, wi, qb, kb, vb, fl, cq, ckv, kr, gl = [
        w[:, off[k]:off[k + 1]] for k in range(len(IN_SIZES))]
    z = lambda n: jnp.zeros((w.shape[0], n), w.dtype)
    cols = [qi, qb, kb, vb, gl, qa, cq, z(CQ_PAD - Q_LORA), ka, va, ki, z(64), ckv,
            kr, z(64), _swap_halves(kr), z(64), wi, fl, z(LANE - IDX_HEADS - B_HEADS), z(64), ki]
    packed = jnp.concatenate(cols, axis=1).astype(BF16)
    assert packed.shape[1] == N_PACK
    return packed


def _pack_w_uq(w):
    w = jnp.pad(w, ((0, CQ_PAD - Q_LORA), (0, 0))).reshape(CQ_PAD, C_HEADS, NOPE_DIM + ROPE_DIM)
    z = jnp.zeros((CQ_PAD, C_HEADS, LANE - ROPE_DIM), w.dtype)
    rope = w[..., NOPE_DIM:]
    wq1 = jnp.concatenate([w, z], axis=-1).reshape(CQ_PAD, -1)
    wq2 = jnp.concatenate([_swap_halves(rope), z], axis=-1).reshape(CQ_PAD, -1)
    return wq1.astype(BF16), wq2.astype(BF16)


def kernel(x, norm_mix_g, w_in, b_forget, g_cq, g_ckv, w_uq, w_ukv, w_branch_a, w_branch_b, w_branch_c,
           w_o, norm_ffn_g, w_up, conv_w, conv_b, w_down, t5_bias, final_g):
    batch, seq, d = x.shape
    assert (seq, d) == (SEQ, D_MODEL)
    x2 = x.reshape(batch * seq, d)

    bias_tiles = _t5_tiles(_bucket_tiles(ATT_T), t5_bias)
    cos_t, sin_t = _rope_tables()
    final_row = final_g.reshape(1, D_MODEL)

    for l in range(DEPTH):
        proj, misc = _inproj(x2, norm_mix_g[l].reshape(1, D_MODEL), _pack_w_in(w_in[l]))

        fbias = jnp.zeros((1, LANE), F32).at[0, MISC_FL:MISC_FL + B_HEADS].set(b_forget[l])
        ccol, crow = _forget_cumsum(misc, fbias, batch)

        wq1, wq2 = _pack_w_uq(w_uq[l])
        gq = jnp.pad(g_cq[l], (0, CQ_PAD - Q_LORA)).reshape(1, CQ_PAD)
        qc, kc, vc = _mla_prep(proj, gq, g_ckv[l].reshape(1, KV_LORA), wq1, wq2,
                               w_ukv[l].astype(BF16), cos_t, sin_t)

        ya = _dsa_attention(proj, misc, bias_tiles, batch)
        yb = _fox_attention(proj, ccol, crow, batch)
        yc = _mla_attention(qc, kc, vc, batch)

        x2 = _merge(x2, ya, yb, yc, proj, w_branch_a[l].astype(BF16), w_branch_b[l].astype(BF16),
                    w_branch_c[l].astype(BF16), w_o[l].astype(BF16))

        cw = jnp.concatenate([conv_w[l], conv_b[l][None, :], jnp.zeros((4, D_FF), F32)], axis=0)
        x2 = _ffn(x2, norm_ffn_g[l].reshape(1, D_MODEL), w_up[l].astype(BF16), cw,
                  w_down[l].astype(BF16), final_row, final_norm=(l == DEPTH - 1))

    return x2.reshape(batch, seq, d)
```

```python
import functools
import math

import jax
import jax.numpy as jnp
import numpy as np
from jax import lax
from jax.experimental import pallas as pl
from jax.experimental.pallas import tpu as pltpu

F32 = jnp.float32
BF16 = jnp.bfloat16

D_MODEL = 2048
SEQ = 2048
DEPTH = 2
CHUNK = 64
HEAD_DIM = 128
EPS = 1e-6
NEG_INF = -1e30

A_HEADS = 4
IDX_HEADS = 16
IDX_DIM = 64
TOPK_MAX = 256
NUM_BUCKETS = 32
MAX_DISTANCE = 128
B_HEADS = 8
C_HEADS = 4
Q_LORA = 448
KV_LORA = 128
NOPE_DIM = 128
ROPE_DIM = 64
V_DIM = 128
ROPE_THETA = 10000.0
D_FF = 5632

IN_SIZES = (
    A_HEADS * HEAD_DIM, HEAD_DIM, HEAD_DIM,
    IDX_HEADS * IDX_DIM, IDX_DIM, IDX_HEADS,
    B_HEADS * HEAD_DIM, B_HEADS * HEAD_DIM, B_HEADS * HEAD_DIM, B_HEADS,
    Q_LORA, KV_LORA, ROPE_DIM,
    3 * D_MODEL,
)

LOG2E = math.log2(math.e)
LANE = 128
V7X_VMEM_BYTES = 64 * 1024 * 1024

OFF_QI = 0
OFF_QB = 1024
OFF_KB = 2048
OFF_VB = 3072
OFF_GL = 4096
OFF_QA = 10240
OFF_CQ = 10752
OFF_KA = 11264
OFF_VA = 11392
OFF_KI_LO = 11520
OFF_CKV = 11648
OFF_KR = 11776
OFF_KRS = 11904
OFF_MISC = 12032
OFF_KI_HI = 12160
N_PACK = 12288
CQ_PAD = 512
MISC_WI = 0
MISC_FL = IDX_HEADS

ATT_T = 256
SELECT_MIN = -1e29
BISECT_MAX_ITERS = 512
SEARCH_STEPS_PER_CHECK = 2
DSA_ONES_ROWS = 16


def _vmem_limit(estimate_bytes):
    return int(min(max(estimate_bytes * 5 // 4 + (4 << 20), 32 << 20), V7X_VMEM_BYTES - (6 << 20)))


def _params(semantics, vmem_estimate):
    return pltpu.CompilerParams(dimension_semantics=semantics,
                                vmem_limit_bytes=_vmem_limit(vmem_estimate))


def _rmsnorm_rows(x_ref, g_ref, out_ref, rows):
    def body(c, carry):
        r = pl.multiple_of(c * 128, 128)
        x = x_ref[pl.ds(r, 128), :]
        ms = jnp.mean(x * x, axis=-1, keepdims=True)
        out_ref[pl.ds(r, 128), :] = (x * lax.rsqrt(ms + EPS) * g_ref[...]).astype(out_ref.dtype)
        return carry
    lax.fori_loop(0, rows // 128, body, 0)


def _inproj_kernel(x_ref, g_ref, w_ref, o_ref, misc_ref, h_ref, *, tm, tn):
    j = pl.program_id(1)

    @pl.when(j == 0)
    def _():
        _rmsnorm_rows(x_ref, g_ref, h_ref, tm)

    acc = jnp.dot(h_ref[...], w_ref[...], preferred_element_type=F32)
    o_ref[...] = acc.astype(o_ref.dtype)

    @pl.when(j == OFF_MISC // tn)
    def _():
        lo = OFF_MISC % tn
        misc_ref[...] = acc[:, lo:lo + LANE]


def _inproj(x2, g, w_pack, *, tm=1024, tn=1024):
    m = x2.shape[0]
    est = 2 * tm * D_MODEL * 4 + tm * D_MODEL * 2 + 2 * D_MODEL * tn * 2 + 2 * tm * tn * 2 + tm * tn * 4
    return pl.pallas_call(
        functools.partial(_inproj_kernel, tm=tm, tn=tn),
        grid=(m // tm, N_PACK // tn),
        in_specs=[
            pl.BlockSpec((tm, D_MODEL), lambda i, j: (i, 0)),
            pl.BlockSpec((1, D_MODEL), lambda i, j: (0, 0)),
            pl.BlockSpec((D_MODEL, tn), lambda i, j: (0, j)),
        ],
        out_specs=[
            pl.BlockSpec((tm, tn), lambda i, j: (i, j)),
            pl.BlockSpec((tm, LANE), lambda i, j: (i, 0)),
        ],
        out_shape=[
            jax.ShapeDtypeStruct((m, N_PACK), BF16),
            jax.ShapeDtypeStruct((m, LANE), F32),
        ],
        scratch_shapes=[pltpu.VMEM((tm, D_MODEL), BF16)],
        compiler_params=_params(("arbitrary", "arbitrary"), est),
        name="inproj",
    )(x2, g, w_pack)


def _forget_cumsum_kernel(misc_ref, bias_ref, ccol_ref, crow_ref, *, blk):
    rows = lax.broadcasted_iota(jnp.int32, (blk, blk), 0)
    cols = lax.broadcasted_iota(jnp.int32, (blk, blk), 1)
    tri = jnp.where(rows >= cols, 1.0, 0.0).astype(BF16)
    carry = jnp.zeros((1, LANE), F32)
    for c in range(SEQ // blk):
        z = misc_ref[c * blk:(c + 1) * blk, :] + bias_ref[...]
        lf = jnp.minimum(z, 0.0) - jnp.log1p(jnp.exp(-jnp.abs(z)))
        p0 = lf.astype(BF16)
        r1 = lf - p0.astype(F32)
        p1 = r1.astype(BF16)
        p2 = (r1 - p1.astype(F32)).astype(BF16)
        cs = (jnp.dot(tri, p0, preferred_element_type=F32)
              + jnp.dot(tri, p1, preferred_element_type=F32)
              + jnp.dot(tri, p2, preferred_element_type=F32)) + carry
        ccol_ref[c * blk:(c + 1) * blk, :] = cs
        crow_ref[:, c * blk:(c + 1) * blk] = cs.T[MISC_FL:MISC_FL + B_HEADS, :]
        carry = cs[blk - 1:blk, :]


def _forget_cumsum(misc, bias_row, batch, *, blk=256):
    est = 4 * SEQ * LANE * 4 + 2 * 8 * SEQ * 4
    return pl.pallas_call(
        functools.partial(_forget_cumsum_kernel, blk=blk),
        grid=(batch,),
        in_specs=[
            pl.BlockSpec((SEQ, LANE), lambda b: (b, 0)),
            pl.BlockSpec((1, LANE), lambda b: (0, 0)),
        ],
        out_specs=[
            pl.BlockSpec((SEQ, LANE), lambda b: (b, 0)),
            pl.BlockSpec((None, B_HEADS, SEQ), lambda b: (b, 0, 0)),
        ],
        out_shape=[
            jax.ShapeDtypeStruct((batch * SEQ, LANE), F32),
            jax.ShapeDtypeStruct((batch, B_HEADS, SEQ), F32),
        ],
        compiler_params=_params(("arbitrary",), est),
        name="forget_cumsum",
    )(misc, bias_row)


def _mla_prep_kernel(cq_ref, ckv_ref, kr_ref, krs_ref, gq_ref, gkv_ref, wq1_ref, wq2_ref, wkv_ref,
                     cos_ref, sin_ref, qc_ref, kc_ref, vc_ref):
    cq = cq_ref[...].astype(F32)
    ms = jnp.sum(cq * cq, axis=-1, keepdims=True) * (1.0 / Q_LORA)
    cqn = (cq * lax.rsqrt(ms + EPS) * gq_ref[...]).astype(BF16)
    ckv = ckv_ref[...].astype(F32)
    ms2 = jnp.mean(ckv * ckv, axis=-1, keepdims=True)
    ckvn = (ckv * lax.rsqrt(ms2 + EPS) * gkv_ref[...]).astype(BF16)

    q1 = jnp.dot(cqn, wq1_ref[...], preferred_element_type=F32)
    q2 = jnp.dot(cqn, wq2_ref[...], preferred_element_type=F32)
    kv = jnp.dot(ckvn, wkv_ref[...], preferred_element_type=F32)
    cos = cos_ref[...]
    sin = sin_ref[...]
    k_rope = (kr_ref[...].astype(F32) * cos + krs_ref[...].astype(F32) * sin).astype(BF16)
    for h in range(C_HEADS):
        qw = NOPE_DIM + LANE
        qc_ref[:, h * qw:h * qw + NOPE_DIM] = q1[:, h * qw:h * qw + NOPE_DIM].astype(BF16)
        q_rope = q1[:, h * qw + NOPE_DIM:(h + 1) * qw] * cos + q2[:, h * LANE:(h + 1) * LANE] * sin
        qc_ref[:, h * qw + NOPE_DIM:(h + 1) * qw] = q_rope.astype(BF16)
        kw = NOPE_DIM + V_DIM
        kc_ref[:, h * qw:h * qw + NOPE_DIM] = kv[:, h * kw:h * kw + NOPE_DIM].astype(BF16)
        kc_ref[:, h * qw + NOPE_DIM:(h + 1) * qw] = k_rope
        vc_ref[:, h * V_DIM:(h + 1) * V_DIM] = kv[:, h * kw + NOPE_DIM:(h + 1) * kw].astype(BF16)


def _mla_prep(proj, gq, gkv, wq1, wq2, wkv, cos_t, sin_t, *, tm=512):
    m = proj.shape[0]
    nseq = SEQ // tm
    qw = C_HEADS * (NOPE_DIM + LANE)
    est = 2 * (tm * 1024 * 2) + 2 * (CQ_PAD * qw * 2 + CQ_PAD * 512 * 2 + 128 * 1024 * 2) \
        + 2 * (2 * tm * qw * 2 + tm * 512 * 2) + 3 * tm * qw * 4
    const = lambda i: (0, 0)
    return pl.pallas_call(
        _mla_prep_kernel,
        grid=(m // tm,),
        in_specs=[
            pl.BlockSpec((tm, CQ_PAD), lambda i: (i, OFF_CQ // CQ_PAD)),
            pl.BlockSpec((tm, LANE), lambda i: (i, OFF_CKV // LANE)),
            pl.BlockSpec((tm, LANE), lambda i: (i, OFF_KR // LANE)),
            pl.BlockSpec((tm, LANE), lambda i: (i, OFF_KRS // LANE)),
            pl.BlockSpec((1, CQ_PAD), const),
            pl.BlockSpec((1, KV_LORA), const),
            pl.BlockSpec((CQ_PAD, qw), const),
            pl.BlockSpec((CQ_PAD, C_HEADS * LANE), const),
            pl.BlockSpec((KV_LORA, C_HEADS * (NOPE_DIM + V_DIM)), const),
            pl.BlockSpec((tm, LANE), lambda i: (i % nseq, 0)),
            pl.BlockSpec((tm, LANE), lambda i: (i % nseq, 0)),
        ],
        out_specs=[
            pl.BlockSpec((tm, qw), lambda i: (i, 0)),
            pl.BlockSpec((tm, qw), lambda i: (i, 0)),
            pl.BlockSpec((tm, C_HEADS * V_DIM), lambda i: (i, 0)),
        ],
        out_shape=[
            jax.ShapeDtypeStruct((m, qw), BF16),
            jax.ShapeDtypeStruct((m, qw), BF16),
            jax.ShapeDtypeStruct((m, C_HEADS * V_DIM), BF16),
        ],
        compiler_params=_params(("arbitrary",), est),
        name="mla_prep",
    )(proj, proj, proj, proj, gq, gkv, wq1, wq2, wkv, cos_t, sin_t)


def _nt_dot(a, b):
    return lax.dot_general(a, b, (((1,), (1,)), ((), ())), preferred_element_type=F32)


def _two_pass_attention(i, *, n_heads, tq, tk, dv, logits_fn, mask_fn, v_fn, store_fn,
                        s_ref, mx_ref, acc_ref):
    nc = tk // LANE
    n_full = (i * tq) // tk
    nkb = ((i + 1) * tq + tk - 1) // tk

    mx_ref[...] = jnp.full(mx_ref.shape, NEG_INF, F32)

    def pass1(kb, masked):
        ks = pl.multiple_of(kb * tk, tk)
        mask = mask_fn(ks) if masked else None
        for h in range(n_heads):
            s = logits_fn(h, ks)
            if masked:
                s = jnp.where(mask, s, NEG_INF)
            s_ref[h, :, pl.ds(ks, tk)] = s
            mx = mx_ref[h]
            for c in range(nc):
                mx = jnp.maximum(mx, s[:, c * LANE:(c + 1) * LANE])
            mx_ref[h] = mx

    def pass1_full(kb, carry):
        pass1(kb, False)
        return carry

    def pass1_masked(kb, carry):
        pass1(kb, True)
        return carry

    lax.fori_loop(0, n_full, pass1_full, 0)
    lax.fori_loop(n_full, nkb, pass1_masked, 0)

    for h in range(n_heads):
        mx_ref[h] = jnp.broadcast_to(jnp.max(mx_ref[h], axis=-1, keepdims=True), (tq, LANE))
    acc_ref[...] = jnp.zeros(acc_ref.shape, F32)
    ones = jnp.ones((tk, LANE), BF16)

    def pass2(kb, carry):
        ks = pl.multiple_of(kb * tk, tk)
        for h in range(n_heads):
            m = mx_ref[h]
            p = jnp.concatenate(
                [jnp.exp2(s_ref[h, :, pl.ds(ks + c * LANE, LANE)] - m).astype(BF16) for c in range(nc)],
                axis=1)
            v1 = jnp.concatenate([v_fn(h, ks), ones], axis=1)
            acc_ref[h] += jnp.dot(p, v1, preferred_element_type=F32)
        return carry

    lax.fori_loop(0, nkb, pass2, 0)

    for h in range(n_heads):
        acc = acc_ref[h]
        store_fn(h, acc[:, :dv] / acc[:, dv:])


def _tile_iotas(tq, tk):
    return (lax.broadcasted_iota(jnp.int32, (tq, tk), 0), lax.broadcasted_iota(jnp.int32, (tq, tk), 1))


def _fox_kernel(q_ref, k_ref, v_ref, ccol_ref, crow_ref, o_ref,
                s_ref, mx_ref, acc_ref, cq_ref, *, tq, tk):
    i = pl.program_id(1)
    scale = HEAD_DIM ** -0.5 * LOG2E
    nc = tk // LANE
    head = lambda h: slice(h * HEAD_DIM, (h + 1) * HEAD_DIM)
    for h in range(B_HEADS):
        cq_ref[h] = jnp.broadcast_to(ccol_ref[:, MISC_FL + h:MISC_FL + h + 1] * LOG2E, (tq, LANE))

    def logits_fn(h, ks):
        s = _nt_dot(q_ref[:, head(h)], k_ref[pl.ds(ks, tk), head(h)]) * scale
        return s + jnp.concatenate([cq_ref[h]] * nc, axis=1) - crow_ref[h:h + 1, pl.ds(ks, tk)] * LOG2E

    def mask_fn(ks):
        rows, cols = _tile_iotas(tq, tk)
        return cols + (ks - i * tq) <= rows

    def store_fn(h, y):
        o_ref[:, head(h)] = y.astype(o_ref.dtype)

    _two_pass_attention(i, n_heads=B_HEADS, tq=tq, tk=tk, dv=HEAD_DIM, logits_fn=logits_fn,
                        mask_fn=mask_fn, v_fn=lambda h, ks: v_ref[pl.ds(ks, tk), head(h)],
                        store_fn=store_fn, s_ref=s_ref, mx_ref=mx_ref, acc_ref=acc_ref)


def _fox_attention(proj, ccol, crow, batch, *, tq=256, tk=512):
    m = proj.shape[0]
    nq = SEQ // tq
    w = B_HEADS * HEAD_DIM
    state = B_HEADS * tq * LANE * 4
    est = (4 * tq * w * 2 + 4 * SEQ * w * 2 + 2 * tq * LANE * 4 + 2 * 8 * SEQ * 4
           + B_HEADS * tq * SEQ * 4 + 4 * state + 6 * tq * tk * 4)
    return pl.pallas_call(
        functools.partial(_fox_kernel, tq=tq, tk=tk),
        grid=(batch, nq),
        in_specs=[
            pl.BlockSpec((tq, w), lambda b, i: (b * nq + i, OFF_QB // w)),
            pl.BlockSpec((SEQ, w), lambda b, i: (b, OFF_KB // w)),
            pl.BlockSpec((SEQ, w), lambda b, i: (b, OFF_VB // w)),
            pl.BlockSpec((tq, LANE), lambda b, i: (b * nq + i, 0)),
            pl.BlockSpec((None, B_HEADS, SEQ), lambda b, i: (b, 0, 0)),
        ],
        out_specs=pl.BlockSpec((tq, w), lambda b, i: (b * nq + i, 0)),
        out_shape=jax.ShapeDtypeStruct((m, w), BF16),
        scratch_shapes=[
            pltpu.VMEM((B_HEADS, tq, SEQ), F32),
            pltpu.VMEM((B_HEADS, tq, LANE), F32),
            pltpu.VMEM((B_HEADS, tq, HEAD_DIM + LANE), F32),
            pltpu.VMEM((B_HEADS, tq, LANE), F32),
        ],
        compiler_params=_params(("arbitrary", "arbitrary"), est),
        name="fox_attention",
    )(proj, proj, proj, ccol, crow)


def _mla_kernel(q_ref, k_ref, v_ref, o_ref, s_ref, mx_ref, acc_ref, *, tq, tk):
    i = pl.program_id(1)
    scale = (NOPE_DIM + ROPE_DIM) ** -0.5 * LOG2E
    qw = NOPE_DIM + LANE
    qhead = lambda h: slice(h * qw, (h + 1) * qw)
    vhead = lambda h: slice(h * V_DIM, (h + 1) * V_DIM)

    def logits_fn(h, ks):
        return _nt_dot(q_ref[:, qhead(h)], k_ref[pl.ds(ks, tk), qhead(h)]) * scale

    def mask_fn(ks):
        rows, cols = _tile_iotas(tq, tk)
        return (cols + ks) // CHUNK <= (rows + i * tq) // CHUNK

    def store_fn(h, y):
        o_ref[:, vhead(h)] = y.astype(o_ref.dtype)

    _two_pass_attention(i, n_heads=C_HEADS, tq=tq, tk=tk, dv=V_DIM, logits_fn=logits_fn,
                        mask_fn=mask_fn, v_fn=lambda h, ks: v_ref[pl.ds(ks, tk), vhead(h)],
                        store_fn=store_fn, s_ref=s_ref, mx_ref=mx_ref, acc_ref=acc_ref)


def _mla_attention(qc, kc, vc, batch, *, tq=256, tk=512):
    m = qc.shape[0]
    nq = SEQ // tq
    qw = C_HEADS * (NOPE_DIM + LANE)
    vw = C_HEADS * V_DIM
    state = C_HEADS * tq * LANE * 4
    est = (2 * tq * qw * 2 + 2 * SEQ * qw * 2 + 2 * SEQ * vw * 2 + 2 * tq * vw * 2
           + C_HEADS * tq * SEQ * 4 + 3 * state + 6 * tq * tk * 4)
    return pl.pallas_call(
        functools.partial(_mla_kernel, tq=tq, tk=tk),
        grid=(batch, nq),
        in_specs=[
            pl.BlockSpec((tq, qw), lambda b, i: (b * nq + i, 0)),
            pl.BlockSpec((SEQ, qw), lambda b, i: (b, 0)),
            pl.BlockSpec((SEQ, vw), lambda b, i: (b, 0)),
        ],
        out_specs=pl.BlockSpec((tq, vw), lambda b, i: (b * nq + i, 0)),
        out_shape=jax.ShapeDtypeStruct((m, vw), BF16),
        scratch_shapes=[
            pltpu.VMEM((C_HEADS, tq, SEQ), F32),
            pltpu.VMEM((C_HEADS, tq, LANE), F32),
            pltpu.VMEM((C_HEADS, tq, V_DIM + LANE), F32),
        ],
        compiler_params=_params(("arbitrary", "arbitrary"), est),
        name="mla_attention",
    )(qc, kc, vc)


def _t5_tiles_kernel(bucket_ref, table_ref, o_ref):
    for d in range(3):
        bucket = bucket_ref[d]
        for h in range(A_HEADS):
            acc = jnp.zeros(bucket.shape, F32)
            for nb in range(NUM_BUCKETS):
                acc = jnp.where(bucket == nb, table_ref[nb, h] * LOG2E, acc)
            o_ref[d, h] = acc


def _t5_tiles(bucket_tiles, t5_bias, *, t=ATT_T):
    return pl.pallas_call(
        _t5_tiles_kernel,
        in_specs=[
            pl.BlockSpec(memory_space=pltpu.VMEM),
            pl.BlockSpec(memory_space=pltpu.SMEM),
        ],
        out_specs=pl.BlockSpec(memory_space=pltpu.VMEM),
        out_shape=jax.ShapeDtypeStruct((3, A_HEADS, t, t), F32),
        name="t5_tiles",
    )(bucket_tiles, t5_bias)


def _dsa_kernel(qi_ref, klo_ref, khi_ref, misc_ref, qa_ref, ka_ref, va_ref, bias_ref, o_ref,
                sc_ref, lg_ref, acc_ref, *, t):
    i = pl.program_id(1)
    groups = t // 8
    key = lax.broadcasted_iota(jnp.int32, (t, t), 0)
    qry = lax.broadcasted_iota(jnp.int32, (t, t), 1)
    admissible = (key // CHUNK) <= (qry // CHUNK)

    def fold(x, op):
        return op(x.reshape(groups, 8, t), axis=0)

    def over_keys(x8, op):
        return jnp.broadcast_to(op(x8, axis=0, keepdims=True), (8, t))

    w_t = misc_ref[...].T[MISC_WI:MISC_WI + IDX_HEADS, :] * (IDX_HEADS ** -0.5 * IDX_DIM ** -0.5)

    def block_scores(kb):
        ks = pl.multiple_of(kb * t, t)
        klo = klo_ref[pl.ds(ks, t), :]
        khi = khi_ref[pl.ds(ks, t), :]
        acc = jnp.zeros((t, t), F32)
        for pair in range(IDX_HEADS // 2):
            qp = qi_ref[:, pair * LANE:(pair + 1) * LANE]
            for sub, kk in enumerate((klo, khi)):
                h = 2 * pair + sub
                acc = acc + jnp.maximum(_nt_dot(kk, qp), 0.0) * w_t[h:h + 1, :]
        return ks, acc

    def score_body(kb, carry):
        mn, mx = carry
        ks, acc = block_scores(kb)
        sc_ref[pl.ds(ks, t), :] = acc
        return jnp.minimum(mn, fold(acc, jnp.min)), jnp.maximum(mx, fold(acc, jnp.max))

    mn, mx = lax.fori_loop(0, i, score_body,
                           (jnp.full((8, t), -SELECT_MIN, F32), jnp.full((8, t), SELECT_MIN, F32)))
    ks, acc = block_scores(i)
    sc_ref[pl.ds(ks, t), :] = jnp.where(admissible, acc, NEG_INF)
    mn = jnp.minimum(mn, fold(jnp.where(admissible, acc, -SELECT_MIN), jnp.min))
    mx = jnp.maximum(mx, fold(jnp.where(admissible, acc, SELECT_MIN), jnp.max))

    def count_ge(thr):
        def body(kb, cnt):
            ks = pl.multiple_of(kb * t, t)
            hit = jnp.where(sc_ref[pl.ds(ks, t), :].reshape(groups, 8, t) >= thr[None], 1.0, 0.0)
            return cnt + jnp.sum(hit, axis=0)
        return over_keys(lax.fori_loop(0, i + 1, body, jnp.zeros((8, t), F32)), jnp.sum)

    k_sel = float(TOPK_MAX)
    search = i * t >= TOPK_MAX
    lo0 = jnp.where(search, over_keys(mn, jnp.min), SELECT_MIN)
    hi0 = jnp.where(search, over_keys(mx, jnp.max), SELECT_MIN)

    def midpoint(lo, hi):
        return 0.5 * lo + 0.5 * hi

    def any_active(lo, hi):
        mid = midpoint(lo, hi)
        return jnp.max(jnp.where((mid > lo) & (mid < hi), 1.0, 0.0)) > 0.5

    def search_cond(state):
        return jnp.logical_and(state[0] < BISECT_MAX_ITERS, state[1])

    def search_body(state):
        it, _, lo, hi = state
        for _ in range(SEARCH_STEPS_PER_CHECK):
            mid = midpoint(lo, hi)
            cnt = count_ge(mid)
            ge = cnt >= k_sel
            lo = jnp.where(ge, mid, lo)
            hi = jnp.where(cnt > k_sel, hi, mid)
        return it + SEARCH_STEPS_PER_CHECK, any_active(lo, hi), lo, hi

    _, _, lo, hi = lax.while_loop(search_cond, search_body,
                                  (jnp.int32(0), any_active(lo0, hi0), lo0, hi0))
    thr = jnp.where(count_ge(hi) >= k_sel, hi, lo)

    scale = HEAD_DIM ** -0.5 * LOG2E
    head = lambda h: slice(h * HEAD_DIM, (h + 1) * HEAD_DIM)
    acc_ref[...] = jnp.zeros(acc_ref.shape, F32)

    def logits_pass(kb, mx):
        ks = pl.multiple_of(kb * t, t)
        dist = jnp.minimum(i - kb, 2)
        k_blk = ka_ref[pl.ds(ks, t), :]
        sel = sc_ref[pl.ds(ks, t), :] >= thr[0:1, :]
        out = []
        for h in range(A_HEADS):
            s = _nt_dot(k_blk, qa_ref[:, head(h)]) * scale + bias_ref[dist, h]
            s = jnp.where(sel, s, NEG_INF)
            lg_ref[h, pl.ds(ks, t), :] = s
            out.append(jnp.maximum(mx[h], fold(s, jnp.max)))
        return tuple(out)

    mx = lax.fori_loop(0, i + 1, logits_pass,
                       tuple(jnp.full((8, t), NEG_INF, F32) for _ in range(A_HEADS)))
    m = [over_keys(mx[h], jnp.max)[0:1, :] for h in range(A_HEADS)]

    ones_rows = jnp.ones((DSA_ONES_ROWS, t), BF16)

    def value_pass(kb, carry):
        ks = pl.multiple_of(kb * t, t)
        v_t = va_ref[pl.ds(ks, t), :].astype(F32).T.astype(BF16)
        v1_t = jnp.concatenate([v_t, ones_rows], axis=0)
        for h in range(A_HEADS):
            p = jnp.exp2(lg_ref[h, pl.ds(ks, t), :] - m[h])
            acc_ref[h] += jnp.dot(v1_t, p.astype(BF16), preferred_element_type=F32)
        return carry

    lax.fori_loop(0, i + 1, value_pass, 0)
    for h in range(A_HEADS):
        acc = acc_ref[h]
        o_ref[:, head(h)] = (acc[:HEAD_DIM] / acc[HEAD_DIM:HEAD_DIM + 1]).T.astype(o_ref.dtype)


def _dsa_attention(proj, misc, bias_tiles, batch, *, t=ATT_T):
    m = proj.shape[0]
    nq = SEQ // t
    qiw = IDX_HEADS * IDX_DIM
    qaw = A_HEADS * HEAD_DIM
    est = (2 * t * qiw * 2 + 2 * 4 * SEQ * LANE * 2 + 2 * t * LANE * 4 + 2 * t * qaw * 2
           + 2 * 3 * A_HEADS * t * t * 4 + 2 * t * qaw * 2
           + (1 + A_HEADS) * t * SEQ * 4 + A_HEADS * HEAD_DIM * t * 4 + 12 * t * t * 4)
    kblock = lambda off: pl.BlockSpec((SEQ, LANE), lambda b, i: (b, off // LANE))
    return pl.pallas_call(
        functools.partial(_dsa_kernel, t=t),
        grid=(batch, nq),
        in_specs=[
            pl.BlockSpec((t, qiw), lambda b, i: (b * nq + i, OFF_QI // qiw)),
            kblock(OFF_KI_LO),
            kblock(OFF_KI_HI),
            pl.BlockSpec((t, LANE), lambda b, i: (b * nq + i, 0)),
            pl.BlockSpec((t, qaw), lambda b, i: (b * nq + i, OFF_QA // qaw)),
            kblock(OFF_KA),
            kblock(OFF_VA),
            pl.BlockSpec((3, A_HEADS, t, t), lambda b, i: (0, 0, 0, 0)),
        ],
        out_specs=pl.BlockSpec((t, qaw), lambda b, i: (b * nq + i, 0)),
        out_shape=jax.ShapeDtypeStruct((m, qaw), BF16),
        scratch_shapes=[
            pltpu.VMEM((SEQ, t), F32),
            pltpu.VMEM((A_HEADS, SEQ, t), F32),
            pltpu.VMEM((A_HEADS, HEAD_DIM + DSA_ONES_ROWS, t), F32),
        ],
        compiler_params=_params(("arbitrary", "arbitrary"), est),
        name="dsa_attention",
    )(proj, proj, proj, misc, proj, proj, proj, bias_tiles)


def _merge_kernel(x_ref, ya_ref, yb_ref, yc_ref, ga_ref, gb_ref, gc_ref,
                  wa_ref, wb_ref, wc_ref, wo_ref, o_ref):
    def branch(y_ref, w_ref, g_ref):
        y = jnp.dot(y_ref[...], w_ref[...], preferred_element_type=F32)
        return jax.nn.sigmoid(g_ref[...].astype(F32)) * y

    merged = branch(ya_ref, wa_ref, ga_ref) + branch(yb_ref, wb_ref, gb_ref) + branch(yc_ref, wc_ref, gc_ref)
    o_ref[...] = x_ref[...] + jnp.dot(merged.astype(BF16), wo_ref[...], preferred_element_type=F32)


def _merge(x2, ya, yb, yc, proj, wa, wb, wc, wo, *, tm=256):
    m = x2.shape[0]
    wbytes = (wa.size + wb.size + wc.size + wo.size) * 2
    est = 2 * wbytes + 4 * tm * D_MODEL * 4 + 2 * tm * 2048 * 2 + 6 * tm * D_MODEL * 2 + 4 * tm * D_MODEL * 4
    const = lambda i: (0, 0)
    gate = lambda k: pl.BlockSpec((tm, D_MODEL), lambda i: (i, OFF_GL // D_MODEL + k))
    return pl.pallas_call(
        _merge_kernel,
        grid=(m // tm,),
        in_specs=[
            pl.BlockSpec((tm, D_MODEL), lambda i: (i, 0)),
            pl.BlockSpec((tm, ya.shape[1]), lambda i: (i, 0)),
            pl.BlockSpec((tm, yb.shape[1]), lambda i: (i, 0)),
            pl.BlockSpec((tm, yc.shape[1]), lambda i: (i, 0)),
            gate(0), gate(1), gate(2),
            pl.BlockSpec(wa.shape, const),
            pl.BlockSpec(wb.shape, const),
            pl.BlockSpec(wc.shape, const),
            pl.BlockSpec(wo.shape, const),
        ],
        out_specs=pl.BlockSpec((tm, D_MODEL), lambda i: (i, 0)),
        out_shape=jax.ShapeDtypeStruct((m, D_MODEL), F32),
        compiler_params=_params(("arbitrary",), est),
        name="merge",
    )(x2, ya, yb, yc, proj, proj, proj, wa, wb, wc, wo)


def _ffn_kernel(x_ref, g_ref, wu_ref, wv_ref, cw_ref, wd_ref, fg_ref, o_ref,
                h_ref, ubuf_ref, halo_ref, *, tm, tf, final_norm):
    i = pl.program_id(0)
    j = pl.program_id(1)
    nj = pl.num_programs(1)

    @pl.when(j == 0)
    def _():
        _rmsnorm_rows(x_ref, g_ref, h_ref, tm)
        o_ref[...] = x_ref[...]

        @pl.when((i % (SEQ // tm)) == 0)
        def _():
            halo_ref[...] = jnp.zeros(halo_ref.shape, F32)

    h = h_ref[...]
    u = jnp.dot(h, wu_ref[...], preferred_element_type=F32)
    v = jnp.dot(h, wv_ref[...], preferred_element_type=F32)

    ubuf_ref[0:8, :] = halo_ref[j]
    ubuf_ref[8:, :] = u
    halo_ref[j] = u[tm - 8:, :]
    conv = (cw_ref[0:1, :] * ubuf_ref[6:6 + tm, :] + cw_ref[1:2, :] * ubuf_ref[7:7 + tm, :]
            + cw_ref[2:3, :] * u + cw_ref[3:4, :])
    act = (jax.nn.gelu(conv) * v).astype(BF16)
    o_ref[...] += jnp.dot(act, wd_ref[...], preferred_element_type=F32)

    if final_norm:
        @pl.when(j == nj - 1)
        def _():
            _rmsnorm_rows(o_ref, fg_ref, o_ref, tm)


def _ffn(x2, g, w_up, cw, w_down, final_g, *, final_norm, tm=1024, tf=512):
    m = x2.shape[0]
    nj = D_FF // tf
    est = (3 * tm * D_MODEL * 4 + tm * D_MODEL * 2 + 2 * 2 * D_MODEL * tf * 2 + 2 * tf * D_MODEL * 2
           + (tm + 8) * tf * 4 + nj * 8 * tf * 4 + 4 * tm * tf * 4)
    return pl.pallas_call(
        functools.partial(_ffn_kernel, tm=tm, tf=tf, final_norm=final_norm),
        grid=(m // tm, nj),
        in_specs=[
            pl.BlockSpec((tm, D_MODEL), lambda i, j: (i, 0), pipeline_mode=pl.Buffered(1)),
            pl.BlockSpec((1, D_MODEL), lambda i, j: (0, 0)),
            pl.BlockSpec((D_MODEL, tf), lambda i, j: (0, j)),
            pl.BlockSpec((D_MODEL, tf), lambda i, j: (0, nj + j)),
            pl.BlockSpec((8, tf), lambda i, j: (0, j)),
            pl.BlockSpec((tf, D_MODEL), lambda i, j: (j, 0)),
            pl.BlockSpec((1, D_MODEL), lambda i, j: (0, 0)),
        ],
        out_specs=pl.BlockSpec((tm, D_MODEL), lambda i, j: (i, 0)),
        out_shape=jax.ShapeDtypeStruct((m, D_MODEL), F32),
        scratch_shapes=[
            pltpu.VMEM((tm, D_MODEL), BF16),
            pltpu.VMEM((tm + 8, tf), F32),
            pltpu.VMEM((nj, 8, tf), F32),
        ],
        compiler_params=_params(("arbitrary", "arbitrary"), est),
        name="conv_ffn",
    )(x2, g, w_up, w_up, cw, w_down, final_g)


def _t5_bucket(rel):
    nb = NUM_BUCKETS // 2
    max_exact = nb // 2
    base = jnp.where(rel > 0, nb, 0)
    n = jnp.abs(rel)
    nf = jnp.maximum(n, 1).astype(F32)
    large = max_exact + (jnp.log(nf / max_exact) / math.log(MAX_DISTANCE / max_exact)
                         * (nb - max_exact)).astype(jnp.int32)
    large = jnp.minimum(large, nb - 1)
    return base + jnp.where(n < max_exact, n, large)


def _bucket_tiles(t):
    key = jnp.arange(t, dtype=jnp.int32)[:, None]
    qry = jnp.arange(t, dtype=jnp.int32)[None, :]
    return jnp.stack([_t5_bucket(key - qry - d * t) for d in range(3)])


def _rope_tables():
    half = ROPE_DIM // 2
    inv = ROPE_THETA ** (-jnp.arange(half, dtype=F32) / half)
    ang = jnp.arange(SEQ, dtype=jnp.int32).astype(F32)[:, None] * inv[None, :]
    cos, sin = jnp.cos(ang), jnp.sin(ang)
    pad = jnp.zeros((SEQ, LANE - ROPE_DIM), F32)
    return jnp.concatenate([cos, cos, pad], axis=1), jnp.concatenate([-sin, sin, pad], axis=1)


def _swap_halves(w):
    half = w.shape[-1] // 2
    return jnp.concatenate([w[..., half:], w[..., :half]], axis=-1)


def _pack_w_in(w):
    off = np.cumsum((0,) + IN_SIZES)
    qa, ka, va, qi, ki, wi, qb, kb, vb, fl, cq, ckv, kr, gl = [
        w[:, off[k]:off[k + 1]] for k in range(len(IN_SIZES))]
    z = lambda n: jnp.zeros((w.shape[0], n), w.dtype)
    cols = [qi, qb, kb, vb, gl, qa, cq, z(CQ_PAD - Q_LORA), ka, va, ki, z(64), ckv,
            kr, z(64), _swap_halves(kr), z(64), wi, fl, z(LANE - IDX_HEADS - B_HEADS), z(64), ki]
    packed = jnp.concatenate(cols, axis=1).astype(BF16)
    assert packed.shape[1] == N_PACK
    return packed


def _pack_w_uq(w):
    w = jnp.pad(w, ((0, CQ_PAD - Q_LORA), (0, 0))).reshape(CQ_PAD, C_HEADS, NOPE_DIM + ROPE_DIM)
    z = jnp.zeros((CQ_PAD, C_HEADS, LANE - ROPE_DIM), w.dtype)
    rope = w[..., NOPE_DIM:]
    wq1 = jnp.concatenate([w, z], axis=-1).reshape(CQ_PAD, -1)
    wq2 = jnp.concatenate([_swap_halves(rope), z], axis=-1).reshape(CQ_PAD, -1)
    return wq1.astype(BF16), wq2.astype(BF16)


def kernel(x, norm_mix_g, w_in, b_forget, g_cq, g_ckv, w_uq, w_ukv, w_branch_a, w_branch_b, w_branch_c,
           w_o, norm_ffn_g, w_up, conv_w, conv_b, w_down, t5_bias, final_g):
    batch, seq, d = x.shape
    assert (seq, d) == (SEQ, D_MODEL)
    x2 = x.reshape(batch * seq, d)

    bias_tiles = _t5_tiles(_bucket_tiles(ATT_T), t5_bias)
    cos_t, sin_t = _rope_tables()
    final_row = final_g.reshape(1, D_MODEL)

    for l in range(DEPTH):
        proj, misc = _inproj(x2, norm_mix_g[l].reshape(1, D_MODEL), _pack_w_in(w_in[l]))

        fbias = jnp.zeros((1, LANE), F32).at[0, MISC_FL:MISC_FL + B_HEADS].set(b_forget[l])
        ccol, crow = _forget_cumsum(misc, fbias, batch)

        wq1, wq2 = _pack_w_uq(w_uq[l])
        gq = jnp.pad(g_cq[l], (0, CQ_PAD - Q_LORA)).reshape(1, CQ_PAD)
        qc, kc, vc = _mla_prep(proj, gq, g_ckv[l].reshape(1, KV_LORA), wq1, wq2,
                               w_ukv[l].astype(BF16), cos_t, sin_t)

        ya = _dsa_attention(proj, misc, bias_tiles, batch)
        yb = _fox_attention(proj, ccol, crow, batch)
        yc = _mla_attention(qc, kc, vc, batch)

        x2 = _merge(x2, ya, yb, yc, proj, w_branch_a[l].astype(BF16), w_branch_b[l].astype(BF16),
                    w_branch_c[l].astype(BF16), w_o[l].astype(BF16))

        cw = jnp.concatenate([conv_w[l], conv_b[l][None, :], jnp.zeros((4, D_FF), F32)], axis=0)
        x2 = _ffn(x2, norm_ffn_g[l].reshape(1, D_MODEL), w_up[l].astype(BF16), cw,
                  w_down[l].astype(BF16), final_row, final_norm=(l == DEPTH - 1))

    return x2.reshape(batch, seq, d)
```

```python
import functools
import math

import jax
import jax.numpy as jnp
import numpy as np
from jax import lax
from jax.experimental import pallas as pl
from jax.experimental.pallas import tpu as pltpu

F32 = jnp.float32
BF16 = jnp.bfloat16

D_MODEL = 2048
SEQ = 2048
DEPTH = 2
CHUNK = 64
HEAD_DIM = 128
EPS = 1e-6
NEG_INF = -1e30

A_HEADS = 4
IDX_HEADS = 16
IDX_DIM = 64
TOPK_MAX = 256
NUM_BUCKETS = 32
MAX_DISTANCE = 128
B_HEADS = 8
C_HEADS = 4
Q_LORA = 448
KV_LORA = 128
NOPE_DIM = 128
ROPE_DIM = 64
V_DIM = 128
ROPE_THETA = 10000.0
D_FF = 5632

IN_SIZES = (
    A_HEADS * HEAD_DIM, HEAD_DIM, HEAD_DIM,
    IDX_HEADS * IDX_DIM, IDX_DIM, IDX_HEADS,
    B_HEADS * HEAD_DIM, B_HEADS * HEAD_DIM, B_HEADS * HEAD_DIM, B_HEADS,
    Q_LORA, KV_LORA, ROPE_DIM,
    3 * D_MODEL,
)

LOG2E = math.log2(math.e)
LANE = 128
V7X_VMEM_BYTES = 64 * 1024 * 1024

OFF_QI = 0
OFF_QB = 1024
OFF_KB = 2048
OFF_VB = 3072
OFF_GL = 4096
OFF_QA = 10240
OFF_CQ = 10752
OFF_KA = 11264
OFF_VA = 11392
OFF_KI_LO = 11520
OFF_CKV = 11648
OFF_KR = 11776
OFF_KRS = 11904
OFF_MISC = 12032
OFF_KI_HI = 12160
N_PACK = 12288
CQ_PAD = 512
MISC_WI = 0
MISC_FL = IDX_HEADS

ATT_T = 256
SELECT_MIN = -1e29
BISECT_MAX_ITERS = 512
SEARCH_STEPS_PER_CHECK = 2
DSA_ONES_ROWS = 16


def _vmem_limit(estimate_bytes):
    return int(min(max(estimate_bytes * 5 // 4 + (4 << 20), 32 << 20), V7X_VMEM_BYTES - (6 << 20)))


def _params(semantics, vmem_estimate):
    return pltpu.CompilerParams(dimension_semantics=semantics,
                                vmem_limit_bytes=_vmem_limit(vmem_estimate))


def _rmsnorm_rows(x_ref, g_ref, out_ref, rows):
    def body(c, carry):
        r = pl.multiple_of(c * 128, 128)
        x = x_ref[pl.ds(r, 128), :]
        ms = jnp.mean(x * x, axis=-1, keepdims=True)
        out_ref[pl.ds(r, 128), :] = (x * lax.rsqrt(ms + EPS) * g_ref[...]).astype(out_ref.dtype)
        return carry
    lax.fori_loop(0, rows // 128, body, 0)


def _inproj_kernel(x_ref, g_ref, w_ref, o_ref, misc_ref, h_ref, *, tm, tn):
    j = pl.program_id(1)

    @pl.when(j == 0)
    def _():
        _rmsnorm_rows(x_ref, g_ref, h_ref, tm)

    acc = jnp.dot(h_ref[...], w_ref[...], preferred_element_type=F32)
    o_ref[...] = acc.astype(o_ref.dtype)

    @pl.when(j == OFF_MISC // tn)
    def _():
        lo = OFF_MISC % tn
        misc_ref[...] = acc[:, lo:lo + LANE]


def _inproj(x2, g, w_pack, *, tm=1024, tn=1024):
    m = x2.shape[0]
    est = 2 * tm * D_MODEL * 4 + tm * D_MODEL * 2 + 2 * D_MODEL * tn * 2 + 2 * tm * tn * 2 + tm * tn * 4
    return pl.pallas_call(
        functools.partial(_inproj_kernel, tm=tm, tn=tn),
        grid=(m // tm, N_PACK // tn),
        in_specs=[
            pl.BlockSpec((tm, D_MODEL), lambda i, j: (i, 0)),
            pl.BlockSpec((1, D_MODEL), lambda i, j: (0, 0)),
            pl.BlockSpec((D_MODEL, tn), lambda i, j: (0, j)),
        ],
        out_specs=[
            pl.BlockSpec((tm, tn), lambda i, j: (i, j)),
            pl.BlockSpec((tm, LANE), lambda i, j: (i, 0)),
        ],
        out_shape=[
            jax.ShapeDtypeStruct((m, N_PACK), BF16),
            jax.ShapeDtypeStruct((m, LANE), F32),
        ],
        scratch_shapes=[pltpu.VMEM((tm, D_MODEL), BF16)],
        compiler_params=_params(("arbitrary", "arbitrary"), est),
        name="inproj",
    )(x2, g, w_pack)


def _forget_cumsum_kernel(misc_ref, bias_ref, ccol_ref, crow_ref, *, blk):
    rows = lax.broadcasted_iota(jnp.int32, (blk, blk), 0)
    cols = lax.broadcasted_iota(jnp.int32, (blk, blk), 1)
    tri = jnp.where(rows >= cols, 1.0, 0.0).astype(BF16)
    carry = jnp.zeros((1, LANE), F32)
    for c in range(SEQ // blk):
        z = misc_ref[c * blk:(c + 1) * blk, :] + bias_ref[...]
        lf = jnp.minimum(z, 0.0) - jnp.log1p(jnp.exp(-jnp.abs(z)))
        p0 = lf.astype(BF16)
        r1 = lf - p0.astype(F32)
        p1 = r1.astype(BF16)
        p2 = (r1 - p1.astype(F32)).astype(BF16)
        cs = (jnp.dot(tri, p0, preferred_element_type=F32)
              + jnp.dot(tri, p1, preferred_element_type=F32)
              + jnp.dot(tri, p2, preferred_element_type=F32)) + carry
        ccol_ref[c * blk:(c + 1) * blk, :] = cs
        crow_ref[:, c * blk:(c + 1) * blk] = cs.T[MISC_FL:MISC_FL + B_HEADS, :]
        carry = cs[blk - 1:blk, :]


def _forget_cumsum(misc, bias_row, batch, *, blk=256):
    est = 4 * SEQ * LANE * 4 + 2 * 8 * SEQ * 4
    return pl.pallas_call(
        functools.partial(_forget_cumsum_kernel, blk=blk),
        grid=(batch,),
        in_specs=[
            pl.BlockSpec((SEQ, LANE), lambda b: (b, 0)),
            pl.BlockSpec((1, LANE), lambda b: (0, 0)),
        ],
        out_specs=[
            pl.BlockSpec((SEQ, LANE), lambda b: (b, 0)),
            pl.BlockSpec((None, B_HEADS, SEQ), lambda b: (b, 0, 0)),
        ],
        out_shape=[
            jax.ShapeDtypeStruct((batch * SEQ, LANE), F32),
            jax.ShapeDtypeStruct((batch, B_HEADS, SEQ), F32),
        ],
        compiler_params=_params(("arbitrary",), est),
        name="forget_cumsum",
    )(misc, bias_row)


def _mla_prep_kernel(cq_ref, ckv_ref, kr_ref, krs_ref, gq_ref, gkv_ref, wq1_ref, wq2_ref, wkv_ref,
                     cos_ref, sin_ref, qc_ref, kc_ref, vc_ref):
    cq = cq_ref[...].astype(F32)
    ms = jnp.sum(cq * cq, axis=-1, keepdims=True) * (1.0 / Q_LORA)
    cqn = (cq * lax.rsqrt(ms + EPS) * gq_ref[...]).astype(BF16)
    ckv = ckv_ref[...].astype(F32)
    ms2 = jnp.mean(ckv * ckv, axis=-1, keepdims=True)
    ckvn = (ckv * lax.rsqrt(ms2 + EPS) * gkv_ref[...]).astype(BF16)

    q1 = jnp.dot(cqn, wq1_ref[...], preferred_element_type=F32)
    q2 = jnp.dot(cqn, wq2_ref[...], preferred_element_type=F32)
    kv = jnp.dot(ckvn, wkv_ref[...], preferred_element_type=F32)
    cos = cos_ref[...]
    sin = sin_ref[...]
    k_rope = (kr_ref[...].astype(F32) * cos + krs_ref[...].astype(F32) * sin).astype(BF16)
    for h in range(C_HEADS):
        qw = NOPE_DIM + LANE
        qc_ref[:, h * qw:h * qw + NOPE_DIM] = q1[:, h * qw:h * qw + NOPE_DIM].astype(BF16)
        q_rope = q1[:, h * qw + NOPE_DIM:(h + 1) * qw] * cos + q2[:, h * LANE:(h + 1) * LANE] * sin
        qc_ref[:, h * qw + NOPE_DIM:(h + 1) * qw] = q_rope.astype(BF16)
        kw = NOPE_DIM + V_DIM
        kc_ref[:, h * qw:h * qw + NOPE_DIM] = kv[:, h * kw:h * kw + NOPE_DIM].astype(BF16)
        kc_ref[:, h * qw + NOPE_DIM:(h + 1) * qw] = k_rope
        vc_ref[:, h * V_DIM:(h + 1) * V_DIM] = kv[:, h * kw + NOPE_DIM:(h + 1) * kw].astype(BF16)


def _mla_prep(proj, gq, gkv, wq1, wq2, wkv, cos_t, sin_t, *, tm=512):
    m = proj.shape[0]
    nseq = SEQ // tm
    qw = C_HEADS * (NOPE_DIM + LANE)
    est = 2 * (tm * 1024 * 2) + 2 * (CQ_PAD * qw * 2 + CQ_PAD * 512 * 2 + 128 * 1024 * 2) \
        + 2 * (2 * tm * qw * 2 + tm * 512 * 2) + 3 * tm * qw * 4
    const = lambda i: (0, 0)
    return pl.pallas_call(
        _mla_prep_kernel,
        grid=(m // tm,),
        in_specs=[
            pl.BlockSpec((tm, CQ_PAD), lambda i: (i, OFF_CQ // CQ_PAD)),
            pl.BlockSpec((tm, LANE), lambda i: (i, OFF_CKV // LANE)),
            pl.BlockSpec((tm, LANE), lambda i: (i, OFF_KR // LANE)),
            pl.BlockSpec((tm, LANE), lambda i: (i, OFF_KRS // LANE)),
            pl.BlockSpec((1, CQ_PAD), const),
            pl.BlockSpec((1, KV_LORA), const),
            pl.BlockSpec((CQ_PAD, qw), const),
            pl.BlockSpec((CQ_PAD, C_HEADS * LANE), const),
            pl.BlockSpec((KV_LORA, C_HEADS * (NOPE_DIM + V_DIM)), const),
            pl.BlockSpec((tm, LANE), lambda i: (i % nseq, 0)),
            pl.BlockSpec((tm, LANE), lambda i: (i % nseq, 0)),
        ],
        out_specs=[
            pl.BlockSpec((tm, qw), lambda i: (i, 0)),
            pl.BlockSpec((tm, qw), lambda i: (i, 0)),
            pl.BlockSpec((tm, C_HEADS * V_DIM), lambda i: (i, 0)),
        ],
        out_shape=[
            jax.ShapeDtypeStruct((m, qw), BF16),
            jax.ShapeDtypeStruct((m, qw), BF16),
            jax.ShapeDtypeStruct((m, C_HEADS * V_DIM), BF16),
        ],
        compiler_params=_params(("arbitrary",), est),
        name="mla_prep",
    )(proj, proj, proj, proj, gq, gkv, wq1, wq2, wkv, cos_t, sin_t)


def _nt_dot(a, b):
    return lax.dot_general(a, b, (((1,), (1,)), ((), ())), preferred_element_type=F32)


def _two_pass_attention(i, *, n_heads, tq, tk, dv, logits_fn, mask_fn, v_fn, store_fn,
                        s_ref, mx_ref, acc_ref):
    nc = tk // LANE
    n_full = (i * tq) // tk
    nkb = ((i + 1) * tq + tk - 1) // tk

    mx_ref[...] = jnp.full(mx_ref.shape, NEG_INF, F32)

    def pass1(kb, masked):
        ks = pl.multiple_of(kb * tk, tk)
        mask = mask_fn(ks) if masked else None
        for h in range(n_heads):
            s = logits_fn(h, ks)
            if masked:
                s = jnp.where(mask, s, NEG_INF)
            s_ref[h, :, pl.ds(ks, tk)] = s
            mx = mx_ref[h]
            for c in range(nc):
                mx = jnp.maximum(mx, s[:, c * LANE:(c + 1) * LANE])
            mx_ref[h] = mx

    def pass1_full(kb, carry):
        pass1(kb, False)
        return carry

    def pass1_masked(kb, carry):
        pass1(kb, True)
        return carry

    lax.fori_loop(0, n_full, pass1_full, 0)
    lax.fori_loop(n_full, nkb, pass1_masked, 0)

    for h in range(n_heads):
        mx_ref[h] = jnp.broadcast_to(jnp.max(mx_ref[h], axis=-1, keepdims=True), (tq, LANE))
    acc_ref[...] = jnp.zeros(acc_ref.shape, F32)
    ones = jnp.ones((tk, LANE), BF16)

    def pass2(kb, carry):
        ks = pl.multiple_of(kb * tk, tk)
        for h in range(n_heads):
            m = mx_ref[h]
            p = jnp.concatenate(
                [jnp.exp2(s_ref[h, :, pl.ds(ks + c * LANE, LANE)] - m).astype(BF16) for c in range(nc)],
                axis=1)
            v1 = jnp.concatenate([v_fn(h, ks), ones], axis=1)
            acc_ref[h] += jnp.dot(p, v1, preferred_element_type=F32)
        return carry

    lax.fori_loop(0, nkb, pass2, 0)

    for h in range(n_heads):
        acc = acc_ref[h]
        store_fn(h, acc[:, :dv] / acc[:, dv:])


def _tile_iotas(tq, tk):
    return (lax.broadcasted_iota(jnp.int32, (tq, tk), 0), lax.broadcasted_iota(jnp.int32, (tq, tk), 1))


def _fox_kernel(q_ref, k_ref, v_ref, ccol_ref, crow_ref, o_ref,
                s_ref, mx_ref, acc_ref, cq_ref, *, tq, tk):
    i = pl.program_id(1)
    scale = HEAD_DIM ** -0.5 * LOG2E
    nc = tk // LANE
    head = lambda h: slice(h * HEAD_DIM, (h + 1) * HEAD_DIM)
    for h in range(B_HEADS):
        cq_ref[h] = jnp.broadcast_to(ccol_ref[:, MISC_FL + h:MISC_FL + h + 1] * LOG2E, (tq, LANE))

    def logits_fn(h, ks):
        s = _nt_dot(q_ref[:, head(h)], k_ref[pl.ds(ks, tk), head(h)]) * scale
        return s + jnp.concatenate([cq_ref[h]] * nc, axis=1) - crow_ref[h:h + 1, pl.ds(ks, tk)] * LOG2E

    def mask_fn(ks):
        rows, cols = _tile_iotas(tq, tk)
        return cols + (ks - i * tq) <= rows

    def store_fn(h, y):
        o_ref[:, head(h)] = y.astype(o_ref.dtype)

    _two_pass_attention(i, n_heads=B_HEADS, tq=tq, tk=tk, dv=HEAD_DIM, logits_fn=logits_fn,
                        mask_fn=mask_fn, v_fn=lambda h, ks: v_ref[pl.ds(ks, tk), head(h)],
                        store_fn=store_fn, s_ref=s_ref, mx_ref=mx_ref, acc_ref=acc_ref)


def _fox_attention(proj, ccol, crow, batch, *, tq=256, tk=512):
    m = proj.shape[0]
    nq = SEQ // tq
    w = B_HEADS * HEAD_DIM
    state = B_HEADS * tq * LANE * 4
    est = (4 * tq * w * 2 + 4 * SEQ * w * 2 + 2 * tq * LANE * 4 + 2 * 8 * SEQ * 4
           + B_HEADS * tq * SEQ * 4 + 4 * state + 6 * tq * tk * 4)
    return pl.pallas_call(
        functools.partial(_fox_kernel, tq=tq, tk=tk),
        grid=(batch, nq),
        in_specs=[
            pl.BlockSpec((tq, w), lambda b, i: (b * nq + i, OFF_QB // w)),
            pl.BlockSpec((SEQ, w), lambda b, i: (b, OFF_KB // w)),
            pl.BlockSpec((SEQ, w), lambda b, i: (b, OFF_VB // w)),
            pl.BlockSpec((tq, LANE), lambda b, i: (b * nq + i, 0)),
            pl.BlockSpec((None, B_HEADS, SEQ), lambda b, i: (b, 0, 0)),
        ],
        out_specs=pl.BlockSpec((tq, w), lambda b, i: (b * nq + i, 0)),
        out_shape=jax.ShapeDtypeStruct((m, w), BF16),
        scratch_shapes=[
            pltpu.VMEM((B_HEADS, tq, SEQ), F32),
            pltpu.VMEM((B_HEADS, tq, LANE), F32),
            pltpu.VMEM((B_HEADS, tq, HEAD_DIM + LANE), F32),
            pltpu.VMEM((B_HEADS, tq, LANE), F32),
        ],
        compiler_params=_params(("arbitrary", "arbitrary"), est),
        name="fox_attention",
    )(proj, proj, proj, ccol, crow)


def _mla_kernel(q_ref, k_ref, v_ref, o_ref, s_ref, mx_ref, acc_ref, *, tq, tk):
    i = pl.program_id(1)
    scale = (NOPE_DIM + ROPE_DIM) ** -0.5 * LOG2E
    qw = NOPE_DIM + LANE
    qhead = lambda h: slice(h * qw, (h + 1) * qw)
    vhead = lambda h: slice(h * V_DIM, (h + 1) * V_DIM)

    def logits_fn(h, ks):
        return _nt_dot(q_ref[:, qhead(h)], k_ref[pl.ds(ks, tk), qhead(h)]) * scale

    def mask_fn(ks):
        rows, cols = _tile_iotas(tq, tk)
        return (cols + ks) // CHUNK <= (rows + i * tq) // CHUNK

    def store_fn(h, y):
        o_ref[:, vhead(h)] = y.astype(o_ref.dtype)

    _two_pass_attention(i, n_heads=C_HEADS, tq=tq, tk=tk, dv=V_DIM, logits_fn=logits_fn,
                        mask_fn=mask_fn, v_fn=lambda h, ks: v_ref[pl.ds(ks, tk), vhead(h)],
                        store_fn=store_fn, s_ref=s_ref, mx_ref=mx_ref, acc_ref=acc_ref)


def _mla_attention(qc, kc, vc, batch, *, tq=256, tk=512):
    m = qc.shape[0]
    nq = SEQ // tq
    qw = C_HEADS * (NOPE_DIM + LANE)
    vw = C_HEADS * V_DIM
    state = C_HEADS * tq * LANE * 4
    est = (2 * tq * qw * 2 + 2 * SEQ * qw * 2 + 2 * SEQ * vw * 2 + 2 * tq * vw * 2
           + C_HEADS * tq * SEQ * 4 + 3 * state + 6 * tq * tk * 4)
    return pl.pallas_call(
        functools.partial(_mla_kernel, tq=tq, tk=tk),
        grid=(batch, nq),
        in_specs=[
            pl.BlockSpec((tq, qw), lambda b, i: (b * nq + i, 0)),
            pl.BlockSpec((SEQ, qw), lambda b, i: (b, 0)),
            pl.BlockSpec((SEQ, vw), lambda b, i: (b, 0)),
        ],
        out_specs=pl.BlockSpec((tq, vw), lambda b, i: (b * nq + i, 0)),
        out_shape=jax.ShapeDtypeStruct((m, vw), BF16),
        scratch_shapes=[
            pltpu.VMEM((C_HEADS, tq, SEQ), F32),
            pltpu.VMEM((C_HEADS, tq, LANE), F32),
            pltpu.VMEM((C_HEADS, tq, V_DIM + LANE), F32),
        ],
        compiler_params=_params(("arbitrary", "arbitrary"), est),
        name="mla_attention",
    )(qc, kc, vc)


def _t5_tiles_kernel(bucket_ref, table_ref, o_ref):
    for d in range(3):
        bucket = bucket_ref[d]
        for h in range(A_HEADS):
            acc = jnp.zeros(bucket.shape, F32)
            for nb in range(NUM_BUCKETS):
                acc = jnp.where(bucket == nb, table_ref[nb, h] * LOG2E, acc)
            o_ref[d, h] = acc


def _t5_tiles(bucket_tiles, t5_bias, *, t=ATT_T):
    return pl.pallas_call(
        _t5_tiles_kernel,
        in_specs=[
            pl.BlockSpec(memory_space=pltpu.VMEM),
            pl.BlockSpec(memory_space=pltpu.SMEM),
        ],
        out_specs=pl.BlockSpec(memory_space=pltpu.VMEM),
        out_shape=jax.ShapeDtypeStruct((3, A_HEADS, t, t), F32),
        name="t5_tiles",
    )(bucket_tiles, t5_bias)


def _dsa_kernel(qi_ref, klo_ref, khi_ref, misc_ref, qa_ref, ka_ref, va_ref, bias_ref, o_ref,
                sc_ref, lg_ref, acc_ref, *, t):
    i = pl.program_id(1)
    groups = t // 8
    key = lax.broadcasted_iota(jnp.int32, (t, t), 0)
    qry = lax.broadcasted_iota(jnp.int32, (t, t), 1)
    admissible = (key // CHUNK) <= (qry // CHUNK)

    def fold(x, op):
        return op(x.reshape(groups, 8, t), axis=0)

    def over_keys(x8, op):
        return jnp.broadcast_to(op(x8, axis=0, keepdims=True), (8, t))

    w_t = misc_ref[...].T[MISC_WI:MISC_WI + IDX_HEADS, :] * (IDX_HEADS ** -0.5 * IDX_DIM ** -0.5)

    def block_scores(kb):
        ks = pl.multiple_of(kb * t, t)
        klo = klo_ref[pl.ds(ks, t), :]
        khi = khi_ref[pl.ds(ks, t), :]
        acc = jnp.zeros((t, t), F32)
        for pair in range(IDX_HEADS // 2):
            qp = qi_ref[:, pair * LANE:(pair + 1) * LANE]
            for sub, kk in enumerate((klo, khi)):
                h = 2 * pair + sub
                acc = acc + jnp.maximum(_nt_dot(kk, qp), 0.0) * w_t[h:h + 1, :]
        return ks, acc

    def score_body(kb, carry):
        mn, mx = carry
        ks, acc = block_scores(kb)
        sc_ref[pl.ds(ks, t), :] = acc
        return jnp.minimum(mn, fold(acc, jnp.min)), jnp.maximum(mx, fold(acc, jnp.max))

    mn, mx = lax.fori_loop(0, i, score_body,
                           (jnp.full((8, t), -SELECT_MIN, F32), jnp.full((8, t), SELECT_MIN, F32)))
    ks, acc = block_scores(i)
    sc_ref[pl.ds(ks, t), :] = jnp.where(admissible, acc, NEG_INF)
    mn = jnp.minimum(mn, fold(jnp.where(admissible, acc, -SELECT_MIN), jnp.min))
    mx = jnp.maximum(mx, fold(jnp.where(admissible, acc, SELECT_MIN), jnp.max))

    def count_ge(thr):
        def body(kb, cnt):
            ks = pl.multiple_of(kb * t, t)
            hit = jnp.where(sc_ref[pl.ds(ks, t), :].reshape(groups, 8, t) >= thr[None], 1.0, 0.0)
            return cnt + jnp.sum(hit, axis=0)
        return over_keys(lax.fori_loop(0, i + 1, body, jnp.zeros((8, t), F32)), jnp.sum)

    k_sel = float(TOPK_MAX)
    search = i * t >= TOPK_MAX
    lo0 = jnp.where(search, over_keys(mn, jnp.min), SELECT_MIN)
    hi0 = jnp.where(search, over_keys(mx, jnp.max), SELECT_MIN)

    def midpoint(lo, hi):
        return 0.5 * lo + 0.5 * hi

    def any_active(lo, hi):
        mid = midpoint(lo, hi)
        return jnp.max(jnp.where((mid > lo) & (mid < hi), 1.0, 0.0)) > 0.5

    def search_cond(state):
        return jnp.logical_and(state[0] < BISECT_MAX_ITERS, state[1])

    def search_body(state):
        it, _, lo, hi, c_lo = state
        for _ in range(SEARCH_STEPS_PER_CHECK):
            mid = midpoint(lo, hi)
            cnt = count_ge(mid)
            ge = cnt >= k_sel
            lo = jnp.where(ge, mid, lo)
            c_lo = jnp.where(ge, cnt, c_lo)
            hi = jnp.where(cnt > k_sel, hi, mid)
        return it + SEARCH_STEPS_PER_CHECK, any_active(lo, hi), lo, hi, c_lo

    n_adm = (((i * t + qry[:8]) // CHUNK + 1) * CHUNK).astype(F32)
    _, _, lo, hi, c_lo = lax.while_loop(search_cond, search_body,
                                        (jnp.int32(0), any_active(lo0, hi0), lo0, hi0, n_adm))
    c_hi = count_ge(hi)
    thr = jnp.where(c_hi >= k_sel, hi, lo)
    c_thr = jnp.where(c_hi >= k_sel, c_hi, c_lo)

    @pl.when(jnp.max(jnp.where(c_thr > k_sel, 1.0, 0.0)) > 0.5)
    def _():
        key_in_block = (lax.broadcasted_iota(jnp.int32, (groups, 8, t), 0) * 8
                        + lax.broadcasted_iota(jnp.int32, (groups, 8, t), 1))

        def count_where(pred):
            def body(kb, cnt):
                ks = pl.multiple_of(kb * t, t)
                blk = sc_ref[pl.ds(ks, t), :].reshape(groups, 8, t)
                return cnt + jnp.sum(jnp.where(pred(blk, ks), 1.0, 0.0), axis=0)
            return over_keys(lax.fori_loop(0, i + 1, body, jnp.zeros((8, t), F32)), jnp.sum)

        def tied_before(bound):
            return lambda blk, ks: (blk == thr[None]) & ((key_in_block + ks).astype(F32) < bound[None])

        keep = k_sel - count_where(lambda blk, ks: blk > thr[None])

        def cut_body(_, bounds):
            below, above = bounds
            mid = jnp.floor(0.5 * (below + above))
            enough = count_where(tied_before(mid)) >= keep
            return jnp.where(enough, below, mid), jnp.where(enough, mid, above)

        _, cut = lax.fori_loop(0, SEQ.bit_length(), cut_body,
                               (jnp.zeros((8, t), F32), jnp.full((8, t), float(SEQ), F32)))

        def drop_body(kb, carry):
            ks = pl.multiple_of(kb * t, t)
            blk = sc_ref[pl.ds(ks, t), :].reshape(groups, 8, t)
            drop = (blk == thr[None]) & ((key_in_block + ks).astype(F32) >= cut[None])
            sc_ref[pl.ds(ks, t), :] = jnp.where(drop, NEG_INF, blk).reshape(t, t)
            return carry

        lax.fori_loop(0, i + 1, drop_body, 0)

    scale = HEAD_DIM ** -0.5 * LOG2E
    head = lambda h: slice(h * HEAD_DIM, (h + 1) * HEAD_DIM)
    acc_ref[...] = jnp.zeros(acc_ref.shape, F32)

    def logits_pass(kb, mx):
        ks = pl.multiple_of(kb * t, t)
        dist = jnp.minimum(i - kb, 2)
        k_blk = ka_ref[pl.ds(ks, t), :]
        sel = sc_ref[pl.ds(ks, t), :] >= thr[0:1, :]
        out = []
        for h in range(A_HEADS):
            s = _nt_dot(k_blk, qa_ref[:, head(h)]) * scale + bias_ref[dist, h]
            s = jnp.where(sel, s, NEG_INF)
            lg_ref[h, pl.ds(ks, t), :] = s
            out.append(jnp.maximum(mx[h], fold(s, jnp.max)))
        return tuple(out)

    mx = lax.fori_loop(0, i + 1, logits_pass,
                       tuple(jnp.full((8, t), NEG_INF, F32) for _ in range(A_HEADS)))
    m = [over_keys(mx[h], jnp.max)[0:1, :] for h in range(A_HEADS)]

    ones_rows = jnp.ones((DSA_ONES_ROWS, t), BF16)

    def value_pass(kb, carry):
        ks = pl.multiple_of(kb * t, t)
        v_t = va_ref[pl.ds(ks, t), :].astype(F32).T.astype(BF16)
        v1_t = jnp.concatenate([v_t, ones_rows], axis=0)
        for h in range(A_HEADS):
            p = jnp.exp2(lg_ref[h, pl.ds(ks, t), :] - m[h])
            acc_ref[h] += jnp.dot(v1_t, p.astype(BF16), preferred_element_type=F32)
        return carry

    lax.fori_loop(0, i + 1, value_pass, 0)
    for h in range(A_HEADS):
        acc = acc_ref[h]
        o_ref[:, head(h)] = (acc[:HEAD_DIM] / acc[HEAD_DIM:HEAD_DIM + 1]).T.astype(o_ref.dtype)


def _dsa_attention(proj, misc, bias_tiles, batch, *, t=ATT_T):
    m = proj.shape[0]
    nq = SEQ // t
    qiw = IDX_HEADS * IDX_DIM
    qaw = A_HEADS * HEAD_DIM
    est = (2 * t * qiw * 2 + 2 * 4 * SEQ * LANE * 2 + 2 * t * LANE * 4 + 2 * t * qaw * 2
           + 2 * 3 * A_HEADS * t * t * 4 + 2 * t * qaw * 2
           + (1 + A_HEADS) * t * SEQ * 4 + A_HEADS * HEAD_DIM * t * 4 + 12 * t * t * 4)
    kblock = lambda off: pl.BlockSpec((SEQ, LANE), lambda b, i: (b, off // LANE))
    return pl.pallas_call(
        functools.partial(_dsa_kernel, t=t),
        grid=(batch, nq),
        in_specs=[
            pl.BlockSpec((t, qiw), lambda b, i: (b * nq + i, OFF_QI // qiw)),
            kblock(OFF_KI_LO),
            kblock(OFF_KI_HI),
            pl.BlockSpec((t, LANE), lambda b, i: (b * nq + i, 0)),
            pl.BlockSpec((t, qaw), lambda b, i: (b * nq + i, OFF_QA // qaw)),
            kblock(OFF_KA),
            kblock(OFF_VA),
            pl.BlockSpec((3, A_HEADS, t, t), lambda b, i: (0, 0, 0, 0)),
        ],
        out_specs=pl.BlockSpec((t, qaw), lambda b, i: (b * nq + i, 0)),
        out_shape=jax.ShapeDtypeStruct((m, qaw), BF16),
        scratch_shapes=[
            pltpu.VMEM((SEQ, t), F32),
            pltpu.VMEM((A_HEADS, SEQ, t), F32),
            pltpu.VMEM((A_HEADS, HEAD_DIM + DSA_ONES_ROWS, t), F32),
        ],
        compiler_params=_params(("arbitrary", "arbitrary"), est),
        name="dsa_attention",
    )(proj, proj, proj, misc, proj, proj, proj, bias_tiles)


def _merge_kernel(x_ref, ya_ref, yb_ref, yc_ref, ga_ref, gb_ref, gc_ref,
                  wa_ref, wb_ref, wc_ref, wo_ref, o_ref):
    def branch(y_ref, w_ref, g_ref):
        y = jnp.dot(y_ref[...], w_ref[...], preferred_element_type=F32)
        return jax.nn.sigmoid(g_ref[...].astype(F32)) * y

    merged = branch(ya_ref, wa_ref, ga_ref) + branch(yb_ref, wb_ref, gb_ref) + branch(yc_ref, wc_ref, gc_ref)
    o_ref[...] = x_ref[...] + jnp.dot(merged.astype(BF16), wo_ref[...], preferred_element_type=F32)


def _merge(x2, ya, yb, yc, proj, wa, wb, wc, wo, *, tm=256):
    m = x2.shape[0]
    wbytes = (wa.size + wb.size + wc.size + wo.size) * 2
    est = 2 * wbytes + 4 * tm * D_MODEL * 4 + 2 * tm * 2048 * 2 + 6 * tm * D_MODEL * 2 + 4 * tm * D_MODEL * 4
    const = lambda i: (0, 0)
    gate = lambda k: pl.BlockSpec((tm, D_MODEL), lambda i: (i, OFF_GL // D_MODEL + k))
    return pl.pallas_call(
        _merge_kernel,
        grid=(m // tm,),
        in_specs=[
            pl.BlockSpec((tm, D_MODEL), lambda i: (i, 0)),
            pl.BlockSpec((tm, ya.shape[1]), lambda i: (i, 0)),
            pl.BlockSpec((tm, yb.shape[1]), lambda i: (i, 0)),
            pl.BlockSpec((tm, yc.shape[1]), lambda i: (i, 0)),
            gate(0), gate(1), gate(2),
            pl.BlockSpec(wa.shape, const),
            pl.BlockSpec(wb.shape, const),
            pl.BlockSpec(wc.shape, const),
            pl.BlockSpec(wo.shape, const),
        ],
        out_specs=pl.BlockSpec((tm, D_MODEL), lambda i: (i, 0)),
        out_shape=jax.ShapeDtypeStruct((m, D_MODEL), F32),
        compiler_params=_params(("arbitrary",), est),
        name="merge",
    )(x2, ya, yb, yc, proj, proj, proj, wa, wb, wc, wo)


def _ffn_kernel(x_ref, g_ref, wu_ref, wv_ref, cw_ref, wd_ref, fg_ref, o_ref,
                h_ref, ubuf_ref, halo_ref, *, tm, tf, final_norm):
    i = pl.program_id(0)
    j = pl.program_id(1)
    nj = pl.num_programs(1)

    @pl.when(j == 0)
    def _():
        _rmsnorm_rows(x_ref, g_ref, h_ref, tm)
        o_ref[...] = x_ref[...]

        @pl.when((i % (SEQ // tm)) == 0)
        def _():
            halo_ref[...] = jnp.zeros(halo_ref.shape, F32)

    h = h_ref[...]
    u = jnp.dot(h, wu_ref[...], preferred_element_type=F32)
    v = jnp.dot(h, wv_ref[...], preferred_element_type=F32)

    ubuf_ref[0:8, :] = halo_ref[j]
    ubuf_ref[8:, :] = u
    halo_ref[j] = u[tm - 8:, :]
    conv = (cw_ref[0:1, :] * ubuf_ref[6:6 + tm, :] + cw_ref[1:2, :] * ubuf_ref[7:7 + tm, :]
            + cw_ref[2:3, :] * u + cw_ref[3:4, :])
    act = (jax.nn.gelu(conv) * v).astype(BF16)
    o_ref[...] += jnp.dot(act, wd_ref[...], preferred_element_type=F32)

    if final_norm:
        @pl.when(j == nj - 1)
        def _():
            _rmsnorm_rows(o_ref, fg_ref, o_ref, tm)


def _ffn(x2, g, w_up, cw, w_down, final_g, *, final_norm, tm=1024, tf=512):
    m = x2.shape[0]
    nj = D_FF // tf
    est = (3 * tm * D_MODEL * 4 + tm * D_MODEL * 2 + 2 * 2 * D_MODEL * tf * 2 + 2 * tf * D_MODEL * 2
           + (tm + 8) * tf * 4 + nj * 8 * tf * 4 + 4 * tm * tf * 4)
    return pl.pallas_call(
        functools.partial(_ffn_kernel, tm=tm, tf=tf, final_norm=final_norm),
        grid=(m // tm, nj),
        in_specs=[
            pl.BlockSpec((tm, D_MODEL), lambda i, j: (i, 0), pipeline_mode=pl.Buffered(1)),
            pl.BlockSpec((1, D_MODEL), lambda i, j: (0, 0)),
            pl.BlockSpec((D_MODEL, tf), lambda i, j: (0, j)),
            pl.BlockSpec((D_MODEL, tf), lambda i, j: (0, nj + j)),
            pl.BlockSpec((8, tf), lambda i, j: (0, j)),
            pl.BlockSpec((tf, D_MODEL), lambda i, j: (j, 0)),
            pl.BlockSpec((1, D_MODEL), lambda i, j: (0, 0)),
        ],
        out_specs=pl.BlockSpec((tm, D_MODEL), lambda i, j: (i, 0)),
        out_shape=jax.ShapeDtypeStruct((m, D_MODEL), F32),
        scratch_shapes=[
            pltpu.VMEM((tm, D_MODEL), BF16),
            pltpu.VMEM((tm + 8, tf), F32),
            pltpu.VMEM((nj, 8, tf), F32),
        ],
        compiler_params=_params(("arbitrary", "arbitrary"), est),
        name="conv_ffn",
    )(x2, g, w_up, w_up, cw, w_down, final_g)


def _t5_bucket(rel):
    nb = NUM_BUCKETS // 2
    max_exact = nb // 2
    base = jnp.where(rel > 0, nb, 0)
    n = jnp.abs(rel)
    nf = jnp.maximum(n, 1).astype(F32)
    large = max_exact + (jnp.log(nf / max_exact) / math.log(MAX_DISTANCE / max_exact)
                         * (nb - max_exact)).astype(jnp.int32)
    large = jnp.minimum(large, nb - 1)
    return base + jnp.where(n < max_exact, n, large)


def _bucket_tiles(t):
    key = jnp.arange(t, dtype=jnp.int32)[:, None]
    qry = jnp.arange(t, dtype=jnp.int32)[None, :]
    return jnp.stack([_t5_bucket(key - qry - d * t) for d in range(3)])


def _rope_tables():
    half = ROPE_DIM // 2
    inv = ROPE_THETA ** (-jnp.arange(half, dtype=F32) / half)
    ang = jnp.arange(SEQ, dtype=jnp.int32).astype(F32)[:, None] * inv[None, :]
    cos, sin = jnp.cos(ang), jnp.sin(ang)
    pad = jnp.zeros((SEQ, LANE - ROPE_DIM), F32)
    return jnp.concatenate([cos, cos, pad], axis=1), jnp.concatenate([-sin, sin, pad], axis=1)


def _swap_halves(w):
    half = w.shape[-1] // 2
    return jnp.concatenate([w[..., half:], w[..., :half]], axis=-1)


def _cast_kernel(x_ref, o_ref):
    o_ref[...] = x_ref[...].astype(o_ref.dtype)


def _cast_bf16(w_stack, l, *, rows):
    _, r, c = w_stack.shape
    return pl.pallas_call(
        _cast_kernel,
        grid=(r // rows,),
        in_specs=[pl.BlockSpec((None, rows, c), lambda i: (l, i, 0))],
        out_specs=pl.BlockSpec((rows, c), lambda i: (i, 0)),
        out_shape=jax.ShapeDtypeStruct((r, c), BF16),
        compiler_params=_params(("arbitrary",), 2 * rows * c * 6),
        name="cast_bf16",
    )(w_stack)


def _pack_w_in_kernel(w_ref, o_ref):
    off = np.cumsum((0,) + IN_SIZES)
    seg = lambda k: w_ref[:, off[k]:off[k + 1]]
    qa, ka, va, qi, ki, wi, qb, kb, vb, fl, cq, ckv, kr, gl = [seg(k) for k in range(len(IN_SIZES))]
    z = lambda n: jnp.zeros((w_ref.shape[0], n), F32)
    groups = [
        (OFF_QI, [qi]), (OFF_QB, [qb]), (OFF_KB, [kb]), (OFF_VB, [vb]), (OFF_GL, [gl]), (OFF_QA, [qa]),
        (OFF_CQ, [cq, z(CQ_PAD - Q_LORA)]), (OFF_KA, [ka]), (OFF_VA, [va]), (OFF_KI_LO, [ki, z(64)]),
        (OFF_CKV, [ckv]), (OFF_KR, [kr, z(64)]), (OFF_KRS, [_swap_halves(kr), z(64)]),
        (OFF_MISC, [wi, fl, z(LANE - IDX_HEADS - B_HEADS)]), (OFF_KI_HI, [z(64), ki]),
    ]
    for start, pieces in groups:
        block = pieces[0] if len(pieces) == 1 else jnp.concatenate(pieces, axis=1)
        o_ref[:, start:start + block.shape[1]] = block.astype(o_ref.dtype)


def _pack_w_in(w_in, l, *, rows=128):
    _, r, c = w_in.shape
    return pl.pallas_call(
        _pack_w_in_kernel,
        grid=(r // rows,),
        in_specs=[pl.BlockSpec((None, rows, c), lambda i: (l, i, 0))],
        out_specs=pl.BlockSpec((rows, N_PACK), lambda i: (i, 0)),
        out_shape=jax.ShapeDtypeStruct((r, N_PACK), BF16),
        compiler_params=_params(("arbitrary",), 2 * rows * (c * 4 + N_PACK * 2) + rows * N_PACK * 6),
        name="pack_w_in",
    )(w_in)


def _pack_w_uq(w):
    w = jnp.pad(w, ((0, CQ_PAD - Q_LORA), (0, 0))).reshape(CQ_PAD, C_HEADS, NOPE_DIM + ROPE_DIM)
    z = jnp.zeros((CQ_PAD, C_HEADS, LANE - ROPE_DIM), w.dtype)
    rope = w[..., NOPE_DIM:]
    wq1 = jnp.concatenate([w, z], axis=-1).reshape(CQ_PAD, -1)
    wq2 = jnp.concatenate([_swap_halves(rope), z], axis=-1).reshape(CQ_PAD, -1)
    return wq1.astype(BF16), wq2.astype(BF16)


def kernel(x, norm_mix_g, w_in, b_forget, g_cq, g_ckv, w_uq, w_ukv, w_branch_a, w_branch_b, w_branch_c,
           w_o, norm_ffn_g, w_up, conv_w, conv_b, w_down, t5_bias, final_g):
    batch, seq, d = x.shape
    assert (seq, d) == (SEQ, D_MODEL)
    x2 = x.reshape(batch * seq, d)

    bias_tiles = _t5_tiles(_bucket_tiles(ATT_T), t5_bias)
    cos_t, sin_t = _rope_tables()
    final_row = final_g.reshape(1, D_MODEL)

    for l in range(DEPTH):
        proj, misc = _inproj(x2, norm_mix_g[l].reshape(1, D_MODEL), _pack_w_in(w_in, l))

        fbias = jnp.zeros((1, LANE), F32).at[0, MISC_FL:MISC_FL + B_HEADS].set(b_forget[l])
        ccol, crow = _forget_cumsum(misc, fbias, batch)

        wq1, wq2 = _pack_w_uq(w_uq[l])
        gq = jnp.pad(g_cq[l], (0, CQ_PAD - Q_LORA)).reshape(1, CQ_PAD)
        qc, kc, vc = _mla_prep(proj, gq, g_ckv[l].reshape(1, KV_LORA), wq1, wq2,
                               w_ukv[l].astype(BF16), cos_t, sin_t)

        ya = _dsa_attention(proj, misc, bias_tiles, batch)
        yb = _fox_attention(proj, ccol, crow, batch)
        yc = _mla_attention(qc, kc, vc, batch)

        x2 = _merge(x2, ya, yb, yc, proj, _cast_bf16(w_branch_a, l, rows=512),
                    _cast_bf16(w_branch_b, l, rows=512), _cast_bf16(w_branch_c, l, rows=512),
                    _cast_bf16(w_o, l, rows=512))

        cw = jnp.concatenate([conv_w[l], conv_b[l][None, :], jnp.zeros((4, D_FF), F32)], axis=0)
        x2 = _ffn(x2, norm_ffn_g[l].reshape(1, D_MODEL), _cast_bf16(w_up, l, rows=128), cw,
                  _cast_bf16(w_down, l, rows=D_FF // 8), final_row, final_norm=(l == DEPTH - 1))

    return x2.reshape(batch, seq, d)
```

```python
import functools
import math

import jax
import jax.numpy as jnp
import numpy as np
from jax import lax
from jax.experimental import pallas as pl
from jax.experimental.pallas import tpu as pltpu

F32 = jnp.float32
BF16 = jnp.bfloat16

D_MODEL = 2048
SEQ = 2048
DEPTH = 2
CHUNK = 64
HEAD_DIM = 128
EPS = 1e-6
NEG_INF = -1e30

A_HEADS = 4
IDX_HEADS = 16
IDX_DIM = 64
TOPK_MAX = 256
NUM_BUCKETS = 32
MAX_DISTANCE = 128
B_HEADS = 8
C_HEADS = 4
Q_LORA = 448
KV_LORA = 128
NOPE_DIM = 128
ROPE_DIM = 64
V_DIM = 128
ROPE_THETA = 10000.0
D_FF = 5632

IN_SIZES = (
    A_HEADS * HEAD_DIM, HEAD_DIM, HEAD_DIM,
    IDX_HEADS * IDX_DIM, IDX_DIM, IDX_HEADS,
    B_HEADS * HEAD_DIM, B_HEADS * HEAD_DIM, B_HEADS * HEAD_DIM, B_HEADS,
    Q_LORA, KV_LORA, ROPE_DIM,
    3 * D_MODEL,
)

LOG2E = math.log2(math.e)
LANE = 128
V7X_VMEM_BYTES = 64 * 1024 * 1024

OFF_QI = 0
OFF_QB = 1024
OFF_KB = 2048
OFF_VB = 3072
OFF_GL = 4096
OFF_QA = 10240
OFF_CQ = 10752
OFF_KA = 11264
OFF_VA = 11392
OFF_KI_LO = 11520
OFF_CKV = 11648
OFF_KR = 11776
OFF_KRS = 11904
OFF_MISC = 12032
OFF_KI_HI = 12160
N_PACK = 12288
CQ_PAD = 512
MISC_WI = 0
MISC_FL = IDX_HEADS

ATT_T = 256
SELECT_MIN = -1e29
BISECT_MAX_ITERS = 512
SEARCH_STEPS_PER_CHECK = 2
DSA_ONES_ROWS = 16


def _vmem_limit(estimate_bytes):
    return int(min(max(estimate_bytes * 5 // 4 + (4 << 20), 32 << 20), V7X_VMEM_BYTES - (6 << 20)))


def _params(semantics, vmem_estimate):
    return pltpu.CompilerParams(dimension_semantics=semantics,
                                vmem_limit_bytes=_vmem_limit(vmem_estimate))


def _rmsnorm_rows(x_ref, g_ref, out_ref, rows):
    def body(c, carry):
        r = pl.multiple_of(c * 128, 128)
        x = x_ref[pl.ds(r, 128), :]
        ms = jnp.mean(x * x, axis=-1, keepdims=True)
        out_ref[pl.ds(r, 128), :] = (x * lax.rsqrt(ms + EPS) * g_ref[...]).astype(out_ref.dtype)
        return carry
    lax.fori_loop(0, rows // 128, body, 0)


def _inproj_kernel(x_ref, g_ref, w_ref, o_ref, misc_ref, h_ref, *, tm, tn):
    j = pl.program_id(1)

    @pl.when(j == 0)
    def _():
        _rmsnorm_rows(x_ref, g_ref, h_ref, tm)

    acc = _nt_dot(h_ref[...], w_ref[...])
    o_ref[...] = acc.astype(o_ref.dtype)

    @pl.when(j == OFF_MISC // tn)
    def _():
        lo = OFF_MISC % tn
        misc_ref[...] = acc[:, lo:lo + LANE]


def _inproj(x2, g, w_pack, *, tm=1024, tn=1024):
    m = x2.shape[0]
    est = 2 * tm * D_MODEL * 4 + tm * D_MODEL * 2 + 2 * D_MODEL * tn * 2 + 2 * tm * tn * 2 + tm * tn * 4
    return pl.pallas_call(
        functools.partial(_inproj_kernel, tm=tm, tn=tn),
        grid=(m // tm, N_PACK // tn),
        in_specs=[
            pl.BlockSpec((tm, D_MODEL), lambda i, j: (i, 0)),
            pl.BlockSpec((1, D_MODEL), lambda i, j: (0, 0)),
            pl.BlockSpec((tn, D_MODEL), lambda i, j: (j, 0)),
        ],
        out_specs=[
            pl.BlockSpec((tm, tn), lambda i, j: (i, j)),
            pl.BlockSpec((tm, LANE), lambda i, j: (i, 0)),
        ],
        out_shape=[
            jax.ShapeDtypeStruct((m, N_PACK), BF16),
            jax.ShapeDtypeStruct((m, LANE), F32),
        ],
        scratch_shapes=[pltpu.VMEM((tm, D_MODEL), BF16)],
        compiler_params=_params(("arbitrary", "arbitrary"), est),
        name="inproj",
    )(x2, g, w_pack)


def _forget_cumsum_kernel(misc_ref, bias_ref, ccol_ref, crow_ref, *, blk):
    rows = lax.broadcasted_iota(jnp.int32, (blk, blk), 0)
    cols = lax.broadcasted_iota(jnp.int32, (blk, blk), 1)
    tri = jnp.where(rows >= cols, 1.0, 0.0).astype(BF16)
    carry = jnp.zeros((1, LANE), F32)
    for c in range(SEQ // blk):
        z = misc_ref[c * blk:(c + 1) * blk, :] + bias_ref[...]
        lf = jnp.minimum(z, 0.0) - jnp.log1p(jnp.exp(-jnp.abs(z)))
        p0 = lf.astype(BF16)
        r1 = lf - p0.astype(F32)
        p1 = r1.astype(BF16)
        p2 = (r1 - p1.astype(F32)).astype(BF16)
        cs = (jnp.dot(tri, p0, preferred_element_type=F32)
              + jnp.dot(tri, p1, preferred_element_type=F32)
              + jnp.dot(tri, p2, preferred_element_type=F32)) + carry
        ccol_ref[c * blk:(c + 1) * blk, :] = cs
        crow_ref[:, c * blk:(c + 1) * blk] = cs.T[MISC_FL:MISC_FL + B_HEADS, :]
        carry = cs[blk - 1:blk, :]


def _forget_cumsum(misc, bias_row, batch, *, blk=256):
    est = 4 * SEQ * LANE * 4 + 2 * 8 * SEQ * 4
    return pl.pallas_call(
        functools.partial(_forget_cumsum_kernel, blk=blk),
        grid=(batch,),
        in_specs=[
            pl.BlockSpec((SEQ, LANE), lambda b: (b, 0)),
            pl.BlockSpec((1, LANE), lambda b: (0, 0)),
        ],
        out_specs=[
            pl.BlockSpec((SEQ, LANE), lambda b: (b, 0)),
            pl.BlockSpec((None, B_HEADS, SEQ), lambda b: (b, 0, 0)),
        ],
        out_shape=[
            jax.ShapeDtypeStruct((batch * SEQ, LANE), F32),
            jax.ShapeDtypeStruct((batch, B_HEADS, SEQ), F32),
        ],
        compiler_params=_params(("arbitrary",), est),
        name="forget_cumsum",
    )(misc, bias_row)


def _mla_prep_kernel(cq_ref, ckv_ref, kr_ref, krs_ref, gq_ref, gkv_ref, wq1_ref, wq2_ref, wkv_ref,
                     cos_ref, sin_ref, qc_ref, kc_ref, vc_ref):
    cq = cq_ref[...].astype(F32)
    ms = jnp.sum(cq * cq, axis=-1, keepdims=True) * (1.0 / Q_LORA)
    cqn = (cq * lax.rsqrt(ms + EPS) * gq_ref[...]).astype(BF16)
    ckv = ckv_ref[...].astype(F32)
    ms2 = jnp.mean(ckv * ckv, axis=-1, keepdims=True)
    ckvn = (ckv * lax.rsqrt(ms2 + EPS) * gkv_ref[...]).astype(BF16)

    q1 = jnp.dot(cqn, wq1_ref[...], preferred_element_type=F32)
    q2 = jnp.dot(cqn, wq2_ref[...], preferred_element_type=F32)
    kv = jnp.dot(ckvn, wkv_ref[...], preferred_element_type=F32)
    cos = cos_ref[...]
    sin = sin_ref[...]
    k_rope = (kr_ref[...].astype(F32) * cos + krs_ref[...].astype(F32) * sin).astype(BF16)
    for h in range(C_HEADS):
        qw = NOPE_DIM + LANE
        qc_ref[:, h * qw:h * qw + NOPE_DIM] = q1[:, h * qw:h * qw + NOPE_DIM].astype(BF16)
        q_rope = q1[:, h * qw + NOPE_DIM:(h + 1) * qw] * cos + q2[:, h * LANE:(h + 1) * LANE] * sin
        qc_ref[:, h * qw + NOPE_DIM:(h + 1) * qw] = q_rope.astype(BF16)
        kw = NOPE_DIM + V_DIM
        kc_ref[:, h * qw:h * qw + NOPE_DIM] = kv[:, h * kw:h * kw + NOPE_DIM].astype(BF16)
        kc_ref[:, h * qw + NOPE_DIM:(h + 1) * qw] = k_rope
        vc_ref[:, h * V_DIM:(h + 1) * V_DIM] = kv[:, h * kw + NOPE_DIM:(h + 1) * kw].astype(BF16)


def _mla_prep(proj, gq, gkv, wq1, wq2, wkv, cos_t, sin_t, *, tm=512):
    m = proj.shape[0]
    nseq = SEQ // tm
    qw = C_HEADS * (NOPE_DIM + LANE)
    est = 2 * (tm * 1024 * 2) + 2 * (CQ_PAD * qw * 2 + CQ_PAD * 512 * 2 + 128 * 1024 * 2) \
        + 2 * (2 * tm * qw * 2 + tm * 512 * 2) + 3 * tm * qw * 4
    const = lambda i: (0, 0)
    return pl.pallas_call(
        _mla_prep_kernel,
        grid=(m // tm,),
        in_specs=[
            pl.BlockSpec((tm, CQ_PAD), lambda i: (i, OFF_CQ // CQ_PAD)),
            pl.BlockSpec((tm, LANE), lambda i: (i, OFF_CKV // LANE)),
            pl.BlockSpec((tm, LANE), lambda i: (i, OFF_KR // LANE)),
            pl.BlockSpec((tm, LANE), lambda i: (i, OFF_KRS // LANE)),
            pl.BlockSpec((1, CQ_PAD), const),
            pl.BlockSpec((1, KV_LORA), const),
            pl.BlockSpec((CQ_PAD, qw), const),
            pl.BlockSpec((CQ_PAD, C_HEADS * LANE), const),
            pl.BlockSpec((KV_LORA, C_HEADS * (NOPE_DIM + V_DIM)), const),
            pl.BlockSpec((tm, LANE), lambda i: (i % nseq, 0)),
            pl.BlockSpec((tm, LANE), lambda i: (i % nseq, 0)),
        ],
        out_specs=[
            pl.BlockSpec((tm, qw), lambda i: (i, 0)),
            pl.BlockSpec((tm, qw), lambda i: (i, 0)),
            pl.BlockSpec((tm, C_HEADS * V_DIM), lambda i: (i, 0)),
        ],
        out_shape=[
            jax.ShapeDtypeStruct((m, qw), BF16),
            jax.ShapeDtypeStruct((m, qw), BF16),
            jax.ShapeDtypeStruct((m, C_HEADS * V_DIM), BF16),
        ],
        compiler_params=_params(("arbitrary",), est),
        name="mla_prep",
    )(proj, proj, proj, proj, gq, gkv, wq1, wq2, wkv, cos_t, sin_t)


def _nt_dot(a, b):
    return lax.dot_general(a, b, (((1,), (1,)), ((), ())), preferred_element_type=F32)


def _two_pass_attention(i, *, n_heads, tq, tk, dv, logits_fn, mask_fn, v_fn, store_fn,
                        s_ref, mx_ref, acc_ref):
    nc = tk // LANE
    n_full = (i * tq) // tk
    nkb = ((i + 1) * tq + tk - 1) // tk

    mx_ref[...] = jnp.full(mx_ref.shape, NEG_INF, F32)

    def pass1(kb, masked):
        ks = pl.multiple_of(kb * tk, tk)
        mask = mask_fn(ks) if masked else None
        for h in range(n_heads):
            s = logits_fn(h, ks)
            if masked:
                s = jnp.where(mask, s, NEG_INF)
            s_ref[h, :, pl.ds(ks, tk)] = s
            mx = mx_ref[h]
            for c in range(nc):
                mx = jnp.maximum(mx, s[:, c * LANE:(c + 1) * LANE])
            mx_ref[h] = mx

    def pass1_full(kb, carry):
        pass1(kb, False)
        return carry

    def pass1_masked(kb, carry):
        pass1(kb, True)
        return carry

    lax.fori_loop(0, n_full, pass1_full, 0)
    lax.fori_loop(n_full, nkb, pass1_masked, 0)

    for h in range(n_heads):
        mx_ref[h] = jnp.broadcast_to(jnp.max(mx_ref[h], axis=-1, keepdims=True), (tq, LANE))
    acc_ref[...] = jnp.zeros(acc_ref.shape, F32)
    ones = jnp.ones((tk, LANE), BF16)

    def pass2(kb, carry):
        ks = pl.multiple_of(kb * tk, tk)
        for h in range(n_heads):
            m = mx_ref[h]
            p = jnp.concatenate(
                [jnp.exp2(s_ref[h, :, pl.ds(ks + c * LANE, LANE)] - m).astype(BF16) for c in range(nc)],
                axis=1)
            v1 = jnp.concatenate([v_fn(h, ks), ones], axis=1)
            acc_ref[h] += jnp.dot(p, v1, preferred_element_type=F32)
        return carry

    lax.fori_loop(0, nkb, pass2, 0)

    for h in range(n_heads):
        acc = acc_ref[h]
        store_fn(h, acc[:, :dv] / acc[:, dv:])


def _tile_iotas(tq, tk):
    return (lax.broadcasted_iota(jnp.int32, (tq, tk), 0), lax.broadcasted_iota(jnp.int32, (tq, tk), 1))


def _fox_kernel(q_ref, k_ref, v_ref, ccol_ref, crow_ref, o_ref,
                s_ref, mx_ref, acc_ref, cq_ref, *, tq, tk):
    i = pl.program_id(1)
    scale = HEAD_DIM ** -0.5 * LOG2E
    nc = tk // LANE
    head = lambda h: slice(h * HEAD_DIM, (h + 1) * HEAD_DIM)
    for h in range(B_HEADS):
        cq_ref[h] = jnp.broadcast_to(ccol_ref[:, MISC_FL + h:MISC_FL + h + 1] * LOG2E, (tq, LANE))

    def logits_fn(h, ks):
        s = _nt_dot(q_ref[:, head(h)], k_ref[pl.ds(ks, tk), head(h)]) * scale
        return s + jnp.concatenate([cq_ref[h]] * nc, axis=1) - crow_ref[h:h + 1, pl.ds(ks, tk)] * LOG2E

    def mask_fn(ks):
        rows, cols = _tile_iotas(tq, tk)
        return cols + (ks - i * tq) <= rows

    def store_fn(h, y):
        o_ref[:, head(h)] = y.astype(o_ref.dtype)

    _two_pass_attention(i, n_heads=B_HEADS, tq=tq, tk=tk, dv=HEAD_DIM, logits_fn=logits_fn,
                        mask_fn=mask_fn, v_fn=lambda h, ks: v_ref[pl.ds(ks, tk), head(h)],
                        store_fn=store_fn, s_ref=s_ref, mx_ref=mx_ref, acc_ref=acc_ref)


def _fox_attention(proj, ccol, crow, batch, *, tq=256, tk=512):
    m = proj.shape[0]
    nq = SEQ // tq
    w = B_HEADS * HEAD_DIM
    state = B_HEADS * tq * LANE * 4
    est = (4 * tq * w * 2 + 4 * SEQ * w * 2 + 2 * tq * LANE * 4 + 2 * 8 * SEQ * 4
           + B_HEADS * tq * SEQ * 4 + 4 * state + 6 * tq * tk * 4)
    return pl.pallas_call(
        functools.partial(_fox_kernel, tq=tq, tk=tk),
        grid=(batch, nq),
        in_specs=[
            pl.BlockSpec((tq, w), lambda b, i: (b * nq + i, OFF_QB // w)),
            pl.BlockSpec((SEQ, w), lambda b, i: (b, OFF_KB // w)),
            pl.BlockSpec((SEQ, w), lambda b, i: (b, OFF_VB // w)),
            pl.BlockSpec((tq, LANE), lambda b, i: (b * nq + i, 0)),
            pl.BlockSpec((None, B_HEADS, SEQ), lambda b, i: (b, 0, 0)),
        ],
        out_specs=pl.BlockSpec((tq, w), lambda b, i: (b * nq + i, 0)),
        out_shape=jax.ShapeDtypeStruct((m, w), BF16),
        scratch_shapes=[
            pltpu.VMEM((B_HEADS, tq, SEQ), F32),
            pltpu.VMEM((B_HEADS, tq, LANE), F32),
            pltpu.VMEM((B_HEADS, tq, HEAD_DIM + LANE), F32),
            pltpu.VMEM((B_HEADS, tq, LANE), F32),
        ],
        compiler_params=_params(("arbitrary", "arbitrary"), est),
        name="fox_attention",
    )(proj, proj, proj, ccol, crow)


def _mla_kernel(q_ref, k_ref, v_ref, o_ref, s_ref, mx_ref, acc_ref, *, tq, tk):
    i = pl.program_id(1)
    scale = (NOPE_DIM + ROPE_DIM) ** -0.5 * LOG2E
    qw = NOPE_DIM + LANE
    qhead = lambda h: slice(h * qw, (h + 1) * qw)
    vhead = lambda h: slice(h * V_DIM, (h + 1) * V_DIM)

    def logits_fn(h, ks):
        return _nt_dot(q_ref[:, qhead(h)], k_ref[pl.ds(ks, tk), qhead(h)]) * scale

    def mask_fn(ks):
        rows, cols = _tile_iotas(tq, tk)
        return (cols + ks) // CHUNK <= (rows + i * tq) // CHUNK

    def store_fn(h, y):
        o_ref[:, vhead(h)] = y.astype(o_ref.dtype)

    _two_pass_attention(i, n_heads=C_HEADS, tq=tq, tk=tk, dv=V_DIM, logits_fn=logits_fn,
                        mask_fn=mask_fn, v_fn=lambda h, ks: v_ref[pl.ds(ks, tk), vhead(h)],
                        store_fn=store_fn, s_ref=s_ref, mx_ref=mx_ref, acc_ref=acc_ref)


def _mla_attention(qc, kc, vc, batch, *, tq=256, tk=512):
    m = qc.shape[0]
    nq = SEQ // tq
    qw = C_HEADS * (NOPE_DIM + LANE)
    vw = C_HEADS * V_DIM
    state = C_HEADS * tq * LANE * 4
    est = (2 * tq * qw * 2 + 2 * SEQ * qw * 2 + 2 * SEQ * vw * 2 + 2 * tq * vw * 2
           + C_HEADS * tq * SEQ * 4 + 3 * state + 6 * tq * tk * 4)
    return pl.pallas_call(
        functools.partial(_mla_kernel, tq=tq, tk=tk),
        grid=(batch, nq),
        in_specs=[
            pl.BlockSpec((tq, qw), lambda b, i: (b * nq + i, 0)),
            pl.BlockSpec((SEQ, qw), lambda b, i: (b, 0)),
            pl.BlockSpec((SEQ, vw), lambda b, i: (b, 0)),
        ],
        out_specs=pl.BlockSpec((tq, vw), lambda b, i: (b * nq + i, 0)),
        out_shape=jax.ShapeDtypeStruct((m, vw), BF16),
        scratch_shapes=[
            pltpu.VMEM((C_HEADS, tq, SEQ), F32),
            pltpu.VMEM((C_HEADS, tq, LANE), F32),
            pltpu.VMEM((C_HEADS, tq, V_DIM + LANE), F32),
        ],
        compiler_params=_params(("arbitrary", "arbitrary"), est),
        name="mla_attention",
    )(qc, kc, vc)


def _t5_tiles_kernel(bucket_ref, table_ref, o_ref):
    for d in range(3):
        bucket = bucket_ref[d]
        for h in range(A_HEADS):
            acc = jnp.zeros(bucket.shape, F32)
            for nb in range(NUM_BUCKETS):
                acc = jnp.where(bucket == nb, table_ref[nb, h] * LOG2E, acc)
            o_ref[d, h] = acc


def _t5_tiles(bucket_tiles, t5_bias, *, t=ATT_T):
    return pl.pallas_call(
        _t5_tiles_kernel,
        in_specs=[
            pl.BlockSpec(memory_space=pltpu.VMEM),
            pl.BlockSpec(memory_space=pltpu.SMEM),
        ],
        out_specs=pl.BlockSpec(memory_space=pltpu.VMEM),
        out_shape=jax.ShapeDtypeStruct((3, A_HEADS, t, t), F32),
        name="t5_tiles",
    )(bucket_tiles, t5_bias)


def _dsa_kernel(qi_ref, klo_ref, khi_ref, misc_ref, qa_ref, ka_ref, va_ref, bias_ref, o_ref,
                sc_ref, lg_ref, acc_ref, *, t):
    i = pl.program_id(1)
    groups = t // 8
    key = lax.broadcasted_iota(jnp.int32, (t, t), 0)
    qry = lax.broadcasted_iota(jnp.int32, (t, t), 1)
    admissible = (key // CHUNK) <= (qry // CHUNK)

    def fold(x, op):
        return op(x.reshape(groups, 8, t), axis=0)

    def over_keys(x8, op):
        return jnp.broadcast_to(op(x8, axis=0, keepdims=True), (8, t))

    w_t = misc_ref[...].T[MISC_WI:MISC_WI + IDX_HEADS, :] * (IDX_HEADS ** -0.5 * IDX_DIM ** -0.5)

    def block_scores(kb):
        ks = pl.multiple_of(kb * t, t)
        klo = klo_ref[pl.ds(ks, t), :]
        khi = khi_ref[pl.ds(ks, t), :]
        acc = jnp.zeros((t, t), F32)
        for pair in range(IDX_HEADS // 2):
            qp = qi_ref[:, pair * LANE:(pair + 1) * LANE]
            for sub, kk in enumerate((klo, khi)):
                h = 2 * pair + sub
                acc = acc + jnp.maximum(_nt_dot(kk, qp), 0.0) * w_t[h:h + 1, :]
        return ks, acc

    def score_body(kb, carry):
        mn, mx = carry
        ks, acc = block_scores(kb)
        sc_ref[pl.ds(ks, t), :] = acc
        return jnp.minimum(mn, fold(acc, jnp.min)), jnp.maximum(mx, fold(acc, jnp.max))

    mn, mx = lax.fori_loop(0, i, score_body,
                           (jnp.full((8, t), -SELECT_MIN, F32), jnp.full((8, t), SELECT_MIN, F32)))
    ks, acc = block_scores(i)
    sc_ref[pl.ds(ks, t), :] = jnp.where(admissible, acc, NEG_INF)
    mn = jnp.minimum(mn, fold(jnp.where(admissible, acc, -SELECT_MIN), jnp.min))
    mx = jnp.maximum(mx, fold(jnp.where(admissible, acc, SELECT_MIN), jnp.max))

    def count_ge(thr):
        def body(kb, cnt):
            ks = pl.multiple_of(kb * t, t)
            hit = jnp.where(sc_ref[pl.ds(ks, t), :].reshape(groups, 8, t) >= thr[None], 1.0, 0.0)
            return cnt + jnp.sum(hit, axis=0)
        return over_keys(lax.fori_loop(0, i + 1, body, jnp.zeros((8, t), F32)), jnp.sum)

    k_sel = float(TOPK_MAX)
    search = i * t >= TOPK_MAX
    lo0 = jnp.where(search, over_keys(mn, jnp.min), SELECT_MIN)
    hi0 = jnp.where(search, over_keys(mx, jnp.max), SELECT_MIN)

    def midpoint(lo, hi):
        return 0.5 * lo + 0.5 * hi

    def any_active(lo, hi):
        mid = midpoint(lo, hi)
        return jnp.max(jnp.where((mid > lo) & (mid < hi), 1.0, 0.0)) > 0.5

    def search_cond(state):
        return jnp.logical_and(state[0] < BISECT_MAX_ITERS, state[1])

    def search_body(state):
        it, _, lo, hi, c_lo = state
        for _ in range(SEARCH_STEPS_PER_CHECK):
            mid = midpoint(lo, hi)
            cnt = count_ge(mid)
            ge = cnt >= k_sel
            lo = jnp.where(ge, mid, lo)
            c_lo = jnp.where(ge, cnt, c_lo)
            hi = jnp.where(cnt > k_sel, hi, mid)
        return it + SEARCH_STEPS_PER_CHECK, any_active(lo, hi), lo, hi, c_lo

    n_adm = (((i * t + qry[:8]) // CHUNK + 1) * CHUNK).astype(F32)
    _, _, lo, hi, c_lo = lax.while_loop(search_cond, search_body,
                                        (jnp.int32(0), any_active(lo0, hi0), lo0, hi0, n_adm))
    c_hi = count_ge(hi)
    thr = jnp.where(c_hi >= k_sel, hi, lo)
    c_thr = jnp.where(c_hi >= k_sel, c_hi, c_lo)

    @pl.when(jnp.max(jnp.where(c_thr > k_sel, 1.0, 0.0)) > 0.5)
    def _():
        key_in_block = (lax.broadcasted_iota(jnp.int32, (groups, 8, t), 0) * 8
                        + lax.broadcasted_iota(jnp.int32, (groups, 8, t), 1))

        def count_where(pred):
            def body(kb, cnt):
                ks = pl.multiple_of(kb * t, t)
                blk = sc_ref[pl.ds(ks, t), :].reshape(groups, 8, t)
                return cnt + jnp.sum(jnp.where(pred(blk, ks), 1.0, 0.0), axis=0)
            return over_keys(lax.fori_loop(0, i + 1, body, jnp.zeros((8, t), F32)), jnp.sum)

        def tied_before(bound):
            return lambda blk, ks: (blk == thr[None]) & ((key_in_block + ks).astype(F32) < bound[None])

        keep = k_sel - count_where(lambda blk, ks: blk > thr[None])

        def cut_body(_, bounds):
            below, above = bounds
            mid = jnp.floor(0.5 * (below + above))
            enough = count_where(tied_before(mid)) >= keep
            return jnp.where(enough, below, mid), jnp.where(enough, mid, above)

        _, cut = lax.fori_loop(0, SEQ.bit_length(), cut_body,
                               (jnp.zeros((8, t), F32), jnp.full((8, t), float(SEQ), F32)))

        def drop_body(kb, carry):
            ks = pl.multiple_of(kb * t, t)
            blk = sc_ref[pl.ds(ks, t), :].reshape(groups, 8, t)
            drop = (blk == thr[None]) & ((key_in_block + ks).astype(F32) >= cut[None])
            sc_ref[pl.ds(ks, t), :] = jnp.where(drop, NEG_INF, blk).reshape(t, t)
            return carry

        lax.fori_loop(0, i + 1, drop_body, 0)

    scale = HEAD_DIM ** -0.5 * LOG2E
    head = lambda h: slice(h * HEAD_DIM, (h + 1) * HEAD_DIM)
    acc_ref[...] = jnp.zeros(acc_ref.shape, F32)

    def logits_pass(kb, mx):
        ks = pl.multiple_of(kb * t, t)
        dist = jnp.minimum(i - kb, 2)
        k_blk = ka_ref[pl.ds(ks, t), :]
        sel = sc_ref[pl.ds(ks, t), :] >= thr[0:1, :]
        out = []
        for h in range(A_HEADS):
            s = _nt_dot(k_blk, qa_ref[:, head(h)]) * scale + bias_ref[dist, h]
            s = jnp.where(sel, s, NEG_INF)
            lg_ref[h, pl.ds(ks, t), :] = s
            out.append(jnp.maximum(mx[h], fold(s, jnp.max)))
        return tuple(out)

    mx = lax.fori_loop(0, i + 1, logits_pass,
                       tuple(jnp.full((8, t), NEG_INF, F32) for _ in range(A_HEADS)))
    m = [over_keys(mx[h], jnp.max)[0:1, :] for h in range(A_HEADS)]

    ones_rows = jnp.ones((DSA_ONES_ROWS, t), BF16)

    def value_pass(kb, carry):
        ks = pl.multiple_of(kb * t, t)
        v_t = va_ref[pl.ds(ks, t), :].astype(F32).T.astype(BF16)
        v1_t = jnp.concatenate([v_t, ones_rows], axis=0)
        for h in range(A_HEADS):
            p = jnp.exp2(lg_ref[h, pl.ds(ks, t), :] - m[h])
            acc_ref[h] += jnp.dot(v1_t, p.astype(BF16), preferred_element_type=F32)
        return carry

    lax.fori_loop(0, i + 1, value_pass, 0)
    for h in range(A_HEADS):
        acc = acc_ref[h]
        o_ref[:, head(h)] = (acc[:HEAD_DIM] / acc[HEAD_DIM:HEAD_DIM + 1]).T.astype(o_ref.dtype)


def _dsa_attention(proj, misc, bias_tiles, batch, *, t=ATT_T):
    m = proj.shape[0]
    nq = SEQ // t
    qiw = IDX_HEADS * IDX_DIM
    qaw = A_HEADS * HEAD_DIM
    est = (2 * t * qiw * 2 + 2 * 4 * SEQ * LANE * 2 + 2 * t * LANE * 4 + 2 * t * qaw * 2
           + 2 * 3 * A_HEADS * t * t * 4 + 2 * t * qaw * 2
           + (1 + A_HEADS) * t * SEQ * 4 + A_HEADS * HEAD_DIM * t * 4 + 12 * t * t * 4)
    kblock = lambda off: pl.BlockSpec((SEQ, LANE), lambda b, i: (b, off // LANE))
    return pl.pallas_call(
        functools.partial(_dsa_kernel, t=t),
        grid=(batch, nq),
        in_specs=[
            pl.BlockSpec((t, qiw), lambda b, i: (b * nq + i, OFF_QI // qiw)),
            kblock(OFF_KI_LO),
            kblock(OFF_KI_HI),
            pl.BlockSpec((t, LANE), lambda b, i: (b * nq + i, 0)),
            pl.BlockSpec((t, qaw), lambda b, i: (b * nq + i, OFF_QA // qaw)),
            kblock(OFF_KA),
            kblock(OFF_VA),
            pl.BlockSpec((3, A_HEADS, t, t), lambda b, i: (0, 0, 0, 0)),
        ],
        out_specs=pl.BlockSpec((t, qaw), lambda b, i: (b * nq + i, 0)),
        out_shape=jax.ShapeDtypeStruct((m, qaw), BF16),
        scratch_shapes=[
            pltpu.VMEM((SEQ, t), F32),
            pltpu.VMEM((A_HEADS, SEQ, t), F32),
            pltpu.VMEM((A_HEADS, HEAD_DIM + DSA_ONES_ROWS, t), F32),
        ],
        compiler_params=_params(("arbitrary", "arbitrary"), est),
        name="dsa_attention",
    )(proj, proj, proj, misc, proj, proj, proj, bias_tiles)


def _merge_kernel(x_ref, ya_ref, yb_ref, yc_ref, ga_ref, gb_ref, gc_ref,
                  wa_ref, wb_ref, wc_ref, wo_ref, o_ref):
    def branch(y_ref, w_ref, g_ref):
        y = jnp.dot(y_ref[...], w_ref[...], preferred_element_type=F32)
        return jax.nn.sigmoid(g_ref[...].astype(F32)) * y

    merged = branch(ya_ref, wa_ref, ga_ref) + branch(yb_ref, wb_ref, gb_ref) + branch(yc_ref, wc_ref, gc_ref)
    o_ref[...] = x_ref[...] + jnp.dot(merged.astype(BF16), wo_ref[...], preferred_element_type=F32)


def _merge(x2, ya, yb, yc, proj, wa, wb, wc, wo, *, tm=256):
    m = x2.shape[0]
    wbytes = (wa.size + wb.size + wc.size + wo.size) * 2
    est = 2 * wbytes + 4 * tm * D_MODEL * 4 + 2 * tm * 2048 * 2 + 6 * tm * D_MODEL * 2 + 4 * tm * D_MODEL * 4
    const = lambda i: (0, 0)
    gate = lambda k: pl.BlockSpec((tm, D_MODEL), lambda i: (i, OFF_GL // D_MODEL + k))
    return pl.pallas_call(
        _merge_kernel,
        grid=(m // tm,),
        in_specs=[
            pl.BlockSpec((tm, D_MODEL), lambda i: (i, 0)),
            pl.BlockSpec((tm, ya.shape[1]), lambda i: (i, 0)),
            pl.BlockSpec((tm, yb.shape[1]), lambda i: (i, 0)),
            pl.BlockSpec((tm, yc.shape[1]), lambda i: (i, 0)),
            gate(0), gate(1), gate(2),
            pl.BlockSpec(wa.shape, const),
            pl.BlockSpec(wb.shape, const),
            pl.BlockSpec(wc.shape, const),
            pl.BlockSpec(wo.shape, const),
        ],
        out_specs=pl.BlockSpec((tm, D_MODEL), lambda i: (i, 0)),
        out_shape=jax.ShapeDtypeStruct((m, D_MODEL), F32),
        compiler_params=_params(("arbitrary",), est),
        name="merge",
    )(x2, ya, yb, yc, proj, proj, proj, wa, wb, wc, wo)


def _ffn_kernel(x_ref, g_ref, wu_ref, wv_ref, cw_ref, wd_ref, fg_ref, o_ref,
                h_ref, ubuf_ref, halo_ref, *, tm, tf, final_norm):
    i = pl.program_id(0)
    j = pl.program_id(1)
    nj = pl.num_programs(1)

    @pl.when(j == 0)
    def _():
        _rmsnorm_rows(x_ref, g_ref, h_ref, tm)
        o_ref[...] = x_ref[...]

        @pl.when((i % (SEQ // tm)) == 0)
        def _():
            halo_ref[...] = jnp.zeros(halo_ref.shape, F32)

    h = h_ref[...]
    u = jnp.dot(h, wu_ref[...], preferred_element_type=F32)
    v = jnp.dot(h, wv_ref[...], preferred_element_type=F32)

    ubuf_ref[0:8, :] = halo_ref[j]
    ubuf_ref[8:, :] = u
    halo_ref[j] = u[tm - 8:, :]
    conv = (cw_ref[0:1, :] * ubuf_ref[6:6 + tm, :] + cw_ref[1:2, :] * ubuf_ref[7:7 + tm, :]
            + cw_ref[2:3, :] * u + cw_ref[3:4, :])
    act = (jax.nn.gelu(conv) * v).astype(BF16)
    o_ref[...] += jnp.dot(act, wd_ref[...], preferred_element_type=F32)

    if final_norm:
        @pl.when(j == nj - 1)
        def _():
            _rmsnorm_rows(o_ref, fg_ref, o_ref, tm)


def _ffn(x2, g, w_up, cw, w_down, final_g, *, final_norm, tm=1024, tf=512):
    m = x2.shape[0]
    nj = D_FF // tf
    est = (3 * tm * D_MODEL * 4 + tm * D_MODEL * 2 + 2 * 2 * D_MODEL * tf * 2 + 2 * tf * D_MODEL * 2
           + (tm + 8) * tf * 4 + nj * 8 * tf * 4 + 4 * tm * tf * 4)
    return pl.pallas_call(
        functools.partial(_ffn_kernel, tm=tm, tf=tf, final_norm=final_norm),
        grid=(m // tm, nj),
        in_specs=[
            pl.BlockSpec((tm, D_MODEL), lambda i, j: (i, 0), pipeline_mode=pl.Buffered(1)),
            pl.BlockSpec((1, D_MODEL), lambda i, j: (0, 0)),
            pl.BlockSpec((D_MODEL, tf), lambda i, j: (0, j)),
            pl.BlockSpec((D_MODEL, tf), lambda i, j: (0, nj + j)),
            pl.BlockSpec((8, tf), lambda i, j: (0, j)),
            pl.BlockSpec((tf, D_MODEL), lambda i, j: (j, 0)),
            pl.BlockSpec((1, D_MODEL), lambda i, j: (0, 0)),
        ],
        out_specs=pl.BlockSpec((tm, D_MODEL), lambda i, j: (i, 0)),
        out_shape=jax.ShapeDtypeStruct((m, D_MODEL), F32),
        scratch_shapes=[
            pltpu.VMEM((tm, D_MODEL), BF16),
            pltpu.VMEM((tm + 8, tf), F32),
            pltpu.VMEM((nj, 8, tf), F32),
        ],
        compiler_params=_params(("arbitrary", "arbitrary"), est),
        name="conv_ffn",
    )(x2, g, w_up, w_up, cw, w_down, final_g)


def _t5_bucket(rel):
    nb = NUM_BUCKETS // 2
    max_exact = nb // 2
    base = jnp.where(rel > 0, nb, 0)
    n = jnp.abs(rel)
    nf = jnp.maximum(n, 1).astype(F32)
    large = max_exact + (jnp.log(nf / max_exact) / math.log(MAX_DISTANCE / max_exact)
                         * (nb - max_exact)).astype(jnp.int32)
    large = jnp.minimum(large, nb - 1)
    return base + jnp.where(n < max_exact, n, large)


def _bucket_tiles(t):
    key = jnp.arange(t, dtype=jnp.int32)[:, None]
    qry = jnp.arange(t, dtype=jnp.int32)[None, :]
    return jnp.stack([_t5_bucket(key - qry - d * t) for d in range(3)])


def _rope_tables():
    half = ROPE_DIM // 2
    inv = ROPE_THETA ** (-jnp.arange(half, dtype=F32) / half)
    ang = jnp.arange(SEQ, dtype=jnp.int32).astype(F32)[:, None] * inv[None, :]
    cos, sin = jnp.cos(ang), jnp.sin(ang)
    pad = jnp.zeros((SEQ, LANE - ROPE_DIM), F32)
    return jnp.concatenate([cos, cos, pad], axis=1), jnp.concatenate([-sin, sin, pad], axis=1)


def _swap_halves(w):
    half = w.shape[-1] // 2
    return jnp.concatenate([w[..., half:], w[..., :half]], axis=-1)


def _cast_kernel(x_ref, o_ref):
    o_ref[...] = x_ref[...].astype(o_ref.dtype)


def _cast_bf16(w_stack, l, *, rows):
    _, r, c = w_stack.shape
    return pl.pallas_call(
        _cast_kernel,
        grid=(r // rows,),
        in_specs=[pl.BlockSpec((None, rows, c), lambda i: (l, i, 0))],
        out_specs=pl.BlockSpec((rows, c), lambda i: (i, 0)),
        out_shape=jax.ShapeDtypeStruct((r, c), BF16),
        compiler_params=_params(("arbitrary",), 2 * rows * c * 6),
        name="cast_bf16",
    )(w_stack)


def _pack_w_in_kernel(w_ref, o_ref):
    off = np.cumsum((0,) + IN_SIZES)
    seg = lambda k: w_ref[off[k]:off[k + 1], :]
    qa, ka, va, qi, ki, wi, qb, kb, vb, fl, cq, ckv, kr, gl = [seg(k) for k in range(len(IN_SIZES))]
    z = lambda n: jnp.zeros((n, w_ref.shape[1]), F32)
    half = ROPE_DIM // 2
    groups = [
        (OFF_QI, [qi]), (OFF_QB, [qb]), (OFF_KB, [kb]), (OFF_VB, [vb]), (OFF_GL, [gl]), (OFF_QA, [qa]),
        (OFF_CQ, [cq, z(CQ_PAD - Q_LORA)]), (OFF_KA, [ka]), (OFF_VA, [va]), (OFF_KI_LO, [ki, z(64)]),
        (OFF_CKV, [ckv]), (OFF_KR, [kr, z(64)]), (OFF_KRS, [kr[half:], kr[:half], z(64)]),
        (OFF_MISC, [wi, fl, z(LANE - IDX_HEADS - B_HEADS)]), (OFF_KI_HI, [z(64), ki]),
    ]
    for start, pieces in groups:
        block = pieces[0] if len(pieces) == 1 else jnp.concatenate(pieces, axis=0)
        o_ref[start:start + block.shape[0], :] = block.astype(o_ref.dtype)


def _pack_w_in(w_in, l, *, cols=256):
    w_t = jnp.swapaxes(w_in, 1, 2)
    _, n, d = w_t.shape
    return pl.pallas_call(
        _pack_w_in_kernel,
        grid=(d // cols,),
        in_specs=[pl.BlockSpec((None, n, cols), lambda i: (l, 0, i))],
        out_specs=pl.BlockSpec((N_PACK, cols), lambda i: (0, i)),
        out_shape=jax.ShapeDtypeStruct((N_PACK, d), BF16),
        compiler_params=_params(("arbitrary",), 2 * cols * (n * 4 + N_PACK * 2)),
        name="pack_w_in",
    )(w_t)


def _pack_w_uq(w):
    w = jnp.pad(w, ((0, CQ_PAD - Q_LORA), (0, 0))).reshape(CQ_PAD, C_HEADS, NOPE_DIM + ROPE_DIM)
    z = jnp.zeros((CQ_PAD, C_HEADS, LANE - ROPE_DIM), w.dtype)
    rope = w[..., NOPE_DIM:]
    wq1 = jnp.concatenate([w, z], axis=-1).reshape(CQ_PAD, -1)
    wq2 = jnp.concatenate([_swap_halves(rope), z], axis=-1).reshape(CQ_PAD, -1)
    return wq1.astype(BF16), wq2.astype(BF16)


def kernel(x, norm_mix_g, w_in, b_forget, g_cq, g_ckv, w_uq, w_ukv, w_branch_a, w_branch_b, w_branch_c,
           w_o, norm_ffn_g, w_up, conv_w, conv_b, w_down, t5_bias, final_g):
    batch, seq, d = x.shape
    assert (seq, d) == (SEQ, D_MODEL)
    x2 = x.reshape(batch * seq, d)

    bias_tiles = _t5_tiles(_bucket_tiles(ATT_T), t5_bias)
    cos_t, sin_t = _rope_tables()
    final_row = final_g.reshape(1, D_MODEL)

    for l in range(DEPTH):
        proj, misc = _inproj(x2, norm_mix_g[l].reshape(1, D_MODEL), _pack_w_in(w_in, l))

        fbias = jnp.zeros((1, LANE), F32).at[0, MISC_FL:MISC_FL + B_HEADS].set(b_forget[l])
        ccol, crow = _forget_cumsum(misc, fbias, batch)

        wq1, wq2 = _pack_w_uq(w_uq[l])
        gq = jnp.pad(g_cq[l], (0, CQ_PAD - Q_LORA)).reshape(1, CQ_PAD)
        qc, kc, vc = _mla_prep(proj, gq, g_ckv[l].reshape(1, KV_LORA), wq1, wq2,
                               w_ukv[l].astype(BF16), cos_t, sin_t)

        ya = _dsa_attention(proj, misc, bias_tiles, batch)
        yb = _fox_attention(proj, ccol, crow, batch)
        yc = _mla_attention(qc, kc, vc, batch)

        x2 = _merge(x2, ya, yb, yc, proj, _cast_bf16(w_branch_a, l, rows=512),
                    _cast_bf16(w_branch_b, l, rows=512), _cast_bf16(w_branch_c, l, rows=512),
                    _cast_bf16(w_o, l, rows=512))

        cw = jnp.concatenate([conv_w[l], conv_b[l][None, :], jnp.zeros((4, D_FF), F32)], axis=0)
        x2 = _ffn(x2, norm_ffn_g[l].reshape(1, D_MODEL), _cast_bf16(w_up, l, rows=128), cw,
                  _cast_bf16(w_down, l, rows=D_FF // 8), final_row, final_norm=(l == DEPTH - 1))

    return x2.reshape(batch, seq, d)
```

```python
import functools
import math

import jax
import jax.numpy as jnp
import numpy as np
from jax import lax
from jax.experimental import pallas as pl
from jax.experimental.pallas import tpu as pltpu

F32 = jnp.float32
BF16 = jnp.bfloat16

D_MODEL = 2048
SEQ = 2048
DEPTH = 2
CHUNK = 64
HEAD_DIM = 128
EPS = 1e-6
NEG_INF = -1e30

A_HEADS = 4
IDX_HEADS = 16
IDX_DIM = 64
TOPK_MAX = 256
NUM_BUCKETS = 32
MAX_DISTANCE = 128
B_HEADS = 8
C_HEADS = 4
Q_LORA = 448
KV_LORA = 128
NOPE_DIM = 128
ROPE_DIM = 64
V_DIM = 128
ROPE_THETA = 10000.0
D_FF = 5632

IN_SIZES = (
    A_HEADS * HEAD_DIM, HEAD_DIM, HEAD_DIM,
    IDX_HEADS * IDX_DIM, IDX_DIM, IDX_HEADS,
    B_HEADS * HEAD_DIM, B_HEADS * HEAD_DIM, B_HEADS * HEAD_DIM, B_HEADS,
    Q_LORA, KV_LORA, ROPE_DIM,
    3 * D_MODEL,
)

LOG2E = math.log2(math.e)
LANE = 128
V7X_VMEM_BYTES = 64 * 1024 * 1024

OFF_QI = 0
OFF_QB = 1024
OFF_KB = 2048
OFF_VB = 3072
OFF_GL = 4096
OFF_QA = 10240
OFF_CQ = 10752
OFF_KA = 11264
OFF_VA = 11392
OFF_KI_LO = 11520
OFF_CKV = 11648
OFF_KR = 11776
OFF_KRS = 11904
OFF_MISC = 12032
OFF_KI_HI = 12160
N_PACK = 12288
CQ_PAD = 512
MISC_WI = 0
MISC_FL = IDX_HEADS

ATT_T = 256
SELECT_MIN = -1e29
BISECT_MAX_ITERS = 512
SEARCH_STEPS_PER_CHECK = 2
DSA_ONES_ROWS = 16


def _vmem_limit(estimate_bytes):
    limit = V7X_VMEM_BYTES - (6 << 20)
    assert estimate_bytes <= limit, estimate_bytes
    return limit


def _params(semantics, vmem_estimate):
    return pltpu.CompilerParams(dimension_semantics=semantics,
                                vmem_limit_bytes=_vmem_limit(vmem_estimate))


def _rmsnorm_rows(x_ref, g_ref, out_ref, rows):
    def body(c, carry):
        r = pl.multiple_of(c * 128, 128)
        x = x_ref[pl.ds(r, 128), :]
        ms = jnp.mean(x * x, axis=-1, keepdims=True)
        out_ref[pl.ds(r, 128), :] = (x * lax.rsqrt(ms + EPS) * g_ref[...]).astype(out_ref.dtype)
        return carry
    lax.fori_loop(0, rows // 128, body, 0)


def _inproj_kernel(x_ref, g_ref, w_ref, o_ref, misc_ref, h_ref, *, tm, tn):
    j = pl.program_id(1)

    @pl.when(j == 0)
    def _():
        _rmsnorm_rows(x_ref, g_ref, h_ref, tm)

    acc = _nt_dot(h_ref[...], w_ref[...])
    o_ref[...] = acc.astype(o_ref.dtype)

    @pl.when(j == OFF_MISC // tn)
    def _():
        lo = OFF_MISC % tn
        misc_ref[...] = acc[:, lo:lo + LANE]


def _inproj(x2, g, w_pack, *, tm=1024, tn=1024):
    m = x2.shape[0]
    est = 2 * tm * D_MODEL * 4 + tm * D_MODEL * 2 + 2 * D_MODEL * tn * 2 + 2 * tm * tn * 2 + tm * tn * 4
    return pl.pallas_call(
        functools.partial(_inproj_kernel, tm=tm, tn=tn),
        grid=(m // tm, N_PACK // tn),
        in_specs=[
            pl.BlockSpec((tm, D_MODEL), lambda i, j: (i, 0)),
            pl.BlockSpec((1, D_MODEL), lambda i, j: (0, 0)),
            pl.BlockSpec((tn, D_MODEL), lambda i, j: (j, 0)),
        ],
        out_specs=[
            pl.BlockSpec((tm, tn), lambda i, j: (i, j)),
            pl.BlockSpec((tm, LANE), lambda i, j: (i, 0)),
        ],
        out_shape=[
            jax.ShapeDtypeStruct((m, N_PACK), BF16),
            jax.ShapeDtypeStruct((m, LANE), F32),
        ],
        scratch_shapes=[pltpu.VMEM((tm, D_MODEL), BF16)],
        compiler_params=_params(("arbitrary", "arbitrary"), est),
        name="inproj",
    )(x2, g, w_pack)


def _forget_cumsum_kernel(misc_ref, bias_ref, ccol_ref, crow_ref, *, blk):
    rows = lax.broadcasted_iota(jnp.int32, (blk, blk), 0)
    cols = lax.broadcasted_iota(jnp.int32, (blk, blk), 1)
    tri = jnp.where(rows >= cols, 1.0, 0.0).astype(BF16)
    carry = jnp.zeros((1, LANE), F32)
    for c in range(SEQ // blk):
        z = misc_ref[c * blk:(c + 1) * blk, :] + bias_ref[...]
        lf = jnp.minimum(z, 0.0) - jnp.log1p(jnp.exp(-jnp.abs(z)))
        p0 = lf.astype(BF16)
        r1 = lf - p0.astype(F32)
        p1 = r1.astype(BF16)
        p2 = (r1 - p1.astype(F32)).astype(BF16)
        cs = (jnp.dot(tri, p0, preferred_element_type=F32)
              + jnp.dot(tri, p1, preferred_element_type=F32)
              + jnp.dot(tri, p2, preferred_element_type=F32)) + carry
        ccol_ref[c * blk:(c + 1) * blk, :] = cs
        crow_ref[:, c * blk:(c + 1) * blk] = cs.T[MISC_FL:MISC_FL + B_HEADS, :]
        carry = cs[blk - 1:blk, :]


def _forget_cumsum(misc, bias_row, batch, *, blk=256):
    est = 4 * SEQ * LANE * 4 + 2 * 8 * SEQ * 4
    return pl.pallas_call(
        functools.partial(_forget_cumsum_kernel, blk=blk),
        grid=(batch,),
        in_specs=[
            pl.BlockSpec((SEQ, LANE), lambda b: (b, 0)),
            pl.BlockSpec((1, LANE), lambda b: (0, 0)),
        ],
        out_specs=[
            pl.BlockSpec((SEQ, LANE), lambda b: (b, 0)),
            pl.BlockSpec((None, B_HEADS, SEQ), lambda b: (b, 0, 0)),
        ],
        out_shape=[
            jax.ShapeDtypeStruct((batch * SEQ, LANE), F32),
            jax.ShapeDtypeStruct((batch, B_HEADS, SEQ), F32),
        ],
        compiler_params=_params(("arbitrary",), est),
        name="forget_cumsum",
    )(misc, bias_row)


def _mla_prep_kernel(cq_ref, ckv_ref, kr_ref, krs_ref, gq_ref, gkv_ref, wq1_ref, wq2_ref, wkv_ref,
                     cos_ref, sin_ref, qc_ref, kc_ref, vc_ref):
    cq = cq_ref[...].astype(F32)
    ms = jnp.sum(cq * cq, axis=-1, keepdims=True) * (1.0 / Q_LORA)
    cqn = (cq * lax.rsqrt(ms + EPS) * gq_ref[...]).astype(BF16)
    ckv = ckv_ref[...].astype(F32)
    ms2 = jnp.mean(ckv * ckv, axis=-1, keepdims=True)
    ckvn = (ckv * lax.rsqrt(ms2 + EPS) * gkv_ref[...]).astype(BF16)

    q1 = jnp.dot(cqn, wq1_ref[...], preferred_element_type=F32)
    q2 = jnp.dot(cqn, wq2_ref[...], preferred_element_type=F32)
    kv = jnp.dot(ckvn, wkv_ref[...], preferred_element_type=F32)
    cos = cos_ref[...]
    sin = sin_ref[...]
    k_rope = (kr_ref[...].astype(F32) * cos + krs_ref[...].astype(F32) * sin).astype(BF16)
    for h in range(C_HEADS):
        qw = NOPE_DIM + LANE
        qc_ref[:, h * qw:h * qw + NOPE_DIM] = q1[:, h * qw:h * qw + NOPE_DIM].astype(BF16)
        q_rope = q1[:, h * qw + NOPE_DIM:(h + 1) * qw] * cos + q2[:, h * LANE:(h + 1) * LANE] * sin
        qc_ref[:, h * qw + NOPE_DIM:(h + 1) * qw] = q_rope.astype(BF16)
        kw = NOPE_DIM + V_DIM
        kc_ref[:, h * qw:h * qw + NOPE_DIM] = kv[:, h * kw:h * kw + NOPE_DIM].astype(BF16)
        kc_ref[:, h * qw + NOPE_DIM:(h + 1) * qw] = k_rope
        vc_ref[:, h * V_DIM:(h + 1) * V_DIM] = kv[:, h * kw + NOPE_DIM:(h + 1) * kw].astype(BF16)


def _mla_prep(proj, gq, gkv, wq1, wq2, wkv, cos_t, sin_t, *, tm=512):
    m = proj.shape[0]
    nseq = SEQ // tm
    qw = C_HEADS * (NOPE_DIM + LANE)
    est = 2 * (tm * 1024 * 2) + 2 * (CQ_PAD * qw * 2 + CQ_PAD * 512 * 2 + 128 * 1024 * 2) \
        + 2 * (2 * tm * qw * 2 + tm * 512 * 2) + 3 * tm * qw * 4
    const = lambda i: (0, 0)
    return pl.pallas_call(
        _mla_prep_kernel,
        grid=(m // tm,),
        in_specs=[
            pl.BlockSpec((tm, CQ_PAD), lambda i: (i, OFF_CQ // CQ_PAD)),
            pl.BlockSpec((tm, LANE), lambda i: (i, OFF_CKV // LANE)),
            pl.BlockSpec((tm, LANE), lambda i: (i, OFF_KR // LANE)),
            pl.BlockSpec((tm, LANE), lambda i: (i, OFF_KRS // LANE)),
            pl.BlockSpec((1, CQ_PAD), const),
            pl.BlockSpec((1, KV_LORA), const),
            pl.BlockSpec((CQ_PAD, qw), const),
            pl.BlockSpec((CQ_PAD, C_HEADS * LANE), const),
            pl.BlockSpec((KV_LORA, C_HEADS * (NOPE_DIM + V_DIM)), const),
            pl.BlockSpec((tm, LANE), lambda i: (i % nseq, 0)),
            pl.BlockSpec((tm, LANE), lambda i: (i % nseq, 0)),
        ],
        out_specs=[
            pl.BlockSpec((tm, qw), lambda i: (i, 0)),
            pl.BlockSpec((tm, qw), lambda i: (i, 0)),
            pl.BlockSpec((tm, C_HEADS * V_DIM), lambda i: (i, 0)),
        ],
        out_shape=[
            jax.ShapeDtypeStruct((m, qw), BF16),
            jax.ShapeDtypeStruct((m, qw), BF16),
            jax.ShapeDtypeStruct((m, C_HEADS * V_DIM), BF16),
        ],
        compiler_params=_params(("arbitrary",), est),
        name="mla_prep",
    )(proj, proj, proj, proj, gq, gkv, wq1, wq2, wkv, cos_t, sin_t)


def _paired_loop(n, body, init):
    def pair(p, carry):
        return body(2 * p + 1, body(2 * p, carry))
    carry = lax.fori_loop(0, n // 2, pair, init)
    return lax.fori_loop(2 * (n // 2), n, body, carry)


def _nt_dot(a, b):
    return lax.dot_general(a, b, (((1,), (1,)), ((), ())), preferred_element_type=F32)


def _two_pass_attention(i, *, n_heads, tq, tk, dv, logits_fn, mask_fn, v_fn, store_fn,
                        s_ref, mx_ref, acc_ref):
    nc = tk // LANE
    n_full = (i * tq) // tk
    nkb = ((i + 1) * tq + tk - 1) // tk

    mx_ref[...] = jnp.full(mx_ref.shape, NEG_INF, F32)

    def pass1(kb, masked):
        ks = pl.multiple_of(kb * tk, tk)
        mask = mask_fn(ks) if masked else None
        for h in range(n_heads):
            s = logits_fn(h, ks)
            if masked:
                s = jnp.where(mask, s, NEG_INF)
            s_ref[h, :, pl.ds(ks, tk)] = s
            mx = mx_ref[h]
            for c in range(nc):
                mx = jnp.maximum(mx, s[:, c * LANE:(c + 1) * LANE])
            mx_ref[h] = mx

    def pass1_full(kb, carry):
        pass1(kb, False)
        return carry

    def pass1_masked(kb, carry):
        pass1(kb, True)
        return carry

    lax.fori_loop(0, n_full, pass1_full, 0)
    lax.fori_loop(n_full, nkb, pass1_masked, 0)

    for h in range(n_heads):
        mx_ref[h] = jnp.broadcast_to(jnp.max(mx_ref[h], axis=-1, keepdims=True), (tq, LANE))
    acc_ref[...] = jnp.zeros(acc_ref.shape, F32)
    ones = jnp.ones((tk, LANE), BF16)

    def pass2(kb, carry):
        ks = pl.multiple_of(kb * tk, tk)
        for h in range(n_heads):
            m = mx_ref[h]
            p = jnp.concatenate(
                [jnp.exp2(s_ref[h, :, pl.ds(ks + c * LANE, LANE)] - m).astype(BF16) for c in range(nc)],
                axis=1)
            v1 = jnp.concatenate([v_fn(h, ks), ones], axis=1)
            acc_ref[h] += jnp.dot(p, v1, preferred_element_type=F32)
        return carry

    lax.fori_loop(0, nkb, pass2, 0)

    for h in range(n_heads):
        acc = acc_ref[h]
        store_fn(h, acc[:, :dv] / acc[:, dv:])


def _tile_iotas(tq, tk):
    return (lax.broadcasted_iota(jnp.int32, (tq, tk), 0), lax.broadcasted_iota(jnp.int32, (tq, tk), 1))


def _fox_kernel(q_ref, k_ref, v_ref, ccol_ref, crow_ref, o_ref,
                s_ref, mx_ref, acc_ref, cq_ref, *, tq, tk):
    i = pl.program_id(1)
    scale = HEAD_DIM ** -0.5 * LOG2E
    nc = tk // LANE
    head = lambda h: slice(h * HEAD_DIM, (h + 1) * HEAD_DIM)
    for h in range(B_HEADS):
        cq_ref[h] = jnp.broadcast_to(ccol_ref[:, MISC_FL + h:MISC_FL + h + 1] * LOG2E, (tq, LANE))

    def logits_fn(h, ks):
        s = _nt_dot(q_ref[:, head(h)], k_ref[pl.ds(ks, tk), head(h)]) * scale
        return s + jnp.concatenate([cq_ref[h]] * nc, axis=1) - crow_ref[h:h + 1, pl.ds(ks, tk)] * LOG2E

    def mask_fn(ks):
        rows, cols = _tile_iotas(tq, tk)
        return cols + (ks - i * tq) <= rows

    def store_fn(h, y):
        o_ref[:, head(h)] = y.astype(o_ref.dtype)

    _two_pass_attention(i, n_heads=B_HEADS, tq=tq, tk=tk, dv=HEAD_DIM, logits_fn=logits_fn,
                        mask_fn=mask_fn, v_fn=lambda h, ks: v_ref[pl.ds(ks, tk), head(h)],
                        store_fn=store_fn, s_ref=s_ref, mx_ref=mx_ref, acc_ref=acc_ref)


def _fox_attention(proj, ccol, crow, batch, *, tq=256, tk=512):
    m = proj.shape[0]
    nq = SEQ // tq
    w = B_HEADS * HEAD_DIM
    state = B_HEADS * tq * LANE * 4
    est = (4 * tq * w * 2 + 4 * SEQ * w * 2 + 2 * tq * LANE * 4 + 2 * 8 * SEQ * 4
           + B_HEADS * tq * SEQ * 4 + 4 * state + 6 * tq * tk * 4)
    return pl.pallas_call(
        functools.partial(_fox_kernel, tq=tq, tk=tk),
        grid=(batch, nq),
        in_specs=[
            pl.BlockSpec((tq, w), lambda b, i: (b * nq + i, OFF_QB // w)),
            pl.BlockSpec((SEQ, w), lambda b, i: (b, OFF_KB // w)),
            pl.BlockSpec((SEQ, w), lambda b, i: (b, OFF_VB // w)),
            pl.BlockSpec((tq, LANE), lambda b, i: (b * nq + i, 0)),
            pl.BlockSpec((None, B_HEADS, SEQ), lambda b, i: (b, 0, 0)),
        ],
        out_specs=pl.BlockSpec((tq, w), lambda b, i: (b * nq + i, 0)),
        out_shape=jax.ShapeDtypeStruct((m, w), BF16),
        scratch_shapes=[
            pltpu.VMEM((B_HEADS, tq, SEQ), F32),
            pltpu.VMEM((B_HEADS, tq, LANE), F32),
            pltpu.VMEM((B_HEADS, tq, HEAD_DIM + LANE), F32),
            pltpu.VMEM((B_HEADS, tq, LANE), F32),
        ],
        compiler_params=_params(("arbitrary", "arbitrary"), est),
        name="fox_attention",
    )(proj, proj, proj, ccol, crow)


def _mla_kernel(q_ref, k_ref, v_ref, o_ref, s_ref, mx_ref, acc_ref, *, tq, tk):
    i = pl.program_id(1)
    scale = (NOPE_DIM + ROPE_DIM) ** -0.5 * LOG2E
    qw = NOPE_DIM + LANE
    qhead = lambda h: slice(h * qw, (h + 1) * qw)
    vhead = lambda h: slice(h * V_DIM, (h + 1) * V_DIM)

    def logits_fn(h, ks):
        return _nt_dot(q_ref[:, qhead(h)], k_ref[pl.ds(ks, tk), qhead(h)]) * scale

    def mask_fn(ks):
        rows, cols = _tile_iotas(tq, tk)
        return (cols + ks) // CHUNK <= (rows + i * tq) // CHUNK

    def store_fn(h, y):
        o_ref[:, vhead(h)] = y.astype(o_ref.dtype)

    _two_pass_attention(i, n_heads=C_HEADS, tq=tq, tk=tk, dv=V_DIM, logits_fn=logits_fn,
                        mask_fn=mask_fn, v_fn=lambda h, ks: v_ref[pl.ds(ks, tk), vhead(h)],
                        store_fn=store_fn, s_ref=s_ref, mx_ref=mx_ref, acc_ref=acc_ref)


def _mla_attention(qc, kc, vc, batch, *, tq=256, tk=512):
    m = qc.shape[0]
    nq = SEQ // tq
    qw = C_HEADS * (NOPE_DIM + LANE)
    vw = C_HEADS * V_DIM
    state = C_HEADS * tq * LANE * 4
    est = (2 * tq * qw * 2 + 2 * SEQ * qw * 2 + 2 * SEQ * vw * 2 + 2 * tq * vw * 2
           + C_HEADS * tq * SEQ * 4 + 3 * state + 6 * tq * tk * 4)
    return pl.pallas_call(
        functools.partial(_mla_kernel, tq=tq, tk=tk),
        grid=(batch, nq),
        in_specs=[
            pl.BlockSpec((tq, qw), lambda b, i: (b * nq + i, 0)),
            pl.BlockSpec((SEQ, qw), lambda b, i: (b, 0)),
            pl.BlockSpec((SEQ, vw), lambda b, i: (b, 0)),
        ],
        out_specs=pl.BlockSpec((tq, vw), lambda b, i: (b * nq + i, 0)),
        out_shape=jax.ShapeDtypeStruct((m, vw), BF16),
        scratch_shapes=[
            pltpu.VMEM((C_HEADS, tq, SEQ), F32),
            pltpu.VMEM((C_HEADS, tq, LANE), F32),
            pltpu.VMEM((C_HEADS, tq, V_DIM + LANE), F32),
        ],
        compiler_params=_params(("arbitrary", "arbitrary"), est),
        name="mla_attention",
    )(qc, kc, vc)


def _t5_tiles_kernel(bucket_ref, table_ref, o_ref):
    for d in range(3):
        bucket = bucket_ref[d]
        for h in range(A_HEADS):
            acc = jnp.zeros(bucket.shape, F32)
            for nb in range(NUM_BUCKETS):
                acc = jnp.where(bucket == nb, table_ref[nb, h] * LOG2E, acc)
            o_ref[d, h] = acc


def _t5_tiles(bucket_tiles, t5_bias, *, t=ATT_T):
    return pl.pallas_call(
        _t5_tiles_kernel,
        in_specs=[
            pl.BlockSpec(memory_space=pltpu.VMEM),
            pl.BlockSpec(memory_space=pltpu.SMEM),
        ],
        out_specs=pl.BlockSpec(memory_space=pltpu.VMEM),
        out_shape=jax.ShapeDtypeStruct((3, A_HEADS, t, t), F32),
        name="t5_tiles",
    )(bucket_tiles, t5_bias)


def _dsa_kernel(qi_ref, klo_ref, khi_ref, misc_ref, qa_ref, ka_ref, va_ref, bias_ref, o_ref,
                sc_ref, lg_ref, acc_ref, *, t):
    i = pl.program_id(1)
    groups = t // 8
    key = lax.broadcasted_iota(jnp.int32, (t, t), 0)
    qry = lax.broadcasted_iota(jnp.int32, (t, t), 1)
    admissible = (key // CHUNK) <= (qry // CHUNK)

    def fold(x, op):
        return op(x.reshape(groups, 8, t), axis=0)

    def over_keys(x8, op):
        return jnp.broadcast_to(op(x8, axis=0, keepdims=True), (8, t))

    w_t = misc_ref[...].T[MISC_WI:MISC_WI + IDX_HEADS, :] * (IDX_HEADS ** -0.5 * IDX_DIM ** -0.5)

    def block_scores(kb):
        ks = pl.multiple_of(kb * t, t)
        klo = klo_ref[pl.ds(ks, t), :]
        khi = khi_ref[pl.ds(ks, t), :]
        acc = jnp.zeros((t, t), F32)
        for pair in range(IDX_HEADS // 2):
            qp = qi_ref[:, pair * LANE:(pair + 1) * LANE]
            for sub, kk in enumerate((klo, khi)):
                h = 2 * pair + sub
                acc = acc + jnp.maximum(_nt_dot(kk, qp), 0.0) * w_t[h:h + 1, :]
        return ks, acc

    def score_body(kb, carry):
        mn, mx = carry
        ks, acc = block_scores(kb)
        sc_ref[pl.ds(ks, t), :] = acc
        return jnp.minimum(mn, fold(acc, jnp.min)), jnp.maximum(mx, fold(acc, jnp.max))

    mn, mx = lax.fori_loop(0, i, score_body,
                           (jnp.full((8, t), -SELECT_MIN, F32), jnp.full((8, t), SELECT_MIN, F32)))
    ks, acc = block_scores(i)
    sc_ref[pl.ds(ks, t), :] = jnp.where(admissible, acc, NEG_INF)
    mn = jnp.minimum(mn, fold(jnp.where(admissible, acc, -SELECT_MIN), jnp.min))
    mx = jnp.maximum(mx, fold(jnp.where(admissible, acc, SELECT_MIN), jnp.max))

    def count_ge(thr):
        def body(kb, cnt):
            ks = pl.multiple_of(kb * t, t)
            hit = jnp.where(sc_ref[pl.ds(ks, t), :].reshape(groups, 8, t) >= thr[None], 1.0, 0.0)
            return cnt + jnp.sum(hit, axis=0)
        return over_keys(_paired_loop(i + 1, body, jnp.zeros((8, t), F32)), jnp.sum)

    k_sel = float(TOPK_MAX)
    search = i * t >= TOPK_MAX
    lo0 = jnp.where(search, over_keys(mn, jnp.min), SELECT_MIN)
    hi0 = jnp.where(search, over_keys(mx, jnp.max), SELECT_MIN)

    def midpoint(lo, hi):
        return 0.5 * lo + 0.5 * hi

    def any_active(lo, hi):
        mid = midpoint(lo, hi)
        return jnp.max(jnp.where((mid > lo) & (mid < hi), 1.0, 0.0)) > 0.5

    def search_cond(state):
        return jnp.logical_and(state[0] < BISECT_MAX_ITERS, state[1])

    def search_body(state):
        it, _, lo, hi, c_lo = state
        for _ in range(SEARCH_STEPS_PER_CHECK):
            mid = midpoint(lo, hi)
            cnt = count_ge(mid)
            ge = cnt >= k_sel
            lo = jnp.where(ge, mid, lo)
            c_lo = jnp.where(ge, cnt, c_lo)
            hi = jnp.where(cnt > k_sel, hi, mid)
        return it + SEARCH_STEPS_PER_CHECK, any_active(lo, hi), lo, hi, c_lo

    n_adm = (((i * t + qry[:8]) // CHUNK + 1) * CHUNK).astype(F32)
    _, _, lo, hi, c_lo = lax.while_loop(search_cond, search_body,
                                        (jnp.int32(0), any_active(lo0, hi0), lo0, hi0, n_adm))
    c_hi = count_ge(hi)
    thr = jnp.where(c_hi >= k_sel, hi, lo)
    c_thr = jnp.where(c_hi >= k_sel, c_hi, c_lo)

    @pl.when(jnp.max(jnp.where(c_thr > k_sel, 1.0, 0.0)) > 0.5)
    def _():
        key_in_block = (lax.broadcasted_iota(jnp.int32, (groups, 8, t), 0) * 8
                        + lax.broadcasted_iota(jnp.int32, (groups, 8, t), 1))

        def count_where(pred):
            def body(kb, cnt):
                ks = pl.multiple_of(kb * t, t)
                blk = sc_ref[pl.ds(ks, t), :].reshape(groups, 8, t)
                return cnt + jnp.sum(jnp.where(pred(blk, ks), 1.0, 0.0), axis=0)
            return over_keys(lax.fori_loop(0, i + 1, body, jnp.zeros((8, t), F32)), jnp.sum)

        def tied_before(bound):
            return lambda blk, ks: (blk == thr[None]) & ((key_in_block + ks).astype(F32) < bound[None])

        keep = k_sel - count_where(lambda blk, ks: blk > thr[None])

        def cut_body(_, bounds):
            below, above = bounds
            mid = jnp.floor(0.5 * (below + above))
            enough = count_where(tied_before(mid)) >= keep
            return jnp.where(enough, below, mid), jnp.where(enough, mid, above)

        _, cut = lax.fori_loop(0, SEQ.bit_length(), cut_body,
                               (jnp.zeros((8, t), F32), jnp.full((8, t), float(SEQ), F32)))

        def drop_body(kb, carry):
            ks = pl.multiple_of(kb * t, t)
            blk = sc_ref[pl.ds(ks, t), :].reshape(groups, 8, t)
            drop = (blk == thr[None]) & ((key_in_block + ks).astype(F32) >= cut[None])
            sc_ref[pl.ds(ks, t), :] = jnp.where(drop, NEG_INF, blk).reshape(t, t)
            return carry

        lax.fori_loop(0, i + 1, drop_body, 0)

    scale = HEAD_DIM ** -0.5 * LOG2E
    head = lambda h: slice(h * HEAD_DIM, (h + 1) * HEAD_DIM)
    acc_ref[...] = jnp.zeros(acc_ref.shape, F32)

    def logits_pass(kb, mx):
        ks = pl.multiple_of(kb * t, t)
        dist = jnp.minimum(i - kb, 2)
        k_blk = ka_ref[pl.ds(ks, t), :]
        sel = sc_ref[pl.ds(ks, t), :] >= thr[0:1, :]
        out = []
        for h in range(A_HEADS):
            s = _nt_dot(k_blk, qa_ref[:, head(h)]) * scale + bias_ref[dist, h]
            s = jnp.where(sel, s, NEG_INF)
            lg_ref[h, pl.ds(ks, t), :] = s
            out.append(jnp.maximum(mx[h], fold(s, jnp.max)))
        return tuple(out)

    mx = _paired_loop(i + 1, logits_pass,
                      tuple(jnp.full((8, t), NEG_INF, F32) for _ in range(A_HEADS)))
    m = [over_keys(mx[h], jnp.max)[0:1, :] for h in range(A_HEADS)]

    ones_rows = jnp.ones((DSA_ONES_ROWS, t), BF16)

    def value_pass(kb, carry):
        ks = pl.multiple_of(kb * t, t)
        v_t = va_ref[pl.ds(ks, t), :].astype(F32).T.astype(BF16)
        v1_t = jnp.concatenate([v_t, ones_rows], axis=0)
        for h in range(A_HEADS):
            p = jnp.exp2(lg_ref[h, pl.ds(ks, t), :] - m[h])
            acc_ref[h] += jnp.dot(v1_t, p.astype(BF16), preferred_element_type=F32)
        return carry

    _paired_loop(i + 1, value_pass, 0)
    for h in range(A_HEADS):
        acc = acc_ref[h]
        o_ref[:, head(h)] = (acc[:HEAD_DIM] / acc[HEAD_DIM:HEAD_DIM + 1]).T.astype(o_ref.dtype)


def _dsa_attention(proj, misc, bias_tiles, batch, *, t=ATT_T):
    m = proj.shape[0]
    nq = SEQ // t
    qiw = IDX_HEADS * IDX_DIM
    qaw = A_HEADS * HEAD_DIM
    est = (2 * t * qiw * 2 + 2 * 4 * SEQ * LANE * 2 + 2 * t * LANE * 4 + 2 * t * qaw * 2
           + 2 * 3 * A_HEADS * t * t * 4 + 2 * t * qaw * 2
           + (1 + A_HEADS) * t * SEQ * 4 + A_HEADS * HEAD_DIM * t * 4 + 12 * t * t * 4)
    kblock = lambda off: pl.BlockSpec((SEQ, LANE), lambda b, i: (b, off // LANE))
    return pl.pallas_call(
        functools.partial(_dsa_kernel, t=t),
        grid=(batch, nq),
        in_specs=[
            pl.BlockSpec((t, qiw), lambda b, i: (b * nq + i, OFF_QI // qiw)),
            kblock(OFF_KI_LO),
            kblock(OFF_KI_HI),
            pl.BlockSpec((t, LANE), lambda b, i: (b * nq + i, 0)),
            pl.BlockSpec((t, qaw), lambda b, i: (b * nq + i, OFF_QA // qaw)),
            kblock(OFF_KA),
            kblock(OFF_VA),
            pl.BlockSpec((3, A_HEADS, t, t), lambda b, i: (0, 0, 0, 0)),
        ],
        out_specs=pl.BlockSpec((t, qaw), lambda b, i: (b * nq + i, 0)),
        out_shape=jax.ShapeDtypeStruct((m, qaw), BF16),
        scratch_shapes=[
            pltpu.VMEM((SEQ, t), F32),
            pltpu.VMEM((A_HEADS, SEQ, t), F32),
            pltpu.VMEM((A_HEADS, HEAD_DIM + DSA_ONES_ROWS, t), F32),
        ],
        compiler_params=_params(("arbitrary", "arbitrary"), est),
        name="dsa_attention",
    )(proj, proj, proj, misc, proj, proj, proj, bias_tiles)


def _merge_kernel(x_ref, ya_ref, yb_ref, yc_ref, ga_ref, gb_ref, gc_ref,
                  wa_ref, wb_ref, wc_ref, wo_ref, o_ref):
    def branch(y_ref, w_ref, g_ref):
        y = jnp.dot(y_ref[...], w_ref[...], preferred_element_type=F32)
        return jax.nn.sigmoid(g_ref[...].astype(F32)) * y

    merged = branch(ya_ref, wa_ref, ga_ref) + branch(yb_ref, wb_ref, gb_ref) + branch(yc_ref, wc_ref, gc_ref)
    o_ref[...] = x_ref[...] + jnp.dot(merged.astype(BF16), wo_ref[...], preferred_element_type=F32)


def _merge(x2, ya, yb, yc, proj, wa, wb, wc, wo, *, tm=256):
    m = x2.shape[0]
    wbytes = (wa.size + wb.size + wc.size + wo.size) * 2
    est = 2 * wbytes + 4 * tm * D_MODEL * 4 + 2 * tm * 2048 * 2 + 6 * tm * D_MODEL * 2 + 4 * tm * D_MODEL * 4
    const = lambda i: (0, 0)
    gate = lambda k: pl.BlockSpec((tm, D_MODEL), lambda i: (i, OFF_GL // D_MODEL + k))
    return pl.pallas_call(
        _merge_kernel,
        grid=(m // tm,),
        in_specs=[
            pl.BlockSpec((tm, D_MODEL), lambda i: (i, 0)),
            pl.BlockSpec((tm, ya.shape[1]), lambda i: (i, 0)),
            pl.BlockSpec((tm, yb.shape[1]), lambda i: (i, 0)),
            pl.BlockSpec((tm, yc.shape[1]), lambda i: (i, 0)),
            gate(0), gate(1), gate(2),
            pl.BlockSpec(wa.shape, const),
            pl.BlockSpec(wb.shape, const),
            pl.BlockSpec(wc.shape, const),
            pl.BlockSpec(wo.shape, const),
        ],
        out_specs=pl.BlockSpec((tm, D_MODEL), lambda i: (i, 0)),
        out_shape=jax.ShapeDtypeStruct((m, D_MODEL), F32),
        compiler_params=_params(("arbitrary",), est),
        name="merge",
    )(x2, ya, yb, yc, proj, proj, proj, wa, wb, wc, wo)


def _ffn_kernel(x_ref, g_ref, wu_ref, wv_ref, cw_ref, wd_ref, fg_ref, o_ref,
                h_ref, ubuf_ref, halo_ref, *, tm, tf, final_norm):
    i = pl.program_id(0)
    j = pl.program_id(1)
    nj = pl.num_programs(1)

    @pl.when(j == 0)
    def _():
        _rmsnorm_rows(x_ref, g_ref, h_ref, tm)
        o_ref[...] = x_ref[...]

        @pl.when((i % (SEQ // tm)) == 0)
        def _():
            halo_ref[...] = jnp.zeros(halo_ref.shape, F32)

    h = h_ref[...]
    u = jnp.dot(h, wu_ref[...], preferred_element_type=F32)
    v = jnp.dot(h, wv_ref[...], preferred_element_type=F32)

    ubuf_ref[0:8, :] = halo_ref[j]
    ubuf_ref[8:, :] = u
    halo_ref[j] = u[tm - 8:, :]
    conv = (cw_ref[0:1, :] * ubuf_ref[6:6 + tm, :] + cw_ref[1:2, :] * ubuf_ref[7:7 + tm, :]
            + cw_ref[2:3, :] * u + cw_ref[3:4, :])
    act = (jax.nn.gelu(conv) * v).astype(BF16)
    o_ref[...] += jnp.dot(act, wd_ref[...], preferred_element_type=F32)

    if final_norm:
        @pl.when(j == nj - 1)
        def _():
            _rmsnorm_rows(o_ref, fg_ref, o_ref, tm)


def _ffn(x2, g, w_up, cw, w_down, final_g, *, final_norm, tm=1024, tf=512):
    m = x2.shape[0]
    nj = D_FF // tf
    est = (3 * tm * D_MODEL * 4 + tm * D_MODEL * 2 + 2 * 2 * D_MODEL * tf * 2 + 2 * tf * D_MODEL * 2
           + (tm + 8) * tf * 4 + nj * 8 * tf * 4 + 4 * tm * tf * 4)
    return pl.pallas_call(
        functools.partial(_ffn_kernel, tm=tm, tf=tf, final_norm=final_norm),
        grid=(m // tm, nj),
        in_specs=[
            pl.BlockSpec((tm, D_MODEL), lambda i, j: (i, 0), pipeline_mode=pl.Buffered(1)),
            pl.BlockSpec((1, D_MODEL), lambda i, j: (0, 0)),
            pl.BlockSpec((D_MODEL, tf), lambda i, j: (0, j)),
            pl.BlockSpec((D_MODEL, tf), lambda i, j: (0, nj + j)),
            pl.BlockSpec((8, tf), lambda i, j: (0, j)),
            pl.BlockSpec((tf, D_MODEL), lambda i, j: (j, 0)),
            pl.BlockSpec((1, D_MODEL), lambda i, j: (0, 0)),
        ],
        out_specs=pl.BlockSpec((tm, D_MODEL), lambda i, j: (i, 0)),
        out_shape=jax.ShapeDtypeStruct((m, D_MODEL), F32),
        scratch_shapes=[
            pltpu.VMEM((tm, D_MODEL), BF16),
            pltpu.VMEM((tm + 8, tf), F32),
            pltpu.VMEM((nj, 8, tf), F32),
        ],
        compiler_params=_params(("arbitrary", "arbitrary"), est),
        name="conv_ffn",
    )(x2, g, w_up, w_up, cw, w_down, final_g)


def _t5_bucket(rel):
    nb = NUM_BUCKETS // 2
    max_exact = nb // 2
    base = jnp.where(rel > 0, nb, 0)
    n = jnp.abs(rel)
    nf = jnp.maximum(n, 1).astype(F32)
    large = max_exact + (jnp.log(nf / max_exact) / math.log(MAX_DISTANCE / max_exact)
                         * (nb - max_exact)).astype(jnp.int32)
    large = jnp.minimum(large, nb - 1)
    return base + jnp.where(n < max_exact, n, large)


def _bucket_tiles(t):
    key = jnp.arange(t, dtype=jnp.int32)[:, None]
    qry = jnp.arange(t, dtype=jnp.int32)[None, :]
    return jnp.stack([_t5_bucket(key - qry - d * t) for d in range(3)])


def _rope_tables():
    half = ROPE_DIM // 2
    inv = ROPE_THETA ** (-jnp.arange(half, dtype=F32) / half)
    ang = jnp.arange(SEQ, dtype=jnp.int32).astype(F32)[:, None] * inv[None, :]
    cos, sin = jnp.cos(ang), jnp.sin(ang)
    pad = jnp.zeros((SEQ, LANE - ROPE_DIM), F32)
    return jnp.concatenate([cos, cos, pad], axis=1), jnp.concatenate([-sin, sin, pad], axis=1)


def _swap_halves(w):
    half = w.shape[-1] // 2
    return jnp.concatenate([w[..., half:], w[..., :half]], axis=-1)


def _cast_kernel(x_ref, o_ref):
    o_ref[...] = x_ref[...].astype(o_ref.dtype)


def _cast_bf16(w_stack, l, *, rows):
    _, r, c = w_stack.shape
    return pl.pallas_call(
        _cast_kernel,
        grid=(r // rows,),
        in_specs=[pl.BlockSpec((None, rows, c), lambda i: (l, i, 0))],
        out_specs=pl.BlockSpec((rows, c), lambda i: (i, 0)),
        out_shape=jax.ShapeDtypeStruct((r, c), BF16),
        compiler_params=_params(("arbitrary",), 2 * rows * c * 6),
        name="cast_bf16",
    )(w_stack)


def _pack_w_in_kernel(w_ref, o_ref):
    off = np.cumsum((0,) + IN_SIZES)
    seg = lambda k: w_ref[off[k]:off[k + 1], :]
    qa, ka, va, qi, ki, wi, qb, kb, vb, fl, cq, ckv, kr, gl = [seg(k) for k in range(len(IN_SIZES))]
    z = lambda n: jnp.zeros((n, w_ref.shape[1]), F32)
    half = ROPE_DIM // 2
    groups = [
        (OFF_QI, [qi]), (OFF_QB, [qb]), (OFF_KB, [kb]), (OFF_VB, [vb]), (OFF_GL, [gl]), (OFF_QA, [qa]),
        (OFF_CQ, [cq, z(CQ_PAD - Q_LORA)]), (OFF_KA, [ka]), (OFF_VA, [va]), (OFF_KI_LO, [ki, z(64)]),
        (OFF_CKV, [ckv]), (OFF_KR, [kr, z(64)]), (OFF_KRS, [kr[half:], kr[:half], z(64)]),
        (OFF_MISC, [wi, fl, z(LANE - IDX_HEADS - B_HEADS)]), (OFF_KI_HI, [z(64), ki]),
    ]
    for start, pieces in groups:
        block = pieces[0] if len(pieces) == 1 else jnp.concatenate(pieces, axis=0)
        o_ref[start:start + block.shape[0], :] = block.astype(o_ref.dtype)


def _pack_w_in(w_in, l, *, cols=256):
    w_t = jnp.swapaxes(w_in, 1, 2)
    _, n, d = w_t.shape
    return pl.pallas_call(
        _pack_w_in_kernel,
        grid=(d // cols,),
        in_specs=[pl.BlockSpec((None, n, cols), lambda i: (l, 0, i))],
        out_specs=pl.BlockSpec((N_PACK, cols), lambda i: (0, i)),
        out_shape=jax.ShapeDtypeStruct((N_PACK, d), BF16),
        compiler_params=_params(("arbitrary",), 2 * cols * (n * 4 + N_PACK * 2)),
        name="pack_w_in",
    )(w_t)


def _pack_w_uq(w):
    w = jnp.pad(w, ((0, CQ_PAD - Q_LORA), (0, 0))).reshape(CQ_PAD, C_HEADS, NOPE_DIM + ROPE_DIM)
    z = jnp.zeros((CQ_PAD, C_HEADS, LANE - ROPE_DIM), w.dtype)
    rope = w[..., NOPE_DIM:]
    wq1 = jnp.concatenate([w, z], axis=-1).reshape(CQ_PAD, -1)
    wq2 = jnp.concatenate([_swap_halves(rope), z], axis=-1).reshape(CQ_PAD, -1)
    return wq1.astype(BF16), wq2.astype(BF16)


def kernel(x, norm_mix_g, w_in, b_forget, g_cq, g_ckv, w_uq, w_ukv, w_branch_a, w_branch_b, w_branch_c,
           w_o, norm_ffn_g, w_up, conv_w, conv_b, w_down, t5_bias, final_g):
    batch, seq, d = x.shape
    assert (seq, d) == (SEQ, D_MODEL)
    x2 = x.reshape(batch * seq, d)

    bias_tiles = _t5_tiles(_bucket_tiles(ATT_T), t5_bias)
    cos_t, sin_t = _rope_tables()
    final_row = final_g.reshape(1, D_MODEL)

    for l in range(DEPTH):
        proj, misc = _inproj(x2, norm_mix_g[l].reshape(1, D_MODEL), _pack_w_in(w_in, l))

        fbias = jnp.zeros((1, LANE), F32).at[0, MISC_FL:MISC_FL + B_HEADS].set(b_forget[l])
        ccol, crow = _forget_cumsum(misc, fbias, batch)

        wq1, wq2 = _pack_w_uq(w_uq[l])
        gq = jnp.pad(g_cq[l], (0, CQ_PAD - Q_LORA)).reshape(1, CQ_PAD)
        qc, kc, vc = _mla_prep(proj, gq, g_ckv[l].reshape(1, KV_LORA), wq1, wq2,
                               w_ukv[l].astype(BF16), cos_t, sin_t)

        ya = _dsa_attention(proj, misc, bias_tiles, batch)
        yb = _fox_attention(proj, ccol, crow, batch)
        yc = _mla_attention(qc, kc, vc, batch)

        x2 = _merge(x2, ya, yb, yc, proj, _cast_bf16(w_branch_a, l, rows=512),
                    _cast_bf16(w_branch_b, l, rows=512), _cast_bf16(w_branch_c, l, rows=512),
                    _cast_bf16(w_o, l, rows=512))

        cw = jnp.concatenate([conv_w[l], conv_b[l][None, :], jnp.zeros((4, D_FF), F32)], axis=0)
        x2 = _ffn(x2, norm_ffn_g[l].reshape(1, D_MODEL), _cast_bf16(w_up, l, rows=128), cw,
                  _cast_bf16(w_down, l, rows=D_FF // 8), final_row, final_norm=(l == DEPTH - 1))

    return x2.reshape(batch, seq, d)
```

```python
import functools
import math

import jax
import jax.numpy as jnp
import numpy as np
from jax import lax
from jax.experimental import pallas as pl
from jax.experimental.pallas import tpu as pltpu

F32 = jnp.float32
BF16 = jnp.bfloat16

D_MODEL = 2048
SEQ = 2048
DEPTH = 2
CHUNK = 64
HEAD_DIM = 128
EPS = 1e-6
NEG_INF = -1e30

A_HEADS = 4
IDX_HEADS = 16
IDX_DIM = 64
TOPK_MAX = 256
NUM_BUCKETS = 32
MAX_DISTANCE = 128
B_HEADS = 8
C_HEADS = 4
Q_LORA = 448
KV_LORA = 128
NOPE_DIM = 128
ROPE_DIM = 64
V_DIM = 128
ROPE_THETA = 10000.0
D_FF = 5632

IN_SIZES = (
    A_HEADS * HEAD_DIM, HEAD_DIM, HEAD_DIM,
    IDX_HEADS * IDX_DIM, IDX_DIM, IDX_HEADS,
    B_HEADS * HEAD_DIM, B_HEADS * HEAD_DIM, B_HEADS * HEAD_DIM, B_HEADS,
    Q_LORA, KV_LORA, ROPE_DIM,
    3 * D_MODEL,
)

LOG2E = math.log2(math.e)
Q_SCALE = HEAD_DIM ** -0.5 * LOG2E
Q_SCALE_MLA = (NOPE_DIM + ROPE_DIM) ** -0.5 * LOG2E
LANE = 128
V7X_VMEM_BYTES = 64 * 1024 * 1024

OFF_QI = 0
OFF_QB = 1024
OFF_KB = 2048
OFF_VB = 3072
OFF_GL = 4096
OFF_QA = 10240
OFF_CQ = 10752
OFF_KA = 11264
OFF_VA = 11392
OFF_KI_LO = 11520
OFF_CKV = 11648
OFF_KR = 11776
OFF_KRS = 11904
OFF_MISC = 12032
OFF_KI_HI = 12160
N_PACK = 12288
CQ_PAD = 512
MISC_WI = 0
MISC_FL = IDX_HEADS

ATT_T = 256
SELECT_MIN = -1e29
BISECT_MAX_ITERS = 512
SEARCH_STEPS_PER_CHECK = 2
DSA_ONES_ROWS = 16


def _vmem_limit(estimate_bytes):
    limit = V7X_VMEM_BYTES - (6 << 20)
    assert estimate_bytes <= limit, estimate_bytes
    return limit


def _params(semantics, vmem_estimate):
    return pltpu.CompilerParams(dimension_semantics=semantics,
                                vmem_limit_bytes=_vmem_limit(vmem_estimate))


def _rmsnorm_rows(x_ref, g_ref, out_ref, rows):
    def body(c, carry):
        r = pl.multiple_of(c * 128, 128)
        x = x_ref[pl.ds(r, 128), :]
        ms = jnp.mean(x * x, axis=-1, keepdims=True)
        out_ref[pl.ds(r, 128), :] = (x * lax.rsqrt(ms + EPS) * g_ref[...]).astype(out_ref.dtype)
        return carry
    lax.fori_loop(0, rows // 128, body, 0)


def _inproj_kernel(x_ref, g_ref, w_ref, o_ref, misc_ref, h_ref, *, tm, tn):
    j = pl.program_id(1)

    @pl.when(j == 0)
    def _():
        _rmsnorm_rows(x_ref, g_ref, h_ref, tm)

    acc = _nt_dot(h_ref[...], w_ref[...])
    o_ref[...] = acc.astype(o_ref.dtype)

    @pl.when(j == OFF_MISC // tn)
    def _():
        lo = OFF_MISC % tn
        misc_ref[...] = acc[:, lo:lo + LANE]


def _inproj(x2, g, w_pack, *, tm=1024, tn=1024):
    m = x2.shape[0]
    est = 2 * tm * D_MODEL * 4 + tm * D_MODEL * 2 + 2 * D_MODEL * tn * 2 + 2 * tm * tn * 2 + tm * tn * 4
    return pl.pallas_call(
        functools.partial(_inproj_kernel, tm=tm, tn=tn),
        grid=(m // tm, N_PACK // tn),
        in_specs=[
            pl.BlockSpec((tm, D_MODEL), lambda i, j: (i, 0)),
            pl.BlockSpec((1, D_MODEL), lambda i, j: (0, 0)),
            pl.BlockSpec((tn, D_MODEL), lambda i, j: (j, 0)),
        ],
        out_specs=[
            pl.BlockSpec((tm, tn), lambda i, j: (i, j)),
            pl.BlockSpec((tm, LANE), lambda i, j: (i, 0)),
        ],
        out_shape=[
            jax.ShapeDtypeStruct((m, N_PACK), BF16),
            jax.ShapeDtypeStruct((m, LANE), F32),
        ],
        scratch_shapes=[pltpu.VMEM((tm, D_MODEL), BF16)],
        compiler_params=_params(("arbitrary", "arbitrary"), est),
        name="inproj",
    )(x2, g, w_pack)


def _forget_cumsum_kernel(misc_ref, bias_ref, ccol_ref, crow_ref, *, blk):
    rows = lax.broadcasted_iota(jnp.int32, (blk, blk), 0)
    cols = lax.broadcasted_iota(jnp.int32, (blk, blk), 1)
    tri = jnp.where(rows >= cols, 1.0, 0.0).astype(BF16)
    carry = jnp.zeros((1, LANE), F32)
    for c in range(SEQ // blk):
        z = misc_ref[c * blk:(c + 1) * blk, :] + bias_ref[...]
        lf = jnp.minimum(z, 0.0) - jnp.log1p(jnp.exp(-jnp.abs(z)))
        p0 = lf.astype(BF16)
        r1 = lf - p0.astype(F32)
        p1 = r1.astype(BF16)
        p2 = (r1 - p1.astype(F32)).astype(BF16)
        cs = (jnp.dot(tri, p0, preferred_element_type=F32)
              + jnp.dot(tri, p1, preferred_element_type=F32)
              + jnp.dot(tri, p2, preferred_element_type=F32)) + carry
        ccol_ref[c * blk:(c + 1) * blk, :] = cs
        crow_ref[:, c * blk:(c + 1) * blk] = cs.T[MISC_FL:MISC_FL + B_HEADS, :]
        carry = cs[blk - 1:blk, :]


def _forget_cumsum(misc, bias_row, batch, *, blk=256):
    est = 4 * SEQ * LANE * 4 + 2 * 8 * SEQ * 4
    return pl.pallas_call(
        functools.partial(_forget_cumsum_kernel, blk=blk),
        grid=(batch,),
        in_specs=[
            pl.BlockSpec((SEQ, LANE), lambda b: (b, 0)),
            pl.BlockSpec((1, LANE), lambda b: (0, 0)),
        ],
        out_specs=[
            pl.BlockSpec((SEQ, LANE), lambda b: (b, 0)),
            pl.BlockSpec((None, B_HEADS, SEQ), lambda b: (b, 0, 0)),
        ],
        out_shape=[
            jax.ShapeDtypeStruct((batch * SEQ, LANE), F32),
            jax.ShapeDtypeStruct((batch, B_HEADS, SEQ), F32),
        ],
        compiler_params=_params(("arbitrary",), est),
        name="forget_cumsum",
    )(misc, bias_row)


def _mla_prep_kernel(cq_ref, ckv_ref, kr_ref, krs_ref, gq_ref, gkv_ref, wq1_ref, wq2_ref, wkv_ref,
                     cos_ref, sin_ref, qc_ref, kc_ref, vc_ref):
    cq = cq_ref[...].astype(F32)
    ms = jnp.sum(cq * cq, axis=-1, keepdims=True) * (1.0 / Q_LORA)
    cqn = (cq * lax.rsqrt(ms + EPS) * gq_ref[...]).astype(BF16)
    ckv = ckv_ref[...].astype(F32)
    ms2 = jnp.mean(ckv * ckv, axis=-1, keepdims=True)
    ckvn = (ckv * lax.rsqrt(ms2 + EPS) * gkv_ref[...]).astype(BF16)

    q1 = jnp.dot(cqn, wq1_ref[...], preferred_element_type=F32) * Q_SCALE_MLA
    q2 = jnp.dot(cqn, wq2_ref[...], preferred_element_type=F32) * Q_SCALE_MLA
    kv = jnp.dot(ckvn, wkv_ref[...], preferred_element_type=F32)
    cos = cos_ref[...]
    sin = sin_ref[...]
    k_rope = (kr_ref[...].astype(F32) * cos + krs_ref[...].astype(F32) * sin).astype(BF16)
    for h in range(C_HEADS):
        qw = NOPE_DIM + LANE
        qc_ref[:, h * qw:h * qw + NOPE_DIM] = q1[:, h * qw:h * qw + NOPE_DIM].astype(BF16)
        q_rope = q1[:, h * qw + NOPE_DIM:(h + 1) * qw] * cos + q2[:, h * LANE:(h + 1) * LANE] * sin
        qc_ref[:, h * qw + NOPE_DIM:(h + 1) * qw] = q_rope.astype(BF16)
        kw = NOPE_DIM + V_DIM
        kc_ref[:, h * qw:h * qw + NOPE_DIM] = kv[:, h * kw:h * kw + NOPE_DIM].astype(BF16)
        kc_ref[:, h * qw + NOPE_DIM:(h + 1) * qw] = k_rope
        vc_ref[:, h * V_DIM:(h + 1) * V_DIM] = kv[:, h * kw + NOPE_DIM:(h + 1) * kw].astype(BF16)


def _mla_prep(proj, gq, gkv, wq1, wq2, wkv, cos_t, sin_t, *, tm=512):
    m = proj.shape[0]
    nseq = SEQ // tm
    qw = C_HEADS * (NOPE_DIM + LANE)
    est = 2 * (tm * 1024 * 2) + 2 * (CQ_PAD * qw * 2 + CQ_PAD * 512 * 2 + 128 * 1024 * 2) \
        + 2 * (2 * tm * qw * 2 + tm * 512 * 2) + 3 * tm * qw * 4
    const = lambda i: (0, 0)
    return pl.pallas_call(
        _mla_prep_kernel,
        grid=(m // tm,),
        in_specs=[
            pl.BlockSpec((tm, CQ_PAD), lambda i: (i, OFF_CQ // CQ_PAD)),
            pl.BlockSpec((tm, LANE), lambda i: (i, OFF_CKV // LANE)),
            pl.BlockSpec((tm, LANE), lambda i: (i, OFF_KR // LANE)),
            pl.BlockSpec((tm, LANE), lambda i: (i, OFF_KRS // LANE)),
            pl.BlockSpec((1, CQ_PAD), const),
            pl.BlockSpec((1, KV_LORA), const),
            pl.BlockSpec((CQ_PAD, qw), const),
            pl.BlockSpec((CQ_PAD, C_HEADS * LANE), const),
            pl.BlockSpec((KV_LORA, C_HEADS * (NOPE_DIM + V_DIM)), const),
            pl.BlockSpec((tm, LANE), lambda i: (i % nseq, 0)),
            pl.BlockSpec((tm, LANE), lambda i: (i % nseq, 0)),
        ],
        out_specs=[
            pl.BlockSpec((tm, qw), lambda i: (i, 0)),
            pl.BlockSpec((tm, qw), lambda i: (i, 0)),
            pl.BlockSpec((tm, C_HEADS * V_DIM), lambda i: (i, 0)),
        ],
        out_shape=[
            jax.ShapeDtypeStruct((m, qw), BF16),
            jax.ShapeDtypeStruct((m, qw), BF16),
            jax.ShapeDtypeStruct((m, C_HEADS * V_DIM), BF16),
        ],
        compiler_params=_params(("arbitrary",), est),
        name="mla_prep",
    )(proj, proj, proj, proj, gq, gkv, wq1, wq2, wkv, cos_t, sin_t)


def _paired_loop(n, body, init):
    def pair(p, carry):
        return body(2 * p + 1, body(2 * p, carry))
    carry = lax.fori_loop(0, n // 2, pair, init)
    return lax.fori_loop(2 * (n // 2), n, body, carry)


def _nt_dot(a, b):
    return lax.dot_general(a, b, (((1,), (1,)), ((), ())), preferred_element_type=F32)


def _two_pass_attention(i, *, n_heads, tq, tk, dv, logits_fn, mask_fn, v_fn, store_fn,
                        s_ref, mx_ref, acc_ref):
    nc = tk // LANE
    n_full = (i * tq) // tk
    nkb = ((i + 1) * tq + tk - 1) // tk

    mx_ref[...] = jnp.full(mx_ref.shape, NEG_INF, F32)

    def pass1(kb, masked):
        ks = pl.multiple_of(kb * tk, tk)
        mask = mask_fn(ks) if masked else None
        for h in range(n_heads):
            s = logits_fn(h, ks)
            if masked:
                s = jnp.where(mask, s, NEG_INF)
            s_ref[h, :, pl.ds(ks, tk)] = s
            mx = mx_ref[h]
            for c in range(nc):
                mx = jnp.maximum(mx, s[:, c * LANE:(c + 1) * LANE])
            mx_ref[h] = mx

    def pass1_full(kb, carry):
        pass1(kb, False)
        return carry

    def pass1_masked(kb, carry):
        pass1(kb, True)
        return carry

    lax.fori_loop(0, n_full, pass1_full, 0)
    lax.fori_loop(n_full, nkb, pass1_masked, 0)

    for h in range(n_heads):
        mx_ref[h] = jnp.broadcast_to(jnp.max(mx_ref[h], axis=-1, keepdims=True), (tq, LANE))
    acc_ref[...] = jnp.zeros(acc_ref.shape, F32)
    ones = jnp.ones((tk, LANE), BF16)

    def pass2(kb, carry):
        ks = pl.multiple_of(kb * tk, tk)
        for h in range(n_heads):
            m = mx_ref[h]
            p = jnp.concatenate(
                [jnp.exp2(s_ref[h, :, pl.ds(ks + c * LANE, LANE)] - m).astype(BF16) for c in range(nc)],
                axis=1)
            v1 = jnp.concatenate([v_fn(h, ks), ones], axis=1)
            acc_ref[h] += jnp.dot(p, v1, preferred_element_type=F32)
        return carry

    lax.fori_loop(0, nkb, pass2, 0)

    for h in range(n_heads):
        acc = acc_ref[h]
        store_fn(h, acc[:, :dv] / acc[:, dv:])


def _tile_iotas(tq, tk):
    return (lax.broadcasted_iota(jnp.int32, (tq, tk), 0), lax.broadcasted_iota(jnp.int32, (tq, tk), 1))


def _fox_kernel(q_ref, k_ref, v_ref, ccol_ref, crow_ref, o_ref,
                s_ref, mx_ref, acc_ref, cq_ref, *, tq, tk):
    i = pl.program_id(1)
    nc = tk // LANE
    head = lambda h: slice(h * HEAD_DIM, (h + 1) * HEAD_DIM)
    for h in range(B_HEADS):
        cq_ref[h] = jnp.broadcast_to(ccol_ref[:, MISC_FL + h:MISC_FL + h + 1] * LOG2E, (tq, LANE))

    def logits_fn(h, ks):
        s = _nt_dot(q_ref[:, head(h)], k_ref[pl.ds(ks, tk), head(h)])
        return s + jnp.concatenate([cq_ref[h]] * nc, axis=1) - crow_ref[h:h + 1, pl.ds(ks, tk)] * LOG2E

    def mask_fn(ks):
        rows, cols = _tile_iotas(tq, tk)
        return cols + (ks - i * tq) <= rows

    def store_fn(h, y):
        o_ref[:, head(h)] = y.astype(o_ref.dtype)

    _two_pass_attention(i, n_heads=B_HEADS, tq=tq, tk=tk, dv=HEAD_DIM, logits_fn=logits_fn,
                        mask_fn=mask_fn, v_fn=lambda h, ks: v_ref[pl.ds(ks, tk), head(h)],
                        store_fn=store_fn, s_ref=s_ref, mx_ref=mx_ref, acc_ref=acc_ref)


def _fox_attention(proj, ccol, crow, batch, *, tq=256, tk=512):
    m = proj.shape[0]
    nq = SEQ // tq
    w = B_HEADS * HEAD_DIM
    state = B_HEADS * tq * LANE * 4
    est = (4 * tq * w * 2 + 4 * SEQ * w * 2 + 2 * tq * LANE * 4 + 2 * 8 * SEQ * 4
           + B_HEADS * tq * SEQ * 4 + 4 * state + 6 * tq * tk * 4)
    return pl.pallas_call(
        functools.partial(_fox_kernel, tq=tq, tk=tk),
        grid=(batch, nq),
        in_specs=[
            pl.BlockSpec((tq, w), lambda b, i: (b * nq + i, OFF_QB // w)),
            pl.BlockSpec((SEQ, w), lambda b, i: (b, OFF_KB // w)),
            pl.BlockSpec((SEQ, w), lambda b, i: (b, OFF_VB // w)),
            pl.BlockSpec((tq, LANE), lambda b, i: (b * nq + i, 0)),
            pl.BlockSpec((None, B_HEADS, SEQ), lambda b, i: (b, 0, 0)),
        ],
        out_specs=pl.BlockSpec((tq, w), lambda b, i: (b * nq + i, 0)),
        out_shape=jax.ShapeDtypeStruct((m, w), BF16),
        scratch_shapes=[
            pltpu.VMEM((B_HEADS, tq, SEQ), F32),
            pltpu.VMEM((B_HEADS, tq, LANE), F32),
            pltpu.VMEM((B_HEADS, tq, HEAD_DIM + LANE), F32),
            pltpu.VMEM((B_HEADS, tq, LANE), F32),
        ],
        compiler_params=_params(("arbitrary", "arbitrary"), est),
        name="fox_attention",
    )(proj, proj, proj, ccol, crow)


def _mla_kernel(q_ref, k_ref, v_ref, o_ref, s_ref, mx_ref, acc_ref, *, tq, tk):
    i = pl.program_id(1)
    qw = NOPE_DIM + LANE
    qhead = lambda h: slice(h * qw, (h + 1) * qw)
    vhead = lambda h: slice(h * V_DIM, (h + 1) * V_DIM)

    def logits_fn(h, ks):
        return _nt_dot(q_ref[:, qhead(h)], k_ref[pl.ds(ks, tk), qhead(h)])

    def mask_fn(ks):
        rows, cols = _tile_iotas(tq, tk)
        return (cols + ks) // CHUNK <= (rows + i * tq) // CHUNK

    def store_fn(h, y):
        o_ref[:, vhead(h)] = y.astype(o_ref.dtype)

    _two_pass_attention(i, n_heads=C_HEADS, tq=tq, tk=tk, dv=V_DIM, logits_fn=logits_fn,
                        mask_fn=mask_fn, v_fn=lambda h, ks: v_ref[pl.ds(ks, tk), vhead(h)],
                        store_fn=store_fn, s_ref=s_ref, mx_ref=mx_ref, acc_ref=acc_ref)


def _mla_attention(qc, kc, vc, batch, *, tq=256, tk=512):
    m = qc.shape[0]
    nq = SEQ // tq
    qw = C_HEADS * (NOPE_DIM + LANE)
    vw = C_HEADS * V_DIM
    state = C_HEADS * tq * LANE * 4
    est = (2 * tq * qw * 2 + 2 * SEQ * qw * 2 + 2 * SEQ * vw * 2 + 2 * tq * vw * 2
           + C_HEADS * tq * SEQ * 4 + 3 * state + 6 * tq * tk * 4)
    return pl.pallas_call(
        functools.partial(_mla_kernel, tq=tq, tk=tk),
        grid=(batch, nq),
        in_specs=[
            pl.BlockSpec((tq, qw), lambda b, i: (b * nq + i, 0)),
            pl.BlockSpec((SEQ, qw), lambda b, i: (b, 0)),
            pl.BlockSpec((SEQ, vw), lambda b, i: (b, 0)),
        ],
        out_specs=pl.BlockSpec((tq, vw), lambda b, i: (b * nq + i, 0)),
        out_shape=jax.ShapeDtypeStruct((m, vw), BF16),
        scratch_shapes=[
            pltpu.VMEM((C_HEADS, tq, SEQ), F32),
            pltpu.VMEM((C_HEADS, tq, LANE), F32),
            pltpu.VMEM((C_HEADS, tq, V_DIM + LANE), F32),
        ],
        compiler_params=_params(("arbitrary", "arbitrary"), est),
        name="mla_attention",
    )(qc, kc, vc)


def _t5_tiles_kernel(bucket_ref, table_ref, o_ref):
    for d in range(3):
        bucket = bucket_ref[d]
        for h in range(A_HEADS):
            acc = jnp.zeros(bucket.shape, F32)
            for nb in range(NUM_BUCKETS):
                acc = jnp.where(bucket == nb, table_ref[nb, h] * LOG2E, acc)
            o_ref[d, h] = acc


def _t5_tiles(bucket_tiles, t5_bias, *, t=ATT_T):
    return pl.pallas_call(
        _t5_tiles_kernel,
        in_specs=[
            pl.BlockSpec(memory_space=pltpu.VMEM),
            pl.BlockSpec(memory_space=pltpu.SMEM),
        ],
        out_specs=pl.BlockSpec(memory_space=pltpu.VMEM),
        out_shape=jax.ShapeDtypeStruct((3, A_HEADS, t, t), F32),
        name="t5_tiles",
    )(bucket_tiles, t5_bias)


def _dsa_kernel(qi_ref, klo_ref, khi_ref, misc_ref, qa_ref, ka_ref, va_ref, bias_ref, o_ref,
                sc_ref, lg_ref, acc_ref, *, t):
    i = pl.program_id(1)
    groups = t // 8
    key = lax.broadcasted_iota(jnp.int32, (t, t), 0)
    qry = lax.broadcasted_iota(jnp.int32, (t, t), 1)
    admissible = (key // CHUNK) <= (qry // CHUNK)

    def fold(x, op):
        return op(x.reshape(groups, 8, t), axis=0)

    def over_keys(x8, op):
        return jnp.broadcast_to(op(x8, axis=0, keepdims=True), (8, t))

    w_t = misc_ref[...].T[MISC_WI:MISC_WI + IDX_HEADS, :] * (IDX_HEADS ** -0.5 * IDX_DIM ** -0.5)

    def block_scores(kb):
        ks = pl.multiple_of(kb * t, t)
        klo = klo_ref[pl.ds(ks, t), :]
        khi = khi_ref[pl.ds(ks, t), :]
        acc = jnp.zeros((t, t), F32)
        for pair in range(IDX_HEADS // 2):
            qp = qi_ref[:, pair * LANE:(pair + 1) * LANE]
            for sub, kk in enumerate((klo, khi)):
                h = 2 * pair + sub
                acc = acc + jnp.maximum(_nt_dot(kk, qp), 0.0) * w_t[h:h + 1, :]
        return ks, acc

    def score_body(kb, carry):
        mn, mx = carry
        ks, acc = block_scores(kb)
        sc_ref[pl.ds(ks, t), :] = acc
        return jnp.minimum(mn, fold(acc, jnp.min)), jnp.maximum(mx, fold(acc, jnp.max))

    mn, mx = _paired_loop(i, score_body,
                          (jnp.full((8, t), -SELECT_MIN, F32), jnp.full((8, t), SELECT_MIN, F32)))
    ks, acc = block_scores(i)
    sc_ref[pl.ds(ks, t), :] = jnp.where(admissible, acc, NEG_INF)
    mn = jnp.minimum(mn, fold(jnp.where(admissible, acc, -SELECT_MIN), jnp.min))
    mx = jnp.maximum(mx, fold(jnp.where(admissible, acc, SELECT_MIN), jnp.max))

    def count_ge(thr):
        def body(kb, cnt):
            ks = pl.multiple_of(kb * t, t)
            hit = jnp.where(sc_ref[pl.ds(ks, t), :].reshape(groups, 8, t) >= thr[None], 1.0, 0.0)
            return cnt + jnp.sum(hit, axis=0)
        return over_keys(_paired_loop(i + 1, body, jnp.zeros((8, t), F32)), jnp.sum)

    k_sel = float(TOPK_MAX)
    search = i * t >= TOPK_MAX
    lo0 = jnp.where(search, over_keys(mn, jnp.min), SELECT_MIN)
    hi0 = jnp.where(search, over_keys(mx, jnp.max), SELECT_MIN)

    def midpoint(lo, hi):
        return 0.5 * lo + 0.5 * hi

    def any_active(lo, hi):
        mid = midpoint(lo, hi)
        return jnp.max(jnp.where((mid > lo) & (mid < hi), 1.0, 0.0)) > 0.5

    def search_cond(state):
        return jnp.logical_and(state[0] < BISECT_MAX_ITERS, state[1])

    def search_body(state):
        it, _, lo, hi, c_lo = state
        for _ in range(SEARCH_STEPS_PER_CHECK):
            mid = midpoint(lo, hi)
            cnt = count_ge(mid)
            ge = cnt >= k_sel
            lo = jnp.where(ge, mid, lo)
            c_lo = jnp.where(ge, cnt, c_lo)
            hi = jnp.where(cnt > k_sel, hi, mid)
        return it + SEARCH_STEPS_PER_CHECK, any_active(lo, hi), lo, hi, c_lo

    n_adm = (((i * t + qry[:8]) // CHUNK + 1) * CHUNK).astype(F32)
    _, _, lo, hi, c_lo = lax.while_loop(search_cond, search_body,
                                        (jnp.int32(0), any_active(lo0, hi0), lo0, hi0, n_adm))
    c_hi = count_ge(hi)
    thr = jnp.where(c_hi >= k_sel, hi, lo)
    c_thr = jnp.where(c_hi >= k_sel, c_hi, c_lo)

    @pl.when(jnp.max(jnp.where(c_thr > k_sel, 1.0, 0.0)) > 0.5)
    def _():
        key_in_block = (lax.broadcasted_iota(jnp.int32, (groups, 8, t), 0) * 8
                        + lax.broadcasted_iota(jnp.int32, (groups, 8, t), 1))

        def count_where(pred):
            def body(kb, cnt):
                ks = pl.multiple_of(kb * t, t)
                blk = sc_ref[pl.ds(ks, t), :].reshape(groups, 8, t)
                return cnt + jnp.sum(jnp.where(pred(blk, ks), 1.0, 0.0), axis=0)
            return over_keys(lax.fori_loop(0, i + 1, body, jnp.zeros((8, t), F32)), jnp.sum)

        def tied_before(bound):
            return lambda blk, ks: (blk == thr[None]) & ((key_in_block + ks).astype(F32) < bound[None])

        keep = k_sel - count_where(lambda blk, ks: blk > thr[None])

        def cut_body(_, bounds):
            below, above = bounds
            mid = jnp.floor(0.5 * (below + above))
            enough = count_where(tied_before(mid)) >= keep
            return jnp.where(enough, below, mid), jnp.where(enough, mid, above)

        _, cut = lax.fori_loop(0, SEQ.bit_length(), cut_body,
                               (jnp.zeros((8, t), F32), jnp.full((8, t), float(SEQ), F32)))

        def drop_body(kb, carry):
            ks = pl.multiple_of(kb * t, t)
            blk = sc_ref[pl.ds(ks, t), :].reshape(groups, 8, t)
            drop = (blk == thr[None]) & ((key_in_block + ks).astype(F32) >= cut[None])
            sc_ref[pl.ds(ks, t), :] = jnp.where(drop, NEG_INF, blk).reshape(t, t)
            return carry

        lax.fori_loop(0, i + 1, drop_body, 0)

    head = lambda h: slice(h * HEAD_DIM, (h + 1) * HEAD_DIM)
    acc_ref[...] = jnp.zeros(acc_ref.shape, F32)

    def logits_pass(kb, mx):
        ks = pl.multiple_of(kb * t, t)
        dist = jnp.minimum(i - kb, 2)
        k_blk = ka_ref[pl.ds(ks, t), :]
        sel = sc_ref[pl.ds(ks, t), :] >= thr[0:1, :]
        out = []
        for h in range(A_HEADS):
            s = _nt_dot(k_blk, qa_ref[:, head(h)]) + bias_ref[dist, h]
            s = jnp.where(sel, s, NEG_INF)
            lg_ref[h, pl.ds(ks, t), :] = s
            out.append(jnp.maximum(mx[h], fold(s, jnp.max)))
        return tuple(out)

    mx = _paired_loop(i + 1, logits_pass,
                      tuple(jnp.full((8, t), NEG_INF, F32) for _ in range(A_HEADS)))
    m = [over_keys(mx[h], jnp.max)[0:1, :] for h in range(A_HEADS)]

    ones_rows = jnp.ones((DSA_ONES_ROWS, t), BF16)

    def value_pass(kb, carry):
        ks = pl.multiple_of(kb * t, t)
        v_t = va_ref[pl.ds(ks, t), :].astype(F32).T.astype(BF16)
        v1_t = jnp.concatenate([v_t, ones_rows], axis=0)
        for h in range(A_HEADS):
            p = jnp.exp2(lg_ref[h, pl.ds(ks, t), :] - m[h])
            acc_ref[h] += jnp.dot(v1_t, p.astype(BF16), preferred_element_type=F32)
        return carry

    _paired_loop(i + 1, value_pass, 0)
    for h in range(A_HEADS):
        acc = acc_ref[h]
        o_ref[:, head(h)] = (acc[:HEAD_DIM] / acc[HEAD_DIM:HEAD_DIM + 1]).T.astype(o_ref.dtype)


def _dsa_attention(proj, misc, bias_tiles, batch, *, t=ATT_T):
    m = proj.shape[0]
    nq = SEQ // t
    qiw = IDX_HEADS * IDX_DIM
    qaw = A_HEADS * HEAD_DIM
    est = (2 * t * qiw * 2 + 2 * 4 * SEQ * LANE * 2 + 2 * t * LANE * 4 + 2 * t * qaw * 2
           + 2 * 3 * A_HEADS * t * t * 4 + 2 * t * qaw * 2
           + (1 + A_HEADS) * t * SEQ * 4 + A_HEADS * HEAD_DIM * t * 4 + 12 * t * t * 4)
    kblock = lambda off: pl.BlockSpec((SEQ, LANE), lambda b, i: (b, off // LANE))
    return pl.pallas_call(
        functools.partial(_dsa_kernel, t=t),
        grid=(batch, nq),
        in_specs=[
            pl.BlockSpec((t, qiw), lambda b, i: (b * nq + i, OFF_QI // qiw)),
            kblock(OFF_KI_LO),
            kblock(OFF_KI_HI),
            pl.BlockSpec((t, LANE), lambda b, i: (b * nq + i, 0)),
            pl.BlockSpec((t, qaw), lambda b, i: (b * nq + i, OFF_QA // qaw)),
            kblock(OFF_KA),
            kblock(OFF_VA),
            pl.BlockSpec((3, A_HEADS, t, t), lambda b, i: (0, 0, 0, 0)),
        ],
        out_specs=pl.BlockSpec((t, qaw), lambda b, i: (b * nq + i, 0)),
        out_shape=jax.ShapeDtypeStruct((m, qaw), BF16),
        scratch_shapes=[
            pltpu.VMEM((SEQ, t), F32),
            pltpu.VMEM((A_HEADS, SEQ, t), F32),
            pltpu.VMEM((A_HEADS, HEAD_DIM + DSA_ONES_ROWS, t), F32),
        ],
        compiler_params=_params(("arbitrary", "arbitrary"), est),
        name="dsa_attention",
    )(proj, proj, proj, misc, proj, proj, proj, bias_tiles)


def _merge_kernel(x_ref, ya_ref, yb_ref, yc_ref, ga_ref, gb_ref, gc_ref,
                  wa_ref, wb_ref, wc_ref, wo_ref, o_ref):
    def branch(y_ref, w_ref, g_ref):
        y = jnp.dot(y_ref[...], w_ref[...], preferred_element_type=F32)
        return jax.nn.sigmoid(g_ref[...].astype(F32)) * y

    merged = branch(ya_ref, wa_ref, ga_ref) + branch(yb_ref, wb_ref, gb_ref) + branch(yc_ref, wc_ref, gc_ref)
    o_ref[...] = x_ref[...] + jnp.dot(merged.astype(BF16), wo_ref[...], preferred_element_type=F32)


def _merge(x2, ya, yb, yc, proj, wa, wb, wc, wo, *, tm=256):
    m = x2.shape[0]
    wbytes = (wa.size + wb.size + wc.size + wo.size) * 2
    est = 2 * wbytes + 4 * tm * D_MODEL * 4 + 2 * tm * 2048 * 2 + 6 * tm * D_MODEL * 2 + 4 * tm * D_MODEL * 4
    const = lambda i: (0, 0)
    gate = lambda k: pl.BlockSpec((tm, D_MODEL), lambda i: (i, OFF_GL // D_MODEL + k))
    return pl.pallas_call(
        _merge_kernel,
        grid=(m // tm,),
        in_specs=[
            pl.BlockSpec((tm, D_MODEL), lambda i: (i, 0)),
            pl.BlockSpec((tm, ya.shape[1]), lambda i: (i, 0)),
            pl.BlockSpec((tm, yb.shape[1]), lambda i: (i, 0)),
            pl.BlockSpec((tm, yc.shape[1]), lambda i: (i, 0)),
            gate(0), gate(1), gate(2),
            pl.BlockSpec(wa.shape, const),
            pl.BlockSpec(wb.shape, const),
            pl.BlockSpec(wc.shape, const),
            pl.BlockSpec(wo.shape, const),
        ],
        out_specs=pl.BlockSpec((tm, D_MODEL), lambda i: (i, 0)),
        out_shape=jax.ShapeDtypeStruct((m, D_MODEL), F32),
        compiler_params=_params(("arbitrary",), est),
        name="merge",
    )(x2, ya, yb, yc, proj, proj, proj, wa, wb, wc, wo)


def _ffn_kernel(x_ref, g_ref, wu_ref, wv_ref, cw_ref, wd_ref, fg_ref, o_ref,
                h_ref, ubuf_ref, halo_ref, *, tm, tf, final_norm):
    i = pl.program_id(0)
    j = pl.program_id(1)
    nj = pl.num_programs(1)

    @pl.when(j == 0)
    def _():
        _rmsnorm_rows(x_ref, g_ref, h_ref, tm)
        o_ref[...] = x_ref[...]

        @pl.when((i % (SEQ // tm)) == 0)
        def _():
            halo_ref[...] = jnp.zeros(halo_ref.shape, F32)

    h = h_ref[...]
    u = jnp.dot(h, wu_ref[...], preferred_element_type=F32)
    v = jnp.dot(h, wv_ref[...], preferred_element_type=F32)

    ubuf_ref[0:8, :] = halo_ref[j]
    ubuf_ref[8:, :] = u
    halo_ref[j] = u[tm - 8:, :]
    conv = (cw_ref[0:1, :] * ubuf_ref[6:6 + tm, :] + cw_ref[1:2, :] * ubuf_ref[7:7 + tm, :]
            + cw_ref[2:3, :] * u + cw_ref[3:4, :])
    act = (jax.nn.gelu(conv) * v).astype(BF16)
    o_ref[...] += jnp.dot(act, wd_ref[...], preferred_element_type=F32)

    if final_norm:
        @pl.when(j == nj - 1)
        def _():
            _rmsnorm_rows(o_ref, fg_ref, o_ref, tm)


def _ffn(x2, g, w_up, cw, w_down, final_g, *, final_norm, tm=1024, tf=512):
    m = x2.shape[0]
    nj = D_FF // tf
    est = (3 * tm * D_MODEL * 4 + tm * D_MODEL * 2 + 2 * 2 * D_MODEL * tf * 2 + 2 * tf * D_MODEL * 2
           + (tm + 8) * tf * 4 + nj * 8 * tf * 4 + 4 * tm * tf * 4)
    return pl.pallas_call(
        functools.partial(_ffn_kernel, tm=tm, tf=tf, final_norm=final_norm),
        grid=(m // tm, nj),
        in_specs=[
            pl.BlockSpec((tm, D_MODEL), lambda i, j: (i, 0), pipeline_mode=pl.Buffered(1)),
            pl.BlockSpec((1, D_MODEL), lambda i, j: (0, 0)),
            pl.BlockSpec((D_MODEL, tf), lambda i, j: (0, j)),
            pl.BlockSpec((D_MODEL, tf), lambda i, j: (0, nj + j)),
            pl.BlockSpec((8, tf), lambda i, j: (0, j)),
            pl.BlockSpec((tf, D_MODEL), lambda i, j: (j, 0)),
            pl.BlockSpec((1, D_MODEL), lambda i, j: (0, 0)),
        ],
        out_specs=pl.BlockSpec((tm, D_MODEL), lambda i, j: (i, 0)),
        out_shape=jax.ShapeDtypeStruct((m, D_MODEL), F32),
        scratch_shapes=[
            pltpu.VMEM((tm, D_MODEL), BF16),
            pltpu.VMEM((tm + 8, tf), F32),
            pltpu.VMEM((nj, 8, tf), F32),
        ],
        compiler_params=_params(("arbitrary", "arbitrary"), est),
        name="conv_ffn",
    )(x2, g, w_up, w_up, cw, w_down, final_g)


def _t5_bucket(rel):
    nb = NUM_BUCKETS // 2
    max_exact = nb // 2
    base = jnp.where(rel > 0, nb, 0)
    n = jnp.abs(rel)
    nf = jnp.maximum(n, 1).astype(F32)
    large = max_exact + (jnp.log(nf / max_exact) / math.log(MAX_DISTANCE / max_exact)
                         * (nb - max_exact)).astype(jnp.int32)
    large = jnp.minimum(large, nb - 1)
    return base + jnp.where(n < max_exact, n, large)


def _bucket_tiles(t):
    key = jnp.arange(t, dtype=jnp.int32)[:, None]
    qry = jnp.arange(t, dtype=jnp.int32)[None, :]
    return jnp.stack([_t5_bucket(key - qry - d * t) for d in range(3)])


def _rope_tables():
    half = ROPE_DIM // 2
    inv = ROPE_THETA ** (-jnp.arange(half, dtype=F32) / half)
    ang = jnp.arange(SEQ, dtype=jnp.int32).astype(F32)[:, None] * inv[None, :]
    cos, sin = jnp.cos(ang), jnp.sin(ang)
    pad = jnp.zeros((SEQ, LANE - ROPE_DIM), F32)
    return jnp.concatenate([cos, cos, pad], axis=1), jnp.concatenate([-sin, sin, pad], axis=1)


def _swap_halves(w):
    half = w.shape[-1] // 2
    return jnp.concatenate([w[..., half:], w[..., :half]], axis=-1)


def _cast_kernel(x_ref, o_ref):
    o_ref[...] = x_ref[...].astype(o_ref.dtype)


def _cast_bf16(w_stack, l, *, rows):
    _, r, c = w_stack.shape
    return pl.pallas_call(
        _cast_kernel,
        grid=(r // rows,),
        in_specs=[pl.BlockSpec((None, rows, c), lambda i: (l, i, 0))],
        out_specs=pl.BlockSpec((rows, c), lambda i: (i, 0)),
        out_shape=jax.ShapeDtypeStruct((r, c), BF16),
        compiler_params=_params(("arbitrary",), 2 * rows * c * 6),
        name="cast_bf16",
    )(w_stack)


def _pack_w_in_kernel(w_ref, o_ref):
    off = np.cumsum((0,) + IN_SIZES)
    seg = lambda k: w_ref[off[k]:off[k + 1], :]
    qa, ka, va, qi, ki, wi, qb, kb, vb, fl, cq, ckv, kr, gl = [seg(k) for k in range(len(IN_SIZES))]
    qa, qb = qa * Q_SCALE, qb * Q_SCALE
    z = lambda n: jnp.zeros((n, w_ref.shape[1]), F32)
    half = ROPE_DIM // 2
    groups = [
        (OFF_QI, [qi]), (OFF_QB, [qb]), (OFF_KB, [kb]), (OFF_VB, [vb]), (OFF_GL, [gl]), (OFF_QA, [qa]),
        (OFF_CQ, [cq, z(CQ_PAD - Q_LORA)]), (OFF_KA, [ka]), (OFF_VA, [va]), (OFF_KI_LO, [ki, z(64)]),
        (OFF_CKV, [ckv]), (OFF_KR, [kr, z(64)]), (OFF_KRS, [kr[half:], kr[:half], z(64)]),
        (OFF_MISC, [wi, fl, z(LANE - IDX_HEADS - B_HEADS)]), (OFF_KI_HI, [z(64), ki]),
    ]
    for start, pieces in groups:
        block = pieces[0] if len(pieces) == 1 else jnp.concatenate(pieces, axis=0)
        o_ref[start:start + block.shape[0], :] = block.astype(o_ref.dtype)


def _pack_w_in(w_in, l, *, cols=256):
    w_t = jnp.swapaxes(w_in, 1, 2)
    _, n, d = w_t.shape
    return pl.pallas_call(
        _pack_w_in_kernel,
        grid=(d // cols,),
        in_specs=[pl.BlockSpec((None, n, cols), lambda i: (l, 0, i))],
        out_specs=pl.BlockSpec((N_PACK, cols), lambda i: (0, i)),
        out_shape=jax.ShapeDtypeStruct((N_PACK, d), BF16),
        compiler_params=_params(("arbitrary",), 2 * cols * (n * 4 + N_PACK * 2)),
        name="pack_w_in",
    )(w_t)


def _pack_w_uq(w):
    w = jnp.pad(w, ((0, CQ_PAD - Q_LORA), (0, 0))).reshape(CQ_PAD, C_HEADS, NOPE_DIM + ROPE_DIM)
    z = jnp.zeros((CQ_PAD, C_HEADS, LANE - ROPE_DIM), w.dtype)
    rope = w[..., NOPE_DIM:]
    wq1 = jnp.concatenate([w, z], axis=-1).reshape(CQ_PAD, -1)
    wq2 = jnp.concatenate([_swap_halves(rope), z], axis=-1).reshape(CQ_PAD, -1)
    return wq1.astype(BF16), wq2.astype(BF16)


def kernel(x, norm_mix_g, w_in, b_forget, g_cq, g_ckv, w_uq, w_ukv, w_branch_a, w_branch_b, w_branch_c,
           w_o, norm_ffn_g, w_up, conv_w, conv_b, w_down, t5_bias, final_g):
    batch, seq, d = x.shape
    assert (seq, d) == (SEQ, D_MODEL)
    x2 = x.reshape(batch * seq, d)

    bias_tiles = _t5_tiles(_bucket_tiles(ATT_T), t5_bias)
    cos_t, sin_t = _rope_tables()
    final_row = final_g.reshape(1, D_MODEL)

    for l in range(DEPTH):
        proj, misc = _inproj(x2, norm_mix_g[l].reshape(1, D_MODEL), _pack_w_in(w_in, l))

        fbias = jnp.zeros((1, LANE), F32).at[0, MISC_FL:MISC_FL + B_HEADS].set(b_forget[l])
        ccol, crow = _forget_cumsum(misc, fbias, batch)

        wq1, wq2 = _pack_w_uq(w_uq[l])
        gq = jnp.pad(g_cq[l], (0, CQ_PAD - Q_LORA)).reshape(1, CQ_PAD)
        qc, kc, vc = _mla_prep(proj, gq, g_ckv[l].reshape(1, KV_LORA), wq1, wq2,
                               w_ukv[l].astype(BF16), cos_t, sin_t)

        ya = _dsa_attention(proj, misc, bias_tiles, batch)
        yb = _fox_attention(proj, ccol, crow, batch)
        yc = _mla_attention(qc, kc, vc, batch)

        x2 = _merge(x2, ya, yb, yc, proj, _cast_bf16(w_branch_a, l, rows=512),
                    _cast_bf16(w_branch_b, l, rows=512), _cast_bf16(w_branch_c, l, rows=512),
                    _cast_bf16(w_o, l, rows=512))

        cw = jnp.concatenate([conv_w[l], conv_b[l][None, :], jnp.zeros((4, D_FF), F32)], axis=0)
        x2 = _ffn(x2, norm_ffn_g[l].reshape(1, D_MODEL), _cast_bf16(w_up, l, rows=128), cw,
                  _cast_bf16(w_down, l, rows=D_FF // 8), final_row, final_norm=(l == DEPTH - 1))

    return x2.reshape(batch, seq, d)
```

```python
import functools
import math

import jax
import jax.numpy as jnp
import numpy as np
from jax import lax
from jax.experimental import pallas as pl
from jax.experimental.pallas import tpu as pltpu

F32 = jnp.float32
BF16 = jnp.bfloat16

D_MODEL = 2048
SEQ = 2048
DEPTH = 2
CHUNK = 64
HEAD_DIM = 128
EPS = 1e-6
NEG_INF = -1e30

A_HEADS = 4
IDX_HEADS = 16
IDX_DIM = 64
TOPK_MAX = 256
NUM_BUCKETS = 32
MAX_DISTANCE = 128
B_HEADS = 8
C_HEADS = 4
Q_LORA = 448
KV_LORA = 128
NOPE_DIM = 128
ROPE_DIM = 64
V_DIM = 128
ROPE_THETA = 10000.0
D_FF = 5632

IN_SIZES = (
    A_HEADS * HEAD_DIM, HEAD_DIM, HEAD_DIM,
    IDX_HEADS * IDX_DIM, IDX_DIM, IDX_HEADS,
    B_HEADS * HEAD_DIM, B_HEADS * HEAD_DIM, B_HEADS * HEAD_DIM, B_HEADS,
    Q_LORA, KV_LORA, ROPE_DIM,
    3 * D_MODEL,
)

LOG2E = math.log2(math.e)
Q_SCALE = HEAD_DIM ** -0.5 * LOG2E
Q_SCALE_MLA = (NOPE_DIM + ROPE_DIM) ** -0.5 * LOG2E
LANE = 128
V7X_VMEM_BYTES = 64 * 1024 * 1024

OFF_QI = 0
OFF_QB = 1024
OFF_KB = 2048
OFF_VB = 3072
OFF_GL = 4096
OFF_QA = 10240
OFF_CQ = 10752
OFF_KA = 11264
OFF_VA = 11392
OFF_KI_LO = 11520
OFF_CKV = 11648
OFF_KR = 11776
OFF_KRS = 11904
OFF_MISC = 12032
OFF_KI_HI = 12160
N_PACK = 12288
CQ_PAD = 512
MISC_WI = 0
MISC_FL = IDX_HEADS

ATT_T = 256
SELECT_MIN = -1e29
BISECT_MAX_ITERS = 512
SEARCH_STEPS_PER_CHECK = 2
DSA_ONES_ROWS = 16


def _vmem_limit(estimate_bytes):
    limit = V7X_VMEM_BYTES - (6 << 20)
    assert estimate_bytes <= limit, estimate_bytes
    return limit


def _params(semantics, vmem_estimate):
    return pltpu.CompilerParams(dimension_semantics=semantics,
                                vmem_limit_bytes=_vmem_limit(vmem_estimate))


def _rmsnorm_rows(x_ref, g_ref, out_ref, rows):
    def body(c, carry):
        r = pl.multiple_of(c * 128, 128)
        x = x_ref[pl.ds(r, 128), :]
        ms = jnp.mean(x * x, axis=-1, keepdims=True)
        out_ref[pl.ds(r, 128), :] = (x * lax.rsqrt(ms + EPS) * g_ref[...]).astype(out_ref.dtype)
        return carry
    lax.fori_loop(0, rows // 128, body, 0)


def _inproj_kernel(*refs, tm, tn, n_cast):
    x_ref, g_ref, w_ref = refs[:3]
    cast_in = refs[3:3 + n_cast]
    o_ref, misc_ref = refs[3 + n_cast:5 + n_cast]
    cast_out = refs[5 + n_cast:5 + 2 * n_cast]
    h_ref = refs[5 + 2 * n_cast]
    j = pl.program_id(1)

    @pl.when(j == 0)
    def _():
        _rmsnorm_rows(x_ref, g_ref, h_ref, tm)

    acc = _nt_dot(h_ref[...], w_ref[...])
    o_ref[...] = acc.astype(o_ref.dtype)

    for src, dst in zip(cast_in, cast_out):
        dst[...] = src[...].astype(dst.dtype)

    @pl.when(j == OFF_MISC // tn)
    def _():
        lo = OFF_MISC % tn
        misc_ref[...] = acc[:, lo:lo + LANE]


def _cast_specs(w_stack, l, ni, nj, rows_follow_i, n_chunks):
    _, r, c = w_stack.shape
    assert n_chunks <= nj
    chunk = lambda j: jnp.minimum(j, n_chunks - 1)
    if rows_follow_i:
        block = (r // ni, c // n_chunks)
        idx = lambda i, j: (i, chunk(j))
    else:
        block = (r // n_chunks, c // ni)
        idx = lambda i, j: (chunk(j), i)
    assert block[0] % 16 == 0 and block[1] % LANE == 0, block
    in_spec = pl.BlockSpec((None,) + block, lambda i, j: (l,) + idx(i, j))
    out_spec = pl.BlockSpec(block, idx)
    return in_spec, out_spec, jax.ShapeDtypeStruct((r, c), BF16), block[0] * block[1]


def _inproj(x2, g, w_pack, casts, l, *, tm=1024, tn=1024):
    m = x2.shape[0]
    ni, nj = m // tm, N_PACK // tn
    plans = [_cast_specs(w, l, ni, nj, rows_i, n) for w, rows_i, n in casts]
    est = (2 * tm * D_MODEL * 4 + tm * D_MODEL * 2 + 2 * D_MODEL * tn * 2 + 2 * tm * tn * 2 + tm * tn * 4
           + sum(2 * 6 * p[3] for p in plans))
    return pl.pallas_call(
        functools.partial(_inproj_kernel, tm=tm, tn=tn, n_cast=len(casts)),
        grid=(ni, nj),
        in_specs=[
            pl.BlockSpec((tm, D_MODEL), lambda i, j: (i, 0)),
            pl.BlockSpec((1, D_MODEL), lambda i, j: (0, 0)),
            pl.BlockSpec((tn, D_MODEL), lambda i, j: (j, 0)),
        ] + [p[0] for p in plans],
        out_specs=[
            pl.BlockSpec((tm, tn), lambda i, j: (i, j)),
            pl.BlockSpec((tm, LANE), lambda i, j: (i, 0)),
        ] + [p[1] for p in plans],
        out_shape=[
            jax.ShapeDtypeStruct((m, N_PACK), BF16),
            jax.ShapeDtypeStruct((m, LANE), F32),
        ] + [p[2] for p in plans],
        scratch_shapes=[pltpu.VMEM((tm, D_MODEL), BF16)],
        compiler_params=_params(("arbitrary", "arbitrary"), est),
        name="inproj",
    )(x2, g, w_pack, *[w for w, _, _ in casts])


def _forget_cumsum_kernel(misc_ref, bias_ref, ccol_ref, crow_ref, *, blk):
    rows = lax.broadcasted_iota(jnp.int32, (blk, blk), 0)
    cols = lax.broadcasted_iota(jnp.int32, (blk, blk), 1)
    tri = jnp.where(rows >= cols, 1.0, 0.0).astype(BF16)
    carry = jnp.zeros((1, LANE), F32)
    for c in range(SEQ // blk):
        z = misc_ref[c * blk:(c + 1) * blk, :] + bias_ref[...]
        lf = jnp.minimum(z, 0.0) - jnp.log1p(jnp.exp(-jnp.abs(z)))
        p0 = lf.astype(BF16)
        r1 = lf - p0.astype(F32)
        p1 = r1.astype(BF16)
        p2 = (r1 - p1.astype(F32)).astype(BF16)
        cs = (jnp.dot(tri, p0, preferred_element_type=F32)
              + jnp.dot(tri, p1, preferred_element_type=F32)
              + jnp.dot(tri, p2, preferred_element_type=F32)) + carry
        ccol_ref[c * blk:(c + 1) * blk, :] = cs
        crow_ref[:, c * blk:(c + 1) * blk] = cs.T[MISC_FL:MISC_FL + B_HEADS, :]
        carry = cs[blk - 1:blk, :]


def _forget_cumsum(misc, bias_row, batch, *, blk=256):
    est = 4 * SEQ * LANE * 4 + 2 * 8 * SEQ * 4
    return pl.pallas_call(
        functools.partial(_forget_cumsum_kernel, blk=blk),
        grid=(batch,),
        in_specs=[
            pl.BlockSpec((SEQ, LANE), lambda b: (b, 0)),
            pl.BlockSpec((1, LANE), lambda b: (0, 0)),
        ],
        out_specs=[
            pl.BlockSpec((SEQ, LANE), lambda b: (b, 0)),
            pl.BlockSpec((None, B_HEADS, SEQ), lambda b: (b, 0, 0)),
        ],
        out_shape=[
            jax.ShapeDtypeStruct((batch * SEQ, LANE), F32),
            jax.ShapeDtypeStruct((batch, B_HEADS, SEQ), F32),
        ],
        compiler_params=_params(("arbitrary",), est),
        name="forget_cumsum",
    )(misc, bias_row)


def _mla_prep_kernel(cq_ref, ckv_ref, kr_ref, krs_ref, gq_ref, gkv_ref, wq1_ref, wq2_ref, wkv_ref,
                     cos_ref, sin_ref, qc_ref, kc_ref, vc_ref):
    cq = cq_ref[...].astype(F32)
    ms = jnp.sum(cq * cq, axis=-1, keepdims=True) * (1.0 / Q_LORA)
    cqn = (cq * lax.rsqrt(ms + EPS) * gq_ref[...]).astype(BF16)
    ckv = ckv_ref[...].astype(F32)
    ms2 = jnp.mean(ckv * ckv, axis=-1, keepdims=True)
    ckvn = (ckv * lax.rsqrt(ms2 + EPS) * gkv_ref[...]).astype(BF16)

    q1 = jnp.dot(cqn, wq1_ref[...], preferred_element_type=F32) * Q_SCALE_MLA
    q2 = jnp.dot(cqn, wq2_ref[...], preferred_element_type=F32) * Q_SCALE_MLA
    kv = jnp.dot(ckvn, wkv_ref[...], preferred_element_type=F32)
    cos = cos_ref[...]
    sin = sin_ref[...]
    k_rope = (kr_ref[...].astype(F32) * cos + krs_ref[...].astype(F32) * sin).astype(BF16)
    for h in range(C_HEADS):
        qw = NOPE_DIM + LANE
        qc_ref[:, h * qw:h * qw + NOPE_DIM] = q1[:, h * qw:h * qw + NOPE_DIM].astype(BF16)
        q_rope = q1[:, h * qw + NOPE_DIM:(h + 1) * qw] * cos + q2[:, h * LANE:(h + 1) * LANE] * sin
        qc_ref[:, h * qw + NOPE_DIM:(h + 1) * qw] = q_rope.astype(BF16)
        kw = NOPE_DIM + V_DIM
        kc_ref[:, h * qw:h * qw + NOPE_DIM] = kv[:, h * kw:h * kw + NOPE_DIM].astype(BF16)
        kc_ref[:, h * qw + NOPE_DIM:(h + 1) * qw] = k_rope
        vc_ref[:, h * V_DIM:(h + 1) * V_DIM] = kv[:, h * kw + NOPE_DIM:(h + 1) * kw].astype(BF16)


def _mla_prep(proj, gq, gkv, wq1, wq2, wkv, cos_t, sin_t, *, tm=512):
    m = proj.shape[0]
    nseq = SEQ // tm
    qw = C_HEADS * (NOPE_DIM + LANE)
    est = 2 * (tm * 1024 * 2) + 2 * (CQ_PAD * qw * 2 + CQ_PAD * 512 * 2 + 128 * 1024 * 2) \
        + 2 * (2 * tm * qw * 2 + tm * 512 * 2) + 3 * tm * qw * 4
    const = lambda i: (0, 0)
    return pl.pallas_call(
        _mla_prep_kernel,
        grid=(m // tm,),
        in_specs=[
            pl.BlockSpec((tm, CQ_PAD), lambda i: (i, OFF_CQ // CQ_PAD)),
            pl.BlockSpec((tm, LANE), lambda i: (i, OFF_CKV // LANE)),
            pl.BlockSpec((tm, LANE), lambda i: (i, OFF_KR // LANE)),
            pl.BlockSpec((tm, LANE), lambda i: (i, OFF_KRS // LANE)),
            pl.BlockSpec((1, CQ_PAD), const),
            pl.BlockSpec((1, KV_LORA), const),
            pl.BlockSpec((CQ_PAD, qw), const),
            pl.BlockSpec((CQ_PAD, C_HEADS * LANE), const),
            pl.BlockSpec((KV_LORA, C_HEADS * (NOPE_DIM + V_DIM)), const),
            pl.BlockSpec((tm, LANE), lambda i: (i % nseq, 0)),
            pl.BlockSpec((tm, LANE), lambda i: (i % nseq, 0)),
        ],
        out_specs=[
            pl.BlockSpec((tm, qw), lambda i: (i, 0)),
            pl.BlockSpec((tm, qw), lambda i: (i, 0)),
            pl.BlockSpec((tm, C_HEADS * V_DIM), lambda i: (i, 0)),
        ],
        out_shape=[
            jax.ShapeDtypeStruct((m, qw), BF16),
            jax.ShapeDtypeStruct((m, qw), BF16),
            jax.ShapeDtypeStruct((m, C_HEADS * V_DIM), BF16),
        ],
        compiler_params=_params(("arbitrary",), est),
        name="mla_prep",
    )(proj, proj, proj, proj, gq, gkv, wq1, wq2, wkv, cos_t, sin_t)


def _paired_loop(n, body, init):
    def pair(p, carry):
        return body(2 * p + 1, body(2 * p, carry))
    carry = lax.fori_loop(0, n // 2, pair, init)
    return lax.fori_loop(2 * (n // 2), n, body, carry)


def _nt_dot(a, b):
    return lax.dot_general(a, b, (((1,), (1,)), ((), ())), preferred_element_type=F32)


def _two_pass_attention(i, *, n_heads, tq, tk, dv, logits_fn, mask_fn, v_fn, store_fn,
                        s_ref, mx_ref, acc_ref):
    nc = tk // LANE
    n_full = (i * tq) // tk
    nkb = ((i + 1) * tq + tk - 1) // tk

    mx_ref[...] = jnp.full(mx_ref.shape, NEG_INF, F32)

    def pass1(kb, masked):
        ks = pl.multiple_of(kb * tk, tk)
        mask = mask_fn(ks) if masked else None
        for h in range(n_heads):
            s = logits_fn(h, ks)
            if masked:
                s = jnp.where(mask, s, NEG_INF)
            s_ref[h, :, pl.ds(ks, tk)] = s
            mx = mx_ref[h]
            for c in range(nc):
                mx = jnp.maximum(mx, s[:, c * LANE:(c + 1) * LANE])
            mx_ref[h] = mx

    def pass1_full(kb, carry):
        pass1(kb, False)
        return carry

    def pass1_masked(kb, carry):
        pass1(kb, True)
        return carry

    lax.fori_loop(0, n_full, pass1_full, 0)
    lax.fori_loop(n_full, nkb, pass1_masked, 0)

    for h in range(n_heads):
        mx_ref[h] = jnp.broadcast_to(jnp.max(mx_ref[h], axis=-1, keepdims=True), (tq, LANE))
    acc_ref[...] = jnp.zeros(acc_ref.shape, F32)
    ones = jnp.ones((tk, LANE), BF16)

    def pass2(kb, carry):
        ks = pl.multiple_of(kb * tk, tk)
        for h in range(n_heads):
            m = mx_ref[h]
            p = jnp.concatenate(
                [jnp.exp2(s_ref[h, :, pl.ds(ks + c * LANE, LANE)] - m).astype(BF16) for c in range(nc)],
                axis=1)
            v1 = jnp.concatenate([v_fn(h, ks), ones], axis=1)
            acc_ref[h] += jnp.dot(p, v1, preferred_element_type=F32)
        return carry

    lax.fori_loop(0, nkb, pass2, 0)

    for h in range(n_heads):
        acc = acc_ref[h]
        store_fn(h, acc[:, :dv] / acc[:, dv:])


def _tile_iotas(tq, tk):
    return (lax.broadcasted_iota(jnp.int32, (tq, tk), 0), lax.broadcasted_iota(jnp.int32, (tq, tk), 1))


def _fox_kernel(q_ref, k_ref, v_ref, ccol_ref, crow_ref, o_ref,
                s_ref, mx_ref, acc_ref, cq_ref, *, tq, tk):
    i = pl.program_id(1)
    nc = tk // LANE
    head = lambda h: slice(h * HEAD_DIM, (h + 1) * HEAD_DIM)
    for h in range(B_HEADS):
        cq_ref[h] = jnp.broadcast_to(ccol_ref[:, MISC_FL + h:MISC_FL + h + 1] * LOG2E, (tq, LANE))

    def logits_fn(h, ks):
        s = _nt_dot(q_ref[:, head(h)], k_ref[pl.ds(ks, tk), head(h)])
        return s + jnp.concatenate([cq_ref[h]] * nc, axis=1) - crow_ref[h:h + 1, pl.ds(ks, tk)] * LOG2E

    def mask_fn(ks):
        rows, cols = _tile_iotas(tq, tk)
        return cols + (ks - i * tq) <= rows

    def store_fn(h, y):
        o_ref[:, head(h)] = y.astype(o_ref.dtype)

    _two_pass_attention(i, n_heads=B_HEADS, tq=tq, tk=tk, dv=HEAD_DIM, logits_fn=logits_fn,
                        mask_fn=mask_fn, v_fn=lambda h, ks: v_ref[pl.ds(ks, tk), head(h)],
                        store_fn=store_fn, s_ref=s_ref, mx_ref=mx_ref, acc_ref=acc_ref)


def _fox_attention(proj, ccol, crow, batch, *, tq=256, tk=512):
    m = proj.shape[0]
    nq = SEQ // tq
    w = B_HEADS * HEAD_DIM
    state = B_HEADS * tq * LANE * 4
    est = (4 * tq * w * 2 + 4 * SEQ * w * 2 + 2 * tq * LANE * 4 + 2 * 8 * SEQ * 4
           + B_HEADS * tq * SEQ * 4 + 4 * state + 6 * tq * tk * 4)
    return pl.pallas_call(
        functools.partial(_fox_kernel, tq=tq, tk=tk),
        grid=(batch, nq),
        in_specs=[
            pl.BlockSpec((tq, w), lambda b, i: (b * nq + i, OFF_QB // w)),
            pl.BlockSpec((SEQ, w), lambda b, i: (b, OFF_KB // w)),
            pl.BlockSpec((SEQ, w), lambda b, i: (b, OFF_VB // w)),
            pl.BlockSpec((tq, LANE), lambda b, i: (b * nq + i, 0)),
            pl.BlockSpec((None, B_HEADS, SEQ), lambda b, i: (b, 0, 0)),
        ],
        out_specs=pl.BlockSpec((tq, w), lambda b, i: (b * nq + i, 0)),
        out_shape=jax.ShapeDtypeStruct((m, w), BF16),
        scratch_shapes=[
            pltpu.VMEM((B_HEADS, tq, SEQ), F32),
            pltpu.VMEM((B_HEADS, tq, LANE), F32),
            pltpu.VMEM((B_HEADS, tq, HEAD_DIM + LANE), F32),
            pltpu.VMEM((B_HEADS, tq, LANE), F32),
        ],
        compiler_params=_params(("arbitrary", "arbitrary"), est),
        name="fox_attention",
    )(proj, proj, proj, ccol, crow)


def _mla_kernel(q_ref, k_ref, v_ref, o_ref, s_ref, mx_ref, acc_ref, *, tq, tk):
    i = pl.program_id(1)
    qw = NOPE_DIM + LANE
    qhead = lambda h: slice(h * qw, (h + 1) * qw)
    vhead = lambda h: slice(h * V_DIM, (h + 1) * V_DIM)

    def logits_fn(h, ks):
        return _nt_dot(q_ref[:, qhead(h)], k_ref[pl.ds(ks, tk), qhead(h)])

    def mask_fn(ks):
        rows, cols = _tile_iotas(tq, tk)
        return (cols + ks) // CHUNK <= (rows + i * tq) // CHUNK

    def store_fn(h, y):
        o_ref[:, vhead(h)] = y.astype(o_ref.dtype)

    _two_pass_attention(i, n_heads=C_HEADS, tq=tq, tk=tk, dv=V_DIM, logits_fn=logits_fn,
                        mask_fn=mask_fn, v_fn=lambda h, ks: v_ref[pl.ds(ks, tk), vhead(h)],
                        store_fn=store_fn, s_ref=s_ref, mx_ref=mx_ref, acc_ref=acc_ref)


def _mla_attention(qc, kc, vc, batch, *, tq=256, tk=512):
    m = qc.shape[0]
    nq = SEQ // tq
    qw = C_HEADS * (NOPE_DIM + LANE)
    vw = C_HEADS * V_DIM
    state = C_HEADS * tq * LANE * 4
    est = (2 * tq * qw * 2 + 2 * SEQ * qw * 2 + 2 * SEQ * vw * 2 + 2 * tq * vw * 2
           + C_HEADS * tq * SEQ * 4 + 3 * state + 6 * tq * tk * 4)
    return pl.pallas_call(
        functools.partial(_mla_kernel, tq=tq, tk=tk),
        grid=(batch, nq),
        in_specs=[
            pl.BlockSpec((tq, qw), lambda b, i: (b * nq + i, 0)),
            pl.BlockSpec((SEQ, qw), lambda b, i: (b, 0)),
            pl.BlockSpec((SEQ, vw), lambda b, i: (b, 0)),
        ],
        out_specs=pl.BlockSpec((tq, vw), lambda b, i: (b * nq + i, 0)),
        out_shape=jax.ShapeDtypeStruct((m, vw), BF16),
        scratch_shapes=[
            pltpu.VMEM((C_HEADS, tq, SEQ), F32),
            pltpu.VMEM((C_HEADS, tq, LANE), F32),
            pltpu.VMEM((C_HEADS, tq, V_DIM + LANE), F32),
        ],
        compiler_params=_params(("arbitrary", "arbitrary"), est),
        name="mla_attention",
    )(qc, kc, vc)


def _t5_tiles_kernel(bucket_ref, table_ref, o_ref):
    for d in range(3):
        bucket = bucket_ref[d]
        for h in range(A_HEADS):
            acc = jnp.zeros(bucket.shape, F32)
            for nb in range(NUM_BUCKETS):
                acc = jnp.where(bucket == nb, table_ref[nb, h] * LOG2E, acc)
            o_ref[d, h] = acc


def _t5_tiles(bucket_tiles, t5_bias, *, t=ATT_T):
    return pl.pallas_call(
        _t5_tiles_kernel,
        in_specs=[
            pl.BlockSpec(memory_space=pltpu.VMEM),
            pl.BlockSpec(memory_space=pltpu.SMEM),
        ],
        out_specs=pl.BlockSpec(memory_space=pltpu.VMEM),
        out_shape=jax.ShapeDtypeStruct((3, A_HEADS, t, t), F32),
        name="t5_tiles",
    )(bucket_tiles, t5_bias)


def _dsa_kernel(qi_ref, klo_ref, khi_ref, misc_ref, qa_ref, ka_ref, va_ref, bias_ref, o_ref,
                sc_ref, lg_ref, acc_ref, *, t):
    i = pl.program_id(1)
    groups = t // 8
    key = lax.broadcasted_iota(jnp.int32, (t, t), 0)
    qry = lax.broadcasted_iota(jnp.int32, (t, t), 1)
    admissible = (key // CHUNK) <= (qry // CHUNK)

    def fold(x, op):
        return op(x.reshape(groups, 8, t), axis=0)

    def over_keys(x8, op):
        return jnp.broadcast_to(op(x8, axis=0, keepdims=True), (8, t))

    w_t = misc_ref[...].T[MISC_WI:MISC_WI + IDX_HEADS, :] * (IDX_HEADS ** -0.5 * IDX_DIM ** -0.5)

    def block_scores(kb):
        ks = pl.multiple_of(kb * t, t)
        klo = klo_ref[pl.ds(ks, t), :]
        khi = khi_ref[pl.ds(ks, t), :]
        acc = jnp.zeros((t, t), F32)
        for pair in range(IDX_HEADS // 2):
            qp = qi_ref[:, pair * LANE:(pair + 1) * LANE]
            for sub, kk in enumerate((klo, khi)):
                h = 2 * pair + sub
                acc = acc + jnp.maximum(_nt_dot(kk, qp), 0.0) * w_t[h:h + 1, :]
        return ks, acc

    def score_body(kb, carry):
        mn, mx = carry
        ks, acc = block_scores(kb)
        sc_ref[pl.ds(ks, t), :] = acc
        return jnp.minimum(mn, fold(acc, jnp.min)), jnp.maximum(mx, fold(acc, jnp.max))

    mn, mx = _paired_loop(i, score_body,
                          (jnp.full((8, t), -SELECT_MIN, F32), jnp.full((8, t), SELECT_MIN, F32)))
    ks, acc = block_scores(i)
    sc_ref[pl.ds(ks, t), :] = jnp.where(admissible, acc, NEG_INF)
    mn = jnp.minimum(mn, fold(jnp.where(admissible, acc, -SELECT_MIN), jnp.min))
    mx = jnp.maximum(mx, fold(jnp.where(admissible, acc, SELECT_MIN), jnp.max))

    def count_ge(thr):
        def body(kb, cnt):
            ks = pl.multiple_of(kb * t, t)
            hit = jnp.where(sc_ref[pl.ds(ks, t), :].reshape(groups, 8, t) >= thr[None], 1.0, 0.0)
            return cnt + jnp.sum(hit, axis=0)
        return over_keys(_paired_loop(i + 1, body, jnp.zeros((8, t), F32)), jnp.sum)

    k_sel = float(TOPK_MAX)
    search = i * t >= TOPK_MAX
    lo0 = jnp.where(search, over_keys(mn, jnp.min), SELECT_MIN)
    hi0 = jnp.where(search, over_keys(mx, jnp.max), SELECT_MIN)

    def midpoint(lo, hi):
        return 0.5 * lo + 0.5 * hi

    def any_active(lo, hi):
        mid = midpoint(lo, hi)
        return jnp.max(jnp.where((mid > lo) & (mid < hi), 1.0, 0.0)) > 0.5

    def search_cond(state):
        return jnp.logical_and(state[0] < BISECT_MAX_ITERS, state[1])

    def search_body(state):
        it, _, lo, hi, c_lo = state
        for _ in range(SEARCH_STEPS_PER_CHECK):
            mid = midpoint(lo, hi)
            cnt = count_ge(mid)
            ge = cnt >= k_sel
            lo = jnp.where(ge, mid, lo)
            c_lo = jnp.where(ge, cnt, c_lo)
            hi = jnp.where(cnt > k_sel, hi, mid)
        return it + SEARCH_STEPS_PER_CHECK, any_active(lo, hi), lo, hi, c_lo

    n_adm = (((i * t + qry[:8]) // CHUNK + 1) * CHUNK).astype(F32)
    _, _, lo, hi, c_lo = lax.while_loop(search_cond, search_body,
                                        (jnp.int32(0), any_active(lo0, hi0), lo0, hi0, n_adm))
    c_hi = count_ge(hi)
    thr = jnp.where(c_hi >= k_sel, hi, lo)
    c_thr = jnp.where(c_hi >= k_sel, c_hi, c_lo)

    @pl.when(jnp.max(jnp.where(c_thr > k_sel, 1.0, 0.0)) > 0.5)
    def _():
        key_in_block = (lax.broadcasted_iota(jnp.int32, (groups, 8, t), 0) * 8
                        + lax.broadcasted_iota(jnp.int32, (groups, 8, t), 1))

        def count_where(pred):
            def body(kb, cnt):
                ks = pl.multiple_of(kb * t, t)
                blk = sc_ref[pl.ds(ks, t), :].reshape(groups, 8, t)
                return cnt + jnp.sum(jnp.where(pred(blk, ks), 1.0, 0.0), axis=0)
            return over_keys(lax.fori_loop(0, i + 1, body, jnp.zeros((8, t), F32)), jnp.sum)

        def tied_before(bound):
            return lambda blk, ks: (blk == thr[None]) & ((key_in_block + ks).astype(F32) < bound[None])

        keep = k_sel - count_where(lambda blk, ks: blk > thr[None])

        def cut_body(_, bounds):
            below, above = bounds
            mid = jnp.floor(0.5 * (below + above))
            enough = count_where(tied_before(mid)) >= keep
            return jnp.where(enough, below, mid), jnp.where(enough, mid, above)

        _, cut = lax.fori_loop(0, SEQ.bit_length(), cut_body,
                               (jnp.zeros((8, t), F32), jnp.full((8, t), float(SEQ), F32)))

        def drop_body(kb, carry):
            ks = pl.multiple_of(kb * t, t)
            blk = sc_ref[pl.ds(ks, t), :].reshape(groups, 8, t)
            drop = (blk == thr[None]) & ((key_in_block + ks).astype(F32) >= cut[None])
            sc_ref[pl.ds(ks, t), :] = jnp.where(drop, NEG_INF, blk).reshape(t, t)
            return carry

        lax.fori_loop(0, i + 1, drop_body, 0)

    head = lambda h: slice(h * HEAD_DIM, (h + 1) * HEAD_DIM)
    acc_ref[...] = jnp.zeros(acc_ref.shape, F32)

    def logits_pass(kb, mx):
        ks = pl.multiple_of(kb * t, t)
        dist = jnp.minimum(i - kb, 2)
        k_blk = ka_ref[pl.ds(ks, t), :]
        sel = sc_ref[pl.ds(ks, t), :] >= thr[0:1, :]
        out = []
        for h in range(A_HEADS):
            s = _nt_dot(k_blk, qa_ref[:, head(h)]) + bias_ref[dist, h]
            s = jnp.where(sel, s, NEG_INF)
            lg_ref[h, pl.ds(ks, t), :] = s
            out.append(jnp.maximum(mx[h], fold(s, jnp.max)))
        return tuple(out)

    mx = _paired_loop(i + 1, logits_pass,
                      tuple(jnp.full((8, t), NEG_INF, F32) for _ in range(A_HEADS)))
    m = [over_keys(mx[h], jnp.max)[0:1, :] for h in range(A_HEADS)]

    ones_rows = jnp.ones((DSA_ONES_ROWS, t), BF16)

    def value_pass(kb, carry):
        ks = pl.multiple_of(kb * t, t)
        v_t = va_ref[pl.ds(ks, t), :].astype(F32).T.astype(BF16)
        v1_t = jnp.concatenate([v_t, ones_rows], axis=0)
        for h in range(A_HEADS):
            p = jnp.exp2(lg_ref[h, pl.ds(ks, t), :] - m[h])
            acc_ref[h] += jnp.dot(v1_t, p.astype(BF16), preferred_element_type=F32)
        return carry

    _paired_loop(i + 1, value_pass, 0)
    for h in range(A_HEADS):
        acc = acc_ref[h]
        o_ref[:, head(h)] = (acc[:HEAD_DIM] / acc[HEAD_DIM:HEAD_DIM + 1]).T.astype(o_ref.dtype)


def _dsa_attention(proj, misc, bias_tiles, batch, *, t=ATT_T):
    m = proj.shape[0]
    nq = SEQ // t
    qiw = IDX_HEADS * IDX_DIM
    qaw = A_HEADS * HEAD_DIM
    est = (2 * t * qiw * 2 + 2 * 4 * SEQ * LANE * 2 + 2 * t * LANE * 4 + 2 * t * qaw * 2
           + 2 * 3 * A_HEADS * t * t * 4 + 2 * t * qaw * 2
           + (1 + A_HEADS) * t * SEQ * 4 + A_HEADS * HEAD_DIM * t * 4 + 12 * t * t * 4)
    kblock = lambda off: pl.BlockSpec((SEQ, LANE), lambda b, i: (b, off // LANE))
    return pl.pallas_call(
        functools.partial(_dsa_kernel, t=t),
        grid=(batch, nq),
        in_specs=[
            pl.BlockSpec((t, qiw), lambda b, i: (b * nq + i, OFF_QI // qiw)),
            kblock(OFF_KI_LO),
            kblock(OFF_KI_HI),
            pl.BlockSpec((t, LANE), lambda b, i: (b * nq + i, 0)),
            pl.BlockSpec((t, qaw), lambda b, i: (b * nq + i, OFF_QA // qaw)),
            kblock(OFF_KA),
            kblock(OFF_VA),
            pl.BlockSpec((3, A_HEADS, t, t), lambda b, i: (0, 0, 0, 0)),
        ],
        out_specs=pl.BlockSpec((t, qaw), lambda b, i: (b * nq + i, 0)),
        out_shape=jax.ShapeDtypeStruct((m, qaw), BF16),
        scratch_shapes=[
            pltpu.VMEM((SEQ, t), F32),
            pltpu.VMEM((A_HEADS, SEQ, t), F32),
            pltpu.VMEM((A_HEADS, HEAD_DIM + DSA_ONES_ROWS, t), F32),
        ],
        compiler_params=_params(("arbitrary", "arbitrary"), est),
        name="dsa_attention",
    )(proj, proj, proj, misc, proj, proj, proj, bias_tiles)


def _merge_kernel(x_ref, ya_ref, yb_ref, yc_ref, ga_ref, gb_ref, gc_ref,
                  wa_ref, wb_ref, wc_ref, wo_ref, o_ref):
    def branch(y_ref, w_ref, g_ref):
        y = jnp.dot(y_ref[...], w_ref[...], preferred_element_type=F32)
        return jax.nn.sigmoid(g_ref[...].astype(F32)) * y

    merged = branch(ya_ref, wa_ref, ga_ref) + branch(yb_ref, wb_ref, gb_ref) + branch(yc_ref, wc_ref, gc_ref)
    o_ref[...] = x_ref[...] + jnp.dot(merged.astype(BF16), wo_ref[...], preferred_element_type=F32)


def _merge(x2, ya, yb, yc, proj, wa, wb, wc, wo, *, tm=256):
    m = x2.shape[0]
    wbytes = (wa.size + wb.size + wc.size + wo.size) * 2
    est = 2 * wbytes + 4 * tm * D_MODEL * 4 + 2 * tm * 2048 * 2 + 6 * tm * D_MODEL * 2 + 4 * tm * D_MODEL * 4
    const = lambda i: (0, 0)
    gate = lambda k: pl.BlockSpec((tm, D_MODEL), lambda i: (i, OFF_GL // D_MODEL + k))
    return pl.pallas_call(
        _merge_kernel,
        grid=(m // tm,),
        in_specs=[
            pl.BlockSpec((tm, D_MODEL), lambda i: (i, 0)),
            pl.BlockSpec((tm, ya.shape[1]), lambda i: (i, 0)),
            pl.BlockSpec((tm, yb.shape[1]), lambda i: (i, 0)),
            pl.BlockSpec((tm, yc.shape[1]), lambda i: (i, 0)),
            gate(0), gate(1), gate(2),
            pl.BlockSpec(wa.shape, const),
            pl.BlockSpec(wb.shape, const),
            pl.BlockSpec(wc.shape, const),
            pl.BlockSpec(wo.shape, const),
        ],
        out_specs=pl.BlockSpec((tm, D_MODEL), lambda i: (i, 0)),
        out_shape=jax.ShapeDtypeStruct((m, D_MODEL), F32),
        compiler_params=_params(("arbitrary",), est),
        name="merge",
    )(x2, ya, yb, yc, proj, proj, proj, wa, wb, wc, wo)


def _ffn_kernel(x_ref, g_ref, wu_ref, wv_ref, cw_ref, wd_ref, fg_ref, o_ref,
                h_ref, ubuf_ref, halo_ref, *, tm, tf, final_norm):
    i = pl.program_id(0)
    j = pl.program_id(1)
    nj = pl.num_programs(1)

    @pl.when(j == 0)
    def _():
        _rmsnorm_rows(x_ref, g_ref, h_ref, tm)
        o_ref[...] = x_ref[...]

        @pl.when((i % (SEQ // tm)) == 0)
        def _():
            halo_ref[...] = jnp.zeros(halo_ref.shape, F32)

    h = h_ref[...]
    u = jnp.dot(h, wu_ref[...], preferred_element_type=F32)
    v = jnp.dot(h, wv_ref[...], preferred_element_type=F32)

    ubuf_ref[0:8, :] = halo_ref[j]
    ubuf_ref[8:, :] = u
    halo_ref[j] = u[tm - 8:, :]
    conv = (cw_ref[0:1, :] * ubuf_ref[6:6 + tm, :] + cw_ref[1:2, :] * ubuf_ref[7:7 + tm, :]
            + cw_ref[2:3, :] * u + cw_ref[3:4, :])
    act = (jax.nn.gelu(conv) * v).astype(BF16)
    o_ref[...] += jnp.dot(act, wd_ref[...], preferred_element_type=F32)

    if final_norm:
        @pl.when(j == nj - 1)
        def _():
            _rmsnorm_rows(o_ref, fg_ref, o_ref, tm)


def _ffn(x2, g, w_up, cw, w_down, final_g, *, final_norm, tm=1024, tf=512):
    m = x2.shape[0]
    nj = D_FF // tf
    est = (3 * tm * D_MODEL * 4 + tm * D_MODEL * 2 + 2 * 2 * D_MODEL * tf * 2 + 2 * tf * D_MODEL * 2
           + (tm + 8) * tf * 4 + nj * 8 * tf * 4 + 4 * tm * tf * 4)
    return pl.pallas_call(
        functools.partial(_ffn_kernel, tm=tm, tf=tf, final_norm=final_norm),
        grid=(m // tm, nj),
        in_specs=[
            pl.BlockSpec((tm, D_MODEL), lambda i, j: (i, 0), pipeline_mode=pl.Buffered(1)),
            pl.BlockSpec((1, D_MODEL), lambda i, j: (0, 0)),
            pl.BlockSpec((D_MODEL, tf), lambda i, j: (0, j)),
            pl.BlockSpec((D_MODEL, tf), lambda i, j: (0, nj + j)),
            pl.BlockSpec((8, tf), lambda i, j: (0, j)),
            pl.BlockSpec((tf, D_MODEL), lambda i, j: (j, 0)),
            pl.BlockSpec((1, D_MODEL), lambda i, j: (0, 0)),
        ],
        out_specs=pl.BlockSpec((tm, D_MODEL), lambda i, j: (i, 0)),
        out_shape=jax.ShapeDtypeStruct((m, D_MODEL), F32),
        scratch_shapes=[
            pltpu.VMEM((tm, D_MODEL), BF16),
            pltpu.VMEM((tm + 8, tf), F32),
            pltpu.VMEM((nj, 8, tf), F32),
        ],
        compiler_params=_params(("arbitrary", "arbitrary"), est),
        name="conv_ffn",
    )(x2, g, w_up, w_up, cw, w_down, final_g)


def _t5_bucket(rel):
    nb = NUM_BUCKETS // 2
    max_exact = nb // 2
    base = jnp.where(rel > 0, nb, 0)
    n = jnp.abs(rel)
    nf = jnp.maximum(n, 1).astype(F32)
    large = max_exact + (jnp.log(nf / max_exact) / math.log(MAX_DISTANCE / max_exact)
                         * (nb - max_exact)).astype(jnp.int32)
    large = jnp.minimum(large, nb - 1)
    return base + jnp.where(n < max_exact, n, large)


def _bucket_tiles(t):
    key = jnp.arange(t, dtype=jnp.int32)[:, None]
    qry = jnp.arange(t, dtype=jnp.int32)[None, :]
    return jnp.stack([_t5_bucket(key - qry - d * t) for d in range(3)])


def _rope_tables():
    half = ROPE_DIM // 2
    inv = ROPE_THETA ** (-jnp.arange(half, dtype=F32) / half)
    ang = jnp.arange(SEQ, dtype=jnp.int32).astype(F32)[:, None] * inv[None, :]
    cos, sin = jnp.cos(ang), jnp.sin(ang)
    pad = jnp.zeros((SEQ, LANE - ROPE_DIM), F32)
    return jnp.concatenate([cos, cos, pad], axis=1), jnp.concatenate([-sin, sin, pad], axis=1)


def _swap_halves(w):
    half = w.shape[-1] // 2
    return jnp.concatenate([w[..., half:], w[..., :half]], axis=-1)


def _pack_w_in_kernel(w_ref, o_ref):
    off = np.cumsum((0,) + IN_SIZES)
    seg = lambda k: w_ref[off[k]:off[k + 1], :]
    qa, ka, va, qi, ki, wi, qb, kb, vb, fl, cq, ckv, kr, gl = [seg(k) for k in range(len(IN_SIZES))]
    qa, qb = qa * Q_SCALE, qb * Q_SCALE
    z = lambda n: jnp.zeros((n, w_ref.shape[1]), F32)
    half = ROPE_DIM // 2
    groups = [
        (OFF_QI, [qi]), (OFF_QB, [qb]), (OFF_KB, [kb]), (OFF_VB, [vb]), (OFF_GL, [gl]), (OFF_QA, [qa]),
        (OFF_CQ, [cq, z(CQ_PAD - Q_LORA)]), (OFF_KA, [ka]), (OFF_VA, [va]), (OFF_KI_LO, [ki, z(64)]),
        (OFF_CKV, [ckv]), (OFF_KR, [kr, z(64)]), (OFF_KRS, [kr[half:], kr[:half], z(64)]),
        (OFF_MISC, [wi, fl, z(LANE - IDX_HEADS - B_HEADS)]), (OFF_KI_HI, [z(64), ki]),
    ]
    for start, pieces in groups:
        block = pieces[0] if len(pieces) == 1 else jnp.concatenate(pieces, axis=0)
        o_ref[start:start + block.shape[0], :] = block.astype(o_ref.dtype)


def _pack_w_in(w_in, l, *, cols=256):
    w_t = jnp.swapaxes(w_in, 1, 2)
    _, n, d = w_t.shape
    return pl.pallas_call(
        _pack_w_in_kernel,
        grid=(d // cols,),
        in_specs=[pl.BlockSpec((None, n, cols), lambda i: (l, 0, i))],
        out_specs=pl.BlockSpec((N_PACK, cols), lambda i: (0, i)),
        out_shape=jax.ShapeDtypeStruct((N_PACK, d), BF16),
        compiler_params=_params(("arbitrary",), 2 * cols * (n * 4 + N_PACK * 2)),
        name="pack_w_in",
    )(w_t)


def _pack_w_uq(w):
    w = jnp.pad(w, ((0, CQ_PAD - Q_LORA), (0, 0))).reshape(CQ_PAD, C_HEADS, NOPE_DIM + ROPE_DIM)
    z = jnp.zeros((CQ_PAD, C_HEADS, LANE - ROPE_DIM), w.dtype)
    rope = w[..., NOPE_DIM:]
    wq1 = jnp.concatenate([w, z], axis=-1).reshape(CQ_PAD, -1)
    wq2 = jnp.concatenate([_swap_halves(rope), z], axis=-1).reshape(CQ_PAD, -1)
    return wq1.astype(BF16), wq2.astype(BF16)


def kernel(x, norm_mix_g, w_in, b_forget, g_cq, g_ckv, w_uq, w_ukv, w_branch_a, w_branch_b, w_branch_c,
           w_o, norm_ffn_g, w_up, conv_w, conv_b, w_down, t5_bias, final_g):
    batch, seq, d = x.shape
    assert (seq, d) == (SEQ, D_MODEL)
    x2 = x.reshape(batch * seq, d)

    bias_tiles = _t5_tiles(_bucket_tiles(ATT_T), t5_bias)
    cos_t, sin_t = _rope_tables()
    final_row = final_g.reshape(1, D_MODEL)

    for l in range(DEPTH):
        casts = [(w_up, True, 11), (w_down, False, 11), (w_o, True, 8),
                 (w_branch_a, True, 8), (w_branch_b, True, 8), (w_branch_c, True, 8)]
        proj, misc, w_up_l, w_down_l, w_o_l, w_a_l, w_b_l, w_c_l = _inproj(
            x2, norm_mix_g[l].reshape(1, D_MODEL), _pack_w_in(w_in, l), casts, l)

        fbias = jnp.zeros((1, LANE), F32).at[0, MISC_FL:MISC_FL + B_HEADS].set(b_forget[l])
        ccol, crow = _forget_cumsum(misc, fbias, batch)

        wq1, wq2 = _pack_w_uq(w_uq[l])
        gq = jnp.pad(g_cq[l], (0, CQ_PAD - Q_LORA)).reshape(1, CQ_PAD)
        qc, kc, vc = _mla_prep(proj, gq, g_ckv[l].reshape(1, KV_LORA), wq1, wq2,
                               w_ukv[l].astype(BF16), cos_t, sin_t)

        ya = _dsa_attention(proj, misc, bias_tiles, batch)
        yb = _fox_attention(proj, ccol, crow, batch)
        yc = _mla_attention(qc, kc, vc, batch)

        x2 = _merge(x2, ya, yb, yc, proj, w_a_l, w_b_l, w_c_l, w_o_l)

        cw = jnp.concatenate([conv_w[l], conv_b[l][None, :], jnp.zeros((4, D_FF), F32)], axis=0)
        x2 = _ffn(x2, norm_ffn_g[l].reshape(1, D_MODEL), w_up_l, cw, w_down_l, final_row,
                  final_norm=(l == DEPTH - 1))

    return x2.reshape(batch, seq, d)
```

```python
import functools
import math

import jax
import jax.numpy as jnp
import numpy as np
from jax import lax
from jax.experimental import pallas as pl
from jax.experimental.pallas import tpu as pltpu

F32 = jnp.float32
BF16 = jnp.bfloat16

D_MODEL = 2048
SEQ = 2048
DEPTH = 2
CHUNK = 64
HEAD_DIM = 128
EPS = 1e-6
NEG_INF = -1e30

A_HEADS = 4
IDX_HEADS = 16
IDX_DIM = 64
TOPK_MAX = 256
NUM_BUCKETS = 32
MAX_DISTANCE = 128
B_HEADS = 8
C_HEADS = 4
Q_LORA = 448
KV_LORA = 128
NOPE_DIM = 128
ROPE_DIM = 64
V_DIM = 128
ROPE_THETA = 10000.0
D_FF = 5632

IN_SIZES = (
    A_HEADS * HEAD_DIM, HEAD_DIM, HEAD_DIM,
    IDX_HEADS * IDX_DIM, IDX_DIM, IDX_HEADS,
    B_HEADS * HEAD_DIM, B_HEADS * HEAD_DIM, B_HEADS * HEAD_DIM, B_HEADS,
    Q_LORA, KV_LORA, ROPE_DIM,
    3 * D_MODEL,
)

LOG2E = math.log2(math.e)
Q_SCALE = HEAD_DIM ** -0.5 * LOG2E
Q_SCALE_MLA = (NOPE_DIM + ROPE_DIM) ** -0.5 * LOG2E
LANE = 128
V7X_VMEM_BYTES = 64 * 1024 * 1024

OFF_QI = 0
OFF_QB = 1024
OFF_KB = 2048
OFF_VB = 3072
OFF_GL = 4096
OFF_QA = 10240
OFF_CQ = 10752
OFF_KA = 11264
OFF_VA = 11392
OFF_KI_LO = 11520
OFF_CKV = 11648
OFF_KR = 11776
OFF_KRS = 11904
OFF_MISC = 12032
OFF_KI_HI = 12160
N_PACK = 12288
CQ_PAD = 512
MISC_WI = 0
MISC_FL = IDX_HEADS

ATT_T = 256
SELECT_MIN = -1e29
BISECT_MAX_ITERS = 512
SEARCH_STEPS_PER_CHECK = 2
DSA_ONES_ROWS = 16


def _vmem_limit(estimate_bytes):
    limit = V7X_VMEM_BYTES - (6 << 20)
    assert estimate_bytes <= limit, estimate_bytes
    return limit


def _params(semantics, vmem_estimate):
    return pltpu.CompilerParams(dimension_semantics=semantics,
                                vmem_limit_bytes=_vmem_limit(vmem_estimate))


def _rmsnorm_rows(x_ref, g_ref, out_ref, rows):
    def body(c, carry):
        r = pl.multiple_of(c * 128, 128)
        x = x_ref[pl.ds(r, 128), :]
        ms = jnp.mean(x * x, axis=-1, keepdims=True)
        out_ref[pl.ds(r, 128), :] = (x * lax.rsqrt(ms + EPS) * g_ref[...]).astype(out_ref.dtype)
        return carry
    lax.fori_loop(0, rows // 128, body, 0)


def _inproj_kernel(*refs, tm, tn, n_cast):
    x_ref, g_ref, w_ref = refs[:3]
    cast_in = refs[3:3 + n_cast]
    o_ref, misc_ref = refs[3 + n_cast:5 + n_cast]
    cast_out = refs[5 + n_cast:5 + 2 * n_cast]
    h_ref = refs[5 + 2 * n_cast]
    j = pl.program_id(1)

    @pl.when(j == 0)
    def _():
        _rmsnorm_rows(x_ref, g_ref, h_ref, tm)

    acc = _nt_dot(h_ref[...], w_ref[...])
    o_ref[...] = acc.astype(o_ref.dtype)

    for src, dst in zip(cast_in, cast_out):
        dst[...] = src[...].astype(dst.dtype)

    @pl.when(j == OFF_MISC // tn)
    def _():
        lo = OFF_MISC % tn
        misc_ref[...] = acc[:, lo:lo + LANE]


def _cast_specs(w_stack, l, ni, nj, rows_follow_i, n_chunks):
    _, r, c = w_stack.shape
    assert n_chunks <= nj
    chunk = lambda j: jnp.minimum(j, n_chunks - 1)
    if rows_follow_i:
        block = (r // ni, c // n_chunks)
        idx = lambda i, j: (i, chunk(j))
    else:
        block = (r // n_chunks, c // ni)
        idx = lambda i, j: (chunk(j), i)
    assert block[0] % 16 == 0 and block[1] % LANE == 0, block
    in_spec = pl.BlockSpec((None,) + block, lambda i, j: (l,) + idx(i, j))
    out_spec = pl.BlockSpec(block, idx)
    return in_spec, out_spec, jax.ShapeDtypeStruct((r, c), BF16), block[0] * block[1]


def _inproj(x2, g, w_pack, casts, l, *, tm=1024, tn=1024):
    m = x2.shape[0]
    ni, nj = m // tm, N_PACK // tn
    plans = [_cast_specs(w, l, ni, nj, rows_i, n) for w, rows_i, n in casts]
    est = (2 * tm * D_MODEL * 4 + tm * D_MODEL * 2 + 2 * D_MODEL * tn * 2 + 2 * tm * tn * 2 + tm * tn * 4
           + sum(2 * 6 * p[3] for p in plans))
    return pl.pallas_call(
        functools.partial(_inproj_kernel, tm=tm, tn=tn, n_cast=len(casts)),
        grid=(ni, nj),
        in_specs=[
            pl.BlockSpec((tm, D_MODEL), lambda i, j: (i, 0)),
            pl.BlockSpec((1, D_MODEL), lambda i, j: (0, 0)),
            pl.BlockSpec((tn, D_MODEL), lambda i, j: (j, 0)),
        ] + [p[0] for p in plans],
        out_specs=[
            pl.BlockSpec((tm, tn), lambda i, j: (i, j)),
            pl.BlockSpec((tm, LANE), lambda i, j: (i, 0)),
        ] + [p[1] for p in plans],
        out_shape=[
            jax.ShapeDtypeStruct((m, N_PACK), BF16),
            jax.ShapeDtypeStruct((m, LANE), F32),
        ] + [p[2] for p in plans],
        scratch_shapes=[pltpu.VMEM((tm, D_MODEL), BF16)],
        compiler_params=_params(("arbitrary", "arbitrary"), est),
        name="inproj",
    )(x2, g, w_pack, *[w for w, _, _ in casts])


def _forget_cumsum_kernel(misc_ref, bias_ref, ccol_ref, crow_ref, *, blk):
    rows = lax.broadcasted_iota(jnp.int32, (blk, blk), 0)
    cols = lax.broadcasted_iota(jnp.int32, (blk, blk), 1)
    tri = jnp.where(rows >= cols, 1.0, 0.0).astype(BF16)
    carry = jnp.zeros((1, LANE), F32)
    for c in range(SEQ // blk):
        z = misc_ref[c * blk:(c + 1) * blk, :] + bias_ref[...]
        lf = jnp.minimum(z, 0.0) - jnp.log1p(jnp.exp(-jnp.abs(z)))
        p0 = lf.astype(BF16)
        r1 = lf - p0.astype(F32)
        p1 = r1.astype(BF16)
        p2 = (r1 - p1.astype(F32)).astype(BF16)
        cs = (jnp.dot(tri, p0, preferred_element_type=F32)
              + jnp.dot(tri, p1, preferred_element_type=F32)
              + jnp.dot(tri, p2, preferred_element_type=F32)) + carry
        ccol_ref[c * blk:(c + 1) * blk, :] = cs
        crow_ref[:, c * blk:(c + 1) * blk] = cs.T[MISC_FL:MISC_FL + B_HEADS, :]
        carry = cs[blk - 1:blk, :]


def _forget_cumsum(misc, bias_row, batch, *, blk=256):
    est = 4 * SEQ * LANE * 4 + 2 * 8 * SEQ * 4
    return pl.pallas_call(
        functools.partial(_forget_cumsum_kernel, blk=blk),
        grid=(batch,),
        in_specs=[
            pl.BlockSpec((SEQ, LANE), lambda b: (b, 0)),
            pl.BlockSpec((1, LANE), lambda b: (0, 0)),
        ],
        out_specs=[
            pl.BlockSpec((SEQ, LANE), lambda b: (b, 0)),
            pl.BlockSpec((None, B_HEADS, SEQ), lambda b: (b, 0, 0)),
        ],
        out_shape=[
            jax.ShapeDtypeStruct((batch * SEQ, LANE), F32),
            jax.ShapeDtypeStruct((batch, B_HEADS, SEQ), F32),
        ],
        compiler_params=_params(("arbitrary",), est),
        name="forget_cumsum",
    )(misc, bias_row)


def _mla_prep_kernel(cq_ref, ckv_ref, kr_ref, krs_ref, gq_ref, gkv_ref, wq1_ref, wq2_ref, wkv_ref,
                     cos_ref, sin_ref, qc_ref, kc_ref, vc_ref):
    cq = cq_ref[...].astype(F32)
    ms = jnp.sum(cq * cq, axis=-1, keepdims=True) * (1.0 / Q_LORA)
    cqn = (cq * lax.rsqrt(ms + EPS) * gq_ref[...]).astype(BF16)
    ckv = ckv_ref[...].astype(F32)
    ms2 = jnp.mean(ckv * ckv, axis=-1, keepdims=True)
    ckvn = (ckv * lax.rsqrt(ms2 + EPS) * gkv_ref[...]).astype(BF16)

    q1 = jnp.dot(cqn, wq1_ref[...], preferred_element_type=F32) * Q_SCALE_MLA
    q2 = jnp.dot(cqn, wq2_ref[...], preferred_element_type=F32) * Q_SCALE_MLA
    kv = jnp.dot(ckvn, wkv_ref[...], preferred_element_type=F32)
    cos = cos_ref[...]
    sin = sin_ref[...]
    k_rope = (kr_ref[...].astype(F32) * cos + krs_ref[...].astype(F32) * sin).astype(BF16)
    for h in range(C_HEADS):
        qw = NOPE_DIM + LANE
        qc_ref[:, h * qw:h * qw + NOPE_DIM] = q1[:, h * qw:h * qw + NOPE_DIM].astype(BF16)
        q_rope = q1[:, h * qw + NOPE_DIM:(h + 1) * qw] * cos + q2[:, h * LANE:(h + 1) * LANE] * sin
        qc_ref[:, h * qw + NOPE_DIM:(h + 1) * qw] = q_rope.astype(BF16)
        kw = NOPE_DIM + V_DIM
        kc_ref[:, h * qw:h * qw + NOPE_DIM] = kv[:, h * kw:h * kw + NOPE_DIM].astype(BF16)
        kc_ref[:, h * qw + NOPE_DIM:(h + 1) * qw] = k_rope
        vc_ref[:, h * V_DIM:(h + 1) * V_DIM] = kv[:, h * kw + NOPE_DIM:(h + 1) * kw].astype(BF16)


def _mla_prep(proj, gq, gkv, wq1, wq2, wkv, cos_t, sin_t, *, tm=512):
    m = proj.shape[0]
    nseq = SEQ // tm
    qw = C_HEADS * (NOPE_DIM + LANE)
    est = 2 * (tm * 1024 * 2) + 2 * (CQ_PAD * qw * 2 + CQ_PAD * 512 * 2 + 128 * 1024 * 2) \
        + 2 * (2 * tm * qw * 2 + tm * 512 * 2) + 3 * tm * qw * 4
    const = lambda i: (0, 0)
    return pl.pallas_call(
        _mla_prep_kernel,
        grid=(m // tm,),
        in_specs=[
            pl.BlockSpec((tm, CQ_PAD), lambda i: (i, OFF_CQ // CQ_PAD)),
            pl.BlockSpec((tm, LANE), lambda i: (i, OFF_CKV // LANE)),
            pl.BlockSpec((tm, LANE), lambda i: (i, OFF_KR // LANE)),
            pl.BlockSpec((tm, LANE), lambda i: (i, OFF_KRS // LANE)),
            pl.BlockSpec((1, CQ_PAD), const),
            pl.BlockSpec((1, KV_LORA), const),
            pl.BlockSpec((CQ_PAD, qw), const),
            pl.BlockSpec((CQ_PAD, C_HEADS * LANE), const),
            pl.BlockSpec((KV_LORA, C_HEADS * (NOPE_DIM + V_DIM)), const),
            pl.BlockSpec((tm, LANE), lambda i: (i % nseq, 0)),
            pl.BlockSpec((tm, LANE), lambda i: (i % nseq, 0)),
        ],
        out_specs=[
            pl.BlockSpec((tm, qw), lambda i: (i, 0)),
            pl.BlockSpec((tm, qw), lambda i: (i, 0)),
            pl.BlockSpec((tm, C_HEADS * V_DIM), lambda i: (i, 0)),
        ],
        out_shape=[
            jax.ShapeDtypeStruct((m, qw), BF16),
            jax.ShapeDtypeStruct((m, qw), BF16),
            jax.ShapeDtypeStruct((m, C_HEADS * V_DIM), BF16),
        ],
        compiler_params=_params(("arbitrary",), est),
        name="mla_prep",
    )(proj, proj, proj, proj, gq, gkv, wq1, wq2, wkv, cos_t, sin_t)


def _paired_loop(n, body, init):
    def pair(p, carry):
        return body(2 * p + 1, body(2 * p, carry))
    carry = lax.fori_loop(0, n // 2, pair, init)
    return lax.fori_loop(2 * (n // 2), n, body, carry)


def _nt_dot(a, b):
    return lax.dot_general(a, b, (((1,), (1,)), ((), ())), preferred_element_type=F32)


def _two_pass_attention(i, *, n_heads, tq, tk, dv, logits_fn, mask_fn, v_fn, store_fn,
                        s_ref, mx_ref, acc_ref):
    nc = tk // LANE
    n_full = (i * tq) // tk
    nkb = ((i + 1) * tq + tk - 1) // tk

    mx_ref[...] = jnp.full(mx_ref.shape, NEG_INF, F32)

    def pass1(kb, masked):
        ks = pl.multiple_of(kb * tk, tk)
        mask = mask_fn(ks) if masked else None
        for h in range(n_heads):
            s = logits_fn(h, ks)
            if masked:
                s = jnp.where(mask, s, NEG_INF)
            s_ref[h, :, pl.ds(ks, tk)] = s
            mx = mx_ref[h]
            for c in range(nc):
                mx = jnp.maximum(mx, s[:, c * LANE:(c + 1) * LANE])
            mx_ref[h] = mx

    def pass1_full(kb, carry):
        pass1(kb, False)
        return carry

    def pass1_masked(kb, carry):
        pass1(kb, True)
        return carry

    _paired_loop(n_full, pass1_full, 0)
    lax.fori_loop(n_full, nkb, pass1_masked, 0)

    for h in range(n_heads):
        mx_ref[h] = jnp.broadcast_to(jnp.max(mx_ref[h], axis=-1, keepdims=True), (tq, LANE))
    acc_ref[...] = jnp.zeros(acc_ref.shape, F32)
    ones = jnp.ones((tk, LANE), BF16)

    def pass2(kb, carry):
        ks = pl.multiple_of(kb * tk, tk)
        for h in range(n_heads):
            m = mx_ref[h]
            p = jnp.concatenate(
                [jnp.exp2(s_ref[h, :, pl.ds(ks + c * LANE, LANE)] - m).astype(BF16) for c in range(nc)],
                axis=1)
            v1 = jnp.concatenate([v_fn(h, ks), ones], axis=1)
            acc_ref[h] += jnp.dot(p, v1, preferred_element_type=F32)
        return carry

    _paired_loop(nkb, pass2, 0)

    for h in range(n_heads):
        acc = acc_ref[h]
        store_fn(h, acc[:, :dv] / acc[:, dv:])


def _tile_iotas(tq, tk):
    return (lax.broadcasted_iota(jnp.int32, (tq, tk), 0), lax.broadcasted_iota(jnp.int32, (tq, tk), 1))


def _fox_kernel(q_ref, k_ref, v_ref, ccol_ref, crow_ref, o_ref,
                s_ref, mx_ref, acc_ref, cq_ref, *, tq, tk):
    i = pl.program_id(1)
    nc = tk // LANE
    head = lambda h: slice(h * HEAD_DIM, (h + 1) * HEAD_DIM)
    for h in range(B_HEADS):
        cq_ref[h] = jnp.broadcast_to(ccol_ref[:, MISC_FL + h:MISC_FL + h + 1] * LOG2E, (tq, LANE))

    def logits_fn(h, ks):
        s = _nt_dot(q_ref[:, head(h)], k_ref[pl.ds(ks, tk), head(h)])
        return s + jnp.concatenate([cq_ref[h]] * nc, axis=1) - crow_ref[h:h + 1, pl.ds(ks, tk)] * LOG2E

    def mask_fn(ks):
        rows, cols = _tile_iotas(tq, tk)
        return cols + (ks - i * tq) <= rows

    def store_fn(h, y):
        o_ref[:, head(h)] = y.astype(o_ref.dtype)

    _two_pass_attention(i, n_heads=B_HEADS, tq=tq, tk=tk, dv=HEAD_DIM, logits_fn=logits_fn,
                        mask_fn=mask_fn, v_fn=lambda h, ks: v_ref[pl.ds(ks, tk), head(h)],
                        store_fn=store_fn, s_ref=s_ref, mx_ref=mx_ref, acc_ref=acc_ref)


def _fox_attention(proj, ccol, crow, batch, *, tq=256, tk=512):
    m = proj.shape[0]
    nq = SEQ // tq
    w = B_HEADS * HEAD_DIM
    state = B_HEADS * tq * LANE * 4
    est = (4 * tq * w * 2 + 4 * SEQ * w * 2 + 2 * tq * LANE * 4 + 2 * 8 * SEQ * 4
           + B_HEADS * tq * SEQ * 4 + 4 * state + 6 * tq * tk * 4)
    return pl.pallas_call(
        functools.partial(_fox_kernel, tq=tq, tk=tk),
        grid=(batch, nq),
        in_specs=[
            pl.BlockSpec((tq, w), lambda b, i: (b * nq + i, OFF_QB // w)),
            pl.BlockSpec((SEQ, w), lambda b, i: (b, OFF_KB // w)),
            pl.BlockSpec((SEQ, w), lambda b, i: (b, OFF_VB // w)),
            pl.BlockSpec((tq, LANE), lambda b, i: (b * nq + i, 0)),
            pl.BlockSpec((None, B_HEADS, SEQ), lambda b, i: (b, 0, 0)),
        ],
        out_specs=pl.BlockSpec((tq, w), lambda b, i: (b * nq + i, 0)),
        out_shape=jax.ShapeDtypeStruct((m, w), BF16),
        scratch_shapes=[
            pltpu.VMEM((B_HEADS, tq, SEQ), F32),
            pltpu.VMEM((B_HEADS, tq, LANE), F32),
            pltpu.VMEM((B_HEADS, tq, HEAD_DIM + LANE), F32),
            pltpu.VMEM((B_HEADS, tq, LANE), F32),
        ],
        compiler_params=_params(("arbitrary", "arbitrary"), est),
        name="fox_attention",
    )(proj, proj, proj, ccol, crow)


def _mla_kernel(q_ref, k_ref, v_ref, o_ref, s_ref, mx_ref, acc_ref, *, tq, tk):
    i = pl.program_id(1)
    qw = NOPE_DIM + LANE
    qhead = lambda h: slice(h * qw, (h + 1) * qw)
    vhead = lambda h: slice(h * V_DIM, (h + 1) * V_DIM)

    def logits_fn(h, ks):
        return _nt_dot(q_ref[:, qhead(h)], k_ref[pl.ds(ks, tk), qhead(h)])

    def mask_fn(ks):
        rows, cols = _tile_iotas(tq, tk)
        return (cols + ks) // CHUNK <= (rows + i * tq) // CHUNK

    def store_fn(h, y):
        o_ref[:, vhead(h)] = y.astype(o_ref.dtype)

    _two_pass_attention(i, n_heads=C_HEADS, tq=tq, tk=tk, dv=V_DIM, logits_fn=logits_fn,
                        mask_fn=mask_fn, v_fn=lambda h, ks: v_ref[pl.ds(ks, tk), vhead(h)],
                        store_fn=store_fn, s_ref=s_ref, mx_ref=mx_ref, acc_ref=acc_ref)


def _mla_attention(qc, kc, vc, batch, *, tq=256, tk=512):
    m = qc.shape[0]
    nq = SEQ // tq
    qw = C_HEADS * (NOPE_DIM + LANE)
    vw = C_HEADS * V_DIM
    state = C_HEADS * tq * LANE * 4
    est = (2 * tq * qw * 2 + 2 * SEQ * qw * 2 + 2 * SEQ * vw * 2 + 2 * tq * vw * 2
           + C_HEADS * tq * SEQ * 4 + 3 * state + 6 * tq * tk * 4)
    return pl.pallas_call(
        functools.partial(_mla_kernel, tq=tq, tk=tk),
        grid=(batch, nq),
        in_specs=[
            pl.BlockSpec((tq, qw), lambda b, i: (b * nq + i, 0)),
            pl.BlockSpec((SEQ, qw), lambda b, i: (b, 0)),
            pl.BlockSpec((SEQ, vw), lambda b, i: (b, 0)),
        ],
        out_specs=pl.BlockSpec((tq, vw), lambda b, i: (b * nq + i, 0)),
        out_shape=jax.ShapeDtypeStruct((m, vw), BF16),
        scratch_shapes=[
            pltpu.VMEM((C_HEADS, tq, SEQ), F32),
            pltpu.VMEM((C_HEADS, tq, LANE), F32),
            pltpu.VMEM((C_HEADS, tq, V_DIM + LANE), F32),
        ],
        compiler_params=_params(("arbitrary", "arbitrary"), est),
        name="mla_attention",
    )(qc, kc, vc)


def _t5_tiles_kernel(bucket_ref, table_ref, o_ref):
    for d in range(3):
        bucket = bucket_ref[d]
        for h in range(A_HEADS):
            acc = jnp.zeros(bucket.shape, F32)
            for nb in range(NUM_BUCKETS):
                acc = jnp.where(bucket == nb, table_ref[nb, h] * LOG2E, acc)
            o_ref[d, h] = acc


def _t5_tiles(bucket_tiles, t5_bias, *, t=ATT_T):
    return pl.pallas_call(
        _t5_tiles_kernel,
        in_specs=[
            pl.BlockSpec(memory_space=pltpu.VMEM),
            pl.BlockSpec(memory_space=pltpu.SMEM),
        ],
        out_specs=pl.BlockSpec(memory_space=pltpu.VMEM),
        out_shape=jax.ShapeDtypeStruct((3, A_HEADS, t, t), F32),
        name="t5_tiles",
    )(bucket_tiles, t5_bias)


def _dsa_kernel(qi_ref, klo_ref, khi_ref, misc_ref, qa_ref, ka_ref, va_ref, bias_ref, o_ref,
                sc_ref, lg_ref, acc_ref, *, t):
    i = pl.program_id(1)
    groups = t // 8
    key = lax.broadcasted_iota(jnp.int32, (t, t), 0)
    qry = lax.broadcasted_iota(jnp.int32, (t, t), 1)
    admissible = (key // CHUNK) <= (qry // CHUNK)

    def fold(x, op):
        return op(x.reshape(groups, 8, t), axis=0)

    def over_keys(x8, op):
        return jnp.broadcast_to(op(x8, axis=0, keepdims=True), (8, t))

    w_t = misc_ref[...].T[MISC_WI:MISC_WI + IDX_HEADS, :] * (IDX_HEADS ** -0.5 * IDX_DIM ** -0.5)

    def block_scores(kb):
        ks = pl.multiple_of(kb * t, t)
        klo = klo_ref[pl.ds(ks, t), :]
        khi = khi_ref[pl.ds(ks, t), :]
        acc = jnp.zeros((t, t), F32)
        for pair in range(IDX_HEADS // 2):
            qp = qi_ref[:, pair * LANE:(pair + 1) * LANE]
            for sub, kk in enumerate((klo, khi)):
                h = 2 * pair + sub
                acc = acc + jnp.maximum(_nt_dot(kk, qp), 0.0) * w_t[h:h + 1, :]
        return ks, acc

    def score_body(kb, carry):
        mn, mx = carry
        ks, acc = block_scores(kb)
        sc_ref[pl.ds(ks, t), :] = acc
        return jnp.minimum(mn, fold(acc, jnp.min)), jnp.maximum(mx, fold(acc, jnp.max))

    mn, mx = _paired_loop(i, score_body,
                          (jnp.full((8, t), -SELECT_MIN, F32), jnp.full((8, t), SELECT_MIN, F32)))
    ks, acc = block_scores(i)
    sc_ref[pl.ds(ks, t), :] = jnp.where(admissible, acc, NEG_INF)
    mn = jnp.minimum(mn, fold(jnp.where(admissible, acc, -SELECT_MIN), jnp.min))
    mx = jnp.maximum(mx, fold(jnp.where(admissible, acc, SELECT_MIN), jnp.max))

    def count_ge(thr):
        def body(kb, cnt):
            ks = pl.multiple_of(kb * t, t)
            hit = jnp.where(sc_ref[pl.ds(ks, t), :].reshape(groups, 8, t) >= thr[None], 1.0, 0.0)
            return cnt + jnp.sum(hit, axis=0)
        return over_keys(_paired_loop(i + 1, body, jnp.zeros((8, t), F32)), jnp.sum)

    k_sel = float(TOPK_MAX)
    search = i * t >= TOPK_MAX
    lo0 = jnp.where(search, over_keys(mn, jnp.min), SELECT_MIN)
    hi0 = jnp.where(search, over_keys(mx, jnp.max), SELECT_MIN)

    def midpoint(lo, hi):
        return 0.5 * lo + 0.5 * hi

    def any_active(lo, hi):
        mid = midpoint(lo, hi)
        return jnp.max(jnp.where((mid > lo) & (mid < hi), 1.0, 0.0)) > 0.5

    def search_cond(state):
        return jnp.logical_and(state[0] < BISECT_MAX_ITERS, state[1])

    def search_body(state):
        it, _, lo, hi, c_lo = state
        for _ in range(SEARCH_STEPS_PER_CHECK):
            mid = midpoint(lo, hi)
            cnt = count_ge(mid)
            ge = cnt >= k_sel
            lo = jnp.where(ge, mid, lo)
            c_lo = jnp.where(ge, cnt, c_lo)
            hi = jnp.where(cnt > k_sel, hi, mid)
        return it + SEARCH_STEPS_PER_CHECK, any_active(lo, hi), lo, hi, c_lo

    n_adm = (((i * t + qry[:8]) // CHUNK + 1) * CHUNK).astype(F32)
    _, _, lo, hi, c_lo = lax.while_loop(search_cond, search_body,
                                        (jnp.int32(0), any_active(lo0, hi0), lo0, hi0, n_adm))
    c_hi = count_ge(hi)
    thr = jnp.where(c_hi >= k_sel, hi, lo)
    c_thr = jnp.where(c_hi >= k_sel, c_hi, c_lo)

    @pl.when(jnp.max(jnp.where(c_thr > k_sel, 1.0, 0.0)) > 0.5)
    def _():
        key_in_block = (lax.broadcasted_iota(jnp.int32, (groups, 8, t), 0) * 8
                        + lax.broadcasted_iota(jnp.int32, (groups, 8, t), 1))

        def count_where(pred):
            def body(kb, cnt):
                ks = pl.multiple_of(kb * t, t)
                blk = sc_ref[pl.ds(ks, t), :].reshape(groups, 8, t)
                return cnt + jnp.sum(jnp.where(pred(blk, ks), 1.0, 0.0), axis=0)
            return over_keys(lax.fori_loop(0, i + 1, body, jnp.zeros((8, t), F32)), jnp.sum)

        def tied_before(bound):
            return lambda blk, ks: (blk == thr[None]) & ((key_in_block + ks).astype(F32) < bound[None])

        keep = k_sel - count_where(lambda blk, ks: blk > thr[None])

        def cut_body(_, bounds):
            below, above = bounds
            mid = jnp.floor(0.5 * (below + above))
            enough = count_where(tied_before(mid)) >= keep
            return jnp.where(enough, below, mid), jnp.where(enough, mid, above)

        _, cut = lax.fori_loop(0, SEQ.bit_length(), cut_body,
                               (jnp.zeros((8, t), F32), jnp.full((8, t), float(SEQ), F32)))

        def drop_body(kb, carry):
            ks = pl.multiple_of(kb * t, t)
            blk = sc_ref[pl.ds(ks, t), :].reshape(groups, 8, t)
            drop = (blk == thr[None]) & ((key_in_block + ks).astype(F32) >= cut[None])
            sc_ref[pl.ds(ks, t), :] = jnp.where(drop, NEG_INF, blk).reshape(t, t)
            return carry

        lax.fori_loop(0, i + 1, drop_body, 0)

    head = lambda h: slice(h * HEAD_DIM, (h + 1) * HEAD_DIM)
    acc_ref[...] = jnp.zeros(acc_ref.shape, F32)

    def logits_pass(kb, mx):
        ks = pl.multiple_of(kb * t, t)
        dist = jnp.minimum(i - kb, 2)
        k_blk = ka_ref[pl.ds(ks, t), :]
        sel = sc_ref[pl.ds(ks, t), :] >= thr[0:1, :]
        out = []
        for h in range(A_HEADS):
            s = _nt_dot(k_blk, qa_ref[:, head(h)]) + bias_ref[dist, h]
            s = jnp.where(sel, s, NEG_INF)
            lg_ref[h, pl.ds(ks, t), :] = s
            out.append(jnp.maximum(mx[h], fold(s, jnp.max)))
        return tuple(out)

    mx = _paired_loop(i + 1, logits_pass,
                      tuple(jnp.full((8, t), NEG_INF, F32) for _ in range(A_HEADS)))
    m = [over_keys(mx[h], jnp.max)[0:1, :] for h in range(A_HEADS)]

    ones_rows = jnp.ones((DSA_ONES_ROWS, t), BF16)

    def value_pass(kb, carry):
        ks = pl.multiple_of(kb * t, t)
        v_t = va_ref[pl.ds(ks, t), :].astype(F32).T.astype(BF16)
        v1_t = jnp.concatenate([v_t, ones_rows], axis=0)
        for h in range(A_HEADS):
            p = jnp.exp2(lg_ref[h, pl.ds(ks, t), :] - m[h])
            acc_ref[h] += jnp.dot(v1_t, p.astype(BF16), preferred_element_type=F32)
        return carry

    _paired_loop(i + 1, value_pass, 0)
    for h in range(A_HEADS):
        acc = acc_ref[h]
        o_ref[:, head(h)] = (acc[:HEAD_DIM] / acc[HEAD_DIM:HEAD_DIM + 1]).T.astype(o_ref.dtype)


def _dsa_attention(proj, misc, bias_tiles, batch, *, t=ATT_T):
    m = proj.shape[0]
    nq = SEQ // t
    qiw = IDX_HEADS * IDX_DIM
    qaw = A_HEADS * HEAD_DIM
    est = (2 * t * qiw * 2 + 2 * 4 * SEQ * LANE * 2 + 2 * t * LANE * 4 + 2 * t * qaw * 2
           + 2 * 3 * A_HEADS * t * t * 4 + 2 * t * qaw * 2
           + (1 + A_HEADS) * t * SEQ * 4 + A_HEADS * HEAD_DIM * t * 4 + 12 * t * t * 4)
    kblock = lambda off: pl.BlockSpec((SEQ, LANE), lambda b, i: (b, off // LANE))
    return pl.pallas_call(
        functools.partial(_dsa_kernel, t=t),
        grid=(batch, nq),
        in_specs=[
            pl.BlockSpec((t, qiw), lambda b, i: (b * nq + i, OFF_QI // qiw)),
            kblock(OFF_KI_LO),
            kblock(OFF_KI_HI),
            pl.BlockSpec((t, LANE), lambda b, i: (b * nq + i, 0)),
            pl.BlockSpec((t, qaw), lambda b, i: (b * nq + i, OFF_QA // qaw)),
            kblock(OFF_KA),
            kblock(OFF_VA),
            pl.BlockSpec((3, A_HEADS, t, t), lambda b, i: (0, 0, 0, 0)),
        ],
        out_specs=pl.BlockSpec((t, qaw), lambda b, i: (b * nq + i, 0)),
        out_shape=jax.ShapeDtypeStruct((m, qaw), BF16),
        scratch_shapes=[
            pltpu.VMEM((SEQ, t), F32),
            pltpu.VMEM((A_HEADS, SEQ, t), F32),
            pltpu.VMEM((A_HEADS, HEAD_DIM + DSA_ONES_ROWS, t), F32),
        ],
        compiler_params=_params(("arbitrary", "arbitrary"), est),
        name="dsa_attention",
    )(proj, proj, proj, misc, proj, proj, proj, bias_tiles)


def _merge_kernel(x_ref, ya_ref, yb_ref, yc_ref, ga_ref, gb_ref, gc_ref,
                  wa_ref, wb_ref, wc_ref, wo_ref, gn_ref, o_ref, hn_ref):
    def branch(y_ref, w_ref, g_ref):
        y = jnp.dot(y_ref[...], w_ref[...], preferred_element_type=F32)
        return jax.nn.sigmoid(g_ref[...].astype(F32)) * y

    merged = branch(ya_ref, wa_ref, ga_ref) + branch(yb_ref, wb_ref, gb_ref) + branch(yc_ref, wc_ref, gc_ref)
    x_new = x_ref[...] + jnp.dot(merged.astype(BF16), wo_ref[...], preferred_element_type=F32)
    o_ref[...] = x_new
    ms = jnp.mean(x_new * x_new, axis=-1, keepdims=True)
    hn_ref[...] = (x_new * lax.rsqrt(ms + EPS) * gn_ref[...]).astype(hn_ref.dtype)


def _merge(x2, ya, yb, yc, proj, wa, wb, wc, wo, g_next, *, tm=256):
    m = x2.shape[0]
    wbytes = (wa.size + wb.size + wc.size + wo.size) * 2
    est = 2 * wbytes + 4 * tm * D_MODEL * 4 + 2 * tm * 2048 * 2 + 8 * tm * D_MODEL * 2 + 4 * tm * D_MODEL * 4
    const = lambda i: (0, 0)
    gate = lambda k: pl.BlockSpec((tm, D_MODEL), lambda i: (i, OFF_GL // D_MODEL + k))
    return pl.pallas_call(
        _merge_kernel,
        grid=(m // tm,),
        in_specs=[
            pl.BlockSpec((tm, D_MODEL), lambda i: (i, 0)),
            pl.BlockSpec((tm, ya.shape[1]), lambda i: (i, 0)),
            pl.BlockSpec((tm, yb.shape[1]), lambda i: (i, 0)),
            pl.BlockSpec((tm, yc.shape[1]), lambda i: (i, 0)),
            gate(0), gate(1), gate(2),
            pl.BlockSpec(wa.shape, const),
            pl.BlockSpec(wb.shape, const),
            pl.BlockSpec(wc.shape, const),
            pl.BlockSpec(wo.shape, const),
            pl.BlockSpec((1, D_MODEL), const),
        ],
        out_specs=[
            pl.BlockSpec((tm, D_MODEL), lambda i: (i, 0)),
            pl.BlockSpec((tm, D_MODEL), lambda i: (i, 0)),
        ],
        out_shape=[
            jax.ShapeDtypeStruct((m, D_MODEL), F32),
            jax.ShapeDtypeStruct((m, D_MODEL), BF16),
        ],
        compiler_params=_params(("arbitrary",), est),
        name="merge",
    )(x2, ya, yb, yc, proj, proj, proj, wa, wb, wc, wo, g_next)


def _ffn_kernel(x_ref, h_ref, wu_ref, wv_ref, cw_ref, wd_ref, fg_ref, o_ref,
                ubuf_ref, halo_ref, *, tm, tf, final_norm):
    i = pl.program_id(0)
    j = pl.program_id(1)
    nj = pl.num_programs(1)

    @pl.when(j == 0)
    def _():
        o_ref[...] = x_ref[...]

        @pl.when((i % (SEQ // tm)) == 0)
        def _():
            halo_ref[...] = jnp.zeros(halo_ref.shape, F32)

    h = h_ref[...]
    u = jnp.dot(h, wu_ref[...], preferred_element_type=F32)
    v = jnp.dot(h, wv_ref[...], preferred_element_type=F32)

    ubuf_ref[0:8, :] = halo_ref[j]
    ubuf_ref[8:, :] = u
    halo_ref[j] = u[tm - 8:, :]
    conv = (cw_ref[0:1, :] * ubuf_ref[6:6 + tm, :] + cw_ref[1:2, :] * ubuf_ref[7:7 + tm, :]
            + cw_ref[2:3, :] * u + cw_ref[3:4, :])
    act = (jax.nn.gelu(conv) * v).astype(BF16)
    o_ref[...] += jnp.dot(act, wd_ref[...], preferred_element_type=F32)

    if final_norm:
        @pl.when(j == nj - 1)
        def _():
            _rmsnorm_rows(o_ref, fg_ref, o_ref, tm)


def _ffn(x2, h2, w_up, cw, w_down, final_g, *, final_norm, tm=1024, tf=512):
    m = x2.shape[0]
    nj = D_FF // tf
    est = (3 * tm * D_MODEL * 4 + 2 * tm * D_MODEL * 2 + 2 * 2 * D_MODEL * tf * 2 + 2 * tf * D_MODEL * 2
           + (tm + 8) * tf * 4 + nj * 8 * tf * 4 + 4 * tm * tf * 4)
    return pl.pallas_call(
        functools.partial(_ffn_kernel, tm=tm, tf=tf, final_norm=final_norm),
        grid=(m // tm, nj),
        in_specs=[
            pl.BlockSpec((tm, D_MODEL), lambda i, j: (i, 0), pipeline_mode=pl.Buffered(1)),
            pl.BlockSpec((tm, D_MODEL), lambda i, j: (i, 0)),
            pl.BlockSpec((D_MODEL, tf), lambda i, j: (0, j)),
            pl.BlockSpec((D_MODEL, tf), lambda i, j: (0, nj + j)),
            pl.BlockSpec((8, tf), lambda i, j: (0, j)),
            pl.BlockSpec((tf, D_MODEL), lambda i, j: (j, 0)),
            pl.BlockSpec((1, D_MODEL), lambda i, j: (0, 0)),
        ],
        out_specs=pl.BlockSpec((tm, D_MODEL), lambda i, j: (i, 0)),
        out_shape=jax.ShapeDtypeStruct((m, D_MODEL), F32),
        scratch_shapes=[
            pltpu.VMEM((tm + 8, tf), F32),
            pltpu.VMEM((nj, 8, tf), F32),
        ],
        compiler_params=_params(("arbitrary", "arbitrary"), est),
        name="conv_ffn",
    )(x2, h2, w_up, w_up, cw, w_down, final_g)


def _t5_bucket(rel):
    nb = NUM_BUCKETS // 2
    max_exact = nb // 2
    base = jnp.where(rel > 0, nb, 0)
    n = jnp.abs(rel)
    nf = jnp.maximum(n, 1).astype(F32)
    large = max_exact + (jnp.log(nf / max_exact) / math.log(MAX_DISTANCE / max_exact)
                         * (nb - max_exact)).astype(jnp.int32)
    large = jnp.minimum(large, nb - 1)
    return base + jnp.where(n < max_exact, n, large)


def _bucket_tiles(t):
    key = jnp.arange(t, dtype=jnp.int32)[:, None]
    qry = jnp.arange(t, dtype=jnp.int32)[None, :]
    return jnp.stack([_t5_bucket(key - qry - d * t) for d in range(3)])


def _rope_tables():
    half = ROPE_DIM // 2
    inv = ROPE_THETA ** (-jnp.arange(half, dtype=F32) / half)
    ang = jnp.arange(SEQ, dtype=jnp.int32).astype(F32)[:, None] * inv[None, :]
    cos, sin = jnp.cos(ang), jnp.sin(ang)
    pad = jnp.zeros((SEQ, LANE - ROPE_DIM), F32)
    return jnp.concatenate([cos, cos, pad], axis=1), jnp.concatenate([-sin, sin, pad], axis=1)


def _swap_halves(w):
    half = w.shape[-1] // 2
    return jnp.concatenate([w[..., half:], w[..., :half]], axis=-1)


def _pack_w_in_kernel(w_ref, o_ref):
    off = np.cumsum((0,) + IN_SIZES)
    seg = lambda k: w_ref[off[k]:off[k + 1], :]
    qa, ka, va, qi, ki, wi, qb, kb, vb, fl, cq, ckv, kr, gl = [seg(k) for k in range(len(IN_SIZES))]
    qa, qb = qa * Q_SCALE, qb * Q_SCALE
    z = lambda n: jnp.zeros((n, w_ref.shape[1]), F32)
    half = ROPE_DIM // 2
    groups = [
        (OFF_QI, [qi]), (OFF_QB, [qb]), (OFF_KB, [kb]), (OFF_VB, [vb]), (OFF_GL, [gl]), (OFF_QA, [qa]),
        (OFF_CQ, [cq, z(CQ_PAD - Q_LORA)]), (OFF_KA, [ka]), (OFF_VA, [va]), (OFF_KI_LO, [ki, z(64)]),
        (OFF_CKV, [ckv]), (OFF_KR, [kr, z(64)]), (OFF_KRS, [kr[half:], kr[:half], z(64)]),
        (OFF_MISC, [wi, fl, z(LANE - IDX_HEADS - B_HEADS)]), (OFF_KI_HI, [z(64), ki]),
    ]
    for start, pieces in groups:
        block = pieces[0] if len(pieces) == 1 else jnp.concatenate(pieces, axis=0)
        o_ref[start:start + block.shape[0], :] = block.astype(o_ref.dtype)


def _pack_w_in(w_in, l, *, cols=256):
    w_t = jnp.swapaxes(w_in, 1, 2)
    _, n, d = w_t.shape
    return pl.pallas_call(
        _pack_w_in_kernel,
        grid=(d // cols,),
        in_specs=[pl.BlockSpec((None, n, cols), lambda i: (l, 0, i))],
        out_specs=pl.BlockSpec((N_PACK, cols), lambda i: (0, i)),
        out_shape=jax.ShapeDtypeStruct((N_PACK, d), BF16),
        compiler_params=_params(("arbitrary",), 2 * cols * (n * 4 + N_PACK * 2)),
        name="pack_w_in",
    )(w_t)


def _pack_w_uq(w):
    w = jnp.pad(w, ((0, CQ_PAD - Q_LORA), (0, 0))).reshape(CQ_PAD, C_HEADS, NOPE_DIM + ROPE_DIM)
    z = jnp.zeros((CQ_PAD, C_HEADS, LANE - ROPE_DIM), w.dtype)
    rope = w[..., NOPE_DIM:]
    wq1 = jnp.concatenate([w, z], axis=-1).reshape(CQ_PAD, -1)
    wq2 = jnp.concatenate([_swap_halves(rope), z], axis=-1).reshape(CQ_PAD, -1)
    return wq1.astype(BF16), wq2.astype(BF16)


def kernel(x, norm_mix_g, w_in, b_forget, g_cq, g_ckv, w_uq, w_ukv, w_branch_a, w_branch_b, w_branch_c,
           w_o, norm_ffn_g, w_up, conv_w, conv_b, w_down, t5_bias, final_g):
    batch, seq, d = x.shape
    assert (seq, d) == (SEQ, D_MODEL)
    x2 = x.reshape(batch * seq, d)

    bias_tiles = _t5_tiles(_bucket_tiles(ATT_T), t5_bias)
    cos_t, sin_t = _rope_tables()
    final_row = final_g.reshape(1, D_MODEL)

    for l in range(DEPTH):
        casts = [(w_up, True, 11), (w_down, False, 11), (w_o, True, 8),
                 (w_branch_a, True, 8), (w_branch_b, True, 8), (w_branch_c, True, 8)]
        proj, misc, w_up_l, w_down_l, w_o_l, w_a_l, w_b_l, w_c_l = _inproj(
            x2, norm_mix_g[l].reshape(1, D_MODEL), _pack_w_in(w_in, l), casts, l)

        fbias = jnp.zeros((1, LANE), F32).at[0, MISC_FL:MISC_FL + B_HEADS].set(b_forget[l])
        ccol, crow = _forget_cumsum(misc, fbias, batch)

        wq1, wq2 = _pack_w_uq(w_uq[l])
        gq = jnp.pad(g_cq[l], (0, CQ_PAD - Q_LORA)).reshape(1, CQ_PAD)
        qc, kc, vc = _mla_prep(proj, gq, g_ckv[l].reshape(1, KV_LORA), wq1, wq2,
                               w_ukv[l].astype(BF16), cos_t, sin_t)

        ya = _dsa_attention(proj, misc, bias_tiles, batch)
        yb = _fox_attention(proj, ccol, crow, batch)
        yc = _mla_attention(qc, kc, vc, batch)

        x2, h2 = _merge(x2, ya, yb, yc, proj, w_a_l, w_b_l, w_c_l, w_o_l,
                        norm_ffn_g[l].reshape(1, D_MODEL))

        cw = jnp.concatenate([conv_w[l], conv_b[l][None, :], jnp.zeros((4, D_FF), F32)], axis=0)
        x2 = _ffn(x2, h2, w_up_l, cw, w_down_l, final_row, final_norm=(l == DEPTH - 1))

    return x2.reshape(batch, seq, d)
```

```python
import functools
import math

import jax
import jax.numpy as jnp
import numpy as np
from jax import lax
from jax.experimental import pallas as pl
from jax.experimental.pallas import tpu as pltpu

F32 = jnp.float32
BF16 = jnp.bfloat16

D_MODEL = 2048
SEQ = 2048
DEPTH = 2
CHUNK = 64
HEAD_DIM = 128
EPS = 1e-6
NEG_INF = -1e30

A_HEADS = 4
IDX_HEADS = 16
IDX_DIM = 64
TOPK_MAX = 256
NUM_BUCKETS = 32
MAX_DISTANCE = 128
B_HEADS = 8
C_HEADS = 4
Q_LORA = 448
KV_LORA = 128
NOPE_DIM = 128
ROPE_DIM = 64
V_DIM = 128
ROPE_THETA = 10000.0
D_FF = 5632

IN_SIZES = (
    A_HEADS * HEAD_DIM, HEAD_DIM, HEAD_DIM,
    IDX_HEADS * IDX_DIM, IDX_DIM, IDX_HEADS,
    B_HEADS * HEAD_DIM, B_HEADS * HEAD_DIM, B_HEADS * HEAD_DIM, B_HEADS,
    Q_LORA, KV_LORA, ROPE_DIM,
    3 * D_MODEL,
)

LOG2E = math.log2(math.e)
Q_SCALE = HEAD_DIM ** -0.5 * LOG2E
Q_SCALE_MLA = (NOPE_DIM + ROPE_DIM) ** -0.5 * LOG2E
LANE = 128
V7X_VMEM_BYTES = 64 * 1024 * 1024

OFF_QI = 0
OFF_QB = 1024
OFF_KB = 2048
OFF_VB = 3072
OFF_GL = 4096
OFF_QA = 10240
OFF_CQ = 10752
OFF_KA = 11264
OFF_VA = 11392
OFF_KI_LO = 11520
OFF_CKV = 11648
OFF_KR = 11776
OFF_KRS = 11904
OFF_MISC = 12032
OFF_KI_HI = 12160
N_PACK = 12288
CQ_PAD = 512
MISC_WI = 0
MISC_FL = IDX_HEADS

ATT_T = 256
SELECT_MIN = -1e29
BISECT_MAX_ITERS = 512
SEARCH_STEPS_PER_CHECK = 2
DSA_ONES_ROWS = 16


def _vmem_limit(estimate_bytes):
    limit = V7X_VMEM_BYTES - (6 << 20)
    assert estimate_bytes <= limit, estimate_bytes
    return limit


def _params(semantics, vmem_estimate):
    return pltpu.CompilerParams(dimension_semantics=semantics,
                                vmem_limit_bytes=_vmem_limit(vmem_estimate))


def _rmsnorm_rows(x_ref, g_ref, out_ref, rows):
    def body(c, carry):
        r = pl.multiple_of(c * 128, 128)
        x = x_ref[pl.ds(r, 128), :]
        ms = jnp.mean(x * x, axis=-1, keepdims=True)
        out_ref[pl.ds(r, 128), :] = (x * lax.rsqrt(ms + EPS) * g_ref[...]).astype(out_ref.dtype)
        return carry
    lax.fori_loop(0, rows // 128, body, 0)


def _inproj_kernel(*refs, tm, tn, n_cast):
    x_ref, g_ref, w_ref = refs[:3]
    cast_in = refs[3:3 + n_cast]
    o_ref, misc_ref = refs[3 + n_cast:5 + n_cast]
    cast_out = refs[5 + n_cast:5 + 2 * n_cast]
    h_ref = refs[5 + 2 * n_cast]
    j = pl.program_id(1)

    @pl.when(j == 0)
    def _():
        _rmsnorm_rows(x_ref, g_ref, h_ref, tm)

    acc = _nt_dot(h_ref[...], w_ref[...])
    o_ref[...] = acc.astype(o_ref.dtype)

    for src, dst in zip(cast_in, cast_out):
        dst[...] = src[...].astype(dst.dtype)

    @pl.when(j == OFF_MISC // tn)
    def _():
        lo = OFF_MISC % tn
        misc_ref[...] = acc[:, lo:lo + LANE]


def _cast_specs(w_stack, l, ni, nj, rows_follow_i, n_chunks):
    _, r, c = w_stack.shape
    assert n_chunks <= nj
    chunk = lambda j: jnp.minimum(j, n_chunks - 1)
    if rows_follow_i:
        block = (r // ni, c // n_chunks)
        idx = lambda i, j: (i, chunk(j))
    else:
        block = (r // n_chunks, c // ni)
        idx = lambda i, j: (chunk(j), i)
    assert block[0] % 16 == 0 and block[1] % LANE == 0, block
    in_spec = pl.BlockSpec((None,) + block, lambda i, j: (l,) + idx(i, j))
    out_spec = pl.BlockSpec(block, idx)
    return in_spec, out_spec, jax.ShapeDtypeStruct((r, c), BF16), block[0] * block[1]


def _inproj(x2, g, w_pack, casts, l, *, tm=1024, tn=1024):
    m = x2.shape[0]
    ni, nj = m // tm, N_PACK // tn
    plans = [_cast_specs(w, l, ni, nj, rows_i, n) for w, rows_i, n in casts]
    est = (2 * tm * D_MODEL * 4 + tm * D_MODEL * 2 + 2 * D_MODEL * tn * 2 + 2 * tm * tn * 2 + tm * tn * 4
           + sum(2 * 6 * p[3] for p in plans))
    return pl.pallas_call(
        functools.partial(_inproj_kernel, tm=tm, tn=tn, n_cast=len(casts)),
        grid=(ni, nj),
        in_specs=[
            pl.BlockSpec((tm, D_MODEL), lambda i, j: (i, 0)),
            pl.BlockSpec((1, D_MODEL), lambda i, j: (0, 0)),
            pl.BlockSpec((tn, D_MODEL), lambda i, j: (j, 0)),
        ] + [p[0] for p in plans],
        out_specs=[
            pl.BlockSpec((tm, tn), lambda i, j: (i, j)),
            pl.BlockSpec((tm, LANE), lambda i, j: (i, 0)),
        ] + [p[1] for p in plans],
        out_shape=[
            jax.ShapeDtypeStruct((m, N_PACK), BF16),
            jax.ShapeDtypeStruct((m, LANE), F32),
        ] + [p[2] for p in plans],
        scratch_shapes=[pltpu.VMEM((tm, D_MODEL), BF16)],
        compiler_params=_params(("arbitrary", "arbitrary"), est),
        name="inproj",
    )(x2, g, w_pack, *[w for w, _, _ in casts])


def _forget_cumsum_kernel(misc_ref, bias_ref, ccol_ref, crow_ref, *, blk):
    rows = lax.broadcasted_iota(jnp.int32, (blk, blk), 0)
    cols = lax.broadcasted_iota(jnp.int32, (blk, blk), 1)
    tri = jnp.where(rows >= cols, 1.0, 0.0).astype(BF16)
    carry = jnp.zeros((1, LANE), F32)
    for c in range(SEQ // blk):
        z = misc_ref[c * blk:(c + 1) * blk, :] + bias_ref[...]
        lf = jnp.minimum(z, 0.0) - jnp.log1p(jnp.exp(-jnp.abs(z)))
        p0 = lf.astype(BF16)
        r1 = lf - p0.astype(F32)
        p1 = r1.astype(BF16)
        p2 = (r1 - p1.astype(F32)).astype(BF16)
        cs = (jnp.dot(tri, p0, preferred_element_type=F32)
              + jnp.dot(tri, p1, preferred_element_type=F32)
              + jnp.dot(tri, p2, preferred_element_type=F32)) + carry
        ccol_ref[c * blk:(c + 1) * blk, :] = cs
        crow_ref[:, c * blk:(c + 1) * blk] = cs.T[MISC_FL:MISC_FL + B_HEADS, :]
        carry = cs[blk - 1:blk, :]


def _forget_cumsum(misc, bias_row, batch, *, blk=256):
    est = 4 * SEQ * LANE * 4 + 2 * 8 * SEQ * 4
    return pl.pallas_call(
        functools.partial(_forget_cumsum_kernel, blk=blk),
        grid=(batch,),
        in_specs=[
            pl.BlockSpec((SEQ, LANE), lambda b: (b, 0)),
            pl.BlockSpec((1, LANE), lambda b: (0, 0)),
        ],
        out_specs=[
            pl.BlockSpec((SEQ, LANE), lambda b: (b, 0)),
            pl.BlockSpec((None, B_HEADS, SEQ), lambda b: (b, 0, 0)),
        ],
        out_shape=[
            jax.ShapeDtypeStruct((batch * SEQ, LANE), F32),
            jax.ShapeDtypeStruct((batch, B_HEADS, SEQ), F32),
        ],
        compiler_params=_params(("arbitrary",), est),
        name="forget_cumsum",
    )(misc, bias_row)


def _mla_prep_kernel(cq_ref, ckv_ref, kr_ref, krs_ref, gq_ref, gkv_ref, wq1_ref, wq2_ref, wkv_ref,
                     cos_ref, sin_ref, *rest):
    if len(rest) == 5:
        w_next_ref, qc_ref, kc_ref, vc_ref, w_next_packed_ref = rest
        _pack_w_in_kernel(w_next_ref, w_next_packed_ref)
    else:
        qc_ref, kc_ref, vc_ref = rest
    cq = cq_ref[...].astype(F32)
    ms = jnp.sum(cq * cq, axis=-1, keepdims=True) * (1.0 / Q_LORA)
    cqn = (cq * lax.rsqrt(ms + EPS) * gq_ref[...]).astype(BF16)
    ckv = ckv_ref[...].astype(F32)
    ms2 = jnp.mean(ckv * ckv, axis=-1, keepdims=True)
    ckvn = (ckv * lax.rsqrt(ms2 + EPS) * gkv_ref[...]).astype(BF16)

    q1 = jnp.dot(cqn, wq1_ref[...], preferred_element_type=F32) * Q_SCALE_MLA
    q2 = jnp.dot(cqn, wq2_ref[...], preferred_element_type=F32) * Q_SCALE_MLA
    kv = jnp.dot(ckvn, wkv_ref[...], preferred_element_type=F32)
    cos = cos_ref[...]
    sin = sin_ref[...]
    k_rope = (kr_ref[...].astype(F32) * cos + krs_ref[...].astype(F32) * sin).astype(BF16)
    for h in range(C_HEADS):
        qw = NOPE_DIM + LANE
        qc_ref[:, h * qw:h * qw + NOPE_DIM] = q1[:, h * qw:h * qw + NOPE_DIM].astype(BF16)
        q_rope = q1[:, h * qw + NOPE_DIM:(h + 1) * qw] * cos + q2[:, h * LANE:(h + 1) * LANE] * sin
        qc_ref[:, h * qw + NOPE_DIM:(h + 1) * qw] = q_rope.astype(BF16)
        kw = NOPE_DIM + V_DIM
        kc_ref[:, h * qw:h * qw + NOPE_DIM] = kv[:, h * kw:h * kw + NOPE_DIM].astype(BF16)
        kc_ref[:, h * qw + NOPE_DIM:(h + 1) * qw] = k_rope
        vc_ref[:, h * V_DIM:(h + 1) * V_DIM] = kv[:, h * kw + NOPE_DIM:(h + 1) * kw].astype(BF16)


def _mla_prep(proj, gq, gkv, wq1, wq2, wkv, cos_t, sin_t, w_in=None, pack_layer=None, *, tm=512):
    m = proj.shape[0]
    nseq = SEQ // tm
    steps = m // tm
    qw = C_HEADS * (NOPE_DIM + LANE)
    est = 2 * (tm * 1024 * 2) + 2 * (CQ_PAD * qw * 2 + CQ_PAD * 512 * 2 + 128 * 1024 * 2) \
        + 2 * (2 * tm * qw * 2 + tm * 512 * 2) + 3 * tm * qw * 4
    const = lambda i: (0, 0)
    extra_in, extra_out, extra_shape, extra_args = [], [], [], []
    if pack_layer is not None:
        w_t = jnp.swapaxes(w_in, 1, 2)
        _, n, d = w_t.shape
        cols = d // steps
        extra_in = [pl.BlockSpec((None, n, cols), lambda i: (pack_layer, 0, i))]
        extra_out = [pl.BlockSpec((N_PACK, cols), lambda i: (0, i))]
        extra_shape = [jax.ShapeDtypeStruct((N_PACK, d), BF16)]
        extra_args = [w_t]
        est += 2 * cols * (n * 4 + N_PACK * 2)
    return pl.pallas_call(
        _mla_prep_kernel,
        grid=(steps,),
        in_specs=[
            pl.BlockSpec((tm, CQ_PAD), lambda i: (i, OFF_CQ // CQ_PAD)),
            pl.BlockSpec((tm, LANE), lambda i: (i, OFF_CKV // LANE)),
            pl.BlockSpec((tm, LANE), lambda i: (i, OFF_KR // LANE)),
            pl.BlockSpec((tm, LANE), lambda i: (i, OFF_KRS // LANE)),
            pl.BlockSpec((1, CQ_PAD), const),
            pl.BlockSpec((1, KV_LORA), const),
            pl.BlockSpec((CQ_PAD, qw), const),
            pl.BlockSpec((CQ_PAD, C_HEADS * LANE), const),
            pl.BlockSpec((KV_LORA, C_HEADS * (NOPE_DIM + V_DIM)), const),
            pl.BlockSpec((tm, LANE), lambda i: (i % nseq, 0)),
            pl.BlockSpec((tm, LANE), lambda i: (i % nseq, 0)),
        ] + extra_in,
        out_specs=[
            pl.BlockSpec((tm, qw), lambda i: (i, 0)),
            pl.BlockSpec((tm, qw), lambda i: (i, 0)),
            pl.BlockSpec((tm, C_HEADS * V_DIM), lambda i: (i, 0)),
        ] + extra_out,
        out_shape=[
            jax.ShapeDtypeStruct((m, qw), BF16),
            jax.ShapeDtypeStruct((m, qw), BF16),
            jax.ShapeDtypeStruct((m, C_HEADS * V_DIM), BF16),
        ] + extra_shape,
        compiler_params=_params(("arbitrary",), est),
        name="mla_prep",
    )(proj, proj, proj, proj, gq, gkv, wq1, wq2, wkv, cos_t, sin_t, *extra_args)


def _paired_loop(n, body, init):
    def pair(p, carry):
        return body(2 * p + 1, body(2 * p, carry))
    carry = lax.fori_loop(0, n // 2, pair, init)
    return lax.fori_loop(2 * (n // 2), n, body, carry)


def _nt_dot(a, b):
    return lax.dot_general(a, b, (((1,), (1,)), ((), ())), preferred_element_type=F32)


def _two_pass_attention(i, *, n_heads, tq, tk, dv, logits_fn, mask_fn, v_fn, store_fn,
                        s_ref, mx_ref, acc_ref):
    nc = tk // LANE
    n_full = (i * tq) // tk
    nkb = ((i + 1) * tq + tk - 1) // tk

    mx_ref[...] = jnp.full(mx_ref.shape, NEG_INF, F32)

    def pass1(kb, masked):
        ks = pl.multiple_of(kb * tk, tk)
        mask = mask_fn(ks) if masked else None
        for h in range(n_heads):
            s = logits_fn(h, ks)
            if masked:
                s = jnp.where(mask, s, NEG_INF)
            s_ref[h, :, pl.ds(ks, tk)] = s
            mx = mx_ref[h]
            for c in range(nc):
                mx = jnp.maximum(mx, s[:, c * LANE:(c + 1) * LANE])
            mx_ref[h] = mx

    def pass1_full(kb, carry):
        pass1(kb, False)
        return carry

    def pass1_masked(kb, carry):
        pass1(kb, True)
        return carry

    _paired_loop(n_full, pass1_full, 0)
    lax.fori_loop(n_full, nkb, pass1_masked, 0)

    for h in range(n_heads):
        mx_ref[h] = jnp.broadcast_to(jnp.max(mx_ref[h], axis=-1, keepdims=True), (tq, LANE))
    acc_ref[...] = jnp.zeros(acc_ref.shape, F32)
    ones = jnp.ones((tk, LANE), BF16)

    def pass2(kb, carry):
        ks = pl.multiple_of(kb * tk, tk)
        for h in range(n_heads):
            m = mx_ref[h]
            p = jnp.concatenate(
                [jnp.exp2(s_ref[h, :, pl.ds(ks + c * LANE, LANE)] - m).astype(BF16) for c in range(nc)],
                axis=1)
            v1 = jnp.concatenate([v_fn(h, ks), ones], axis=1)
            acc_ref[h] += jnp.dot(p, v1, preferred_element_type=F32)
        return carry

    _paired_loop(nkb, pass2, 0)

    for h in range(n_heads):
        acc = acc_ref[h]
        store_fn(h, acc[:, :dv] / acc[:, dv:])


def _tile_iotas(tq, tk):
    return (lax.broadcasted_iota(jnp.int32, (tq, tk), 0), lax.broadcasted_iota(jnp.int32, (tq, tk), 1))


def _fox_kernel(q_ref, k_ref, v_ref, ccol_ref, crow_ref, o_ref,
                s_ref, mx_ref, acc_ref, cq_ref, *, tq, tk):
    i = pl.program_id(1)
    nc = tk // LANE
    head = lambda h: slice(h * HEAD_DIM, (h + 1) * HEAD_DIM)
    for h in range(B_HEADS):
        cq_ref[h] = jnp.broadcast_to(ccol_ref[:, MISC_FL + h:MISC_FL + h + 1] * LOG2E, (tq, LANE))

    def logits_fn(h, ks):
        s = _nt_dot(q_ref[:, head(h)], k_ref[pl.ds(ks, tk), head(h)])
        return s + jnp.concatenate([cq_ref[h]] * nc, axis=1) - crow_ref[h:h + 1, pl.ds(ks, tk)] * LOG2E

    def mask_fn(ks):
        rows, cols = _tile_iotas(tq, tk)
        return cols + (ks - i * tq) <= rows

    def store_fn(h, y):
        o_ref[:, head(h)] = y.astype(o_ref.dtype)

    _two_pass_attention(i, n_heads=B_HEADS, tq=tq, tk=tk, dv=HEAD_DIM, logits_fn=logits_fn,
                        mask_fn=mask_fn, v_fn=lambda h, ks: v_ref[pl.ds(ks, tk), head(h)],
                        store_fn=store_fn, s_ref=s_ref, mx_ref=mx_ref, acc_ref=acc_ref)


def _fox_attention(proj, ccol, crow, batch, *, tq=256, tk=512):
    m = proj.shape[0]
    nq = SEQ // tq
    w = B_HEADS * HEAD_DIM
    state = B_HEADS * tq * LANE * 4
    est = (4 * tq * w * 2 + 4 * SEQ * w * 2 + 2 * tq * LANE * 4 + 2 * 8 * SEQ * 4
           + B_HEADS * tq * SEQ * 4 + 4 * state + 6 * tq * tk * 4)
    return pl.pallas_call(
        functools.partial(_fox_kernel, tq=tq, tk=tk),
        grid=(batch, nq),
        in_specs=[
            pl.BlockSpec((tq, w), lambda b, i: (b * nq + i, OFF_QB // w)),
            pl.BlockSpec((SEQ, w), lambda b, i: (b, OFF_KB // w)),
            pl.BlockSpec((SEQ, w), lambda b, i: (b, OFF_VB // w)),
            pl.BlockSpec((tq, LANE), lambda b, i: (b * nq + i, 0)),
            pl.BlockSpec((None, B_HEADS, SEQ), lambda b, i: (b, 0, 0)),
        ],
        out_specs=pl.BlockSpec((tq, w), lambda b, i: (b * nq + i, 0)),
        out_shape=jax.ShapeDtypeStruct((m, w), BF16),
        scratch_shapes=[
            pltpu.VMEM((B_HEADS, tq, SEQ), F32),
            pltpu.VMEM((B_HEADS, tq, LANE), F32),
            pltpu.VMEM((B_HEADS, tq, HEAD_DIM + LANE), F32),
            pltpu.VMEM((B_HEADS, tq, LANE), F32),
        ],
        compiler_params=_params(("arbitrary", "arbitrary"), est),
        name="fox_attention",
    )(proj, proj, proj, ccol, crow)


def _mla_kernel(q_ref, k_ref, v_ref, o_ref, s_ref, mx_ref, acc_ref, *, tq, tk):
    i = pl.program_id(1)
    qw = NOPE_DIM + LANE
    qhead = lambda h: slice(h * qw, (h + 1) * qw)
    vhead = lambda h: slice(h * V_DIM, (h + 1) * V_DIM)

    def logits_fn(h, ks):
        return _nt_dot(q_ref[:, qhead(h)], k_ref[pl.ds(ks, tk), qhead(h)])

    def mask_fn(ks):
        rows, cols = _tile_iotas(tq, tk)
        return (cols + ks) // CHUNK <= (rows + i * tq) // CHUNK

    def store_fn(h, y):
        o_ref[:, vhead(h)] = y.astype(o_ref.dtype)

    _two_pass_attention(i, n_heads=C_HEADS, tq=tq, tk=tk, dv=V_DIM, logits_fn=logits_fn,
                        mask_fn=mask_fn, v_fn=lambda h, ks: v_ref[pl.ds(ks, tk), vhead(h)],
                        store_fn=store_fn, s_ref=s_ref, mx_ref=mx_ref, acc_ref=acc_ref)


def _mla_attention(qc, kc, vc, batch, *, tq=256, tk=512):
    m = qc.shape[0]
    nq = SEQ // tq
    qw = C_HEADS * (NOPE_DIM + LANE)
    vw = C_HEADS * V_DIM
    state = C_HEADS * tq * LANE * 4
    est = (2 * tq * qw * 2 + 2 * SEQ * qw * 2 + 2 * SEQ * vw * 2 + 2 * tq * vw * 2
           + C_HEADS * tq * SEQ * 4 + 3 * state + 6 * tq * tk * 4)
    return pl.pallas_call(
        functools.partial(_mla_kernel, tq=tq, tk=tk),
        grid=(batch, nq),
        in_specs=[
            pl.BlockSpec((tq, qw), lambda b, i: (b * nq + i, 0)),
            pl.BlockSpec((SEQ, qw), lambda b, i: (b, 0)),
            pl.BlockSpec((SEQ, vw), lambda b, i: (b, 0)),
        ],
        out_specs=pl.BlockSpec((tq, vw), lambda b, i: (b * nq + i, 0)),
        out_shape=jax.ShapeDtypeStruct((m, vw), BF16),
        scratch_shapes=[
            pltpu.VMEM((C_HEADS, tq, SEQ), F32),
            pltpu.VMEM((C_HEADS, tq, LANE), F32),
            pltpu.VMEM((C_HEADS, tq, V_DIM + LANE), F32),
        ],
        compiler_params=_params(("arbitrary", "arbitrary"), est),
        name="mla_attention",
    )(qc, kc, vc)


def _t5_tiles_kernel(bucket_ref, table_ref, o_ref):
    for d in range(3):
        bucket = bucket_ref[d]
        for h in range(A_HEADS):
            acc = jnp.zeros(bucket.shape, F32)
            for nb in range(NUM_BUCKETS):
                acc = jnp.where(bucket == nb, table_ref[nb, h] * LOG2E, acc)
            o_ref[d, h] = acc


def _t5_tiles(bucket_tiles, t5_bias, *, t=ATT_T):
    return pl.pallas_call(
        _t5_tiles_kernel,
        in_specs=[
            pl.BlockSpec(memory_space=pltpu.VMEM),
            pl.BlockSpec(memory_space=pltpu.SMEM),
        ],
        out_specs=pl.BlockSpec(memory_space=pltpu.VMEM),
        out_shape=jax.ShapeDtypeStruct((3, A_HEADS, t, t), F32),
        name="t5_tiles",
    )(bucket_tiles, t5_bias)


def _dsa_kernel(qi_ref, klo_ref, khi_ref, misc_ref, qa_ref, ka_ref, va_ref, bias_ref, o_ref,
                sc_ref, lg_ref, acc_ref, *, t):
    i = pl.program_id(1)
    groups = t // 8
    key = lax.broadcasted_iota(jnp.int32, (t, t), 0)
    qry = lax.broadcasted_iota(jnp.int32, (t, t), 1)
    admissible = (key // CHUNK) <= (qry // CHUNK)

    def fold(x, op):
        return op(x.reshape(groups, 8, t), axis=0)

    def over_keys(x8, op):
        return jnp.broadcast_to(op(x8, axis=0, keepdims=True), (8, t))

    w_t = misc_ref[...].T[MISC_WI:MISC_WI + IDX_HEADS, :] * (IDX_HEADS ** -0.5 * IDX_DIM ** -0.5)

    def block_scores(kb):
        ks = pl.multiple_of(kb * t, t)
        klo = klo_ref[pl.ds(ks, t), :]
        khi = khi_ref[pl.ds(ks, t), :]
        acc = jnp.zeros((t, t), F32)
        for pair in range(IDX_HEADS // 2):
            qp = qi_ref[:, pair * LANE:(pair + 1) * LANE]
            for sub, kk in enumerate((klo, khi)):
                h = 2 * pair + sub
                acc = acc + jnp.maximum(_nt_dot(kk, qp), 0.0) * w_t[h:h + 1, :]
        return ks, acc

    def score_body(kb, carry):
        mn, mx = carry
        ks, acc = block_scores(kb)
        sc_ref[pl.ds(ks, t), :] = acc
        return jnp.minimum(mn, fold(acc, jnp.min)), jnp.maximum(mx, fold(acc, jnp.max))

    mn, mx = _paired_loop(i, score_body,
                          (jnp.full((8, t), -SELECT_MIN, F32), jnp.full((8, t), SELECT_MIN, F32)))
    ks, acc = block_scores(i)
    sc_ref[pl.ds(ks, t), :] = jnp.where(admissible, acc, NEG_INF)
    mn = jnp.minimum(mn, fold(jnp.where(admissible, acc, -SELECT_MIN), jnp.min))
    mx = jnp.maximum(mx, fold(jnp.where(admissible, acc, SELECT_MIN), jnp.max))

    def count_ge(thr):
        def body(kb, cnt):
            ks = pl.multiple_of(kb * t, t)
            hit = jnp.where(sc_ref[pl.ds(ks, t), :].reshape(groups, 8, t) >= thr[None], 1.0, 0.0)
            return cnt + jnp.sum(hit, axis=0)
        return over_keys(_paired_loop(i + 1, body, jnp.zeros((8, t), F32)), jnp.sum)

    k_sel = float(TOPK_MAX)
    search = i * t >= TOPK_MAX
    lo0 = jnp.where(search, over_keys(mn, jnp.min), SELECT_MIN)
    hi0 = jnp.where(search, over_keys(mx, jnp.max), SELECT_MIN)

    def midpoint(lo, hi):
        return 0.5 * lo + 0.5 * hi

    def any_active(lo, hi):
        mid = midpoint(lo, hi)
        return jnp.max(jnp.where((mid > lo) & (mid < hi), 1.0, 0.0)) > 0.5

    def search_cond(state):
        return jnp.logical_and(state[0] < BISECT_MAX_ITERS, state[1])

    def search_body(state):
        it, _, lo, hi, c_lo = state
        for _ in range(SEARCH_STEPS_PER_CHECK):
            mid = midpoint(lo, hi)
            cnt = count_ge(mid)
            ge = cnt >= k_sel
            lo = jnp.where(ge, mid, lo)
            c_lo = jnp.where(ge, cnt, c_lo)
            hi = jnp.where(cnt > k_sel, hi, mid)
        return it + SEARCH_STEPS_PER_CHECK, any_active(lo, hi), lo, hi, c_lo

    n_adm = (((i * t + qry[:8]) // CHUNK + 1) * CHUNK).astype(F32)
    _, _, lo, hi, c_lo = lax.while_loop(search_cond, search_body,
                                        (jnp.int32(0), any_active(lo0, hi0), lo0, hi0, n_adm))
    c_hi = count_ge(hi)
    thr = jnp.where(c_hi >= k_sel, hi, lo)
    c_thr = jnp.where(c_hi >= k_sel, c_hi, c_lo)

    @pl.when(jnp.max(jnp.where(c_thr > k_sel, 1.0, 0.0)) > 0.5)
    def _():
        key_in_block = (lax.broadcasted_iota(jnp.int32, (groups, 8, t), 0) * 8
                        + lax.broadcasted_iota(jnp.int32, (groups, 8, t), 1))

        def count_where(pred):
            def body(kb, cnt):
                ks = pl.multiple_of(kb * t, t)
                blk = sc_ref[pl.ds(ks, t), :].reshape(groups, 8, t)
                return cnt + jnp.sum(jnp.where(pred(blk, ks), 1.0, 0.0), axis=0)
            return over_keys(lax.fori_loop(0, i + 1, body, jnp.zeros((8, t), F32)), jnp.sum)

        def tied_before(bound):
            return lambda blk, ks: (blk == thr[None]) & ((key_in_block + ks).astype(F32) < bound[None])

        keep = k_sel - count_where(lambda blk, ks: blk > thr[None])

        def cut_body(_, bounds):
            below, above = bounds
            mid = jnp.floor(0.5 * (below + above))
            enough = count_where(tied_before(mid)) >= keep
            return jnp.where(enough, below, mid), jnp.where(enough, mid, above)

        _, cut = lax.fori_loop(0, SEQ.bit_length(), cut_body,
                               (jnp.zeros((8, t), F32), jnp.full((8, t), float(SEQ), F32)))

        def drop_body(kb, carry):
            ks = pl.multiple_of(kb * t, t)
            blk = sc_ref[pl.ds(ks, t), :].reshape(groups, 8, t)
            drop = (blk == thr[None]) & ((key_in_block + ks).astype(F32) >= cut[None])
            sc_ref[pl.ds(ks, t), :] = jnp.where(drop, NEG_INF, blk).reshape(t, t)
            return carry

        lax.fori_loop(0, i + 1, drop_body, 0)

    head = lambda h: slice(h * HEAD_DIM, (h + 1) * HEAD_DIM)
    acc_ref[...] = jnp.zeros(acc_ref.shape, F32)

    def logits_pass(kb, mx):
        ks = pl.multiple_of(kb * t, t)
        dist = jnp.minimum(i - kb, 2)
        k_blk = ka_ref[pl.ds(ks, t), :]
        sel = sc_ref[pl.ds(ks, t), :] >= thr[0:1, :]
        out = []
        for h in range(A_HEADS):
            s = _nt_dot(k_blk, qa_ref[:, head(h)]) + bias_ref[dist, h]
            s = jnp.where(sel, s, NEG_INF)
            lg_ref[h, pl.ds(ks, t), :] = s
            out.append(jnp.maximum(mx[h], fold(s, jnp.max)))
        return tuple(out)

    mx = _paired_loop(i + 1, logits_pass,
                      tuple(jnp.full((8, t), NEG_INF, F32) for _ in range(A_HEADS)))
    m = [over_keys(mx[h], jnp.max)[0:1, :] for h in range(A_HEADS)]

    ones_rows = jnp.ones((DSA_ONES_ROWS, t), BF16)

    def value_pass(kb, carry):
        ks = pl.multiple_of(kb * t, t)
        v_t = va_ref[pl.ds(ks, t), :].astype(F32).T.astype(BF16)
        v1_t = jnp.concatenate([v_t, ones_rows], axis=0)
        for h in range(A_HEADS):
            p = jnp.exp2(lg_ref[h, pl.ds(ks, t), :] - m[h])
            acc_ref[h] += jnp.dot(v1_t, p.astype(BF16), preferred_element_type=F32)
        return carry

    _paired_loop(i + 1, value_pass, 0)
    for h in range(A_HEADS):
        acc = acc_ref[h]
        o_ref[:, head(h)] = (acc[:HEAD_DIM] / acc[HEAD_DIM:HEAD_DIM + 1]).T.astype(o_ref.dtype)


def _dsa_attention(proj, misc, bias_tiles, batch, *, t=ATT_T):
    m = proj.shape[0]
    nq = SEQ // t
    qiw = IDX_HEADS * IDX_DIM
    qaw = A_HEADS * HEAD_DIM
    est = (2 * t * qiw * 2 + 2 * 4 * SEQ * LANE * 2 + 2 * t * LANE * 4 + 2 * t * qaw * 2
           + 2 * 3 * A_HEADS * t * t * 4 + 2 * t * qaw * 2
           + (1 + A_HEADS) * t * SEQ * 4 + A_HEADS * HEAD_DIM * t * 4 + 12 * t * t * 4)
    kblock = lambda off: pl.BlockSpec((SEQ, LANE), lambda b, i: (b, off // LANE))
    return pl.pallas_call(
        functools.partial(_dsa_kernel, t=t),
        grid=(batch, nq),
        in_specs=[
            pl.BlockSpec((t, qiw), lambda b, i: (b * nq + i, OFF_QI // qiw)),
            kblock(OFF_KI_LO),
            kblock(OFF_KI_HI),
            pl.BlockSpec((t, LANE), lambda b, i: (b * nq + i, 0)),
            pl.BlockSpec((t, qaw), lambda b, i: (b * nq + i, OFF_QA // qaw)),
            kblock(OFF_KA),
            kblock(OFF_VA),
            pl.BlockSpec((3, A_HEADS, t, t), lambda b, i: (0, 0, 0, 0)),
        ],
        out_specs=pl.BlockSpec((t, qaw), lambda b, i: (b * nq + i, 0)),
        out_shape=jax.ShapeDtypeStruct((m, qaw), BF16),
        scratch_shapes=[
            pltpu.VMEM((SEQ, t), F32),
            pltpu.VMEM((A_HEADS, SEQ, t), F32),
            pltpu.VMEM((A_HEADS, HEAD_DIM + DSA_ONES_ROWS, t), F32),
        ],
        compiler_params=_params(("arbitrary", "arbitrary"), est),
        name="dsa_attention",
    )(proj, proj, proj, misc, proj, proj, proj, bias_tiles)


def _merge_kernel(x_ref, ya_ref, yb_ref, yc_ref, ga_ref, gb_ref, gc_ref,
                  wa_ref, wb_ref, wc_ref, wo_ref, gn_ref, o_ref, hn_ref):
    def branch(y_ref, w_ref, g_ref):
        y = jnp.dot(y_ref[...], w_ref[...], preferred_element_type=F32)
        return jax.nn.sigmoid(g_ref[...].astype(F32)) * y

    merged = branch(ya_ref, wa_ref, ga_ref) + branch(yb_ref, wb_ref, gb_ref) + branch(yc_ref, wc_ref, gc_ref)
    x_new = x_ref[...] + jnp.dot(merged.astype(BF16), wo_ref[...], preferred_element_type=F32)
    o_ref[...] = x_new
    ms = jnp.mean(x_new * x_new, axis=-1, keepdims=True)
    hn_ref[...] = (x_new * lax.rsqrt(ms + EPS) * gn_ref[...]).astype(hn_ref.dtype)


def _merge(x2, ya, yb, yc, proj, wa, wb, wc, wo, g_next, *, tm=256):
    m = x2.shape[0]
    wbytes = (wa.size + wb.size + wc.size + wo.size) * 2
    est = 2 * wbytes + 4 * tm * D_MODEL * 4 + 2 * tm * 2048 * 2 + 8 * tm * D_MODEL * 2 + 4 * tm * D_MODEL * 4
    const = lambda i: (0, 0)
    gate = lambda k: pl.BlockSpec((tm, D_MODEL), lambda i: (i, OFF_GL // D_MODEL + k))
    return pl.pallas_call(
        _merge_kernel,
        grid=(m // tm,),
        in_specs=[
            pl.BlockSpec((tm, D_MODEL), lambda i: (i, 0)),
            pl.BlockSpec((tm, ya.shape[1]), lambda i: (i, 0)),
            pl.BlockSpec((tm, yb.shape[1]), lambda i: (i, 0)),
            pl.BlockSpec((tm, yc.shape[1]), lambda i: (i, 0)),
            gate(0), gate(1), gate(2),
            pl.BlockSpec(wa.shape, const),
            pl.BlockSpec(wb.shape, const),
            pl.BlockSpec(wc.shape, const),
            pl.BlockSpec(wo.shape, const),
            pl.BlockSpec((1, D_MODEL), const),
        ],
        out_specs=[
            pl.BlockSpec((tm, D_MODEL), lambda i: (i, 0)),
            pl.BlockSpec((tm, D_MODEL), lambda i: (i, 0)),
        ],
        out_shape=[
            jax.ShapeDtypeStruct((m, D_MODEL), F32),
            jax.ShapeDtypeStruct((m, D_MODEL), BF16),
        ],
        compiler_params=_params(("arbitrary",), est),
        name="merge",
    )(x2, ya, yb, yc, proj, proj, proj, wa, wb, wc, wo, g_next)


def _ffn_kernel(x_ref, h_ref, wu_ref, wv_ref, cw_ref, wd_ref, fg_ref, o_ref,
                ubuf_ref, halo_ref, *, tm, tf, final_norm):
    i = pl.program_id(0)
    j = pl.program_id(1)
    nj = pl.num_programs(1)

    @pl.when(j == 0)
    def _():
        o_ref[...] = x_ref[...]

        @pl.when((i % (SEQ // tm)) == 0)
        def _():
            halo_ref[...] = jnp.zeros(halo_ref.shape, F32)

    h = h_ref[...]
    u = jnp.dot(h, wu_ref[...], preferred_element_type=F32)
    v = jnp.dot(h, wv_ref[...], preferred_element_type=F32)

    ubuf_ref[0:8, :] = halo_ref[j]
    ubuf_ref[8:, :] = u
    halo_ref[j] = u[tm - 8:, :]
    conv = (cw_ref[0:1, :] * ubuf_ref[6:6 + tm, :] + cw_ref[1:2, :] * ubuf_ref[7:7 + tm, :]
            + cw_ref[2:3, :] * u + cw_ref[3:4, :])
    act = (jax.nn.gelu(conv) * v).astype(BF16)
    o_ref[...] += jnp.dot(act, wd_ref[...], preferred_element_type=F32)

    if final_norm:
        @pl.when(j == nj - 1)
        def _():
            _rmsnorm_rows(o_ref, fg_ref, o_ref, tm)


def _ffn(x2, h2, w_up, cw, w_down, final_g, *, final_norm, tm=1024, tf=512):
    m = x2.shape[0]
    nj = D_FF // tf
    est = (3 * tm * D_MODEL * 4 + 2 * tm * D_MODEL * 2 + 2 * 2 * D_MODEL * tf * 2 + 2 * tf * D_MODEL * 2
           + (tm + 8) * tf * 4 + nj * 8 * tf * 4 + 4 * tm * tf * 4)
    return pl.pallas_call(
        functools.partial(_ffn_kernel, tm=tm, tf=tf, final_norm=final_norm),
        grid=(m // tm, nj),
        in_specs=[
            pl.BlockSpec((tm, D_MODEL), lambda i, j: (i, 0), pipeline_mode=pl.Buffered(1)),
            pl.BlockSpec((tm, D_MODEL), lambda i, j: (i, 0)),
            pl.BlockSpec((D_MODEL, tf), lambda i, j: (0, j)),
            pl.BlockSpec((D_MODEL, tf), lambda i, j: (0, nj + j)),
            pl.BlockSpec((8, tf), lambda i, j: (0, j)),
            pl.BlockSpec((tf, D_MODEL), lambda i, j: (j, 0)),
            pl.BlockSpec((1, D_MODEL), lambda i, j: (0, 0)),
        ],
        out_specs=pl.BlockSpec((tm, D_MODEL), lambda i, j: (i, 0)),
        out_shape=jax.ShapeDtypeStruct((m, D_MODEL), F32),
        scratch_shapes=[
            pltpu.VMEM((tm + 8, tf), F32),
            pltpu.VMEM((nj, 8, tf), F32),
        ],
        compiler_params=_params(("arbitrary", "arbitrary"), est),
        name="conv_ffn",
    )(x2, h2, w_up, w_up, cw, w_down, final_g)


def _t5_bucket(rel):
    nb = NUM_BUCKETS // 2
    max_exact = nb // 2
    base = jnp.where(rel > 0, nb, 0)
    n = jnp.abs(rel)
    nf = jnp.maximum(n, 1).astype(F32)
    large = max_exact + (jnp.log(nf / max_exact) / math.log(MAX_DISTANCE / max_exact)
                         * (nb - max_exact)).astype(jnp.int32)
    large = jnp.minimum(large, nb - 1)
    return base + jnp.where(n < max_exact, n, large)


def _bucket_tiles(t):
    key = jnp.arange(t, dtype=jnp.int32)[:, None]
    qry = jnp.arange(t, dtype=jnp.int32)[None, :]
    return jnp.stack([_t5_bucket(key - qry - d * t) for d in range(3)])


def _rope_tables():
    half = ROPE_DIM // 2
    inv = ROPE_THETA ** (-jnp.arange(half, dtype=F32) / half)
    ang = jnp.arange(SEQ, dtype=jnp.int32).astype(F32)[:, None] * inv[None, :]
    cos, sin = jnp.cos(ang), jnp.sin(ang)
    pad = jnp.zeros((SEQ, LANE - ROPE_DIM), F32)
    return jnp.concatenate([cos, cos, pad], axis=1), jnp.concatenate([-sin, sin, pad], axis=1)


def _swap_halves(w):
    half = w.shape[-1] // 2
    return jnp.concatenate([w[..., half:], w[..., :half]], axis=-1)


def _pack_w_in_kernel(w_ref, o_ref):
    off = np.cumsum((0,) + IN_SIZES)
    seg = lambda k: w_ref[off[k]:off[k + 1], :]
    qa, ka, va, qi, ki, wi, qb, kb, vb, fl, cq, ckv, kr, gl = [seg(k) for k in range(len(IN_SIZES))]
    qa, qb = qa * Q_SCALE, qb * Q_SCALE
    z = lambda n: jnp.zeros((n, w_ref.shape[1]), F32)
    half = ROPE_DIM // 2
    groups = [
        (OFF_QI, [qi]), (OFF_QB, [qb]), (OFF_KB, [kb]), (OFF_VB, [vb]), (OFF_GL, [gl]), (OFF_QA, [qa]),
        (OFF_CQ, [cq, z(CQ_PAD - Q_LORA)]), (OFF_KA, [ka]), (OFF_VA, [va]), (OFF_KI_LO, [ki, z(64)]),
        (OFF_CKV, [ckv]), (OFF_KR, [kr, z(64)]), (OFF_KRS, [kr[half:], kr[:half], z(64)]),
        (OFF_MISC, [wi, fl, z(LANE - IDX_HEADS - B_HEADS)]), (OFF_KI_HI, [z(64), ki]),
    ]
    for start, pieces in groups:
        block = pieces[0] if len(pieces) == 1 else jnp.concatenate(pieces, axis=0)
        o_ref[start:start + block.shape[0], :] = block.astype(o_ref.dtype)


def _pack_w_in(w_in, l, *, cols=256):
    w_t = jnp.swapaxes(w_in, 1, 2)
    _, n, d = w_t.shape
    return pl.pallas_call(
        _pack_w_in_kernel,
        grid=(d // cols,),
        in_specs=[pl.BlockSpec((None, n, cols), lambda i: (l, 0, i))],
        out_specs=pl.BlockSpec((N_PACK, cols), lambda i: (0, i)),
        out_shape=jax.ShapeDtypeStruct((N_PACK, d), BF16),
        compiler_params=_params(("arbitrary",), 2 * cols * (n * 4 + N_PACK * 2)),
        name="pack_w_in",
    )(w_t)


def _pack_w_uq(w):
    w = jnp.pad(w, ((0, CQ_PAD - Q_LORA), (0, 0))).reshape(CQ_PAD, C_HEADS, NOPE_DIM + ROPE_DIM)
    z = jnp.zeros((CQ_PAD, C_HEADS, LANE - ROPE_DIM), w.dtype)
    rope = w[..., NOPE_DIM:]
    wq1 = jnp.concatenate([w, z], axis=-1).reshape(CQ_PAD, -1)
    wq2 = jnp.concatenate([_swap_halves(rope), z], axis=-1).reshape(CQ_PAD, -1)
    return wq1.astype(BF16), wq2.astype(BF16)


def kernel(x, norm_mix_g, w_in, b_forget, g_cq, g_ckv, w_uq, w_ukv, w_branch_a, w_branch_b, w_branch_c,
           w_o, norm_ffn_g, w_up, conv_w, conv_b, w_down, t5_bias, final_g):
    batch, seq, d = x.shape
    assert (seq, d) == (SEQ, D_MODEL)
    x2 = x.reshape(batch * seq, d)

    bias_tiles = _t5_tiles(_bucket_tiles(ATT_T), t5_bias)
    cos_t, sin_t = _rope_tables()
    final_row = final_g.reshape(1, D_MODEL)

    w_in_packed = _pack_w_in(w_in, 0)
    for l in range(DEPTH):
        casts = [(w_up, True, 11), (w_down, False, 11), (w_o, True, 8),
                 (w_branch_a, True, 8), (w_branch_b, True, 8), (w_branch_c, True, 8)]
        proj, misc, w_up_l, w_down_l, w_o_l, w_a_l, w_b_l, w_c_l = _inproj(
            x2, norm_mix_g[l].reshape(1, D_MODEL), w_in_packed, casts, l)

        fbias = jnp.zeros((1, LANE), F32).at[0, MISC_FL:MISC_FL + B_HEADS].set(b_forget[l])
        ccol, crow = _forget_cumsum(misc, fbias, batch)

        wq1, wq2 = _pack_w_uq(w_uq[l])
        gq = jnp.pad(g_cq[l], (0, CQ_PAD - Q_LORA)).reshape(1, CQ_PAD)
        prep = _mla_prep(proj, gq, g_ckv[l].reshape(1, KV_LORA), wq1, wq2, w_ukv[l].astype(BF16),
                         cos_t, sin_t, w_in, l + 1 if l + 1 < DEPTH else None)
        qc, kc, vc = prep[:3]
        if l + 1 < DEPTH:
            w_in_packed = prep[3]

        ya = _dsa_attention(proj, misc, bias_tiles, batch)
        yb = _fox_attention(proj, ccol, crow, batch)
        yc = _mla_attention(qc, kc, vc, batch)

        x2, h2 = _merge(x2, ya, yb, yc, proj, w_a_l, w_b_l, w_c_l, w_o_l,
                        norm_ffn_g[l].reshape(1, D_MODEL))

        cw = jnp.concatenate([conv_w[l], conv_b[l][None, :], jnp.zeros((4, D_FF), F32)], axis=0)
        x2 = _ffn(x2, h2, w_up_l, cw, w_down_l, final_row, final_norm=(l == DEPTH - 1))

    return x2.reshape(batch, seq, d)
```

```python
import functools
import math

import jax
import jax.numpy as jnp
import numpy as np
from jax import lax
from jax.experimental import pallas as pl
from jax.experimental.pallas import tpu as pltpu

F32 = jnp.float32
BF16 = jnp.bfloat16

D_MODEL = 2048
SEQ = 2048
DEPTH = 2
CHUNK = 64
HEAD_DIM = 128
EPS = 1e-6
NEG_INF = -1e30

A_HEADS = 4
IDX_HEADS = 16
IDX_DIM = 64
TOPK_MAX = 256
NUM_BUCKETS = 32
MAX_DISTANCE = 128
B_HEADS = 8
C_HEADS = 4
Q_LORA = 448
KV_LORA = 128
NOPE_DIM = 128
ROPE_DIM = 64
V_DIM = 128
ROPE_THETA = 10000.0
D_FF = 5632

IN_SIZES = (
    A_HEADS * HEAD_DIM, HEAD_DIM, HEAD_DIM,
    IDX_HEADS * IDX_DIM, IDX_DIM, IDX_HEADS,
    B_HEADS * HEAD_DIM, B_HEADS * HEAD_DIM, B_HEADS * HEAD_DIM, B_HEADS,
    Q_LORA, KV_LORA, ROPE_DIM,
    3 * D_MODEL,
)

LOG2E = math.log2(math.e)
Q_SCALE = HEAD_DIM ** -0.5 * LOG2E
Q_SCALE_MLA = (NOPE_DIM + ROPE_DIM) ** -0.5 * LOG2E
LANE = 128
V7X_VMEM_BYTES = 64 * 1024 * 1024

OFF_QI = 0
OFF_QB = 1024
OFF_KB = 2048
OFF_VB = 3072
OFF_GL = 4096
OFF_QA = 10240
OFF_CQ = 10752
OFF_KA = 11264
OFF_VA = 11392
OFF_KI_LO = 11520
OFF_CKV = 11648
OFF_KR = 11776
OFF_KRS = 11904
OFF_MISC = 12032
OFF_KI_HI = 12160
N_PACK = 12288
CQ_PAD = 512
MISC_WI = 0
MISC_FL = IDX_HEADS

ATT_T = 256
SELECT_MIN = -1e29
BISECT_MAX_ITERS = 512
SEARCH_STEPS_PER_CHECK = 4
DSA_ONES_ROWS = 16


def _vmem_limit(estimate_bytes):
    limit = V7X_VMEM_BYTES - (6 << 20)
    assert estimate_bytes <= limit, estimate_bytes
    return limit


def _params(semantics, vmem_estimate):
    return pltpu.CompilerParams(dimension_semantics=semantics,
                                vmem_limit_bytes=_vmem_limit(vmem_estimate))


def _rmsnorm_rows(x_ref, g_ref, out_ref, rows):
    def body(c, carry):
        r = pl.multiple_of(c * 128, 128)
        x = x_ref[pl.ds(r, 128), :]
        ms = jnp.mean(x * x, axis=-1, keepdims=True)
        out_ref[pl.ds(r, 128), :] = (x * lax.rsqrt(ms + EPS) * g_ref[...]).astype(out_ref.dtype)
        return carry
    lax.fori_loop(0, rows // 128, body, 0)


def _inproj_kernel(*refs, tm, tn, n_cast):
    x_ref, g_ref, w_ref = refs[:3]
    cast_in = refs[3:3 + n_cast]
    o_ref, misc_ref = refs[3 + n_cast:5 + n_cast]
    cast_out = refs[5 + n_cast:5 + 2 * n_cast]
    h_ref = refs[5 + 2 * n_cast]
    j = pl.program_id(1)

    @pl.when(j == 0)
    def _():
        _rmsnorm_rows(x_ref, g_ref, h_ref, tm)

    acc = _nt_dot(h_ref[...], w_ref[...])
    o_ref[...] = acc.astype(o_ref.dtype)

    for src, dst in zip(cast_in, cast_out):
        dst[...] = src[...].astype(dst.dtype)

    @pl.when(j == OFF_MISC // tn)
    def _():
        lo = OFF_MISC % tn
        misc_ref[...] = acc[:, lo:lo + LANE]


def _cast_specs(w_stack, l, ni, nj, rows_follow_i, n_chunks):
    _, r, c = w_stack.shape
    assert n_chunks <= nj
    chunk = lambda j: jnp.minimum(j, n_chunks - 1)
    if rows_follow_i:
        block = (r // ni, c // n_chunks)
        idx = lambda i, j: (i, chunk(j))
    else:
        block = (r // n_chunks, c // ni)
        idx = lambda i, j: (chunk(j), i)
    assert block[0] % 16 == 0 and block[1] % LANE == 0, block
    in_spec = pl.BlockSpec((None,) + block, lambda i, j: (l,) + idx(i, j))
    out_spec = pl.BlockSpec(block, idx)
    return in_spec, out_spec, jax.ShapeDtypeStruct((r, c), BF16), block[0] * block[1]


def _inproj(x2, g, w_pack, casts, l, *, tm=1024, tn=1024):
    m = x2.shape[0]
    ni, nj = m // tm, N_PACK // tn
    plans = [_cast_specs(w, l, ni, nj, rows_i, n) for w, rows_i, n in casts]
    est = (2 * tm * D_MODEL * 4 + tm * D_MODEL * 2 + 2 * D_MODEL * tn * 2 + 2 * tm * tn * 2 + tm * tn * 4
           + sum(2 * 6 * p[3] for p in plans))
    return pl.pallas_call(
        functools.partial(_inproj_kernel, tm=tm, tn=tn, n_cast=len(casts)),
        grid=(ni, nj),
        in_specs=[
            pl.BlockSpec((tm, D_MODEL), lambda i, j: (i, 0)),
            pl.BlockSpec((1, D_MODEL), lambda i, j: (0, 0)),
            pl.BlockSpec((tn, D_MODEL), lambda i, j: (j, 0)),
        ] + [p[0] for p in plans],
        out_specs=[
            pl.BlockSpec((tm, tn), lambda i, j: (i, j)),
            pl.BlockSpec((tm, LANE), lambda i, j: (i, 0)),
        ] + [p[1] for p in plans],
        out_shape=[
            jax.ShapeDtypeStruct((m, N_PACK), BF16),
            jax.ShapeDtypeStruct((m, LANE), F32),
        ] + [p[2] for p in plans],
        scratch_shapes=[pltpu.VMEM((tm, D_MODEL), BF16)],
        compiler_params=_params(("arbitrary", "arbitrary"), est),
        name="inproj",
    )(x2, g, w_pack, *[w for w, _, _ in casts])


def _forget_cumsum_kernel(misc_ref, bias_ref, ccol_ref, crow_ref, *, blk):
    rows = lax.broadcasted_iota(jnp.int32, (blk, blk), 0)
    cols = lax.broadcasted_iota(jnp.int32, (blk, blk), 1)
    tri = jnp.where(rows >= cols, 1.0, 0.0).astype(BF16)
    carry = jnp.zeros((1, LANE), F32)
    for c in range(SEQ // blk):
        z = misc_ref[c * blk:(c + 1) * blk, :] + bias_ref[...]
        lf = jnp.minimum(z, 0.0) - jnp.log1p(jnp.exp(-jnp.abs(z)))
        p0 = lf.astype(BF16)
        r1 = lf - p0.astype(F32)
        p1 = r1.astype(BF16)
        p2 = (r1 - p1.astype(F32)).astype(BF16)
        cs = (jnp.dot(tri, p0, preferred_element_type=F32)
              + jnp.dot(tri, p1, preferred_element_type=F32)
              + jnp.dot(tri, p2, preferred_element_type=F32)) + carry
        ccol_ref[c * blk:(c + 1) * blk, :] = cs
        crow_ref[:, c * blk:(c + 1) * blk] = cs.T[MISC_FL:MISC_FL + B_HEADS, :]
        carry = cs[blk - 1:blk, :]


def _forget_cumsum(misc, bias_row, batch, *, blk=256):
    est = 4 * SEQ * LANE * 4 + 2 * 8 * SEQ * 4
    return pl.pallas_call(
        functools.partial(_forget_cumsum_kernel, blk=blk),
        grid=(batch,),
        in_specs=[
            pl.BlockSpec((SEQ, LANE), lambda b: (b, 0)),
            pl.BlockSpec((1, LANE), lambda b: (0, 0)),
        ],
        out_specs=[
            pl.BlockSpec((SEQ, LANE), lambda b: (b, 0)),
            pl.BlockSpec((None, B_HEADS, SEQ), lambda b: (b, 0, 0)),
        ],
        out_shape=[
            jax.ShapeDtypeStruct((batch * SEQ, LANE), F32),
            jax.ShapeDtypeStruct((batch, B_HEADS, SEQ), F32),
        ],
        compiler_params=_params(("arbitrary",), est),
        name="forget_cumsum",
    )(misc, bias_row)


def _mla_prep_kernel(cq_ref, ckv_ref, kr_ref, krs_ref, gq_ref, gkv_ref, wq1_ref, wq2_ref, wkv_ref,
                     cos_ref, sin_ref, *rest):
    if len(rest) == 5:
        w_next_ref, qc_ref, kc_ref, vc_ref, w_next_packed_ref = rest
        _pack_w_in_kernel(w_next_ref, w_next_packed_ref)
    else:
        qc_ref, kc_ref, vc_ref = rest
    cq = cq_ref[...].astype(F32)
    ms = jnp.sum(cq * cq, axis=-1, keepdims=True) * (1.0 / Q_LORA)
    cqn = (cq * lax.rsqrt(ms + EPS) * gq_ref[...]).astype(BF16)
    ckv = ckv_ref[...].astype(F32)
    ms2 = jnp.mean(ckv * ckv, axis=-1, keepdims=True)
    ckvn = (ckv * lax.rsqrt(ms2 + EPS) * gkv_ref[...]).astype(BF16)

    q1 = jnp.dot(cqn, wq1_ref[...], preferred_element_type=F32) * Q_SCALE_MLA
    q2 = jnp.dot(cqn, wq2_ref[...], preferred_element_type=F32) * Q_SCALE_MLA
    kv = jnp.dot(ckvn, wkv_ref[...], preferred_element_type=F32)
    cos = cos_ref[...]
    sin = sin_ref[...]
    k_rope = (kr_ref[...].astype(F32) * cos + krs_ref[...].astype(F32) * sin).astype(BF16)
    for h in range(C_HEADS):
        qw = NOPE_DIM + LANE
        qc_ref[:, h * qw:h * qw + NOPE_DIM] = q1[:, h * qw:h * qw + NOPE_DIM].astype(BF16)
        q_rope = q1[:, h * qw + NOPE_DIM:(h + 1) * qw] * cos + q2[:, h * LANE:(h + 1) * LANE] * sin
        qc_ref[:, h * qw + NOPE_DIM:(h + 1) * qw] = q_rope.astype(BF16)
        kw = NOPE_DIM + V_DIM
        kc_ref[:, h * qw:h * qw + NOPE_DIM] = kv[:, h * kw:h * kw + NOPE_DIM].astype(BF16)
        kc_ref[:, h * qw + NOPE_DIM:(h + 1) * qw] = k_rope
        vc_ref[:, h * V_DIM:(h + 1) * V_DIM] = kv[:, h * kw + NOPE_DIM:(h + 1) * kw].astype(BF16)


def _mla_prep(proj, gq, gkv, wq1, wq2, wkv, cos_t, sin_t, w_in=None, pack_layer=None, *, tm=512):
    m = proj.shape[0]
    nseq = SEQ // tm
    steps = m // tm
    qw = C_HEADS * (NOPE_DIM + LANE)
    est = 2 * (tm * 1024 * 2) + 2 * (CQ_PAD * qw * 2 + CQ_PAD * 512 * 2 + 128 * 1024 * 2) \
        + 2 * (2 * tm * qw * 2 + tm * 512 * 2) + 3 * tm * qw * 4
    const = lambda i: (0, 0)
    extra_in, extra_out, extra_shape, extra_args = [], [], [], []
    if pack_layer is not None:
        w_t = jnp.swapaxes(w_in, 1, 2)
        _, n, d = w_t.shape
        cols = d // steps
        extra_in = [pl.BlockSpec((None, n, cols), lambda i: (pack_layer, 0, i))]
        extra_out = [pl.BlockSpec((N_PACK, cols), lambda i: (0, i))]
        extra_shape = [jax.ShapeDtypeStruct((N_PACK, d), BF16)]
        extra_args = [w_t]
        est += 2 * cols * (n * 4 + N_PACK * 2)
    return pl.pallas_call(
        _mla_prep_kernel,
        grid=(steps,),
        in_specs=[
            pl.BlockSpec((tm, CQ_PAD), lambda i: (i, OFF_CQ // CQ_PAD)),
            pl.BlockSpec((tm, LANE), lambda i: (i, OFF_CKV // LANE)),
            pl.BlockSpec((tm, LANE), lambda i: (i, OFF_KR // LANE)),
            pl.BlockSpec((tm, LANE), lambda i: (i, OFF_KRS // LANE)),
            pl.BlockSpec((1, CQ_PAD), const),
            pl.BlockSpec((1, KV_LORA), const),
            pl.BlockSpec((CQ_PAD, qw), const),
            pl.BlockSpec((CQ_PAD, C_HEADS * LANE), const),
            pl.BlockSpec((KV_LORA, C_HEADS * (NOPE_DIM + V_DIM)), const),
            pl.BlockSpec((tm, LANE), lambda i: (i % nseq, 0)),
            pl.BlockSpec((tm, LANE), lambda i: (i % nseq, 0)),
        ] + extra_in,
        out_specs=[
            pl.BlockSpec((tm, qw), lambda i: (i, 0)),
            pl.BlockSpec((tm, qw), lambda i: (i, 0)),
            pl.BlockSpec((tm, C_HEADS * V_DIM), lambda i: (i, 0)),
        ] + extra_out,
        out_shape=[
            jax.ShapeDtypeStruct((m, qw), BF16),
            jax.ShapeDtypeStruct((m, qw), BF16),
            jax.ShapeDtypeStruct((m, C_HEADS * V_DIM), BF16),
        ] + extra_shape,
        compiler_params=_params(("arbitrary",), est),
        name="mla_prep",
    )(proj, proj, proj, proj, gq, gkv, wq1, wq2, wkv, cos_t, sin_t, *extra_args)


def _paired_loop(n, body, init):
    def pair(p, carry):
        return body(2 * p + 1, body(2 * p, carry))
    carry = lax.fori_loop(0, n // 2, pair, init)
    return lax.fori_loop(2 * (n // 2), n, body, carry)


def _nt_dot(a, b):
    return lax.dot_general(a, b, (((1,), (1,)), ((), ())), preferred_element_type=F32)


def _two_pass_attention(i, *, n_heads, tq, tk, dv, logits_fn, diag_mask, v_fn, store_fn,
                        s_ref, mx_ref, acc_ref):
    assert tk == 2 * tq
    n_wide = (i * tq) // tk
    has_narrow = (i * tq) % tk != 0
    narrow_ks = pl.multiple_of(n_wide * tk, tk)
    diag_ks = pl.multiple_of(i * tq, tq)

    mx_ref[...] = jnp.full(mx_ref.shape, NEG_INF, F32)

    def pass1(ks, width, mask=None):
        for h in range(n_heads):
            s = logits_fn(h, ks, width)
            if mask is not None:
                s = jnp.where(mask, s, NEG_INF)
            s_ref[h, :, pl.ds(ks, width)] = s
            mx = mx_ref[h]
            for c in range(width // LANE):
                mx = jnp.maximum(mx, s[:, c * LANE:(c + 1) * LANE])
            mx_ref[h] = mx

    def pass1_wide(kb, carry):
        pass1(pl.multiple_of(kb * tk, tk), tk)
        return carry

    _paired_loop(n_wide, pass1_wide, 0)

    @pl.when(has_narrow)
    def _():
        pass1(narrow_ks, tq)

    pass1(diag_ks, tq, diag_mask)

    for h in range(n_heads):
        mx_ref[h] = jnp.broadcast_to(jnp.max(mx_ref[h], axis=-1, keepdims=True), (tq, LANE))
    acc_ref[...] = jnp.zeros(acc_ref.shape, F32)

    def pass2(ks, width):
        ones = jnp.ones((width, LANE), BF16)
        for h in range(n_heads):
            m = mx_ref[h]
            p = jnp.concatenate(
                [jnp.exp2(s_ref[h, :, pl.ds(ks + c * LANE, LANE)] - m).astype(BF16)
                 for c in range(width // LANE)], axis=1)
            v1 = jnp.concatenate([v_fn(h, ks, width), ones], axis=1)
            acc_ref[h] += jnp.dot(p, v1, preferred_element_type=F32)

    def pass2_wide(kb, carry):
        pass2(pl.multiple_of(kb * tk, tk), tk)
        return carry

    _paired_loop(n_wide, pass2_wide, 0)

    @pl.when(has_narrow)
    def _():
        pass2(narrow_ks, tq)

    pass2(diag_ks, tq)

    for h in range(n_heads):
        acc = acc_ref[h]
        store_fn(h, acc[:, :dv] / acc[:, dv:])


def _tile_iotas(tq, tk):
    return (lax.broadcasted_iota(jnp.int32, (tq, tk), 0), lax.broadcasted_iota(jnp.int32, (tq, tk), 1))


def _fox_kernel(q_ref, k_ref, v_ref, ccol_ref, crow_ref, o_ref,
                s_ref, mx_ref, acc_ref, cq_ref, *, tq, tk):
    i = pl.program_id(1)
    head = lambda h: slice(h * HEAD_DIM, (h + 1) * HEAD_DIM)
    for h in range(B_HEADS):
        cq_ref[h] = jnp.broadcast_to(ccol_ref[:, MISC_FL + h:MISC_FL + h + 1] * LOG2E, (tq, LANE))

    def logits_fn(h, ks, width):
        s = _nt_dot(q_ref[:, head(h)], k_ref[pl.ds(ks, width), head(h)])
        cq = jnp.concatenate([cq_ref[h]] * (width // LANE), axis=1)
        return s + cq - crow_ref[h:h + 1, pl.ds(ks, width)] * LOG2E

    def store_fn(h, y):
        o_ref[:, head(h)] = y.astype(o_ref.dtype)

    rows, cols = _tile_iotas(tq, tq)
    _two_pass_attention(i, n_heads=B_HEADS, tq=tq, tk=tk, dv=HEAD_DIM, logits_fn=logits_fn,
                        diag_mask=cols <= rows,
                        v_fn=lambda h, ks, width: v_ref[pl.ds(ks, width), head(h)],
                        store_fn=store_fn, s_ref=s_ref, mx_ref=mx_ref, acc_ref=acc_ref)


def _fox_attention(proj, ccol, crow, batch, *, tq=256, tk=512):
    m = proj.shape[0]
    nq = SEQ // tq
    w = B_HEADS * HEAD_DIM
    state = B_HEADS * tq * LANE * 4
    est = (4 * tq * w * 2 + 4 * SEQ * w * 2 + 2 * tq * LANE * 4 + 2 * 8 * SEQ * 4
           + B_HEADS * tq * SEQ * 4 + 4 * state + 6 * tq * tk * 4)
    return pl.pallas_call(
        functools.partial(_fox_kernel, tq=tq, tk=tk),
        grid=(batch, nq),
        in_specs=[
            pl.BlockSpec((tq, w), lambda b, i: (b * nq + i, OFF_QB // w)),
            pl.BlockSpec((SEQ, w), lambda b, i: (b, OFF_KB // w)),
            pl.BlockSpec((SEQ, w), lambda b, i: (b, OFF_VB // w)),
            pl.BlockSpec((tq, LANE), lambda b, i: (b * nq + i, 0)),
            pl.BlockSpec((None, B_HEADS, SEQ), lambda b, i: (b, 0, 0)),
        ],
        out_specs=pl.BlockSpec((tq, w), lambda b, i: (b * nq + i, 0)),
        out_shape=jax.ShapeDtypeStruct((m, w), BF16),
        scratch_shapes=[
            pltpu.VMEM((B_HEADS, tq, SEQ), F32),
            pltpu.VMEM((B_HEADS, tq, LANE), F32),
            pltpu.VMEM((B_HEADS, tq, HEAD_DIM + LANE), F32),
            pltpu.VMEM((B_HEADS, tq, LANE), F32),
        ],
        compiler_params=_params(("arbitrary", "arbitrary"), est),
        name="fox_attention",
    )(proj, proj, proj, ccol, crow)


def _mla_kernel(q_ref, k_ref, v_ref, o_ref, s_ref, mx_ref, acc_ref, *, tq, tk):
    i = pl.program_id(1)
    qw = NOPE_DIM + LANE
    qhead = lambda h: slice(h * qw, (h + 1) * qw)
    vhead = lambda h: slice(h * V_DIM, (h + 1) * V_DIM)

    def logits_fn(h, ks, width):
        return _nt_dot(q_ref[:, qhead(h)], k_ref[pl.ds(ks, width), qhead(h)])

    def store_fn(h, y):
        o_ref[:, vhead(h)] = y.astype(o_ref.dtype)

    rows, cols = _tile_iotas(tq, tq)
    _two_pass_attention(i, n_heads=C_HEADS, tq=tq, tk=tk, dv=V_DIM, logits_fn=logits_fn,
                        diag_mask=cols // CHUNK <= rows // CHUNK,
                        v_fn=lambda h, ks, width: v_ref[pl.ds(ks, width), vhead(h)],
                        store_fn=store_fn, s_ref=s_ref, mx_ref=mx_ref, acc_ref=acc_ref)


def _mla_attention(qc, kc, vc, batch, *, tq=256, tk=512):
    m = qc.shape[0]
    nq = SEQ // tq
    qw = C_HEADS * (NOPE_DIM + LANE)
    vw = C_HEADS * V_DIM
    state = C_HEADS * tq * LANE * 4
    est = (2 * tq * qw * 2 + 2 * SEQ * qw * 2 + 2 * SEQ * vw * 2 + 2 * tq * vw * 2
           + C_HEADS * tq * SEQ * 4 + 3 * state + 6 * tq * tk * 4)
    return pl.pallas_call(
        functools.partial(_mla_kernel, tq=tq, tk=tk),
        grid=(batch, nq),
        in_specs=[
            pl.BlockSpec((tq, qw), lambda b, i: (b * nq + i, 0)),
            pl.BlockSpec((SEQ, qw), lambda b, i: (b, 0)),
            pl.BlockSpec((SEQ, vw), lambda b, i: (b, 0)),
        ],
        out_specs=pl.BlockSpec((tq, vw), lambda b, i: (b * nq + i, 0)),
        out_shape=jax.ShapeDtypeStruct((m, vw), BF16),
        scratch_shapes=[
            pltpu.VMEM((C_HEADS, tq, SEQ), F32),
            pltpu.VMEM((C_HEADS, tq, LANE), F32),
            pltpu.VMEM((C_HEADS, tq, V_DIM + LANE), F32),
        ],
        compiler_params=_params(("arbitrary", "arbitrary"), est),
        name="mla_attention",
    )(qc, kc, vc)


def _t5_tiles_kernel(bucket_ref, table_ref, o_ref):
    for d in range(3):
        bucket = bucket_ref[d]
        for h in range(A_HEADS):
            acc = jnp.zeros(bucket.shape, F32)
            for nb in range(NUM_BUCKETS):
                acc = jnp.where(bucket == nb, table_ref[nb, h] * LOG2E, acc)
            o_ref[d, h] = acc


def _t5_tiles(bucket_tiles, t5_bias, *, t=ATT_T):
    return pl.pallas_call(
        _t5_tiles_kernel,
        in_specs=[
            pl.BlockSpec(memory_space=pltpu.VMEM),
            pl.BlockSpec(memory_space=pltpu.SMEM),
        ],
        out_specs=pl.BlockSpec(memory_space=pltpu.VMEM),
        out_shape=jax.ShapeDtypeStruct((3, A_HEADS, t, t), F32),
        name="t5_tiles",
    )(bucket_tiles, t5_bias)


def _dsa_kernel(qi_ref, klo_ref, khi_ref, misc_ref, qa_ref, ka_ref, va_ref, bias_ref, o_ref,
                sc_ref, lg_ref, acc_ref, *, t):
    i = pl.program_id(1)
    groups = t // 8
    key = lax.broadcasted_iota(jnp.int32, (t, t), 0)
    qry = lax.broadcasted_iota(jnp.int32, (t, t), 1)
    admissible = (key // CHUNK) <= (qry // CHUNK)

    def fold(x, op):
        return op(x.reshape(groups, 8, t), axis=0)

    def over_keys(x8, op):
        return jnp.broadcast_to(op(x8, axis=0, keepdims=True), (8, t))

    w_t = misc_ref[...].T[MISC_WI:MISC_WI + IDX_HEADS, :] * (IDX_HEADS ** -0.5 * IDX_DIM ** -0.5)

    def block_scores(kb):
        ks = pl.multiple_of(kb * t, t)
        klo = klo_ref[pl.ds(ks, t), :]
        khi = khi_ref[pl.ds(ks, t), :]
        acc = jnp.zeros((t, t), F32)
        for pair in range(IDX_HEADS // 2):
            qp = qi_ref[:, pair * LANE:(pair + 1) * LANE]
            for sub, kk in enumerate((klo, khi)):
                h = 2 * pair + sub
                acc = acc + jnp.maximum(_nt_dot(kk, qp), 0.0) * w_t[h:h + 1, :]
        return ks, acc

    def score_body(kb, carry):
        mn, mx = carry
        ks, acc = block_scores(kb)
        sc_ref[pl.ds(ks, t), :] = acc
        return jnp.minimum(mn, fold(acc, jnp.min)), jnp.maximum(mx, fold(acc, jnp.max))

    mn, mx = _paired_loop(i, score_body,
                          (jnp.full((8, t), -SELECT_MIN, F32), jnp.full((8, t), SELECT_MIN, F32)))
    ks, acc = block_scores(i)
    sc_ref[pl.ds(ks, t), :] = jnp.where(admissible, acc, NEG_INF)
    mn = jnp.minimum(mn, fold(jnp.where(admissible, acc, -SELECT_MIN), jnp.min))
    mx = jnp.maximum(mx, fold(jnp.where(admissible, acc, SELECT_MIN), jnp.max))

    def count_ge(thr):
        def body(kb, cnt):
            ks = pl.multiple_of(kb * t, t)
            hit = jnp.where(sc_ref[pl.ds(ks, t), :].reshape(groups, 8, t) >= thr[None], 1.0, 0.0)
            return cnt + jnp.sum(hit, axis=0)
        return over_keys(_paired_loop(i + 1, body, jnp.zeros((8, t), F32)), jnp.sum)

    k_sel = float(TOPK_MAX)
    search = i * t >= TOPK_MAX
    lo0 = jnp.where(search, over_keys(mn, jnp.min), SELECT_MIN)
    hi0 = jnp.where(search, over_keys(mx, jnp.max), SELECT_MIN)

    def midpoint(lo, hi):
        return 0.5 * lo + 0.5 * hi

    def any_active(lo, hi):
        mid = midpoint(lo, hi)
        return jnp.max(jnp.where((mid > lo) & (mid < hi), 1.0, 0.0)) > 0.5

    def search_cond(state):
        return jnp.logical_and(state[0] < BISECT_MAX_ITERS, state[1])

    def search_body(state):
        it, _, lo, hi, c_lo = state
        for _ in range(SEARCH_STEPS_PER_CHECK):
            mid = midpoint(lo, hi)
            cnt = count_ge(mid)
            ge = cnt >= k_sel
            lo = jnp.where(ge, mid, lo)
            c_lo = jnp.where(ge, cnt, c_lo)
            hi = jnp.where(cnt > k_sel, hi, mid)
        return it + SEARCH_STEPS_PER_CHECK, any_active(lo, hi), lo, hi, c_lo

    n_adm = (((i * t + qry[:8]) // CHUNK + 1) * CHUNK).astype(F32)
    _, _, lo, hi, c_lo = lax.while_loop(search_cond, search_body,
                                        (jnp.int32(0), any_active(lo0, hi0), lo0, hi0, n_adm))
    c_hi = count_ge(hi)
    thr = jnp.where(c_hi >= k_sel, hi, lo)
    c_thr = jnp.where(c_hi >= k_sel, c_hi, c_lo)

    @pl.when(jnp.max(jnp.where(c_thr > k_sel, 1.0, 0.0)) > 0.5)
    def _():
        key_in_block = (lax.broadcasted_iota(jnp.int32, (groups, 8, t), 0) * 8
                        + lax.broadcasted_iota(jnp.int32, (groups, 8, t), 1))

        def count_where(pred):
            def body(kb, cnt):
                ks = pl.multiple_of(kb * t, t)
                blk = sc_ref[pl.ds(ks, t), :].reshape(groups, 8, t)
                return cnt + jnp.sum(jnp.where(pred(blk, ks), 1.0, 0.0), axis=0)
            return over_keys(lax.fori_loop(0, i + 1, body, jnp.zeros((8, t), F32)), jnp.sum)

        def tied_before(bound):
            return lambda blk, ks: (blk == thr[None]) & ((key_in_block + ks).astype(F32) < bound[None])

        keep = k_sel - count_where(lambda blk, ks: blk > thr[None])

        def cut_body(_, bounds):
            below, above = bounds
            mid = jnp.floor(0.5 * (below + above))
            enough = count_where(tied_before(mid)) >= keep
            return jnp.where(enough, below, mid), jnp.where(enough, mid, above)

        _, cut = lax.fori_loop(0, SEQ.bit_length(), cut_body,
                               (jnp.zeros((8, t), F32), jnp.full((8, t), float(SEQ), F32)))

        def drop_body(kb, carry):
            ks = pl.multiple_of(kb * t, t)
            blk = sc_ref[pl.ds(ks, t), :].reshape(groups, 8, t)
            drop = (blk == thr[None]) & ((key_in_block + ks).astype(F32) >= cut[None])
            sc_ref[pl.ds(ks, t), :] = jnp.where(drop, NEG_INF, blk).reshape(t, t)
            return carry

        lax.fori_loop(0, i + 1, drop_body, 0)

    head = lambda h: slice(h * HEAD_DIM, (h + 1) * HEAD_DIM)
    acc_ref[...] = jnp.zeros(acc_ref.shape, F32)

    def logits_pass(kb, mx):
        ks = pl.multiple_of(kb * t, t)
        dist = jnp.minimum(i - kb, 2)
        k_blk = ka_ref[pl.ds(ks, t), :]
        sel = sc_ref[pl.ds(ks, t), :] >= thr[0:1, :]
        out = []
        for h in range(A_HEADS):
            s = _nt_dot(k_blk, qa_ref[:, head(h)]) + bias_ref[dist, h]
            s = jnp.where(sel, s, NEG_INF)
            lg_ref[h, pl.ds(ks, t), :] = s
            out.append(jnp.maximum(mx[h], fold(s, jnp.max)))
        return tuple(out)

    mx = _paired_loop(i + 1, logits_pass,
                      tuple(jnp.full((8, t), NEG_INF, F32) for _ in range(A_HEADS)))
    m = [over_keys(mx[h], jnp.max)[0:1, :] for h in range(A_HEADS)]

    ones_rows = jnp.ones((DSA_ONES_ROWS, t), BF16)

    def value_pass(kb, carry):
        ks = pl.multiple_of(kb * t, t)
        v_t = va_ref[pl.ds(ks, t), :].astype(F32).T.astype(BF16)
        v1_t = jnp.concatenate([v_t, ones_rows], axis=0)
        for h in range(A_HEADS):
            p = jnp.exp2(lg_ref[h, pl.ds(ks, t), :] - m[h])
            acc_ref[h] += jnp.dot(v1_t, p.astype(BF16), preferred_element_type=F32)
        return carry

    _paired_loop(i + 1, value_pass, 0)
    for h in range(A_HEADS):
        acc = acc_ref[h]
        o_ref[:, head(h)] = (acc[:HEAD_DIM] / acc[HEAD_DIM:HEAD_DIM + 1]).T.astype(o_ref.dtype)


def _dsa_attention(proj, misc, bias_tiles, batch, *, t=ATT_T):
    m = proj.shape[0]
    nq = SEQ // t
    qiw = IDX_HEADS * IDX_DIM
    qaw = A_HEADS * HEAD_DIM
    est = (2 * t * qiw * 2 + 2 * 4 * SEQ * LANE * 2 + 2 * t * LANE * 4 + 2 * t * qaw * 2
           + 2 * 3 * A_HEADS * t * t * 4 + 2 * t * qaw * 2
           + (1 + A_HEADS) * t * SEQ * 4 + A_HEADS * HEAD_DIM * t * 4 + 12 * t * t * 4)
    kblock = lambda off: pl.BlockSpec((SEQ, LANE), lambda b, i: (b, off // LANE))
    return pl.pallas_call(
        functools.partial(_dsa_kernel, t=t),
        grid=(batch, nq),
        in_specs=[
            pl.BlockSpec((t, qiw), lambda b, i: (b * nq + i, OFF_QI // qiw)),
            kblock(OFF_KI_LO),
            kblock(OFF_KI_HI),
            pl.BlockSpec((t, LANE), lambda b, i: (b * nq + i, 0)),
            pl.BlockSpec((t, qaw), lambda b, i: (b * nq + i, OFF_QA // qaw)),
            kblock(OFF_KA),
            kblock(OFF_VA),
            pl.BlockSpec((3, A_HEADS, t, t), lambda b, i: (0, 0, 0, 0)),
        ],
        out_specs=pl.BlockSpec((t, qaw), lambda b, i: (b * nq + i, 0)),
        out_shape=jax.ShapeDtypeStruct((m, qaw), BF16),
        scratch_shapes=[
            pltpu.VMEM((SEQ, t), F32),
            pltpu.VMEM((A_HEADS, SEQ, t), F32),
            pltpu.VMEM((A_HEADS, HEAD_DIM + DSA_ONES_ROWS, t), F32),
        ],
        compiler_params=_params(("arbitrary", "arbitrary"), est),
        name="dsa_attention",
    )(proj, proj, proj, misc, proj, proj, proj, bias_tiles)


def _merge_kernel(x_ref, ya_ref, yb_ref, yc_ref, ga_ref, gb_ref, gc_ref,
                  wa_ref, wb_ref, wc_ref, wo_ref, gn_ref, o_ref, hn_ref):
    def branch(y_ref, w_ref, g_ref):
        y = jnp.dot(y_ref[...], w_ref[...], preferred_element_type=F32)
        return jax.nn.sigmoid(g_ref[...].astype(F32)) * y

    merged = branch(ya_ref, wa_ref, ga_ref) + branch(yb_ref, wb_ref, gb_ref) + branch(yc_ref, wc_ref, gc_ref)
    x_new = x_ref[...] + jnp.dot(merged.astype(BF16), wo_ref[...], preferred_element_type=F32)
    o_ref[...] = x_new
    ms = jnp.mean(x_new * x_new, axis=-1, keepdims=True)
    hn_ref[...] = (x_new * lax.rsqrt(ms + EPS) * gn_ref[...]).astype(hn_ref.dtype)


def _merge(x2, ya, yb, yc, proj, wa, wb, wc, wo, g_next, *, tm=256):
    m = x2.shape[0]
    wbytes = (wa.size + wb.size + wc.size + wo.size) * 2
    est = 2 * wbytes + 4 * tm * D_MODEL * 4 + 2 * tm * 2048 * 2 + 8 * tm * D_MODEL * 2 + 4 * tm * D_MODEL * 4
    const = lambda i: (0, 0)
    gate = lambda k: pl.BlockSpec((tm, D_MODEL), lambda i: (i, OFF_GL // D_MODEL + k))
    return pl.pallas_call(
        _merge_kernel,
        grid=(m // tm,),
        in_specs=[
            pl.BlockSpec((tm, D_MODEL), lambda i: (i, 0)),
            pl.BlockSpec((tm, ya.shape[1]), lambda i: (i, 0)),
            pl.BlockSpec((tm, yb.shape[1]), lambda i: (i, 0)),
            pl.BlockSpec((tm, yc.shape[1]), lambda i: (i, 0)),
            gate(0), gate(1), gate(2),
            pl.BlockSpec(wa.shape, const),
            pl.BlockSpec(wb.shape, const),
            pl.BlockSpec(wc.shape, const),
            pl.BlockSpec(wo.shape, const),
            pl.BlockSpec((1, D_MODEL), const),
        ],
        out_specs=[
            pl.BlockSpec((tm, D_MODEL), lambda i: (i, 0)),
            pl.BlockSpec((tm, D_MODEL), lambda i: (i, 0)),
        ],
        out_shape=[
            jax.ShapeDtypeStruct((m, D_MODEL), F32),
            jax.ShapeDtypeStruct((m, D_MODEL), BF16),
        ],
        compiler_params=_params(("arbitrary",), est),
        name="merge",
    )(x2, ya, yb, yc, proj, proj, proj, wa, wb, wc, wo, g_next)


def _ffn_kernel(x_ref, h_ref, wu_ref, wv_ref, cw_ref, wd_ref, fg_ref, o_ref,
                ubuf_ref, halo_ref, *, tm, tf, final_norm):
    i = pl.program_id(0)
    j = pl.program_id(1)
    nj = pl.num_programs(1)

    @pl.when(j == 0)
    def _():
        o_ref[...] = x_ref[...]

        @pl.when((i % (SEQ // tm)) == 0)
        def _():
            halo_ref[...] = jnp.zeros(halo_ref.shape, F32)

    h = h_ref[...]
    u = jnp.dot(h, wu_ref[...], preferred_element_type=F32)
    v = jnp.dot(h, wv_ref[...], preferred_element_type=F32)

    ubuf_ref[0:8, :] = halo_ref[j]
    ubuf_ref[8:, :] = u
    halo_ref[j] = u[tm - 8:, :]
    conv = (cw_ref[0:1, :] * ubuf_ref[6:6 + tm, :] + cw_ref[1:2, :] * ubuf_ref[7:7 + tm, :]
            + cw_ref[2:3, :] * u + cw_ref[3:4, :])
    act = (jax.nn.gelu(conv) * v).astype(BF16)
    o_ref[...] += jnp.dot(act, wd_ref[...], preferred_element_type=F32)

    if final_norm:
        @pl.when(j == nj - 1)
        def _():
            _rmsnorm_rows(o_ref, fg_ref, o_ref, tm)


def _ffn(x2, h2, w_up, cw, w_down, final_g, *, final_norm, tm=1024, tf=512):
    m = x2.shape[0]
    nj = D_FF // tf
    est = (3 * tm * D_MODEL * 4 + 2 * tm * D_MODEL * 2 + 2 * 2 * D_MODEL * tf * 2 + 2 * tf * D_MODEL * 2
           + (tm + 8) * tf * 4 + nj * 8 * tf * 4 + 4 * tm * tf * 4)
    return pl.pallas_call(
        functools.partial(_ffn_kernel, tm=tm, tf=tf, final_norm=final_norm),
        grid=(m // tm, nj),
        in_specs=[
            pl.BlockSpec((tm, D_MODEL), lambda i, j: (i, 0), pipeline_mode=pl.Buffered(1)),
            pl.BlockSpec((tm, D_MODEL), lambda i, j: (i, 0)),
            pl.BlockSpec((D_MODEL, tf), lambda i, j: (0, j)),
            pl.BlockSpec((D_MODEL, tf), lambda i, j: (0, nj + j)),
            pl.BlockSpec((8, tf), lambda i, j: (0, j)),
            pl.BlockSpec((tf, D_MODEL), lambda i, j: (j, 0)),
            pl.BlockSpec((1, D_MODEL), lambda i, j: (0, 0)),
        ],
        out_specs=pl.BlockSpec((tm, D_MODEL), lambda i, j: (i, 0)),
        out_shape=jax.ShapeDtypeStruct((m, D_MODEL), F32),
        scratch_shapes=[
            pltpu.VMEM((tm + 8, tf), F32),
            pltpu.VMEM((nj, 8, tf), F32),
        ],
        compiler_params=_params(("arbitrary", "arbitrary"), est),
        name="conv_ffn",
    )(x2, h2, w_up, w_up, cw, w_down, final_g)


def _t5_bucket(rel):
    nb = NUM_BUCKETS // 2
    max_exact = nb // 2
    base = jnp.where(rel > 0, nb, 0)
    n = jnp.abs(rel)
    nf = jnp.maximum(n, 1).astype(F32)
    large = max_exact + (jnp.log(nf / max_exact) / math.log(MAX_DISTANCE / max_exact)
                         * (nb - max_exact)).astype(jnp.int32)
    large = jnp.minimum(large, nb - 1)
    return base + jnp.where(n < max_exact, n, large)


def _bucket_tiles(t):
    key = jnp.arange(t, dtype=jnp.int32)[:, None]
    qry = jnp.arange(t, dtype=jnp.int32)[None, :]
    return jnp.stack([_t5_bucket(key - qry - d * t) for d in range(3)])


def _rope_tables():
    half = ROPE_DIM // 2
    inv = ROPE_THETA ** (-jnp.arange(half, dtype=F32) / half)
    ang = jnp.arange(SEQ, dtype=jnp.int32).astype(F32)[:, None] * inv[None, :]
    cos, sin = jnp.cos(ang), jnp.sin(ang)
    pad = jnp.zeros((SEQ, LANE - ROPE_DIM), F32)
    return jnp.concatenate([cos, cos, pad], axis=1), jnp.concatenate([-sin, sin, pad], axis=1)


def _swap_halves(w):
    half = w.shape[-1] // 2
    return jnp.concatenate([w[..., half:], w[..., :half]], axis=-1)


def _pack_w_in_kernel(w_ref, o_ref):
    off = np.cumsum((0,) + IN_SIZES)
    seg = lambda k: w_ref[off[k]:off[k + 1], :]
    qa, ka, va, qi, ki, wi, qb, kb, vb, fl, cq, ckv, kr, gl = [seg(k) for k in range(len(IN_SIZES))]
    qa, qb = qa * Q_SCALE, qb * Q_SCALE
    z = lambda n: jnp.zeros((n, w_ref.shape[1]), F32)
    half = ROPE_DIM // 2
    groups = [
        (OFF_QI, [qi]), (OFF_QB, [qb]), (OFF_KB, [kb]), (OFF_VB, [vb]), (OFF_GL, [gl]), (OFF_QA, [qa]),
        (OFF_CQ, [cq, z(CQ_PAD - Q_LORA)]), (OFF_KA, [ka]), (OFF_VA, [va]), (OFF_KI_LO, [ki, z(64)]),
        (OFF_CKV, [ckv]), (OFF_KR, [kr, z(64)]), (OFF_KRS, [kr[half:], kr[:half], z(64)]),
        (OFF_MISC, [wi, fl, z(LANE - IDX_HEADS - B_HEADS)]), (OFF_KI_HI, [z(64), ki]),
    ]
    for start, pieces in groups:
        block = pieces[0] if len(pieces) == 1 else jnp.concatenate(pieces, axis=0)
        o_ref[start:start + block.shape[0], :] = block.astype(o_ref.dtype)


def _pack_w_in(w_in, l, *, cols=256):
    w_t = jnp.swapaxes(w_in, 1, 2)
    _, n, d = w_t.shape
    return pl.pallas_call(
        _pack_w_in_kernel,
        grid=(d // cols,),
        in_specs=[pl.BlockSpec((None, n, cols), lambda i: (l, 0, i))],
        out_specs=pl.BlockSpec((N_PACK, cols), lambda i: (0, i)),
        out_shape=jax.ShapeDtypeStruct((N_PACK, d), BF16),
        compiler_params=_params(("arbitrary",), 2 * cols * (n * 4 + N_PACK * 2)),
        name="pack_w_in",
    )(w_t)


def _pack_w_uq(w):
    w = jnp.pad(w, ((0, CQ_PAD - Q_LORA), (0, 0))).reshape(CQ_PAD, C_HEADS, NOPE_DIM + ROPE_DIM)
    z = jnp.zeros((CQ_PAD, C_HEADS, LANE - ROPE_DIM), w.dtype)
    rope = w[..., NOPE_DIM:]
    wq1 = jnp.concatenate([w, z], axis=-1).reshape(CQ_PAD, -1)
    wq2 = jnp.concatenate([_swap_halves(rope), z], axis=-1).reshape(CQ_PAD, -1)
    return wq1.astype(BF16), wq2.astype(BF16)


def kernel(x, norm_mix_g, w_in, b_forget, g_cq, g_ckv, w_uq, w_ukv, w_branch_a, w_branch_b, w_branch_c,
           w_o, norm_ffn_g, w_up, conv_w, conv_b, w_down, t5_bias, final_g):
    batch, seq, d = x.shape
    assert (seq, d) == (SEQ, D_MODEL)
    x2 = x.reshape(batch * seq, d)

    bias_tiles = _t5_tiles(_bucket_tiles(ATT_T), t5_bias)
    cos_t, sin_t = _rope_tables()
    final_row = final_g.reshape(1, D_MODEL)

    w_in_packed = _pack_w_in(w_in, 0)
    for l in range(DEPTH):
        casts = [(w_up, True, 11), (w_down, False, 11), (w_o, True, 8),
                 (w_branch_a, True, 8), (w_branch_b, True, 8), (w_branch_c, True, 8)]
        proj, misc, w_up_l, w_down_l, w_o_l, w_a_l, w_b_l, w_c_l = _inproj(
            x2, norm_mix_g[l].reshape(1, D_MODEL), w_in_packed, casts, l)

        fbias = jnp.zeros((1, LANE), F32).at[0, MISC_FL:MISC_FL + B_HEADS].set(b_forget[l])
        ccol, crow = _forget_cumsum(misc, fbias, batch)

        wq1, wq2 = _pack_w_uq(w_uq[l])
        gq = jnp.pad(g_cq[l], (0, CQ_PAD - Q_LORA)).reshape(1, CQ_PAD)
        prep = _mla_prep(proj, gq, g_ckv[l].reshape(1, KV_LORA), wq1, wq2, w_ukv[l].astype(BF16),
                         cos_t, sin_t, w_in, l + 1 if l + 1 < DEPTH else None)
        qc, kc, vc = prep[:3]
        if l + 1 < DEPTH:
            w_in_packed = prep[3]

        ya = _dsa_attention(proj, misc, bias_tiles, batch)
        yb = _fox_attention(proj, ccol, crow, batch)
        yc = _mla_attention(qc, kc, vc, batch)

        x2, h2 = _merge(x2, ya, yb, yc, proj, w_a_l, w_b_l, w_c_l, w_o_l,
                        norm_ffn_g[l].reshape(1, D_MODEL))

        cw = jnp.concatenate([conv_w[l], conv_b[l][None, :], jnp.zeros((4, D_FF), F32)], axis=0)
        x2 = _ffn(x2, h2, w_up_l, cw, w_down_l, final_row, final_norm=(l == DEPTH - 1))

    return x2.reshape(batch, seq, d)
```

```python
import functools
import math

import jax
import jax.numpy as jnp
import numpy as np
from jax import lax
from jax.experimental import pallas as pl
from jax.experimental.pallas import tpu as pltpu

F32 = jnp.float32
BF16 = jnp.bfloat16

D_MODEL = 2048
SEQ = 2048
DEPTH = 2
CHUNK = 64
HEAD_DIM = 128
EPS = 1e-6
NEG_INF = -1e30

A_HEADS = 4
IDX_HEADS = 16
IDX_DIM = 64
TOPK_MAX = 256
NUM_BUCKETS = 32
MAX_DISTANCE = 128
B_HEADS = 8
C_HEADS = 4
Q_LORA = 448
KV_LORA = 128
NOPE_DIM = 128
ROPE_DIM = 64
V_DIM = 128
ROPE_THETA = 10000.0
D_FF = 5632

IN_SIZES = (
    A_HEADS * HEAD_DIM, HEAD_DIM, HEAD_DIM,
    IDX_HEADS * IDX_DIM, IDX_DIM, IDX_HEADS,
    B_HEADS * HEAD_DIM, B_HEADS * HEAD_DIM, B_HEADS * HEAD_DIM, B_HEADS,
    Q_LORA, KV_LORA, ROPE_DIM,
    3 * D_MODEL,
)

LOG2E = math.log2(math.e)
Q_SCALE = HEAD_DIM ** -0.5 * LOG2E
Q_SCALE_MLA = (NOPE_DIM + ROPE_DIM) ** -0.5 * LOG2E
LANE = 128
V7X_VMEM_BYTES = 64 * 1024 * 1024

OFF_QI = 0
OFF_QB = 1024
OFF_KB = 2048
OFF_VB = 3072
OFF_GL = 4096
OFF_QA = 10240
OFF_CQ = 10752
OFF_KA = 11264
OFF_VA = 11392
OFF_KI_LO = 11520
OFF_CKV = 11648
OFF_KR = 11776
OFF_KRS = 11904
OFF_MISC = 12032
OFF_KI_HI = 12160
N_PACK = 12288
CQ_PAD = 512
MISC_WI = 0
MISC_FL = IDX_HEADS

ATT_T = 256
SELECT_MIN = -1e29
BISECT_MAX_ITERS = 512
SEARCH_STEPS_PER_CHECK = 4
DSA_ONES_ROWS = 16


def _vmem_limit(estimate_bytes):
    limit = V7X_VMEM_BYTES - (6 << 20)
    assert estimate_bytes <= limit, estimate_bytes
    return limit


def _params(semantics, vmem_estimate):
    return pltpu.CompilerParams(dimension_semantics=semantics,
                                vmem_limit_bytes=_vmem_limit(vmem_estimate))


def _rmsnorm_rows(x_ref, g_ref, out_ref, rows):
    def body(c, carry):
        r = pl.multiple_of(c * 128, 128)
        x = x_ref[pl.ds(r, 128), :]
        ms = jnp.mean(x * x, axis=-1, keepdims=True)
        out_ref[pl.ds(r, 128), :] = (x * lax.rsqrt(ms + EPS) * g_ref[...]).astype(out_ref.dtype)
        return carry
    lax.fori_loop(0, rows // 128, body, 0)


def _inproj_kernel(*refs, tm, tn, n_cast):
    x_ref, g_ref, w_ref = refs[:3]
    cast_in = refs[3:3 + n_cast]
    o_ref, misc_ref = refs[3 + n_cast:5 + n_cast]
    cast_out = refs[5 + n_cast:5 + 2 * n_cast]
    h_ref = refs[5 + 2 * n_cast]
    j = pl.program_id(1)

    @pl.when(j == 0)
    def _():
        _rmsnorm_rows(x_ref, g_ref, h_ref, tm)

    acc = _nt_dot(h_ref[...], w_ref[...])
    o_ref[...] = acc.astype(o_ref.dtype)

    for src, dst in zip(cast_in, cast_out):
        dst[...] = src[...].astype(dst.dtype)

    @pl.when(j == OFF_MISC // tn)
    def _():
        lo = OFF_MISC % tn
        misc_ref[...] = acc[:, lo:lo + LANE]


def _cast_specs(w_stack, l, ni, nj, rows_follow_i, n_chunks):
    _, r, c = w_stack.shape
    assert n_chunks <= nj
    chunk = lambda j: jnp.minimum(j, n_chunks - 1)
    if rows_follow_i:
        block = (r // ni, c // n_chunks)
        idx = lambda i, j: (i, chunk(j))
    else:
        block = (r // n_chunks, c // ni)
        idx = lambda i, j: (chunk(j), i)
    assert block[0] % 16 == 0 and block[1] % LANE == 0, block
    in_spec = pl.BlockSpec((None,) + block, lambda i, j: (l,) + idx(i, j))
    out_spec = pl.BlockSpec(block, idx)
    return in_spec, out_spec, jax.ShapeDtypeStruct((r, c), BF16), block[0] * block[1]


def _inproj(x2, g, w_pack, casts, l, *, tm=1024, tn=1024):
    m = x2.shape[0]
    ni, nj = m // tm, N_PACK // tn
    plans = [_cast_specs(w, l, ni, nj, rows_i, n) for w, rows_i, n in casts]
    est = (2 * tm * D_MODEL * 4 + tm * D_MODEL * 2 + 2 * D_MODEL * tn * 2 + 2 * tm * tn * 2 + tm * tn * 4
           + sum(2 * 6 * p[3] for p in plans))
    return pl.pallas_call(
        functools.partial(_inproj_kernel, tm=tm, tn=tn, n_cast=len(casts)),
        grid=(ni, nj),
        in_specs=[
            pl.BlockSpec((tm, D_MODEL), lambda i, j: (i, 0)),
            pl.BlockSpec((1, D_MODEL), lambda i, j: (0, 0)),
            pl.BlockSpec((tn, D_MODEL), lambda i, j: (j, 0)),
        ] + [p[0] for p in plans],
        out_specs=[
            pl.BlockSpec((tm, tn), lambda i, j: (i, j)),
            pl.BlockSpec((tm, LANE), lambda i, j: (i, 0)),
        ] + [p[1] for p in plans],
        out_shape=[
            jax.ShapeDtypeStruct((m, N_PACK), BF16),
            jax.ShapeDtypeStruct((m, LANE), F32),
        ] + [p[2] for p in plans],
        scratch_shapes=[pltpu.VMEM((tm, D_MODEL), BF16)],
        compiler_params=_params(("arbitrary", "arbitrary"), est),
        name="inproj",
    )(x2, g, w_pack, *[w for w, _, _ in casts])


def _forget_cumsum_kernel(misc_ref, bias_ref, ccol_ref, crow_ref, *, blk):
    rows = lax.broadcasted_iota(jnp.int32, (blk, blk), 0)
    cols = lax.broadcasted_iota(jnp.int32, (blk, blk), 1)
    tri = jnp.where(rows >= cols, 1.0, 0.0).astype(BF16)
    carry = jnp.zeros((1, LANE), F32)
    for c in range(SEQ // blk):
        z = misc_ref[c * blk:(c + 1) * blk, :] + bias_ref[...]
        lf = jnp.minimum(z, 0.0) - jnp.log1p(jnp.exp(-jnp.abs(z)))
        p0 = lf.astype(BF16)
        r1 = lf - p0.astype(F32)
        p1 = r1.astype(BF16)
        p2 = (r1 - p1.astype(F32)).astype(BF16)
        cs = (jnp.dot(tri, p0, preferred_element_type=F32)
              + jnp.dot(tri, p1, preferred_element_type=F32)
              + jnp.dot(tri, p2, preferred_element_type=F32)) + carry
        ccol_ref[c * blk:(c + 1) * blk, :] = cs
        crow_ref[:, c * blk:(c + 1) * blk] = cs.T[MISC_FL:MISC_FL + B_HEADS, :]
        carry = cs[blk - 1:blk, :]


def _forget_cumsum(misc, bias_row, batch, *, blk=256):
    est = 4 * SEQ * LANE * 4 + 2 * 8 * SEQ * 4
    return pl.pallas_call(
        functools.partial(_forget_cumsum_kernel, blk=blk),
        grid=(batch,),
        in_specs=[
            pl.BlockSpec((SEQ, LANE), lambda b: (b, 0)),
            pl.BlockSpec((1, LANE), lambda b: (0, 0)),
        ],
        out_specs=[
            pl.BlockSpec((SEQ, LANE), lambda b: (b, 0)),
            pl.BlockSpec((None, B_HEADS, SEQ), lambda b: (b, 0, 0)),
        ],
        out_shape=[
            jax.ShapeDtypeStruct((batch * SEQ, LANE), F32),
            jax.ShapeDtypeStruct((batch, B_HEADS, SEQ), F32),
        ],
        compiler_params=_params(("arbitrary",), est),
        name="forget_cumsum",
    )(misc, bias_row)


def _mla_prep_kernel(cq_ref, ckv_ref, kr_ref, krs_ref, gq_ref, gkv_ref, wq1_ref, wq2_ref, wkv_ref,
                     cos_ref, sin_ref, *rest):
    if len(rest) == 5:
        w_next_ref, qc_ref, kc_ref, vc_ref, w_next_packed_ref = rest
        _pack_w_in_kernel(w_next_ref, w_next_packed_ref)
    else:
        qc_ref, kc_ref, vc_ref = rest
    cq = cq_ref[...].astype(F32)
    ms = jnp.sum(cq * cq, axis=-1, keepdims=True) * (1.0 / Q_LORA)
    cqn = (cq * lax.rsqrt(ms + EPS) * gq_ref[...]).astype(BF16)
    ckv = ckv_ref[...].astype(F32)
    ms2 = jnp.mean(ckv * ckv, axis=-1, keepdims=True)
    ckvn = (ckv * lax.rsqrt(ms2 + EPS) * gkv_ref[...]).astype(BF16)

    q1 = jnp.dot(cqn, wq1_ref[...], preferred_element_type=F32) * Q_SCALE_MLA
    q2 = jnp.dot(cqn, wq2_ref[...], preferred_element_type=F32) * Q_SCALE_MLA
    kv = jnp.dot(ckvn, wkv_ref[...], preferred_element_type=F32)
    cos = cos_ref[...]
    sin = sin_ref[...]
    k_rope = (kr_ref[...].astype(F32) * cos + krs_ref[...].astype(F32) * sin).astype(BF16)
    for h in range(C_HEADS):
        qw = NOPE_DIM + LANE
        qc_ref[:, h * qw:h * qw + NOPE_DIM] = q1[:, h * qw:h * qw + NOPE_DIM].astype(BF16)
        q_rope = q1[:, h * qw + NOPE_DIM:(h + 1) * qw] * cos + q2[:, h * LANE:(h + 1) * LANE] * sin
        qc_ref[:, h * qw + NOPE_DIM:(h + 1) * qw] = q_rope.astype(BF16)
        kw = NOPE_DIM + V_DIM
        kc_ref[:, h * qw:h * qw + NOPE_DIM] = kv[:, h * kw:h * kw + NOPE_DIM].astype(BF16)
        kc_ref[:, h * qw + NOPE_DIM:(h + 1) * qw] = k_rope
        vc_ref[:, h * V_DIM:(h + 1) * V_DIM] = kv[:, h * kw + NOPE_DIM:(h + 1) * kw].astype(BF16)


def _mla_prep(proj, gq, gkv, wq1, wq2, wkv, cos_t, sin_t, w_in=None, pack_layer=None, *, tm=512):
    m = proj.shape[0]
    nseq = SEQ // tm
    steps = m // tm
    qw = C_HEADS * (NOPE_DIM + LANE)
    est = 2 * (tm * 1024 * 2) + 2 * (CQ_PAD * qw * 2 + CQ_PAD * 512 * 2 + 128 * 1024 * 2) \
        + 2 * (2 * tm * qw * 2 + tm * 512 * 2) + 3 * tm * qw * 4
    const = lambda i: (0, 0)
    extra_in, extra_out, extra_shape, extra_args = [], [], [], []
    if pack_layer is not None:
        w_t = jnp.swapaxes(w_in, 1, 2)
        _, n, d = w_t.shape
        cols = 2 * d // steps
        extra_in = [pl.BlockSpec((None, n, cols), lambda i: (pack_layer, 0, i // 2))]
        extra_out = [pl.BlockSpec((N_PACK, cols), lambda i: (0, i // 2))]
        extra_shape = [jax.ShapeDtypeStruct((N_PACK, d), BF16)]
        extra_args = [w_t]
        est += 2 * cols * (n * 4 + N_PACK * 2)
    return pl.pallas_call(
        _mla_prep_kernel,
        grid=(steps,),
        in_specs=[
            pl.BlockSpec((tm, CQ_PAD), lambda i: (i, OFF_CQ // CQ_PAD)),
            pl.BlockSpec((tm, LANE), lambda i: (i, OFF_CKV // LANE)),
            pl.BlockSpec((tm, LANE), lambda i: (i, OFF_KR // LANE)),
            pl.BlockSpec((tm, LANE), lambda i: (i, OFF_KRS // LANE)),
            pl.BlockSpec((1, CQ_PAD), const),
            pl.BlockSpec((1, KV_LORA), const),
            pl.BlockSpec((CQ_PAD, qw), const),
            pl.BlockSpec((CQ_PAD, C_HEADS * LANE), const),
            pl.BlockSpec((KV_LORA, C_HEADS * (NOPE_DIM + V_DIM)), const),
            pl.BlockSpec((tm, LANE), lambda i: (i % nseq, 0)),
            pl.BlockSpec((tm, LANE), lambda i: (i % nseq, 0)),
        ] + extra_in,
        out_specs=[
            pl.BlockSpec((tm, qw), lambda i: (i, 0)),
            pl.BlockSpec((tm, qw), lambda i: (i, 0)),
            pl.BlockSpec((tm, C_HEADS * V_DIM), lambda i: (i, 0)),
        ] + extra_out,
        out_shape=[
            jax.ShapeDtypeStruct((m, qw), BF16),
            jax.ShapeDtypeStruct((m, qw), BF16),
            jax.ShapeDtypeStruct((m, C_HEADS * V_DIM), BF16),
        ] + extra_shape,
        compiler_params=_params(("arbitrary",), est),
        name="mla_prep",
    )(proj, proj, proj, proj, gq, gkv, wq1, wq2, wkv, cos_t, sin_t, *extra_args)


def _paired_loop(n, body, init):
    def pair(p, carry):
        return body(2 * p + 1, body(2 * p, carry))
    carry = lax.fori_loop(0, n // 2, pair, init)
    return lax.fori_loop(2 * (n // 2), n, body, carry)


def _nt_dot(a, b):
    return lax.dot_general(a, b, (((1,), (1,)), ((), ())), preferred_element_type=F32)


def _two_pass_attention(i, *, n_heads, tq, tk, dv, logits_fn, diag_mask, v_fn, store_fn,
                        s_ref, mx_ref, acc_ref):
    assert tk == 2 * tq
    n_wide = (i * tq) // tk
    has_narrow = (i * tq) % tk != 0
    narrow_ks = pl.multiple_of(n_wide * tk, tk)
    diag_ks = pl.multiple_of(i * tq, tq)

    mx_ref[...] = jnp.full(mx_ref.shape, NEG_INF, F32)

    def pass1(ks, width, mask=None):
        for h in range(n_heads):
            s = logits_fn(h, ks, width)
            if mask is not None:
                s = jnp.where(mask, s, NEG_INF)
            s_ref[h, :, pl.ds(ks, width)] = s
            mx = mx_ref[h]
            for c in range(width // LANE):
                mx = jnp.maximum(mx, s[:, c * LANE:(c + 1) * LANE])
            mx_ref[h] = mx

    def pass1_wide(kb, carry):
        pass1(pl.multiple_of(kb * tk, tk), tk)
        return carry

    _paired_loop(n_wide, pass1_wide, 0)

    @pl.when(has_narrow)
    def _():
        pass1(narrow_ks, tq)

    pass1(diag_ks, tq, diag_mask)

    for h in range(n_heads):
        mx_ref[h] = jnp.broadcast_to(jnp.max(mx_ref[h], axis=-1, keepdims=True), (tq, LANE))
    acc_ref[...] = jnp.zeros(acc_ref.shape, F32)

    def pass2(ks, width):
        ones = jnp.ones((width, LANE), BF16)
        for h in range(n_heads):
            m = mx_ref[h]
            p = jnp.concatenate(
                [jnp.exp2(s_ref[h, :, pl.ds(ks + c * LANE, LANE)] - m).astype(BF16)
                 for c in range(width // LANE)], axis=1)
            v1 = jnp.concatenate([v_fn(h, ks, width), ones], axis=1)
            acc_ref[h] += jnp.dot(p, v1, preferred_element_type=F32)

    def pass2_wide(kb, carry):
        pass2(pl.multiple_of(kb * tk, tk), tk)
        return carry

    _paired_loop(n_wide, pass2_wide, 0)

    @pl.when(has_narrow)
    def _():
        pass2(narrow_ks, tq)

    pass2(diag_ks, tq)

    for h in range(n_heads):
        acc = acc_ref[h]
        store_fn(h, acc[:, :dv] / acc[:, dv:])


def _tile_iotas(tq, tk):
    return (lax.broadcasted_iota(jnp.int32, (tq, tk), 0), lax.broadcasted_iota(jnp.int32, (tq, tk), 1))


def _fox_kernel(q_ref, k_ref, v_ref, ccol_ref, crow_ref, o_ref,
                s_ref, mx_ref, acc_ref, cq_ref, *, tq, tk):
    i = pl.program_id(1)
    head = lambda h: slice(h * HEAD_DIM, (h + 1) * HEAD_DIM)
    for h in range(B_HEADS):
        cq_ref[h] = jnp.broadcast_to(ccol_ref[:, MISC_FL + h:MISC_FL + h + 1] * LOG2E, (tq, LANE))

    def logits_fn(h, ks, width):
        s = _nt_dot(q_ref[:, head(h)], k_ref[pl.ds(ks, width), head(h)])
        cq = jnp.concatenate([cq_ref[h]] * (width // LANE), axis=1)
        return s + cq - crow_ref[h:h + 1, pl.ds(ks, width)] * LOG2E

    def store_fn(h, y):
        o_ref[:, head(h)] = y.astype(o_ref.dtype)

    rows, cols = _tile_iotas(tq, tq)
    _two_pass_attention(i, n_heads=B_HEADS, tq=tq, tk=tk, dv=HEAD_DIM, logits_fn=logits_fn,
                        diag_mask=cols <= rows,
                        v_fn=lambda h, ks, width: v_ref[pl.ds(ks, width), head(h)],
                        store_fn=store_fn, s_ref=s_ref, mx_ref=mx_ref, acc_ref=acc_ref)


def _fox_attention(proj, ccol, crow, batch, *, tq=256, tk=512):
    m = proj.shape[0]
    nq = SEQ // tq
    w = B_HEADS * HEAD_DIM
    state = B_HEADS * tq * LANE * 4
    est = (4 * tq * w * 2 + 4 * SEQ * w * 2 + 2 * tq * LANE * 4 + 2 * 8 * SEQ * 4
           + B_HEADS * tq * SEQ * 4 + 4 * state + 6 * tq * tk * 4)
    return pl.pallas_call(
        functools.partial(_fox_kernel, tq=tq, tk=tk),
        grid=(batch, nq),
        in_specs=[
            pl.BlockSpec((tq, w), lambda b, i: (b * nq + i, OFF_QB // w)),
            pl.BlockSpec((SEQ, w), lambda b, i: (b, OFF_KB // w)),
            pl.BlockSpec((SEQ, w), lambda b, i: (b, OFF_VB // w)),
            pl.BlockSpec((tq, LANE), lambda b, i: (b * nq + i, 0)),
            pl.BlockSpec((None, B_HEADS, SEQ), lambda b, i: (b, 0, 0)),
        ],
        out_specs=pl.BlockSpec((tq, w), lambda b, i: (b * nq + i, 0)),
        out_shape=jax.ShapeDtypeStruct((m, w), BF16),
        scratch_shapes=[
            pltpu.VMEM((B_HEADS, tq, SEQ), F32),
            pltpu.VMEM((B_HEADS, tq, LANE), F32),
            pltpu.VMEM((B_HEADS, tq, HEAD_DIM + LANE), F32),
            pltpu.VMEM((B_HEADS, tq, LANE), F32),
        ],
        compiler_params=_params(("arbitrary", "arbitrary"), est),
        name="fox_attention",
    )(proj, proj, proj, ccol, crow)


def _mla_kernel(q_ref, k_ref, v_ref, o_ref, s_ref, mx_ref, acc_ref, *, tq, tk):
    i = pl.program_id(1)
    qw = NOPE_DIM + LANE
    qhead = lambda h: slice(h * qw, (h + 1) * qw)
    vhead = lambda h: slice(h * V_DIM, (h + 1) * V_DIM)

    def logits_fn(h, ks, width):
        return _nt_dot(q_ref[:, qhead(h)], k_ref[pl.ds(ks, width), qhead(h)])

    def store_fn(h, y):
        o_ref[:, vhead(h)] = y.astype(o_ref.dtype)

    rows, cols = _tile_iotas(tq, tq)
    _two_pass_attention(i, n_heads=C_HEADS, tq=tq, tk=tk, dv=V_DIM, logits_fn=logits_fn,
                        diag_mask=cols // CHUNK <= rows // CHUNK,
                        v_fn=lambda h, ks, width: v_ref[pl.ds(ks, width), vhead(h)],
                        store_fn=store_fn, s_ref=s_ref, mx_ref=mx_ref, acc_ref=acc_ref)


def _mla_attention(qc, kc, vc, batch, *, tq=256, tk=512):
    m = qc.shape[0]
    nq = SEQ // tq
    qw = C_HEADS * (NOPE_DIM + LANE)
    vw = C_HEADS * V_DIM
    state = C_HEADS * tq * LANE * 4
    est = (2 * tq * qw * 2 + 2 * SEQ * qw * 2 + 2 * SEQ * vw * 2 + 2 * tq * vw * 2
           + C_HEADS * tq * SEQ * 4 + 3 * state + 6 * tq * tk * 4)
    return pl.pallas_call(
        functools.partial(_mla_kernel, tq=tq, tk=tk),
        grid=(batch, nq),
        in_specs=[
            pl.BlockSpec((tq, qw), lambda b, i: (b * nq + i, 0)),
            pl.BlockSpec((SEQ, qw), lambda b, i: (b, 0)),
            pl.BlockSpec((SEQ, vw), lambda b, i: (b, 0)),
        ],
        out_specs=pl.BlockSpec((tq, vw), lambda b, i: (b * nq + i, 0)),
        out_shape=jax.ShapeDtypeStruct((m, vw), BF16),
        scratch_shapes=[
            pltpu.VMEM((C_HEADS, tq, SEQ), F32),
            pltpu.VMEM((C_HEADS, tq, LANE), F32),
            pltpu.VMEM((C_HEADS, tq, V_DIM + LANE), F32),
        ],
        compiler_params=_params(("arbitrary", "arbitrary"), est),
        name="mla_attention",
    )(qc, kc, vc)


def _t5_tiles_kernel(bucket_ref, table_ref, o_ref):
    for d in range(3):
        bucket = bucket_ref[d]
        for h in range(A_HEADS):
            acc = jnp.zeros(bucket.shape, F32)
            for nb in range(NUM_BUCKETS):
                acc = jnp.where(bucket == nb, table_ref[nb, h] * LOG2E, acc)
            o_ref[d, h] = acc


def _t5_tiles(bucket_tiles, t5_bias, *, t=ATT_T):
    return pl.pallas_call(
        _t5_tiles_kernel,
        in_specs=[
            pl.BlockSpec(memory_space=pltpu.VMEM),
            pl.BlockSpec(memory_space=pltpu.SMEM),
        ],
        out_specs=pl.BlockSpec(memory_space=pltpu.VMEM),
        out_shape=jax.ShapeDtypeStruct((3, A_HEADS, t, t), F32),
        name="t5_tiles",
    )(bucket_tiles, t5_bias)


def _dsa_kernel(qi_ref, klo_ref, khi_ref, misc_ref, qa_ref, ka_ref, va_ref, bias_ref, o_ref,
                sc_ref, lg_ref, acc_ref, *, t):
    i = pl.program_id(1)
    groups = t // 8
    key = lax.broadcasted_iota(jnp.int32, (t, t), 0)
    qry = lax.broadcasted_iota(jnp.int32, (t, t), 1)
    admissible = (key // CHUNK) <= (qry // CHUNK)

    def fold(x, op):
        return op(x.reshape(groups, 8, t), axis=0)

    def over_keys(x8, op):
        return jnp.broadcast_to(op(x8, axis=0, keepdims=True), (8, t))

    w_t = misc_ref[...].T[MISC_WI:MISC_WI + IDX_HEADS, :] * (IDX_HEADS ** -0.5 * IDX_DIM ** -0.5)

    def block_scores(kb):
        ks = pl.multiple_of(kb * t, t)
        klo = klo_ref[pl.ds(ks, t), :]
        khi = khi_ref[pl.ds(ks, t), :]
        acc = jnp.zeros((t, t), F32)
        for pair in range(IDX_HEADS // 2):
            qp = qi_ref[:, pair * LANE:(pair + 1) * LANE]
            for sub, kk in enumerate((klo, khi)):
                h = 2 * pair + sub
                acc = acc + jnp.maximum(_nt_dot(kk, qp), 0.0) * w_t[h:h + 1, :]
        return ks, acc

    def score_body(kb, carry):
        mn, mx = carry
        ks, acc = block_scores(kb)
        sc_ref[pl.ds(ks, t), :] = acc
        return jnp.minimum(mn, fold(acc, jnp.min)), jnp.maximum(mx, fold(acc, jnp.max))

    mn, mx = _paired_loop(i, score_body,
                          (jnp.full((8, t), -SELECT_MIN, F32), jnp.full((8, t), SELECT_MIN, F32)))
    ks, acc = block_scores(i)
    sc_ref[pl.ds(ks, t), :] = jnp.where(admissible, acc, NEG_INF)
    mn = jnp.minimum(mn, fold(jnp.where(admissible, acc, -SELECT_MIN), jnp.min))
    mx = jnp.maximum(mx, fold(jnp.where(admissible, acc, SELECT_MIN), jnp.max))

    def count_ge(thr):
        def body(kb, cnt):
            ks = pl.multiple_of(kb * t, t)
            hit = jnp.where(sc_ref[pl.ds(ks, t), :].reshape(groups, 8, t) >= thr[None], 1.0, 0.0)
            return cnt + jnp.sum(hit, axis=0)
        return over_keys(_paired_loop(i + 1, body, jnp.zeros((8, t), F32)), jnp.sum)

    k_sel = float(TOPK_MAX)
    search = i * t >= TOPK_MAX
    lo0 = jnp.where(search, over_keys(mn, jnp.min), SELECT_MIN)
    hi0 = jnp.where(search, over_keys(mx, jnp.max), SELECT_MIN)

    def midpoint(lo, hi):
        return 0.5 * lo + 0.5 * hi

    def any_active(lo, hi):
        mid = midpoint(lo, hi)
        return jnp.max(jnp.where((mid > lo) & (mid < hi), 1.0, 0.0)) > 0.5

    def search_cond(state):
        return jnp.logical_and(state[0] < BISECT_MAX_ITERS, state[1])

    def search_body(state):
        it, _, lo, hi, c_lo = state
        for _ in range(SEARCH_STEPS_PER_CHECK):
            mid = midpoint(lo, hi)
            cnt = count_ge(mid)
            ge = cnt >= k_sel
            lo = jnp.where(ge, mid, lo)
            c_lo = jnp.where(ge, cnt, c_lo)
            hi = jnp.where(cnt > k_sel, hi, mid)
        return it + SEARCH_STEPS_PER_CHECK, any_active(lo, hi), lo, hi, c_lo

    n_adm = (((i * t + qry[:8]) // CHUNK + 1) * CHUNK).astype(F32)
    _, _, lo, hi, c_lo = lax.while_loop(search_cond, search_body,
                                        (jnp.int32(0), any_active(lo0, hi0), lo0, hi0, n_adm))
    c_hi = count_ge(hi)
    thr = jnp.where(c_hi >= k_sel, hi, lo)
    c_thr = jnp.where(c_hi >= k_sel, c_hi, c_lo)

    @pl.when(jnp.max(jnp.where(c_thr > k_sel, 1.0, 0.0)) > 0.5)
    def _():
        key_in_block = (lax.broadcasted_iota(jnp.int32, (groups, 8, t), 0) * 8
                        + lax.broadcasted_iota(jnp.int32, (groups, 8, t), 1))

        def count_where(pred):
            def body(kb, cnt):
                ks = pl.multiple_of(kb * t, t)
                blk = sc_ref[pl.ds(ks, t), :].reshape(groups, 8, t)
                return cnt + jnp.sum(jnp.where(pred(blk, ks), 1.0, 0.0), axis=0)
            return over_keys(lax.fori_loop(0, i + 1, body, jnp.zeros((8, t), F32)), jnp.sum)

        def tied_before(bound):
            return lambda blk, ks: (blk == thr[None]) & ((key_in_block + ks).astype(F32) < bound[None])

        keep = k_sel - count_where(lambda blk, ks: blk > thr[None])

        def cut_body(_, bounds):
            below, above = bounds
            mid = jnp.floor(0.5 * (below + above))
            enough = count_where(tied_before(mid)) >= keep
            return jnp.where(enough, below, mid), jnp.where(enough, mid, above)

        _, cut = lax.fori_loop(0, SEQ.bit_length(), cut_body,
                               (jnp.zeros((8, t), F32), jnp.full((8, t), float(SEQ), F32)))

        def drop_body(kb, carry):
            ks = pl.multiple_of(kb * t, t)
            blk = sc_ref[pl.ds(ks, t), :].reshape(groups, 8, t)
            drop = (blk == thr[None]) & ((key_in_block + ks).astype(F32) >= cut[None])
            sc_ref[pl.ds(ks, t), :] = jnp.where(drop, NEG_INF, blk).reshape(t, t)
            return carry

        lax.fori_loop(0, i + 1, drop_body, 0)

    head = lambda h: slice(h * HEAD_DIM, (h + 1) * HEAD_DIM)
    acc_ref[...] = jnp.zeros(acc_ref.shape, F32)

    def logits_pass(kb, mx):
        ks = pl.multiple_of(kb * t, t)
        dist = jnp.minimum(i - kb, 2)
        k_blk = ka_ref[pl.ds(ks, t), :]
        sel = sc_ref[pl.ds(ks, t), :] >= thr[0:1, :]
        out = []
        for h in range(A_HEADS):
            s = _nt_dot(k_blk, qa_ref[:, head(h)]) + bias_ref[dist, h]
            s = jnp.where(sel, s, NEG_INF)
            lg_ref[h, pl.ds(ks, t), :] = s
            out.append(jnp.maximum(mx[h], fold(s, jnp.max)))
        return tuple(out)

    mx = _paired_loop(i + 1, logits_pass,
                      tuple(jnp.full((8, t), NEG_INF, F32) for _ in range(A_HEADS)))
    m = [over_keys(mx[h], jnp.max)[0:1, :] for h in range(A_HEADS)]

    ones_rows = jnp.ones((DSA_ONES_ROWS, t), BF16)

    def value_pass(kb, carry):
        ks = pl.multiple_of(kb * t, t)
        v_t = va_ref[pl.ds(ks, t), :].astype(F32).T.astype(BF16)
        v1_t = jnp.concatenate([v_t, ones_rows], axis=0)
        for h in range(A_HEADS):
            p = jnp.exp2(lg_ref[h, pl.ds(ks, t), :] - m[h])
            acc_ref[h] += jnp.dot(v1_t, p.astype(BF16), preferred_element_type=F32)
        return carry

    _paired_loop(i + 1, value_pass, 0)
    for h in range(A_HEADS):
        acc = acc_ref[h]
        o_ref[:, head(h)] = (acc[:HEAD_DIM] / acc[HEAD_DIM:HEAD_DIM + 1]).T.astype(o_ref.dtype)


def _dsa_attention(proj, misc, bias_tiles, batch, *, t=ATT_T):
    m = proj.shape[0]
    nq = SEQ // t
    qiw = IDX_HEADS * IDX_DIM
    qaw = A_HEADS * HEAD_DIM
    est = (2 * t * qiw * 2 + 2 * 4 * SEQ * LANE * 2 + 2 * t * LANE * 4 + 2 * t * qaw * 2
           + 2 * 3 * A_HEADS * t * t * 4 + 2 * t * qaw * 2
           + (1 + A_HEADS) * t * SEQ * 4 + A_HEADS * HEAD_DIM * t * 4 + 12 * t * t * 4)
    kblock = lambda off: pl.BlockSpec((SEQ, LANE), lambda b, i: (b, off // LANE))
    return pl.pallas_call(
        functools.partial(_dsa_kernel, t=t),
        grid=(batch, nq),
        in_specs=[
            pl.BlockSpec((t, qiw), lambda b, i: (b * nq + i, OFF_QI // qiw)),
            kblock(OFF_KI_LO),
            kblock(OFF_KI_HI),
            pl.BlockSpec((t, LANE), lambda b, i: (b * nq + i, 0)),
            pl.BlockSpec((t, qaw), lambda b, i: (b * nq + i, OFF_QA // qaw)),
            kblock(OFF_KA),
            kblock(OFF_VA),
            pl.BlockSpec((3, A_HEADS, t, t), lambda b, i: (0, 0, 0, 0)),
        ],
        out_specs=pl.BlockSpec((t, qaw), lambda b, i: (b * nq + i, 0)),
        out_shape=jax.ShapeDtypeStruct((m, qaw), BF16),
        scratch_shapes=[
            pltpu.VMEM((SEQ, t), F32),
            pltpu.VMEM((A_HEADS, SEQ, t), F32),
            pltpu.VMEM((A_HEADS, HEAD_DIM + DSA_ONES_ROWS, t), F32),
        ],
        compiler_params=_params(("arbitrary", "arbitrary"), est),
        name="dsa_attention",
    )(proj, proj, proj, misc, proj, proj, proj, bias_tiles)


def _merge_kernel(x_ref, ya_ref, yb_ref, yc_ref, ga_ref, gb_ref, gc_ref,
                  wa_ref, wb_ref, wc_ref, wo_ref, gn_ref, o_ref, hn_ref):
    def branch(y_ref, w_ref, g_ref):
        y = jnp.dot(y_ref[...], w_ref[...], preferred_element_type=F32)
        return jax.nn.sigmoid(g_ref[...].astype(F32)) * y

    merged = branch(ya_ref, wa_ref, ga_ref) + branch(yb_ref, wb_ref, gb_ref) + branch(yc_ref, wc_ref, gc_ref)
    x_new = x_ref[...] + jnp.dot(merged.astype(BF16), wo_ref[...], preferred_element_type=F32)
    o_ref[...] = x_new
    ms = jnp.mean(x_new * x_new, axis=-1, keepdims=True)
    hn_ref[...] = (x_new * lax.rsqrt(ms + EPS) * gn_ref[...]).astype(hn_ref.dtype)


def _merge(x2, ya, yb, yc, proj, wa, wb, wc, wo, g_next, *, tm=256):
    m = x2.shape[0]
    wbytes = (wa.size + wb.size + wc.size + wo.size) * 2
    est = 2 * wbytes + 4 * tm * D_MODEL * 4 + 2 * tm * 2048 * 2 + 8 * tm * D_MODEL * 2 + 4 * tm * D_MODEL * 4
    const = lambda i: (0, 0)
    gate = lambda k: pl.BlockSpec((tm, D_MODEL), lambda i: (i, OFF_GL // D_MODEL + k))
    return pl.pallas_call(
        _merge_kernel,
        grid=(m // tm,),
        in_specs=[
            pl.BlockSpec((tm, D_MODEL), lambda i: (i, 0)),
            pl.BlockSpec((tm, ya.shape[1]), lambda i: (i, 0)),
            pl.BlockSpec((tm, yb.shape[1]), lambda i: (i, 0)),
            pl.BlockSpec((tm, yc.shape[1]), lambda i: (i, 0)),
            gate(0), gate(1), gate(2),
            pl.BlockSpec(wa.shape, const),
            pl.BlockSpec(wb.shape, const),
            pl.BlockSpec(wc.shape, const),
            pl.BlockSpec(wo.shape, const),
            pl.BlockSpec((1, D_MODEL), const),
        ],
        out_specs=[
            pl.BlockSpec((tm, D_MODEL), lambda i: (i, 0)),
            pl.BlockSpec((tm, D_MODEL), lambda i: (i, 0)),
        ],
        out_shape=[
            jax.ShapeDtypeStruct((m, D_MODEL), F32),
            jax.ShapeDtypeStruct((m, D_MODEL), BF16),
        ],
        compiler_params=_params(("arbitrary",), est),
        name="merge",
    )(x2, ya, yb, yc, proj, proj, proj, wa, wb, wc, wo, g_next)


def _ffn_kernel(x_ref, h_ref, wu_ref, wv_ref, cw_ref, wd_ref, fg_ref, o_ref,
                ubuf_ref, halo_ref, *, tm, tf, final_norm):
    i = pl.program_id(0)
    j = pl.program_id(1)
    nj = pl.num_programs(1)

    @pl.when(j == 0)
    def _():
        o_ref[...] = x_ref[...]

        @pl.when((i % (SEQ // tm)) == 0)
        def _():
            halo_ref[...] = jnp.zeros(halo_ref.shape, F32)

    h = h_ref[...]
    u = jnp.dot(h, wu_ref[...], preferred_element_type=F32)
    v = jnp.dot(h, wv_ref[...], preferred_element_type=F32)

    ubuf_ref[0:8, :] = halo_ref[j]
    ubuf_ref[8:, :] = u
    halo_ref[j] = u[tm - 8:, :]
    conv = (cw_ref[0:1, :] * ubuf_ref[6:6 + tm, :] + cw_ref[1:2, :] * ubuf_ref[7:7 + tm, :]
            + cw_ref[2:3, :] * u + cw_ref[3:4, :])
    act = (jax.nn.gelu(conv) * v).astype(BF16)
    o_ref[...] += jnp.dot(act, wd_ref[...], preferred_element_type=F32)

    if final_norm:
        @pl.when(j == nj - 1)
        def _():
            _rmsnorm_rows(o_ref, fg_ref, o_ref, tm)


def _ffn(x2, h2, w_up, cw, w_down, final_g, *, final_norm, tm=1024, tf=512):
    m = x2.shape[0]
    nj = D_FF // tf
    est = (3 * tm * D_MODEL * 4 + 2 * tm * D_MODEL * 2 + 2 * 2 * D_MODEL * tf * 2 + 2 * tf * D_MODEL * 2
           + (tm + 8) * tf * 4 + nj * 8 * tf * 4 + 4 * tm * tf * 4)
    return pl.pallas_call(
        functools.partial(_ffn_kernel, tm=tm, tf=tf, final_norm=final_norm),
        grid=(m // tm, nj),
        in_specs=[
            pl.BlockSpec((tm, D_MODEL), lambda i, j: (i, 0), pipeline_mode=pl.Buffered(1)),
            pl.BlockSpec((tm, D_MODEL), lambda i, j: (i, 0)),
            pl.BlockSpec((D_MODEL, tf), lambda i, j: (0, j)),
            pl.BlockSpec((D_MODEL, tf), lambda i, j: (0, nj + j)),
            pl.BlockSpec((8, tf), lambda i, j: (0, j)),
            pl.BlockSpec((tf, D_MODEL), lambda i, j: (j, 0)),
            pl.BlockSpec((1, D_MODEL), lambda i, j: (0, 0)),
        ],
        out_specs=pl.BlockSpec((tm, D_MODEL), lambda i, j: (i, 0)),
        out_shape=jax.ShapeDtypeStruct((m, D_MODEL), F32),
        scratch_shapes=[
            pltpu.VMEM((tm + 8, tf), F32),
            pltpu.VMEM((nj, 8, tf), F32),
        ],
        compiler_params=_params(("arbitrary", "arbitrary"), est),
        name="conv_ffn",
    )(x2, h2, w_up, w_up, cw, w_down, final_g)


def _t5_bucket(rel):
    nb = NUM_BUCKETS // 2
    max_exact = nb // 2
    base = jnp.where(rel > 0, nb, 0)
    n = jnp.abs(rel)
    nf = jnp.maximum(n, 1).astype(F32)
    large = max_exact + (jnp.log(nf / max_exact) / math.log(MAX_DISTANCE / max_exact)
                         * (nb - max_exact)).astype(jnp.int32)
    large = jnp.minimum(large, nb - 1)
    return base + jnp.where(n < max_exact, n, large)


def _bucket_tiles(t):
    key = jnp.arange(t, dtype=jnp.int32)[:, None]
    qry = jnp.arange(t, dtype=jnp.int32)[None, :]
    return jnp.stack([_t5_bucket(key - qry - d * t) for d in range(3)])


def _rope_tables():
    half = ROPE_DIM // 2
    inv = ROPE_THETA ** (-jnp.arange(half, dtype=F32) / half)
    ang = jnp.arange(SEQ, dtype=jnp.int32).astype(F32)[:, None] * inv[None, :]
    cos, sin = jnp.cos(ang), jnp.sin(ang)
    pad = jnp.zeros((SEQ, LANE - ROPE_DIM), F32)
    return jnp.concatenate([cos, cos, pad], axis=1), jnp.concatenate([-sin, sin, pad], axis=1)


def _swap_halves(w):
    half = w.shape[-1] // 2
    return jnp.concatenate([w[..., half:], w[..., :half]], axis=-1)


def _pack_w_in_kernel(w_ref, o_ref):
    off = np.cumsum((0,) + IN_SIZES)
    seg = lambda k: w_ref[off[k]:off[k + 1], :]
    qa, ka, va, qi, ki, wi, qb, kb, vb, fl, cq, ckv, kr, gl = [seg(k) for k in range(len(IN_SIZES))]
    qa, qb = qa * Q_SCALE, qb * Q_SCALE
    z = lambda n: jnp.zeros((n, w_ref.shape[1]), F32)
    half = ROPE_DIM // 2
    groups = [
        (OFF_QI, [qi]), (OFF_QB, [qb]), (OFF_KB, [kb]), (OFF_VB, [vb]), (OFF_GL, [gl]), (OFF_QA, [qa]),
        (OFF_CQ, [cq, z(CQ_PAD - Q_LORA)]), (OFF_KA, [ka]), (OFF_VA, [va]), (OFF_KI_LO, [ki, z(64)]),
        (OFF_CKV, [ckv]), (OFF_KR, [kr, z(64)]), (OFF_KRS, [kr[half:], kr[:half], z(64)]),
        (OFF_MISC, [wi, fl, z(LANE - IDX_HEADS - B_HEADS)]), (OFF_KI_HI, [z(64), ki]),
    ]
    for start, pieces in groups:
        block = pieces[0] if len(pieces) == 1 else jnp.concatenate(pieces, axis=0)
        o_ref[start:start + block.shape[0], :] = block.astype(o_ref.dtype)


def _pack_w_in(w_in, l, *, cols=256):
    w_t = jnp.swapaxes(w_in, 1, 2)
    _, n, d = w_t.shape
    return pl.pallas_call(
        _pack_w_in_kernel,
        grid=(d // cols,),
        in_specs=[pl.BlockSpec((None, n, cols), lambda i: (l, 0, i))],
        out_specs=pl.BlockSpec((N_PACK, cols), lambda i: (0, i)),
        out_shape=jax.ShapeDtypeStruct((N_PACK, d), BF16),
        compiler_params=_params(("arbitrary",), 2 * cols * (n * 4 + N_PACK * 2)),
        name="pack_w_in",
    )(w_t)


def _pack_w_uq(w):
    w = jnp.pad(w, ((0, CQ_PAD - Q_LORA), (0, 0))).reshape(CQ_PAD, C_HEADS, NOPE_DIM + ROPE_DIM)
    z = jnp.zeros((CQ_PAD, C_HEADS, LANE - ROPE_DIM), w.dtype)
    rope = w[..., NOPE_DIM:]
    wq1 = jnp.concatenate([w, z], axis=-1).reshape(CQ_PAD, -1)
    wq2 = jnp.concatenate([_swap_halves(rope), z], axis=-1).reshape(CQ_PAD, -1)
    return wq1.astype(BF16), wq2.astype(BF16)


def kernel(x, norm_mix_g, w_in, b_forget, g_cq, g_ckv, w_uq, w_ukv, w_branch_a, w_branch_b, w_branch_c,
           w_o, norm_ffn_g, w_up, conv_w, conv_b, w_down, t5_bias, final_g):
    batch, seq, d = x.shape
    assert (seq, d) == (SEQ, D_MODEL)
    x2 = x.reshape(batch * seq, d)

    bias_tiles = _t5_tiles(_bucket_tiles(ATT_T), t5_bias)
    cos_t, sin_t = _rope_tables()
    final_row = final_g.reshape(1, D_MODEL)

    w_in_packed = _pack_w_in(w_in, 0)
    for l in range(DEPTH):
        casts = [(w_up, True, 11), (w_down, False, 11), (w_o, True, 8),
                 (w_branch_a, True, 8), (w_branch_b, True, 8), (w_branch_c, True, 8)]
        proj, misc, w_up_l, w_down_l, w_o_l, w_a_l, w_b_l, w_c_l = _inproj(
            x2, norm_mix_g[l].reshape(1, D_MODEL), w_in_packed, casts, l)

        fbias = jnp.zeros((1, LANE), F32).at[0, MISC_FL:MISC_FL + B_HEADS].set(b_forget[l])
        ccol, crow = _forget_cumsum(misc, fbias, batch)

        wq1, wq2 = _pack_w_uq(w_uq[l])
        gq = jnp.pad(g_cq[l], (0, CQ_PAD - Q_LORA)).reshape(1, CQ_PAD)
        prep = _mla_prep(proj, gq, g_ckv[l].reshape(1, KV_LORA), wq1, wq2, w_ukv[l].astype(BF16),
                         cos_t, sin_t, w_in, l + 1 if l + 1 < DEPTH else None)
        qc, kc, vc = prep[:3]
        if l + 1 < DEPTH:
            w_in_packed = prep[3]

        ya = _dsa_attention(proj, misc, bias_tiles, batch)
        yb = _fox_attention(proj, ccol, crow, batch)
        yc = _mla_attention(qc, kc, vc, batch)

        x2, h2 = _merge(x2, ya, yb, yc, proj, w_a_l, w_b_l, w_c_l, w_o_l,
                        norm_ffn_g[l].reshape(1, D_MODEL))

        cw = jnp.concatenate([conv_w[l], conv_b[l][None, :], jnp.zeros((4, D_FF), F32)], axis=0)
        x2 = _ffn(x2, h2, w_up_l, cw, w_down_l, final_row, final_norm=(l == DEPTH - 1))

    return x2.reshape(batch, seq, d)
```

```python
import functools
import math

import jax
import jax.numpy as jnp
import numpy as np
from jax import lax
from jax.experimental import pallas as pl
from jax.experimental.pallas import tpu as pltpu

F32 = jnp.float32
BF16 = jnp.bfloat16

D_MODEL = 2048
SEQ = 2048
DEPTH = 2
CHUNK = 64
HEAD_DIM = 128
EPS = 1e-6
NEG_INF = -1e30

A_HEADS = 4
IDX_HEADS = 16
IDX_DIM = 64
TOPK_MAX = 256
NUM_BUCKETS = 32
MAX_DISTANCE = 128
B_HEADS = 8
C_HEADS = 4
Q_LORA = 448
KV_LORA = 128
NOPE_DIM = 128
ROPE_DIM = 64
V_DIM = 128
ROPE_THETA = 10000.0
D_FF = 5632

IN_SIZES = (
    A_HEADS * HEAD_DIM, HEAD_DIM, HEAD_DIM,
    IDX_HEADS * IDX_DIM, IDX_DIM, IDX_HEADS,
    B_HEADS * HEAD_DIM, B_HEADS * HEAD_DIM, B_HEADS * HEAD_DIM, B_HEADS,
    Q_LORA, KV_LORA, ROPE_DIM,
    3 * D_MODEL,
)

LOG2E = math.log2(math.e)
Q_SCALE = HEAD_DIM ** -0.5 * LOG2E
Q_SCALE_MLA = (NOPE_DIM + ROPE_DIM) ** -0.5 * LOG2E
LANE = 128
V7X_VMEM_BYTES = 64 * 1024 * 1024

OFF_QI = 0
OFF_QB = 1024
OFF_KB = 2048
OFF_VB = 3072
OFF_GL = 4096
OFF_QA = 10240
OFF_CQ = 10752
OFF_KA = 11264
OFF_VA = 11392
OFF_KI_LO = 11520
OFF_CKV = 11648
OFF_KR = 11776
OFF_KRS = 11904
OFF_MISC = 12032
OFF_KI_HI = 12160
N_PACK = 12288
CQ_PAD = 512
MISC_WI = 0
MISC_FL = IDX_HEADS

ATT_T = 256
SELECT_MIN = -1e29
BISECT_MAX_ITERS = 512
SEARCH_STEPS_PER_CHECK = 4
DSA_ONES_ROWS = 16


def _vmem_limit(estimate_bytes):
    limit = V7X_VMEM_BYTES - (6 << 20)
    assert estimate_bytes <= limit, estimate_bytes
    return limit


def _params(semantics, vmem_estimate):
    return pltpu.CompilerParams(dimension_semantics=semantics,
                                vmem_limit_bytes=_vmem_limit(vmem_estimate))


def _rmsnorm_rows(x_ref, g_ref, out_ref, rows):
    def body(c, carry):
        r = pl.multiple_of(c * 128, 128)
        x = x_ref[pl.ds(r, 128), :]
        ms = jnp.mean(x * x, axis=-1, keepdims=True)
        out_ref[pl.ds(r, 128), :] = (x * lax.rsqrt(ms + EPS) * g_ref[...]).astype(out_ref.dtype)
        return carry
    lax.fori_loop(0, rows // 128, body, 0)


def _inproj_kernel(*refs, tm, tn, n_cast):
    x_ref, g_ref, w_ref = refs[:3]
    cast_in = refs[3:3 + n_cast]
    o_ref, misc_ref = refs[3 + n_cast:5 + n_cast]
    cast_out = refs[5 + n_cast:5 + 2 * n_cast]
    h_ref = refs[5 + 2 * n_cast]
    j = pl.program_id(1)

    @pl.when(j == 0)
    def _():
        _rmsnorm_rows(x_ref, g_ref, h_ref, tm)

    acc = _nt_dot(h_ref[...], w_ref[...])
    o_ref[...] = acc.astype(o_ref.dtype)

    for src, dst in zip(cast_in, cast_out):
        dst[...] = src[...].astype(dst.dtype)

    @pl.when(j == OFF_MISC // tn)
    def _():
        lo = OFF_MISC % tn
        misc_ref[...] = acc[:, lo:lo + LANE]


def _cast_specs(w_stack, l, ni, nj, rows_follow_i, n_chunks):
    _, r, c = w_stack.shape
    assert n_chunks <= nj
    chunk = lambda j: jnp.minimum(j, n_chunks - 1)
    if rows_follow_i:
        block = (r // ni, c // n_chunks)
        idx = lambda i, j: (i, chunk(j))
    else:
        block = (r // n_chunks, c // ni)
        idx = lambda i, j: (chunk(j), i)
    assert block[0] % 16 == 0 and block[1] % LANE == 0, block
    in_spec = pl.BlockSpec((None,) + block, lambda i, j: (l,) + idx(i, j))
    out_spec = pl.BlockSpec(block, idx)
    return in_spec, out_spec, jax.ShapeDtypeStruct((r, c), BF16), block[0] * block[1]


def _inproj(x2, g, w_pack, casts, l, *, tm=1024, tn=1024):
    m = x2.shape[0]
    ni, nj = m // tm, N_PACK // tn
    plans = [_cast_specs(w, l, ni, nj, rows_i, n) for w, rows_i, n in casts]
    est = (2 * tm * D_MODEL * 4 + tm * D_MODEL * 2 + 2 * D_MODEL * tn * 2 + 2 * tm * tn * 2 + tm * tn * 4
           + sum(2 * 6 * p[3] for p in plans))
    return pl.pallas_call(
        functools.partial(_inproj_kernel, tm=tm, tn=tn, n_cast=len(casts)),
        grid=(ni, nj),
        in_specs=[
            pl.BlockSpec((tm, D_MODEL), lambda i, j: (i, 0)),
            pl.BlockSpec((1, D_MODEL), lambda i, j: (0, 0)),
            pl.BlockSpec((tn, D_MODEL), lambda i, j: (j, 0)),
        ] + [p[0] for p in plans],
        out_specs=[
            pl.BlockSpec((tm, tn), lambda i, j: (i, j)),
            pl.BlockSpec((tm, LANE), lambda i, j: (i, 0)),
        ] + [p[1] for p in plans],
        out_shape=[
            jax.ShapeDtypeStruct((m, N_PACK), BF16),
            jax.ShapeDtypeStruct((m, LANE), F32),
        ] + [p[2] for p in plans],
        scratch_shapes=[pltpu.VMEM((tm, D_MODEL), BF16)],
        compiler_params=_params(("arbitrary", "arbitrary"), est),
        name="inproj",
    )(x2, g, w_pack, *[w for w, _, _ in casts])


def _forget_cumsum_kernel(misc_ref, bias_ref, ccol_ref, crow_ref, *, blk):
    rows = lax.broadcasted_iota(jnp.int32, (blk, blk), 0)
    cols = lax.broadcasted_iota(jnp.int32, (blk, blk), 1)
    tri = jnp.where(rows >= cols, 1.0, 0.0).astype(BF16)
    carry = jnp.zeros((1, LANE), F32)
    for c in range(SEQ // blk):
        z = misc_ref[c * blk:(c + 1) * blk, :] + bias_ref[...]
        lf = jnp.minimum(z, 0.0) - jnp.log1p(jnp.exp(-jnp.abs(z)))
        p0 = lf.astype(BF16)
        r1 = lf - p0.astype(F32)
        p1 = r1.astype(BF16)
        p2 = (r1 - p1.astype(F32)).astype(BF16)
        cs = (jnp.dot(tri, p0, preferred_element_type=F32)
              + jnp.dot(tri, p1, preferred_element_type=F32)
              + jnp.dot(tri, p2, preferred_element_type=F32)) + carry
        ccol_ref[c * blk:(c + 1) * blk, :] = cs
        crow_ref[:, c * blk:(c + 1) * blk] = cs.T[MISC_FL:MISC_FL + B_HEADS, :]
        carry = cs[blk - 1:blk, :]


def _forget_cumsum(misc, bias_row, batch, *, blk=256):
    est = 4 * SEQ * LANE * 4 + 2 * 8 * SEQ * 4
    return pl.pallas_call(
        functools.partial(_forget_cumsum_kernel, blk=blk),
        grid=(batch,),
        in_specs=[
            pl.BlockSpec((SEQ, LANE), lambda b: (b, 0)),
            pl.BlockSpec((1, LANE), lambda b: (0, 0)),
        ],
        out_specs=[
            pl.BlockSpec((SEQ, LANE), lambda b: (b, 0)),
            pl.BlockSpec((None, B_HEADS, SEQ), lambda b: (b, 0, 0)),
        ],
        out_shape=[
            jax.ShapeDtypeStruct((batch * SEQ, LANE), F32),
            jax.ShapeDtypeStruct((batch, B_HEADS, SEQ), F32),
        ],
        compiler_params=_params(("arbitrary",), est),
        name="forget_cumsum",
    )(misc, bias_row)


def _mla_prep_kernel(cq_ref, ckv_ref, kr_ref, krs_ref, gq_ref, gkv_ref, wq1_ref, wq2_ref, wkv_ref,
                     cos_ref, sin_ref, qc_ref, kc_ref, vc_ref):
    cq = cq_ref[...].astype(F32)
    ms = jnp.sum(cq * cq, axis=-1, keepdims=True) * (1.0 / Q_LORA)
    cqn = (cq * lax.rsqrt(ms + EPS) * gq_ref[...]).astype(BF16)
    ckv = ckv_ref[...].astype(F32)
    ms2 = jnp.mean(ckv * ckv, axis=-1, keepdims=True)
    ckvn = (ckv * lax.rsqrt(ms2 + EPS) * gkv_ref[...]).astype(BF16)

    q1 = jnp.dot(cqn, wq1_ref[...], preferred_element_type=F32) * Q_SCALE_MLA
    q2 = jnp.dot(cqn, wq2_ref[...], preferred_element_type=F32) * Q_SCALE_MLA
    kv = jnp.dot(ckvn, wkv_ref[...], preferred_element_type=F32)
    cos = cos_ref[...]
    sin = sin_ref[...]
    k_rope = (kr_ref[...].astype(F32) * cos + krs_ref[...].astype(F32) * sin).astype(BF16)
    for h in range(C_HEADS):
        qw = NOPE_DIM + LANE
        qc_ref[:, h * qw:h * qw + NOPE_DIM] = q1[:, h * qw:h * qw + NOPE_DIM].astype(BF16)
        q_rope = q1[:, h * qw + NOPE_DIM:(h + 1) * qw] * cos + q2[:, h * LANE:(h + 1) * LANE] * sin
        qc_ref[:, h * qw + NOPE_DIM:(h + 1) * qw] = q_rope.astype(BF16)
        kw = NOPE_DIM + V_DIM
        kc_ref[:, h * qw:h * qw + NOPE_DIM] = kv[:, h * kw:h * kw + NOPE_DIM].astype(BF16)
        kc_ref[:, h * qw + NOPE_DIM:(h + 1) * qw] = k_rope
        vc_ref[:, h * V_DIM:(h + 1) * V_DIM] = kv[:, h * kw + NOPE_DIM:(h + 1) * kw].astype(BF16)


def _mla_prep(proj, gq, gkv, wq1, wq2, wkv, cos_t, sin_t, *, tm=512):
    m = proj.shape[0]
    nseq = SEQ // tm
    qw = C_HEADS * (NOPE_DIM + LANE)
    est = 2 * (tm * 1024 * 2) + 2 * (CQ_PAD * qw * 2 + CQ_PAD * 512 * 2 + 128 * 1024 * 2) \
        + 2 * (2 * tm * qw * 2 + tm * 512 * 2) + 3 * tm * qw * 4
    const = lambda i: (0, 0)
    return pl.pallas_call(
        _mla_prep_kernel,
        grid=(m // tm,),
        in_specs=[
            pl.BlockSpec((tm, CQ_PAD), lambda i: (i, OFF_CQ // CQ_PAD)),
            pl.BlockSpec((tm, LANE), lambda i: (i, OFF_CKV // LANE)),
            pl.BlockSpec((tm, LANE), lambda i: (i, OFF_KR // LANE)),
            pl.BlockSpec((tm, LANE), lambda i: (i, OFF_KRS // LANE)),
            pl.BlockSpec((1, CQ_PAD), const),
            pl.BlockSpec((1, KV_LORA), const),
            pl.BlockSpec((CQ_PAD, qw), const),
            pl.BlockSpec((CQ_PAD, C_HEADS * LANE), const),
            pl.BlockSpec((KV_LORA, C_HEADS * (NOPE_DIM + V_DIM)), const),
            pl.BlockSpec((tm, LANE), lambda i: (i % nseq, 0)),
            pl.BlockSpec((tm, LANE), lambda i: (i % nseq, 0)),
        ],
        out_specs=[
            pl.BlockSpec((tm, qw), lambda i: (i, 0)),
            pl.BlockSpec((tm, qw), lambda i: (i, 0)),
            pl.BlockSpec((tm, C_HEADS * V_DIM), lambda i: (i, 0)),
        ],
        out_shape=[
            jax.ShapeDtypeStruct((m, qw), BF16),
            jax.ShapeDtypeStruct((m, qw), BF16),
            jax.ShapeDtypeStruct((m, C_HEADS * V_DIM), BF16),
        ],
        compiler_params=_params(("arbitrary",), est),
        name="mla_prep",
    )(proj, proj, proj, proj, gq, gkv, wq1, wq2, wkv, cos_t, sin_t)


def _paired_loop(n, body, init):
    def pair(p, carry):
        return body(2 * p + 1, body(2 * p, carry))
    carry = lax.fori_loop(0, n // 2, pair, init)
    return lax.fori_loop(2 * (n // 2), n, body, carry)


def _nt_dot(a, b):
    return lax.dot_general(a, b, (((1,), (1,)), ((), ())), preferred_element_type=F32)


def _two_pass_attention(i, *, n_heads, tq, tk, dv, logits_fn, diag_mask, v_fn, store_fn,
                        s_ref, mx_ref, acc_ref):
    assert tk == 2 * tq
    n_wide = (i * tq) // tk
    has_narrow = (i * tq) % tk != 0
    narrow_ks = pl.multiple_of(n_wide * tk, tk)
    diag_ks = pl.multiple_of(i * tq, tq)

    mx_ref[...] = jnp.full(mx_ref.shape, NEG_INF, F32)

    def pass1(ks, width, mask=None):
        for h in range(n_heads):
            s = logits_fn(h, ks, width)
            if mask is not None:
                s = jnp.where(mask, s, NEG_INF)
            s_ref[h, :, pl.ds(ks, width)] = s
            mx = mx_ref[h]
            for c in range(width // LANE):
                mx = jnp.maximum(mx, s[:, c * LANE:(c + 1) * LANE])
            mx_ref[h] = mx

    def pass1_wide(kb, carry):
        pass1(pl.multiple_of(kb * tk, tk), tk)
        return carry

    _paired_loop(n_wide, pass1_wide, 0)

    @pl.when(has_narrow)
    def _():
        pass1(narrow_ks, tq)

    pass1(diag_ks, tq, diag_mask)

    for h in range(n_heads):
        mx_ref[h] = jnp.broadcast_to(jnp.max(mx_ref[h], axis=-1, keepdims=True), (tq, LANE))
    acc_ref[...] = jnp.zeros(acc_ref.shape, F32)

    def pass2(ks, width):
        ones = jnp.ones((width, LANE), BF16)
        for h in range(n_heads):
            m = mx_ref[h]
            p = jnp.concatenate(
                [jnp.exp2(s_ref[h, :, pl.ds(ks + c * LANE, LANE)] - m).astype(BF16)
                 for c in range(width // LANE)], axis=1)
            v1 = jnp.concatenate([v_fn(h, ks, width), ones], axis=1)
            acc_ref[h] += jnp.dot(p, v1, preferred_element_type=F32)

    def pass2_wide(kb, carry):
        pass2(pl.multiple_of(kb * tk, tk), tk)
        return carry

    _paired_loop(n_wide, pass2_wide, 0)

    @pl.when(has_narrow)
    def _():
        pass2(narrow_ks, tq)

    pass2(diag_ks, tq)

    for h in range(n_heads):
        acc = acc_ref[h]
        store_fn(h, acc[:, :dv] / acc[:, dv:])


def _tile_iotas(tq, tk):
    return (lax.broadcasted_iota(jnp.int32, (tq, tk), 0), lax.broadcasted_iota(jnp.int32, (tq, tk), 1))


def _fox_kernel(q_ref, k_ref, v_ref, ccol_ref, crow_ref, o_ref,
                s_ref, mx_ref, acc_ref, cq_ref, *, tq, tk):
    i = pl.program_id(1)
    head = lambda h: slice(h * HEAD_DIM, (h + 1) * HEAD_DIM)
    for h in range(B_HEADS):
        cq_ref[h] = jnp.broadcast_to(ccol_ref[:, MISC_FL + h:MISC_FL + h + 1] * LOG2E, (tq, LANE))

    def logits_fn(h, ks, width):
        s = _nt_dot(q_ref[:, head(h)], k_ref[pl.ds(ks, width), head(h)])
        cq = jnp.concatenate([cq_ref[h]] * (width // LANE), axis=1)
        return s + cq - crow_ref[h:h + 1, pl.ds(ks, width)] * LOG2E

    def store_fn(h, y):
        o_ref[:, head(h)] = y.astype(o_ref.dtype)

    rows, cols = _tile_iotas(tq, tq)
    _two_pass_attention(i, n_heads=B_HEADS, tq=tq, tk=tk, dv=HEAD_DIM, logits_fn=logits_fn,
                        diag_mask=cols <= rows,
                        v_fn=lambda h, ks, width: v_ref[pl.ds(ks, width), head(h)],
                        store_fn=store_fn, s_ref=s_ref, mx_ref=mx_ref, acc_ref=acc_ref)


def _fox_attention(proj, ccol, crow, batch, *, tq=256, tk=512):
    m = proj.shape[0]
    nq = SEQ // tq
    w = B_HEADS * HEAD_DIM
    state = B_HEADS * tq * LANE * 4
    est = (4 * tq * w * 2 + 4 * SEQ * w * 2 + 2 * tq * LANE * 4 + 2 * 8 * SEQ * 4
           + B_HEADS * tq * SEQ * 4 + 4 * state + 6 * tq * tk * 4)
    return pl.pallas_call(
        functools.partial(_fox_kernel, tq=tq, tk=tk),
        grid=(batch, nq),
        in_specs=[
            pl.BlockSpec((tq, w), lambda b, i: (b * nq + i, OFF_QB // w)),
            pl.BlockSpec((SEQ, w), lambda b, i: (b, OFF_KB // w)),
            pl.BlockSpec((SEQ, w), lambda b, i: (b, OFF_VB // w)),
            pl.BlockSpec((tq, LANE), lambda b, i: (b * nq + i, 0)),
            pl.BlockSpec((None, B_HEADS, SEQ), lambda b, i: (b, 0, 0)),
        ],
        out_specs=pl.BlockSpec((tq, w), lambda b, i: (b * nq + i, 0)),
        out_shape=jax.ShapeDtypeStruct((m, w), BF16),
        scratch_shapes=[
            pltpu.VMEM((B_HEADS, tq, SEQ), F32),
            pltpu.VMEM((B_HEADS, tq, LANE), F32),
            pltpu.VMEM((B_HEADS, tq, HEAD_DIM + LANE), F32),
            pltpu.VMEM((B_HEADS, tq, LANE), F32),
        ],
        compiler_params=_params(("arbitrary", "arbitrary"), est),
        name="fox_attention",
    )(proj, proj, proj, ccol, crow)


def _mla_kernel(q_ref, k_ref, v_ref, o_ref, s_ref, mx_ref, acc_ref, *, tq, tk):
    i = pl.program_id(1)
    qw = NOPE_DIM + LANE
    qhead = lambda h: slice(h * qw, (h + 1) * qw)
    vhead = lambda h: slice(h * V_DIM, (h + 1) * V_DIM)

    def logits_fn(h, ks, width):
        return _nt_dot(q_ref[:, qhead(h)], k_ref[pl.ds(ks, width), qhead(h)])

    def store_fn(h, y):
        o_ref[:, vhead(h)] = y.astype(o_ref.dtype)

    rows, cols = _tile_iotas(tq, tq)
    _two_pass_attention(i, n_heads=C_HEADS, tq=tq, tk=tk, dv=V_DIM, logits_fn=logits_fn,
                        diag_mask=cols // CHUNK <= rows // CHUNK,
                        v_fn=lambda h, ks, width: v_ref[pl.ds(ks, width), vhead(h)],
                        store_fn=store_fn, s_ref=s_ref, mx_ref=mx_ref, acc_ref=acc_ref)


def _mla_attention(qc, kc, vc, batch, *, tq=256, tk=512):
    m = qc.shape[0]
    nq = SEQ // tq
    qw = C_HEADS * (NOPE_DIM + LANE)
    vw = C_HEADS * V_DIM
    state = C_HEADS * tq * LANE * 4
    est = (2 * tq * qw * 2 + 2 * SEQ * qw * 2 + 2 * SEQ * vw * 2 + 2 * tq * vw * 2
           + C_HEADS * tq * SEQ * 4 + 3 * state + 6 * tq * tk * 4)
    return pl.pallas_call(
        functools.partial(_mla_kernel, tq=tq, tk=tk),
        grid=(batch, nq),
        in_specs=[
            pl.BlockSpec((tq, qw), lambda b, i: (b * nq + i, 0)),
            pl.BlockSpec((SEQ, qw), lambda b, i: (b, 0)),
            pl.BlockSpec((SEQ, vw), lambda b, i: (b, 0)),
        ],
        out_specs=pl.BlockSpec((tq, vw), lambda b, i: (b * nq + i, 0)),
        out_shape=jax.ShapeDtypeStruct((m, vw), BF16),
        scratch_shapes=[
            pltpu.VMEM((C_HEADS, tq, SEQ), F32),
            pltpu.VMEM((C_HEADS, tq, LANE), F32),
            pltpu.VMEM((C_HEADS, tq, V_DIM + LANE), F32),
        ],
        compiler_params=_params(("arbitrary", "arbitrary"), est),
        name="mla_attention",
    )(qc, kc, vc)


def _t5_tiles_kernel(bucket_ref, table_ref, o_ref):
    for d in range(3):
        bucket = bucket_ref[d]
        for h in range(A_HEADS):
            acc = jnp.zeros(bucket.shape, F32)
            for nb in range(NUM_BUCKETS):
                acc = jnp.where(bucket == nb, table_ref[nb, h] * LOG2E, acc)
            o_ref[d, h] = acc


def _t5_tiles(bucket_tiles, t5_bias, *, t=ATT_T):
    return pl.pallas_call(
        _t5_tiles_kernel,
        in_specs=[
            pl.BlockSpec(memory_space=pltpu.VMEM),
            pl.BlockSpec(memory_space=pltpu.SMEM),
        ],
        out_specs=pl.BlockSpec(memory_space=pltpu.VMEM),
        out_shape=jax.ShapeDtypeStruct((3, A_HEADS, t, t), F32),
        name="t5_tiles",
    )(bucket_tiles, t5_bias)


def _dsa_kernel(qi_ref, klo_ref, khi_ref, misc_ref, qa_ref, ka_ref, va_ref, bias_ref, *rest, t):
    i = pl.program_id(1)
    if len(rest) == 6:
        w_next_ref, o_ref, w_next_packed_ref, sc_ref, lg_ref, acc_ref = rest

        @pl.when(i % 2 == 0)
        def _():
            _pack_w_in_kernel(w_next_ref, w_next_packed_ref)
    else:
        o_ref, sc_ref, lg_ref, acc_ref = rest
    groups = t // 8
    key = lax.broadcasted_iota(jnp.int32, (t, t), 0)
    qry = lax.broadcasted_iota(jnp.int32, (t, t), 1)
    admissible = (key // CHUNK) <= (qry // CHUNK)

    def fold(x, op):
        return op(x.reshape(groups, 8, t), axis=0)

    def over_keys(x8, op):
        return jnp.broadcast_to(op(x8, axis=0, keepdims=True), (8, t))

    w_t = misc_ref[...].T[MISC_WI:MISC_WI + IDX_HEADS, :] * (IDX_HEADS ** -0.5 * IDX_DIM ** -0.5)

    def block_scores(kb):
        ks = pl.multiple_of(kb * t, t)
        klo = klo_ref[pl.ds(ks, t), :]
        khi = khi_ref[pl.ds(ks, t), :]
        acc = jnp.zeros((t, t), F32)
        for pair in range(IDX_HEADS // 2):
            qp = qi_ref[:, pair * LANE:(pair + 1) * LANE]
            for sub, kk in enumerate((klo, khi)):
                h = 2 * pair + sub
                acc = acc + jnp.maximum(_nt_dot(kk, qp), 0.0) * w_t[h:h + 1, :]
        return ks, acc

    def score_body(kb, carry):
        mn, mx = carry
        ks, acc = block_scores(kb)
        sc_ref[pl.ds(ks, t), :] = acc
        return jnp.minimum(mn, fold(acc, jnp.min)), jnp.maximum(mx, fold(acc, jnp.max))

    mn, mx = _paired_loop(i, score_body,
                          (jnp.full((8, t), -SELECT_MIN, F32), jnp.full((8, t), SELECT_MIN, F32)))
    ks, acc = block_scores(i)
    sc_ref[pl.ds(ks, t), :] = jnp.where(admissible, acc, NEG_INF)
    mn = jnp.minimum(mn, fold(jnp.where(admissible, acc, -SELECT_MIN), jnp.min))
    mx = jnp.maximum(mx, fold(jnp.where(admissible, acc, SELECT_MIN), jnp.max))

    def count_ge(thr):
        def body(kb, cnt):
            ks = pl.multiple_of(kb * t, t)
            hit = jnp.where(sc_ref[pl.ds(ks, t), :].reshape(groups, 8, t) >= thr[None], 1.0, 0.0)
            return cnt + jnp.sum(hit, axis=0)
        return over_keys(_paired_loop(i + 1, body, jnp.zeros((8, t), F32)), jnp.sum)

    k_sel = float(TOPK_MAX)
    search = i * t >= TOPK_MAX
    lo0 = jnp.where(search, over_keys(mn, jnp.min), SELECT_MIN)
    hi0 = jnp.where(search, over_keys(mx, jnp.max), SELECT_MIN)

    def midpoint(lo, hi):
        return 0.5 * lo + 0.5 * hi

    def any_active(lo, hi):
        mid = midpoint(lo, hi)
        return jnp.max(jnp.where((mid > lo) & (mid < hi), 1.0, 0.0)) > 0.5

    def search_cond(state):
        return jnp.logical_and(state[0] < BISECT_MAX_ITERS, state[1])

    def search_body(state):
        it, _, lo, hi, c_lo = state
        for _ in range(SEARCH_STEPS_PER_CHECK):
            mid = midpoint(lo, hi)
            cnt = count_ge(mid)
            ge = cnt >= k_sel
            lo = jnp.where(ge, mid, lo)
            c_lo = jnp.where(ge, cnt, c_lo)
            hi = jnp.where(cnt > k_sel, hi, mid)
        return it + SEARCH_STEPS_PER_CHECK, any_active(lo, hi), lo, hi, c_lo

    n_adm = (((i * t + qry[:8]) // CHUNK + 1) * CHUNK).astype(F32)
    _, _, lo, hi, c_lo = lax.while_loop(search_cond, search_body,
                                        (jnp.int32(0), any_active(lo0, hi0), lo0, hi0, n_adm))
    c_hi = count_ge(hi)
    thr = jnp.where(c_hi >= k_sel, hi, lo)
    c_thr = jnp.where(c_hi >= k_sel, c_hi, c_lo)

    @pl.when(jnp.max(jnp.where(c_thr > k_sel, 1.0, 0.0)) > 0.5)
    def _():
        key_in_block = (lax.broadcasted_iota(jnp.int32, (groups, 8, t), 0) * 8
                        + lax.broadcasted_iota(jnp.int32, (groups, 8, t), 1))

        def count_where(pred):
            def body(kb, cnt):
                ks = pl.multiple_of(kb * t, t)
                blk = sc_ref[pl.ds(ks, t), :].reshape(groups, 8, t)
                return cnt + jnp.sum(jnp.where(pred(blk, ks), 1.0, 0.0), axis=0)
            return over_keys(lax.fori_loop(0, i + 1, body, jnp.zeros((8, t), F32)), jnp.sum)

        def tied_before(bound):
            return lambda blk, ks: (blk == thr[None]) & ((key_in_block + ks).astype(F32) < bound[None])

        keep = k_sel - count_where(lambda blk, ks: blk > thr[None])

        def cut_body(_, bounds):
            below, above = bounds
            mid = jnp.floor(0.5 * (below + above))
            enough = count_where(tied_before(mid)) >= keep
            return jnp.where(enough, below, mid), jnp.where(enough, mid, above)

        _, cut = lax.fori_loop(0, SEQ.bit_length(), cut_body,
                               (jnp.zeros((8, t), F32), jnp.full((8, t), float(SEQ), F32)))

        def drop_body(kb, carry):
            ks = pl.multiple_of(kb * t, t)
            blk = sc_ref[pl.ds(ks, t), :].reshape(groups, 8, t)
            drop = (blk == thr[None]) & ((key_in_block + ks).astype(F32) >= cut[None])
            sc_ref[pl.ds(ks, t), :] = jnp.where(drop, NEG_INF, blk).reshape(t, t)
            return carry

        lax.fori_loop(0, i + 1, drop_body, 0)

    head = lambda h: slice(h * HEAD_DIM, (h + 1) * HEAD_DIM)
    acc_ref[...] = jnp.zeros(acc_ref.shape, F32)

    def logits_pass(kb, mx):
        ks = pl.multiple_of(kb * t, t)
        dist = jnp.minimum(i - kb, 2)
        k_blk = ka_ref[pl.ds(ks, t), :]
        sel = sc_ref[pl.ds(ks, t), :] >= thr[0:1, :]
        out = []
        for h in range(A_HEADS):
            s = _nt_dot(k_blk, qa_ref[:, head(h)]) + bias_ref[dist, h]
            s = jnp.where(sel, s, NEG_INF)
            lg_ref[h, pl.ds(ks, t), :] = s
            out.append(jnp.maximum(mx[h], fold(s, jnp.max)))
        return tuple(out)

    mx = _paired_loop(i + 1, logits_pass,
                      tuple(jnp.full((8, t), NEG_INF, F32) for _ in range(A_HEADS)))
    m = [over_keys(mx[h], jnp.max)[0:1, :] for h in range(A_HEADS)]

    ones_rows = jnp.ones((DSA_ONES_ROWS, t), BF16)

    def value_pass(kb, carry):
        ks = pl.multiple_of(kb * t, t)
        v_t = va_ref[pl.ds(ks, t), :].astype(F32).T.astype(BF16)
        v1_t = jnp.concatenate([v_t, ones_rows], axis=0)
        for h in range(A_HEADS):
            p = jnp.exp2(lg_ref[h, pl.ds(ks, t), :] - m[h])
            acc_ref[h] += jnp.dot(v1_t, p.astype(BF16), preferred_element_type=F32)
        return carry

    _paired_loop(i + 1, value_pass, 0)
    for h in range(A_HEADS):
        acc = acc_ref[h]
        o_ref[:, head(h)] = (acc[:HEAD_DIM] / acc[HEAD_DIM:HEAD_DIM + 1]).T.astype(o_ref.dtype)


def _dsa_attention(proj, misc, bias_tiles, batch, w_in=None, pack_layer=None, *, t=ATT_T):
    m = proj.shape[0]
    nq = SEQ // t
    qiw = IDX_HEADS * IDX_DIM
    qaw = A_HEADS * HEAD_DIM
    est = (2 * t * qiw * 2 + 2 * 4 * SEQ * LANE * 2 + 2 * t * LANE * 4 + 2 * t * qaw * 2
           + 2 * 3 * A_HEADS * t * t * 4 + 2 * t * qaw * 2
           + (1 + A_HEADS) * t * SEQ * 4 + A_HEADS * HEAD_DIM * t * 4 + 12 * t * t * 4)
    kblock = lambda off: pl.BlockSpec((SEQ, LANE), lambda b, i: (b, off // LANE))
    extra_in, extra_out, extra_shape, extra_args = [], [], [], []
    if pack_layer is not None:
        w_t = jnp.swapaxes(w_in, 1, 2)
        _, n, d = w_t.shape
        cols = 2 * d // (batch * nq)
        col_block = lambda b, i: (b * nq + i) // 2
        extra_in = [pl.BlockSpec((None, n, cols), lambda b, i: (pack_layer, 0, col_block(b, i)))]
        extra_out = [pl.BlockSpec((N_PACK, cols), lambda b, i: (0, col_block(b, i)))]
        extra_shape = [jax.ShapeDtypeStruct((N_PACK, d), BF16)]
        extra_args = [w_t]
        est += 2 * cols * (n * 4 + N_PACK * 2)
    return pl.pallas_call(
        functools.partial(_dsa_kernel, t=t),
        grid=(batch, nq),
        in_specs=[
            pl.BlockSpec((t, qiw), lambda b, i: (b * nq + i, OFF_QI // qiw)),
            kblock(OFF_KI_LO),
            kblock(OFF_KI_HI),
            pl.BlockSpec((t, LANE), lambda b, i: (b * nq + i, 0)),
            pl.BlockSpec((t, qaw), lambda b, i: (b * nq + i, OFF_QA // qaw)),
            kblock(OFF_KA),
            kblock(OFF_VA),
            pl.BlockSpec((3, A_HEADS, t, t), lambda b, i: (0, 0, 0, 0)),
        ] + extra_in,
        out_specs=[pl.BlockSpec((t, qaw), lambda b, i: (b * nq + i, 0))] + extra_out,
        out_shape=[jax.ShapeDtypeStruct((m, qaw), BF16)] + extra_shape,
        scratch_shapes=[
            pltpu.VMEM((SEQ, t), F32),
            pltpu.VMEM((A_HEADS, SEQ, t), F32),
            pltpu.VMEM((A_HEADS, HEAD_DIM + DSA_ONES_ROWS, t), F32),
        ],
        compiler_params=_params(("arbitrary", "arbitrary"), est),
        name="dsa_attention",
    )(proj, proj, proj, misc, proj, proj, proj, bias_tiles, *extra_args)


def _merge_kernel(x_ref, ya_ref, yb_ref, yc_ref, ga_ref, gb_ref, gc_ref,
                  wa_ref, wb_ref, wc_ref, wo_ref, gn_ref, o_ref, hn_ref):
    def branch(y_ref, w_ref, g_ref):
        y = jnp.dot(y_ref[...], w_ref[...], preferred_element_type=F32)
        return jax.nn.sigmoid(g_ref[...].astype(F32)) * y

    merged = branch(ya_ref, wa_ref, ga_ref) + branch(yb_ref, wb_ref, gb_ref) + branch(yc_ref, wc_ref, gc_ref)
    x_new = x_ref[...] + jnp.dot(merged.astype(BF16), wo_ref[...], preferred_element_type=F32)
    o_ref[...] = x_new
    ms = jnp.mean(x_new * x_new, axis=-1, keepdims=True)
    hn_ref[...] = (x_new * lax.rsqrt(ms + EPS) * gn_ref[...]).astype(hn_ref.dtype)


def _merge(x2, ya, yb, yc, proj, wa, wb, wc, wo, g_next, *, tm=256):
    m = x2.shape[0]
    wbytes = (wa.size + wb.size + wc.size + wo.size) * 2
    est = 2 * wbytes + 4 * tm * D_MODEL * 4 + 2 * tm * 2048 * 2 + 8 * tm * D_MODEL * 2 + 4 * tm * D_MODEL * 4
    const = lambda i: (0, 0)
    gate = lambda k: pl.BlockSpec((tm, D_MODEL), lambda i: (i, OFF_GL // D_MODEL + k))
    return pl.pallas_call(
        _merge_kernel,
        grid=(m // tm,),
        in_specs=[
            pl.BlockSpec((tm, D_MODEL), lambda i: (i, 0)),
            pl.BlockSpec((tm, ya.shape[1]), lambda i: (i, 0)),
            pl.BlockSpec((tm, yb.shape[1]), lambda i: (i, 0)),
            pl.BlockSpec((tm, yc.shape[1]), lambda i: (i, 0)),
            gate(0), gate(1), gate(2),
            pl.BlockSpec(wa.shape, const),
            pl.BlockSpec(wb.shape, const),
            pl.BlockSpec(wc.shape, const),
            pl.BlockSpec(wo.shape, const),
            pl.BlockSpec((1, D_MODEL), const),
        ],
        out_specs=[
            pl.BlockSpec((tm, D_MODEL), lambda i: (i, 0)),
            pl.BlockSpec((tm, D_MODEL), lambda i: (i, 0)),
        ],
        out_shape=[
            jax.ShapeDtypeStruct((m, D_MODEL), F32),
            jax.ShapeDtypeStruct((m, D_MODEL), BF16),
        ],
        compiler_params=_params(("arbitrary",), est),
        name="merge",
    )(x2, ya, yb, yc, proj, proj, proj, wa, wb, wc, wo, g_next)


def _ffn_kernel(x_ref, h_ref, wu_ref, wv_ref, cw_ref, wd_ref, fg_ref, o_ref,
                ubuf_ref, halo_ref, *, tm, tf, final_norm):
    i = pl.program_id(0)
    j = pl.program_id(1)
    nj = pl.num_programs(1)

    @pl.when(j == 0)
    def _():
        o_ref[...] = x_ref[...]

        @pl.when((i % (SEQ // tm)) == 0)
        def _():
            halo_ref[...] = jnp.zeros(halo_ref.shape, F32)

    h = h_ref[...]
    u = jnp.dot(h, wu_ref[...], preferred_element_type=F32)
    v = jnp.dot(h, wv_ref[...], preferred_element_type=F32)

    ubuf_ref[0:8, :] = halo_ref[j]
    ubuf_ref[8:, :] = u
    halo_ref[j] = u[tm - 8:, :]
    conv = (cw_ref[0:1, :] * ubuf_ref[6:6 + tm, :] + cw_ref[1:2, :] * ubuf_ref[7:7 + tm, :]
            + cw_ref[2:3, :] * u + cw_ref[3:4, :])
    act = (jax.nn.gelu(conv) * v).astype(BF16)
    o_ref[...] += jnp.dot(act, wd_ref[...], preferred_element_type=F32)

    if final_norm:
        @pl.when(j == nj - 1)
        def _():
            _rmsnorm_rows(o_ref, fg_ref, o_ref, tm)


def _ffn(x2, h2, w_up, cw, w_down, final_g, *, final_norm, tm=1024, tf=512):
    m = x2.shape[0]
    nj = D_FF // tf
    est = (3 * tm * D_MODEL * 4 + 2 * tm * D_MODEL * 2 + 2 * 2 * D_MODEL * tf * 2 + 2 * tf * D_MODEL * 2
           + (tm + 8) * tf * 4 + nj * 8 * tf * 4 + 4 * tm * tf * 4)
    return pl.pallas_call(
        functools.partial(_ffn_kernel, tm=tm, tf=tf, final_norm=final_norm),
        grid=(m // tm, nj),
        in_specs=[
            pl.BlockSpec((tm, D_MODEL), lambda i, j: (i, 0), pipeline_mode=pl.Buffered(1)),
            pl.BlockSpec((tm, D_MODEL), lambda i, j: (i, 0)),
            pl.BlockSpec((D_MODEL, tf), lambda i, j: (0, j)),
            pl.BlockSpec((D_MODEL, tf), lambda i, j: (0, nj + j)),
            pl.BlockSpec((8, tf), lambda i, j: (0, j)),
            pl.BlockSpec((tf, D_MODEL), lambda i, j: (j, 0)),
            pl.BlockSpec((1, D_MODEL), lambda i, j: (0, 0)),
        ],
        out_specs=pl.BlockSpec((tm, D_MODEL), lambda i, j: (i, 0)),
        out_shape=jax.ShapeDtypeStruct((m, D_MODEL), F32),
        scratch_shapes=[
            pltpu.VMEM((tm + 8, tf), F32),
            pltpu.VMEM((nj, 8, tf), F32),
        ],
        compiler_params=_params(("arbitrary", "arbitrary"), est),
        name="conv_ffn",
    )(x2, h2, w_up, w_up, cw, w_down, final_g)


def _t5_bucket(rel):
    nb = NUM_BUCKETS // 2
    max_exact = nb // 2
    base = jnp.where(rel > 0, nb, 0)
    n = jnp.abs(rel)
    nf = jnp.maximum(n, 1).astype(F32)
    large = max_exact + (jnp.log(nf / max_exact) / math.log(MAX_DISTANCE / max_exact)
                         * (nb - max_exact)).astype(jnp.int32)
    large = jnp.minimum(large, nb - 1)
    return base + jnp.where(n < max_exact, n, large)


def _bucket_tiles(t):
    key = jnp.arange(t, dtype=jnp.int32)[:, None]
    qry = jnp.arange(t, dtype=jnp.int32)[None, :]
    return jnp.stack([_t5_bucket(key - qry - d * t) for d in range(3)])


def _rope_tables():
    half = ROPE_DIM // 2
    inv = ROPE_THETA ** (-jnp.arange(half, dtype=F32) / half)
    ang = jnp.arange(SEQ, dtype=jnp.int32).astype(F32)[:, None] * inv[None, :]
    cos, sin = jnp.cos(ang), jnp.sin(ang)
    pad = jnp.zeros((SEQ, LANE - ROPE_DIM), F32)
    return jnp.concatenate([cos, cos, pad], axis=1), jnp.concatenate([-sin, sin, pad], axis=1)


def _swap_halves(w):
    half = w.shape[-1] // 2
    return jnp.concatenate([w[..., half:], w[..., :half]], axis=-1)


def _pack_w_in_kernel(w_ref, o_ref):
    off = np.cumsum((0,) + IN_SIZES)
    seg = lambda k: w_ref[off[k]:off[k + 1], :]
    qa, ka, va, qi, ki, wi, qb, kb, vb, fl, cq, ckv, kr, gl = [seg(k) for k in range(len(IN_SIZES))]
    qa, qb = qa * Q_SCALE, qb * Q_SCALE
    z = lambda n: jnp.zeros((n, w_ref.shape[1]), F32)
    half = ROPE_DIM // 2
    groups = [
        (OFF_QI, [qi]), (OFF_QB, [qb]), (OFF_KB, [kb]), (OFF_VB, [vb]), (OFF_GL, [gl]), (OFF_QA, [qa]),
        (OFF_CQ, [cq, z(CQ_PAD - Q_LORA)]), (OFF_KA, [ka]), (OFF_VA, [va]), (OFF_KI_LO, [ki, z(64)]),
        (OFF_CKV, [ckv]), (OFF_KR, [kr, z(64)]), (OFF_KRS, [kr[half:], kr[:half], z(64)]),
        (OFF_MISC, [wi, fl, z(LANE - IDX_HEADS - B_HEADS)]), (OFF_KI_HI, [z(64), ki]),
    ]
    for start, pieces in groups:
        block = pieces[0] if len(pieces) == 1 else jnp.concatenate(pieces, axis=0)
        o_ref[start:start + block.shape[0], :] = block.astype(o_ref.dtype)


def _pack_w_in(w_in, l, *, cols=256):
    w_t = jnp.swapaxes(w_in, 1, 2)
    _, n, d = w_t.shape
    return pl.pallas_call(
        _pack_w_in_kernel,
        grid=(d // cols,),
        in_specs=[pl.BlockSpec((None, n, cols), lambda i: (l, 0, i))],
        out_specs=pl.BlockSpec((N_PACK, cols), lambda i: (0, i)),
        out_shape=jax.ShapeDtypeStruct((N_PACK, d), BF16),
        compiler_params=_params(("arbitrary",), 2 * cols * (n * 4 + N_PACK * 2)),
        name="pack_w_in",
    )(w_t)


def _pack_w_uq(w):
    w = jnp.pad(w, ((0, CQ_PAD - Q_LORA), (0, 0))).reshape(CQ_PAD, C_HEADS, NOPE_DIM + ROPE_DIM)
    z = jnp.zeros((CQ_PAD, C_HEADS, LANE - ROPE_DIM), w.dtype)
    rope = w[..., NOPE_DIM:]
    wq1 = jnp.concatenate([w, z], axis=-1).reshape(CQ_PAD, -1)
    wq2 = jnp.concatenate([_swap_halves(rope), z], axis=-1).reshape(CQ_PAD, -1)
    return wq1.astype(BF16), wq2.astype(BF16)


def kernel(x, norm_mix_g, w_in, b_forget, g_cq, g_ckv, w_uq, w_ukv, w_branch_a, w_branch_b, w_branch_c,
           w_o, norm_ffn_g, w_up, conv_w, conv_b, w_down, t5_bias, final_g):
    batch, seq, d = x.shape
    assert (seq, d) == (SEQ, D_MODEL)
    x2 = x.reshape(batch * seq, d)

    bias_tiles = _t5_tiles(_bucket_tiles(ATT_T), t5_bias)
    cos_t, sin_t = _rope_tables()
    final_row = final_g.reshape(1, D_MODEL)

    w_in_packed = _pack_w_in(w_in, 0)
    for l in range(DEPTH):
        casts = [(w_up, True, 11), (w_down, False, 11), (w_o, True, 8),
                 (w_branch_a, True, 8), (w_branch_b, True, 8), (w_branch_c, True, 8)]
        proj, misc, w_up_l, w_down_l, w_o_l, w_a_l, w_b_l, w_c_l = _inproj(
            x2, norm_mix_g[l].reshape(1, D_MODEL), w_in_packed, casts, l)

        fbias = jnp.zeros((1, LANE), F32).at[0, MISC_FL:MISC_FL + B_HEADS].set(b_forget[l])
        ccol, crow = _forget_cumsum(misc, fbias, batch)

        wq1, wq2 = _pack_w_uq(w_uq[l])
        gq = jnp.pad(g_cq[l], (0, CQ_PAD - Q_LORA)).reshape(1, CQ_PAD)
        qc, kc, vc = _mla_prep(proj, gq, g_ckv[l].reshape(1, KV_LORA), wq1, wq2,
                               w_ukv[l].astype(BF16), cos_t, sin_t)

        dsa_out = _dsa_attention(proj, misc, bias_tiles, batch, w_in, l + 1 if l + 1 < DEPTH else None)
        ya = dsa_out[0]
        if l + 1 < DEPTH:
            w_in_packed = dsa_out[1]
        yb = _fox_attention(proj, ccol, crow, batch)
        yc = _mla_attention(qc, kc, vc, batch)

        x2, h2 = _merge(x2, ya, yb, yc, proj, w_a_l, w_b_l, w_c_l, w_o_l,
                        norm_ffn_g[l].reshape(1, D_MODEL))

        cw = jnp.concatenate([conv_w[l], conv_b[l][None, :], jnp.zeros((4, D_FF), F32)], axis=0)
        x2 = _ffn(x2, h2, w_up_l, cw, w_down_l, final_row, final_norm=(l == DEPTH - 1))

    return x2.reshape(batch, seq, d)
```

```python
import functools
import math

import jax
import jax.numpy as jnp
import numpy as np
from jax import lax
from jax.experimental import pallas as pl
from jax.experimental.pallas import tpu as pltpu

F32 = jnp.float32
BF16 = jnp.bfloat16

D_MODEL = 2048
SEQ = 2048
DEPTH = 2
CHUNK = 64
HEAD_DIM = 128
EPS = 1e-6
NEG_INF = -1e30

A_HEADS = 4
IDX_HEADS = 16
IDX_DIM = 64
TOPK_MAX = 256
NUM_BUCKETS = 32
MAX_DISTANCE = 128
B_HEADS = 8
C_HEADS = 4
Q_LORA = 448
KV_LORA = 128
NOPE_DIM = 128
ROPE_DIM = 64
V_DIM = 128
ROPE_THETA = 10000.0
D_FF = 5632

IN_SIZES = (
    A_HEADS * HEAD_DIM, HEAD_DIM, HEAD_DIM,
    IDX_HEADS * IDX_DIM, IDX_DIM, IDX_HEADS,
    B_HEADS * HEAD_DIM, B_HEADS * HEAD_DIM, B_HEADS * HEAD_DIM, B_HEADS,
    Q_LORA, KV_LORA, ROPE_DIM,
    3 * D_MODEL,
)

LOG2E = math.log2(math.e)
Q_SCALE = HEAD_DIM ** -0.5 * LOG2E
Q_SCALE_MLA = (NOPE_DIM + ROPE_DIM) ** -0.5 * LOG2E
LANE = 128
V7X_VMEM_BYTES = 64 * 1024 * 1024

OFF_QI = 0
OFF_QB = 1024
OFF_KB = 2048
OFF_VB = 3072
OFF_GL = 4096
OFF_QA = 10240
OFF_CQ = 10752
OFF_KA = 11264
OFF_VA = 11392
OFF_KI_LO = 11520
OFF_CKV = 11648
OFF_KR = 11776
OFF_KRS = 11904
OFF_MISC = 12032
OFF_KI_HI = 12160
N_PACK = 12288
CQ_PAD = 512
MISC_WI = 0
MISC_FL = IDX_HEADS

ATT_T = 256
SELECT_MIN = -1e29
BISECT_MAX_ITERS = 512
SEARCH_STEPS_PER_CHECK = 4
DSA_ONES_ROWS = 16


def _vmem_limit(estimate_bytes):
    limit = V7X_VMEM_BYTES - (6 << 20)
    assert estimate_bytes <= limit, estimate_bytes
    return limit


def _params(semantics, vmem_estimate):
    return pltpu.CompilerParams(dimension_semantics=semantics,
                                vmem_limit_bytes=_vmem_limit(vmem_estimate))


def _rmsnorm_rows(x_ref, g_ref, out_ref, rows):
    def body(c, carry):
        r = pl.multiple_of(c * 128, 128)
        x = x_ref[pl.ds(r, 128), :]
        ms = jnp.mean(x * x, axis=-1, keepdims=True)
        out_ref[pl.ds(r, 128), :] = (x * lax.rsqrt(ms + EPS) * g_ref[...]).astype(out_ref.dtype)
        return carry
    lax.fori_loop(0, rows // 128, body, 0)


def _inproj_kernel(*refs, tm, tn, n_cast):
    x_ref, g_ref, w_ref = refs[:3]
    cast_in = refs[3:3 + n_cast]
    o_ref, misc_ref = refs[3 + n_cast:5 + n_cast]
    cast_out = refs[5 + n_cast:5 + 2 * n_cast]
    h_ref = refs[5 + 2 * n_cast]
    j = pl.program_id(1)

    @pl.when(j == 0)
    def _():
        _rmsnorm_rows(x_ref, g_ref, h_ref, tm)

    acc = _nt_dot(h_ref[...], w_ref[...])
    o_ref[...] = acc.astype(o_ref.dtype)

    for src, dst in zip(cast_in, cast_out):
        dst[...] = src[...].astype(dst.dtype)

    @pl.when(j == OFF_MISC // tn)
    def _():
        lo = OFF_MISC % tn
        misc_ref[...] = acc[:, lo:lo + LANE]


def _cast_specs(w_stack, l, ni, nj, rows_follow_i, n_chunks):
    _, r, c = w_stack.shape
    assert n_chunks <= nj
    chunk = lambda j: jnp.minimum(j, n_chunks - 1)
    if rows_follow_i:
        block = (r // ni, c // n_chunks)
        idx = lambda i, j: (i, chunk(j))
    else:
        block = (r // n_chunks, c // ni)
        idx = lambda i, j: (chunk(j), i)
    assert block[0] % 16 == 0 and block[1] % LANE == 0, block
    in_spec = pl.BlockSpec((None,) + block, lambda i, j: (l,) + idx(i, j))
    out_spec = pl.BlockSpec(block, idx)
    return in_spec, out_spec, jax.ShapeDtypeStruct((r, c), BF16), block[0] * block[1]


def _inproj(x2, g, w_pack, casts, l, *, tm=1024, tn=1024):
    m = x2.shape[0]
    ni, nj = m // tm, N_PACK // tn
    plans = [_cast_specs(w, l, ni, nj, rows_i, n) for w, rows_i, n in casts]
    est = (2 * tm * D_MODEL * 4 + tm * D_MODEL * 2 + 2 * D_MODEL * tn * 2 + 2 * tm * tn * 2 + tm * tn * 4
           + sum(2 * 6 * p[3] for p in plans))
    return pl.pallas_call(
        functools.partial(_inproj_kernel, tm=tm, tn=tn, n_cast=len(casts)),
        grid=(ni, nj),
        in_specs=[
            pl.BlockSpec((tm, D_MODEL), lambda i, j: (i, 0)),
            pl.BlockSpec((1, D_MODEL), lambda i, j: (0, 0)),
            pl.BlockSpec((tn, D_MODEL), lambda i, j: (j, 0)),
        ] + [p[0] for p in plans],
        out_specs=[
            pl.BlockSpec((tm, tn), lambda i, j: (i, j)),
            pl.BlockSpec((tm, LANE), lambda i, j: (i, 0)),
        ] + [p[1] for p in plans],
        out_shape=[
            jax.ShapeDtypeStruct((m, N_PACK), BF16),
            jax.ShapeDtypeStruct((m, LANE), F32),
        ] + [p[2] for p in plans],
        scratch_shapes=[pltpu.VMEM((tm, D_MODEL), BF16)],
        compiler_params=_params(("arbitrary", "arbitrary"), est),
        name="inproj",
    )(x2, g, w_pack, *[w for w, _, _ in casts])


def _forget_cumsum_kernel(misc_ref, bias_ref, ccol_ref, crow_ref, *, blk):
    rows = lax.broadcasted_iota(jnp.int32, (blk, blk), 0)
    cols = lax.broadcasted_iota(jnp.int32, (blk, blk), 1)
    tri = jnp.where(rows >= cols, 1.0, 0.0).astype(BF16)
    carry = jnp.zeros((1, LANE), F32)
    for c in range(SEQ // blk):
        z = misc_ref[c * blk:(c + 1) * blk, :] + bias_ref[...]
        lf = jnp.minimum(z, 0.0) - jnp.log1p(jnp.exp(-jnp.abs(z)))
        p0 = lf.astype(BF16)
        r1 = lf - p0.astype(F32)
        p1 = r1.astype(BF16)
        p2 = (r1 - p1.astype(F32)).astype(BF16)
        cs = (jnp.dot(tri, p0, preferred_element_type=F32)
              + jnp.dot(tri, p1, preferred_element_type=F32)
              + jnp.dot(tri, p2, preferred_element_type=F32)) + carry
        ccol_ref[c * blk:(c + 1) * blk, :] = cs
        crow_ref[:, c * blk:(c + 1) * blk] = cs.T[MISC_FL:MISC_FL + B_HEADS, :]
        carry = cs[blk - 1:blk, :]


def _forget_cumsum(misc, bias_row, batch, *, blk=256):
    est = 4 * SEQ * LANE * 4 + 2 * 8 * SEQ * 4
    return pl.pallas_call(
        functools.partial(_forget_cumsum_kernel, blk=blk),
        grid=(batch,),
        in_specs=[
            pl.BlockSpec((SEQ, LANE), lambda b: (b, 0)),
            pl.BlockSpec((1, LANE), lambda b: (0, 0)),
        ],
        out_specs=[
            pl.BlockSpec((SEQ, LANE), lambda b: (b, 0)),
            pl.BlockSpec((None, B_HEADS, SEQ), lambda b: (b, 0, 0)),
        ],
        out_shape=[
            jax.ShapeDtypeStruct((batch * SEQ, LANE), F32),
            jax.ShapeDtypeStruct((batch, B_HEADS, SEQ), F32),
        ],
        compiler_params=_params(("arbitrary",), est),
        name="forget_cumsum",
    )(misc, bias_row)


def _mla_prep_kernel(cq_ref, ckv_ref, kr_ref, krs_ref, gq_ref, gkv_ref, wq1_ref, wq2_ref, wkv_ref,
                     cos_ref, sin_ref, qc_ref, kc_ref, vc_ref):
    cq = cq_ref[...].astype(F32)
    ms = jnp.sum(cq * cq, axis=-1, keepdims=True) * (1.0 / Q_LORA)
    cqn = (cq * lax.rsqrt(ms + EPS) * gq_ref[...]).astype(BF16)
    ckv = ckv_ref[...].astype(F32)
    ms2 = jnp.mean(ckv * ckv, axis=-1, keepdims=True)
    ckvn = (ckv * lax.rsqrt(ms2 + EPS) * gkv_ref[...]).astype(BF16)

    q1 = jnp.dot(cqn, wq1_ref[...], preferred_element_type=F32) * Q_SCALE_MLA
    q2 = jnp.dot(cqn, wq2_ref[...], preferred_element_type=F32) * Q_SCALE_MLA
    kv = jnp.dot(ckvn, wkv_ref[...], preferred_element_type=F32)
    cos = cos_ref[...]
    sin = sin_ref[...]
    k_rope = (kr_ref[...].astype(F32) * cos + krs_ref[...].astype(F32) * sin).astype(BF16)
    for h in range(C_HEADS):
        qw = NOPE_DIM + LANE
        qc_ref[:, h * qw:h * qw + NOPE_DIM] = q1[:, h * qw:h * qw + NOPE_DIM].astype(BF16)
        q_rope = q1[:, h * qw + NOPE_DIM:(h + 1) * qw] * cos + q2[:, h * LANE:(h + 1) * LANE] * sin
        qc_ref[:, h * qw + NOPE_DIM:(h + 1) * qw] = q_rope.astype(BF16)
        kw = NOPE_DIM + V_DIM
        kc_ref[:, h * qw:h * qw + NOPE_DIM] = kv[:, h * kw:h * kw + NOPE_DIM].astype(BF16)
        kc_ref[:, h * qw + NOPE_DIM:(h + 1) * qw] = k_rope
        vc_ref[:, h * V_DIM:(h + 1) * V_DIM] = kv[:, h * kw + NOPE_DIM:(h + 1) * kw].astype(BF16)


def _mla_prep(proj, gq, gkv, wq1, wq2, wkv, cos_t, sin_t, *, tm=512):
    m = proj.shape[0]
    nseq = SEQ // tm
    qw = C_HEADS * (NOPE_DIM + LANE)
    est = 2 * (tm * 1024 * 2) + 2 * (CQ_PAD * qw * 2 + CQ_PAD * 512 * 2 + 128 * 1024 * 2) \
        + 2 * (2 * tm * qw * 2 + tm * 512 * 2) + 3 * tm * qw * 4
    const = lambda i: (0, 0)
    return pl.pallas_call(
        _mla_prep_kernel,
        grid=(m // tm,),
        in_specs=[
            pl.BlockSpec((tm, CQ_PAD), lambda i: (i, OFF_CQ // CQ_PAD)),
            pl.BlockSpec((tm, LANE), lambda i: (i, OFF_CKV // LANE)),
            pl.BlockSpec((tm, LANE), lambda i: (i, OFF_KR // LANE)),
            pl.BlockSpec((tm, LANE), lambda i: (i, OFF_KRS // LANE)),
            pl.BlockSpec((1, CQ_PAD), const),
            pl.BlockSpec((1, KV_LORA), const),
            pl.BlockSpec((CQ_PAD, qw), const),
            pl.BlockSpec((CQ_PAD, C_HEADS * LANE), const),
            pl.BlockSpec((KV_LORA, C_HEADS * (NOPE_DIM + V_DIM)), const),
            pl.BlockSpec((tm, LANE), lambda i: (i % nseq, 0)),
            pl.BlockSpec((tm, LANE), lambda i: (i % nseq, 0)),
        ],
        out_specs=[
            pl.BlockSpec((tm, qw), lambda i: (i, 0)),
            pl.BlockSpec((tm, qw), lambda i: (i, 0)),
            pl.BlockSpec((tm, C_HEADS * V_DIM), lambda i: (i, 0)),
        ],
        out_shape=[
            jax.ShapeDtypeStruct((m, qw), BF16),
            jax.ShapeDtypeStruct((m, qw), BF16),
            jax.ShapeDtypeStruct((m, C_HEADS * V_DIM), BF16),
        ],
        compiler_params=_params(("arbitrary",), est),
        name="mla_prep",
    )(proj, proj, proj, proj, gq, gkv, wq1, wq2, wkv, cos_t, sin_t)


def _paired_loop(n, body, init):
    def pair(p, carry):
        return body(2 * p + 1, body(2 * p, carry))
    carry = lax.fori_loop(0, n // 2, pair, init)
    return lax.fori_loop(2 * (n // 2), n, body, carry)


def _nt_dot(a, b):
    return lax.dot_general(a, b, (((1,), (1,)), ((), ())), preferred_element_type=F32)


def _two_pass_attention(i, *, n_heads, tq, tk, dv, logits_fn, diag_mask, v_fn, store_fn,
                        s_ref, mx_ref, acc_ref):
    assert tk == 2 * tq
    n_wide = (i * tq) // tk
    has_narrow = (i * tq) % tk != 0
    narrow_ks = pl.multiple_of(n_wide * tk, tk)
    diag_ks = pl.multiple_of(i * tq, tq)

    mx_ref[...] = jnp.full(mx_ref.shape, NEG_INF, F32)

    def pass1(ks, width, mask=None):
        for h in range(n_heads):
            s = logits_fn(h, ks, width)
            if mask is not None:
                s = jnp.where(mask, s, NEG_INF)
            s_ref[h, :, pl.ds(ks, width)] = s
            mx = mx_ref[h]
            for c in range(width // LANE):
                mx = jnp.maximum(mx, s[:, c * LANE:(c + 1) * LANE])
            mx_ref[h] = mx

    def pass1_wide(kb, carry):
        pass1(pl.multiple_of(kb * tk, tk), tk)
        return carry

    _paired_loop(n_wide, pass1_wide, 0)

    @pl.when(has_narrow)
    def _():
        pass1(narrow_ks, tq)

    pass1(diag_ks, tq, diag_mask)

    for h in range(n_heads):
        mx_ref[h] = jnp.broadcast_to(jnp.max(mx_ref[h], axis=-1, keepdims=True), (tq, LANE))
    acc_ref[...] = jnp.zeros(acc_ref.shape, F32)

    def pass2(ks, width):
        ones = jnp.ones((width, LANE), BF16)
        for h in range(n_heads):
            m = mx_ref[h]
            p = jnp.concatenate(
                [jnp.exp2(s_ref[h, :, pl.ds(ks + c * LANE, LANE)] - m).astype(BF16)
                 for c in range(width // LANE)], axis=1)
            v1 = jnp.concatenate([v_fn(h, ks, width), ones], axis=1)
            acc_ref[h] += jnp.dot(p, v1, preferred_element_type=F32)

    def pass2_wide(kb, carry):
        pass2(pl.multiple_of(kb * tk, tk), tk)
        return carry

    _paired_loop(n_wide, pass2_wide, 0)

    @pl.when(has_narrow)
    def _():
        pass2(narrow_ks, tq)

    pass2(diag_ks, tq)

    for h in range(n_heads):
        acc = acc_ref[h]
        store_fn(h, acc[:, :dv] / acc[:, dv:])


def _tile_iotas(tq, tk):
    return (lax.broadcasted_iota(jnp.int32, (tq, tk), 0), lax.broadcasted_iota(jnp.int32, (tq, tk), 1))


def _fox_kernel(q_ref, k_ref, v_ref, ccol_ref, crow_ref, o_ref,
                s_ref, mx_ref, acc_ref, cq_ref, *, tq, tk):
    i = pl.program_id(1)
    head = lambda h: slice(h * HEAD_DIM, (h + 1) * HEAD_DIM)
    for h in range(B_HEADS):
        cq_ref[h] = jnp.broadcast_to(ccol_ref[:, MISC_FL + h:MISC_FL + h + 1] * LOG2E, (tq, LANE))

    def logits_fn(h, ks, width):
        s = _nt_dot(q_ref[:, head(h)], k_ref[pl.ds(ks, width), head(h)])
        cq = jnp.concatenate([cq_ref[h]] * (width // LANE), axis=1)
        return s + cq - crow_ref[h:h + 1, pl.ds(ks, width)] * LOG2E

    def store_fn(h, y):
        o_ref[:, head(h)] = y.astype(o_ref.dtype)

    rows, cols = _tile_iotas(tq, tq)
    _two_pass_attention(i, n_heads=B_HEADS, tq=tq, tk=tk, dv=HEAD_DIM, logits_fn=logits_fn,
                        diag_mask=cols <= rows,
                        v_fn=lambda h, ks, width: v_ref[pl.ds(ks, width), head(h)],
                        store_fn=store_fn, s_ref=s_ref, mx_ref=mx_ref, acc_ref=acc_ref)


def _fox_attention(proj, ccol, crow, batch, *, tq=256, tk=512):
    m = proj.shape[0]
    nq = SEQ // tq
    w = B_HEADS * HEAD_DIM
    state = B_HEADS * tq * LANE * 4
    est = (4 * tq * w * 2 + 4 * SEQ * w * 2 + 2 * tq * LANE * 4 + 2 * 8 * SEQ * 4
           + B_HEADS * tq * SEQ * 4 + 4 * state + 6 * tq * tk * 4)
    return pl.pallas_call(
        functools.partial(_fox_kernel, tq=tq, tk=tk),
        grid=(batch, nq),
        in_specs=[
            pl.BlockSpec((tq, w), lambda b, i: (b * nq + i, OFF_QB // w)),
            pl.BlockSpec((SEQ, w), lambda b, i: (b, OFF_KB // w)),
            pl.BlockSpec((SEQ, w), lambda b, i: (b, OFF_VB // w)),
            pl.BlockSpec((tq, LANE), lambda b, i: (b * nq + i, 0)),
            pl.BlockSpec((None, B_HEADS, SEQ), lambda b, i: (b, 0, 0)),
        ],
        out_specs=pl.BlockSpec((tq, w), lambda b, i: (b * nq + i, 0)),
        out_shape=jax.ShapeDtypeStruct((m, w), BF16),
        scratch_shapes=[
            pltpu.VMEM((B_HEADS, tq, SEQ), F32),
            pltpu.VMEM((B_HEADS, tq, LANE), F32),
            pltpu.VMEM((B_HEADS, tq, HEAD_DIM + LANE), F32),
            pltpu.VMEM((B_HEADS, tq, LANE), F32),
        ],
        compiler_params=_params(("arbitrary", "arbitrary"), est),
        name="fox_attention",
    )(proj, proj, proj, ccol, crow)


def _mla_kernel(q_ref, k_ref, v_ref, o_ref, s_ref, mx_ref, acc_ref, *, tq, tk):
    i = pl.program_id(1)
    qw = NOPE_DIM + LANE
    qhead = lambda h: slice(h * qw, (h + 1) * qw)
    vhead = lambda h: slice(h * V_DIM, (h + 1) * V_DIM)

    def logits_fn(h, ks, width):
        return _nt_dot(q_ref[:, qhead(h)], k_ref[pl.ds(ks, width), qhead(h)])

    def store_fn(h, y):
        o_ref[:, vhead(h)] = y.astype(o_ref.dtype)

    rows, cols = _tile_iotas(tq, tq)
    _two_pass_attention(i, n_heads=C_HEADS, tq=tq, tk=tk, dv=V_DIM, logits_fn=logits_fn,
                        diag_mask=cols // CHUNK <= rows // CHUNK,
                        v_fn=lambda h, ks, width: v_ref[pl.ds(ks, width), vhead(h)],
                        store_fn=store_fn, s_ref=s_ref, mx_ref=mx_ref, acc_ref=acc_ref)


def _mla_attention(qc, kc, vc, batch, *, tq=512, tk=1024):
    m = qc.shape[0]
    nq = SEQ // tq
    qw = C_HEADS * (NOPE_DIM + LANE)
    vw = C_HEADS * V_DIM
    state = C_HEADS * tq * LANE * 4
    est = (2 * tq * qw * 2 + 2 * SEQ * qw * 2 + 2 * SEQ * vw * 2 + 2 * tq * vw * 2
           + C_HEADS * tq * SEQ * 4 + 3 * state + 6 * tq * tk * 4)
    return pl.pallas_call(
        functools.partial(_mla_kernel, tq=tq, tk=tk),
        grid=(batch, nq),
        in_specs=[
            pl.BlockSpec((tq, qw), lambda b, i: (b * nq + i, 0)),
            pl.BlockSpec((SEQ, qw), lambda b, i: (b, 0)),
            pl.BlockSpec((SEQ, vw), lambda b, i: (b, 0)),
        ],
        out_specs=pl.BlockSpec((tq, vw), lambda b, i: (b * nq + i, 0)),
        out_shape=jax.ShapeDtypeStruct((m, vw), BF16),
        scratch_shapes=[
            pltpu.VMEM((C_HEADS, tq, SEQ), F32),
            pltpu.VMEM((C_HEADS, tq, LANE), F32),
            pltpu.VMEM((C_HEADS, tq, V_DIM + LANE), F32),
        ],
        compiler_params=_params(("arbitrary", "arbitrary"), est),
        name="mla_attention",
    )(qc, kc, vc)


def _t5_tiles_kernel(bucket_ref, table_ref, o_ref):
    for d in range(3):
        bucket = bucket_ref[d]
        for h in range(A_HEADS):
            acc = jnp.zeros(bucket.shape, F32)
            for nb in range(NUM_BUCKETS):
                acc = jnp.where(bucket == nb, table_ref[nb, h] * LOG2E, acc)
            o_ref[d, h] = acc


def _t5_tiles(bucket_tiles, t5_bias, *, t=ATT_T):
    return pl.pallas_call(
        _t5_tiles_kernel,
        in_specs=[
            pl.BlockSpec(memory_space=pltpu.VMEM),
            pl.BlockSpec(memory_space=pltpu.SMEM),
        ],
        out_specs=pl.BlockSpec(memory_space=pltpu.VMEM),
        out_shape=jax.ShapeDtypeStruct((3, A_HEADS, t, t), F32),
        name="t5_tiles",
    )(bucket_tiles, t5_bias)


def _dsa_kernel(qi_ref, klo_ref, khi_ref, misc_ref, qa_ref, ka_ref, va_ref, bias_ref, *rest, t):
    i = pl.program_id(1)
    if len(rest) == 6:
        w_next_ref, o_ref, w_next_packed_ref, sc_ref, lg_ref, acc_ref = rest

        @pl.when(i % 2 == 0)
        def _():
            _pack_w_in_kernel(w_next_ref, w_next_packed_ref)
    else:
        o_ref, sc_ref, lg_ref, acc_ref = rest
    groups = t // 8
    key = lax.broadcasted_iota(jnp.int32, (t, t), 0)
    qry = lax.broadcasted_iota(jnp.int32, (t, t), 1)
    admissible = (key // CHUNK) <= (qry // CHUNK)

    def fold(x, op):
        return op(x.reshape(groups, 8, t), axis=0)

    def over_keys(x8, op):
        return jnp.broadcast_to(op(x8, axis=0, keepdims=True), (8, t))

    w_t = misc_ref[...].T[MISC_WI:MISC_WI + IDX_HEADS, :] * (IDX_HEADS ** -0.5 * IDX_DIM ** -0.5)

    def block_scores(kb):
        ks = pl.multiple_of(kb * t, t)
        klo = klo_ref[pl.ds(ks, t), :]
        khi = khi_ref[pl.ds(ks, t), :]
        acc = jnp.zeros((t, t), F32)
        for pair in range(IDX_HEADS // 2):
            qp = qi_ref[:, pair * LANE:(pair + 1) * LANE]
            for sub, kk in enumerate((klo, khi)):
                h = 2 * pair + sub
                acc = acc + jnp.maximum(_nt_dot(kk, qp), 0.0) * w_t[h:h + 1, :]
        return ks, acc

    def score_body(kb, carry):
        mn, mx = carry
        ks, acc = block_scores(kb)
        sc_ref[pl.ds(ks, t), :] = acc
        return jnp.minimum(mn, fold(acc, jnp.min)), jnp.maximum(mx, fold(acc, jnp.max))

    mn, mx = _paired_loop(i, score_body,
                          (jnp.full((8, t), -SELECT_MIN, F32), jnp.full((8, t), SELECT_MIN, F32)))
    ks, acc = block_scores(i)
    sc_ref[pl.ds(ks, t), :] = jnp.where(admissible, acc, NEG_INF)
    mn = jnp.minimum(mn, fold(jnp.where(admissible, acc, -SELECT_MIN), jnp.min))
    mx = jnp.maximum(mx, fold(jnp.where(admissible, acc, SELECT_MIN), jnp.max))

    def count_ge(thr):
        def body(kb, cnt):
            ks = pl.multiple_of(kb * t, t)
            hit = jnp.where(sc_ref[pl.ds(ks, t), :].reshape(groups, 8, t) >= thr[None], 1.0, 0.0)
            return cnt + jnp.sum(hit, axis=0)
        return over_keys(_paired_loop(i + 1, body, jnp.zeros((8, t), F32)), jnp.sum)

    k_sel = float(TOPK_MAX)
    search = i * t >= TOPK_MAX
    lo0 = jnp.where(search, over_keys(mn, jnp.min), SELECT_MIN)
    hi0 = jnp.where(search, over_keys(mx, jnp.max), SELECT_MIN)

    def midpoint(lo, hi):
        return 0.5 * lo + 0.5 * hi

    def any_active(lo, hi):
        mid = midpoint(lo, hi)
        return jnp.max(jnp.where((mid > lo) & (mid < hi), 1.0, 0.0)) > 0.5

    def search_cond(state):
        return jnp.logical_and(state[0] < BISECT_MAX_ITERS, state[1])

    def search_body(state):
        it, _, lo, hi, c_lo = state
        for _ in range(SEARCH_STEPS_PER_CHECK):
            mid = midpoint(lo, hi)
            cnt = count_ge(mid)
            ge = cnt >= k_sel
            lo = jnp.where(ge, mid, lo)
            c_lo = jnp.where(ge, cnt, c_lo)
            hi = jnp.where(cnt > k_sel, hi, mid)
        return it + SEARCH_STEPS_PER_CHECK, any_active(lo, hi), lo, hi, c_lo

    n_adm = (((i * t + qry[:8]) // CHUNK + 1) * CHUNK).astype(F32)
    _, _, lo, hi, c_lo = lax.while_loop(search_cond, search_body,
                                        (jnp.int32(0), any_active(lo0, hi0), lo0, hi0, n_adm))
    c_hi = count_ge(hi)
    thr = jnp.where(c_hi >= k_sel, hi, lo)
    c_thr = jnp.where(c_hi >= k_sel, c_hi, c_lo)

    @pl.when(jnp.max(jnp.where(c_thr > k_sel, 1.0, 0.0)) > 0.5)
    def _():
        key_in_block = (lax.broadcasted_iota(jnp.int32, (groups, 8, t), 0) * 8
                        + lax.broadcasted_iota(jnp.int32, (groups, 8, t), 1))

        def count_where(pred):
            def body(kb, cnt):
                ks = pl.multiple_of(kb * t, t)
                blk = sc_ref[pl.ds(ks, t), :].reshape(groups, 8, t)
                return cnt + jnp.sum(jnp.where(pred(blk, ks), 1.0, 0.0), axis=0)
            return over_keys(lax.fori_loop(0, i + 1, body, jnp.zeros((8, t), F32)), jnp.sum)

        def tied_before(bound):
            return lambda blk, ks: (blk == thr[None]) & ((key_in_block + ks).astype(F32) < bound[None])

        keep = k_sel - count_where(lambda blk, ks: blk > thr[None])

        def cut_body(_, bounds):
            below, above = bounds
            mid = jnp.floor(0.5 * (below + above))
            enough = count_where(tied_before(mid)) >= keep
            return jnp.where(enough, below, mid), jnp.where(enough, mid, above)

        _, cut = lax.fori_loop(0, SEQ.bit_length(), cut_body,
                               (jnp.zeros((8, t), F32), jnp.full((8, t), float(SEQ), F32)))

        def drop_body(kb, carry):
            ks = pl.multiple_of(kb * t, t)
            blk = sc_ref[pl.ds(ks, t), :].reshape(groups, 8, t)
            drop = (blk == thr[None]) & ((key_in_block + ks).astype(F32) >= cut[None])
            sc_ref[pl.ds(ks, t), :] = jnp.where(drop, NEG_INF, blk).reshape(t, t)
            return carry

        lax.fori_loop(0, i + 1, drop_body, 0)

    head = lambda h: slice(h * HEAD_DIM, (h + 1) * HEAD_DIM)
    acc_ref[...] = jnp.zeros(acc_ref.shape, F32)

    def logits_pass(kb, mx):
        ks = pl.multiple_of(kb * t, t)
        dist = jnp.minimum(i - kb, 2)
        k_blk = ka_ref[pl.ds(ks, t), :]
        sel = sc_ref[pl.ds(ks, t), :] >= thr[0:1, :]
        out = []
        for h in range(A_HEADS):
            s = _nt_dot(k_blk, qa_ref[:, head(h)]) + bias_ref[dist, h]
            s = jnp.where(sel, s, NEG_INF)
            lg_ref[h, pl.ds(ks, t), :] = s
            out.append(jnp.maximum(mx[h], fold(s, jnp.max)))
        return tuple(out)

    mx = _paired_loop(i + 1, logits_pass,
                      tuple(jnp.full((8, t), NEG_INF, F32) for _ in range(A_HEADS)))
    m = [over_keys(mx[h], jnp.max)[0:1, :] for h in range(A_HEADS)]

    ones_rows = jnp.ones((DSA_ONES_ROWS, t), BF16)

    def value_pass(kb, carry):
        ks = pl.multiple_of(kb * t, t)
        v_t = va_ref[pl.ds(ks, t), :].astype(F32).T.astype(BF16)
        v1_t = jnp.concatenate([v_t, ones_rows], axis=0)
        for h in range(A_HEADS):
            p = jnp.exp2(lg_ref[h, pl.ds(ks, t), :] - m[h])
            acc_ref[h] += jnp.dot(v1_t, p.astype(BF16), preferred_element_type=F32)
        return carry

    _paired_loop(i + 1, value_pass, 0)
    for h in range(A_HEADS):
        acc = acc_ref[h]
        o_ref[:, head(h)] = (acc[:HEAD_DIM] / acc[HEAD_DIM:HEAD_DIM + 1]).T.astype(o_ref.dtype)


def _dsa_attention(proj, misc, bias_tiles, batch, w_in=None, pack_layer=None, *, t=ATT_T):
    m = proj.shape[0]
    nq = SEQ // t
    qiw = IDX_HEADS * IDX_DIM
    qaw = A_HEADS * HEAD_DIM
    est = (2 * t * qiw * 2 + 2 * 4 * SEQ * LANE * 2 + 2 * t * LANE * 4 + 2 * t * qaw * 2
           + 2 * 3 * A_HEADS * t * t * 4 + 2 * t * qaw * 2
           + (1 + A_HEADS) * t * SEQ * 4 + A_HEADS * HEAD_DIM * t * 4 + 12 * t * t * 4)
    kblock = lambda off: pl.BlockSpec((SEQ, LANE), lambda b, i: (b, off // LANE))
    extra_in, extra_out, extra_shape, extra_args = [], [], [], []
    if pack_layer is not None:
        w_t = jnp.swapaxes(w_in, 1, 2)
        _, n, d = w_t.shape
        cols = 2 * d // (batch * nq)
        col_block = lambda b, i: (b * nq + i) // 2
        extra_in = [pl.BlockSpec((None, n, cols), lambda b, i: (pack_layer, 0, col_block(b, i)))]
        extra_out = [pl.BlockSpec((N_PACK, cols), lambda b, i: (0, col_block(b, i)))]
        extra_shape = [jax.ShapeDtypeStruct((N_PACK, d), BF16)]
        extra_args = [w_t]
        est += 2 * cols * (n * 4 + N_PACK * 2)
    return pl.pallas_call(
        functools.partial(_dsa_kernel, t=t),
        grid=(batch, nq),
        in_specs=[
            pl.BlockSpec((t, qiw), lambda b, i: (b * nq + i, OFF_QI // qiw)),
            kblock(OFF_KI_LO),
            kblock(OFF_KI_HI),
            pl.BlockSpec((t, LANE), lambda b, i: (b * nq + i, 0)),
            pl.BlockSpec((t, qaw), lambda b, i: (b * nq + i, OFF_QA // qaw)),
            kblock(OFF_KA),
            kblock(OFF_VA),
            pl.BlockSpec((3, A_HEADS, t, t), lambda b, i: (0, 0, 0, 0)),
        ] + extra_in,
        out_specs=[pl.BlockSpec((t, qaw), lambda b, i: (b * nq + i, 0))] + extra_out,
        out_shape=[jax.ShapeDtypeStruct((m, qaw), BF16)] + extra_shape,
        scratch_shapes=[
            pltpu.VMEM((SEQ, t), F32),
            pltpu.VMEM((A_HEADS, SEQ, t), F32),
            pltpu.VMEM((A_HEADS, HEAD_DIM + DSA_ONES_ROWS, t), F32),
        ],
        compiler_params=_params(("arbitrary", "arbitrary"), est),
        name="dsa_attention",
    )(proj, proj, proj, misc, proj, proj, proj, bias_tiles, *extra_args)


def _merge_kernel(x_ref, ya_ref, yb_ref, yc_ref, ga_ref, gb_ref, gc_ref,
                  wa_ref, wb_ref, wc_ref, wo_ref, gn_ref, o_ref, hn_ref):
    def branch(y_ref, w_ref, g_ref):
        y = jnp.dot(y_ref[...], w_ref[...], preferred_element_type=F32)
        return jax.nn.sigmoid(g_ref[...].astype(F32)) * y

    merged = branch(ya_ref, wa_ref, ga_ref) + branch(yb_ref, wb_ref, gb_ref) + branch(yc_ref, wc_ref, gc_ref)
    x_new = x_ref[...] + jnp.dot(merged.astype(BF16), wo_ref[...], preferred_element_type=F32)
    o_ref[...] = x_new
    ms = jnp.mean(x_new * x_new, axis=-1, keepdims=True)
    hn_ref[...] = (x_new * lax.rsqrt(ms + EPS) * gn_ref[...]).astype(hn_ref.dtype)


def _merge(x2, ya, yb, yc, proj, wa, wb, wc, wo, g_next, *, tm=256):
    m = x2.shape[0]
    wbytes = (wa.size + wb.size + wc.size + wo.size) * 2
    est = 2 * wbytes + 4 * tm * D_MODEL * 4 + 2 * tm * 2048 * 2 + 8 * tm * D_MODEL * 2 + 4 * tm * D_MODEL * 4
    const = lambda i: (0, 0)
    gate = lambda k: pl.BlockSpec((tm, D_MODEL), lambda i: (i, OFF_GL // D_MODEL + k))
    return pl.pallas_call(
        _merge_kernel,
        grid=(m // tm,),
        in_specs=[
            pl.BlockSpec((tm, D_MODEL), lambda i: (i, 0)),
            pl.BlockSpec((tm, ya.shape[1]), lambda i: (i, 0)),
            pl.BlockSpec((tm, yb.shape[1]), lambda i: (i, 0)),
            pl.BlockSpec((tm, yc.shape[1]), lambda i: (i, 0)),
            gate(0), gate(1), gate(2),
            pl.BlockSpec(wa.shape, const),
            pl.BlockSpec(wb.shape, const),
            pl.BlockSpec(wc.shape, const),
            pl.BlockSpec(wo.shape, const),
            pl.BlockSpec((1, D_MODEL), const),
        ],
        out_specs=[
            pl.BlockSpec((tm, D_MODEL), lambda i: (i, 0)),
            pl.BlockSpec((tm, D_MODEL), lambda i: (i, 0)),
        ],
        out_shape=[
            jax.ShapeDtypeStruct((m, D_MODEL), F32),
            jax.ShapeDtypeStruct((m, D_MODEL), BF16),
        ],
        compiler_params=_params(("arbitrary",), est),
        name="merge",
    )(x2, ya, yb, yc, proj, proj, proj, wa, wb, wc, wo, g_next)


def _ffn_kernel(x_ref, h_ref, wu_ref, wv_ref, cw_ref, wd_ref, fg_ref, o_ref,
                ubuf_ref, halo_ref, *, tm, tf, final_norm):
    i = pl.program_id(0)
    j = pl.program_id(1)
    nj = pl.num_programs(1)

    @pl.when(j == 0)
    def _():
        o_ref[...] = x_ref[...]

        @pl.when((i % (SEQ // tm)) == 0)
        def _():
            halo_ref[...] = jnp.zeros(halo_ref.shape, F32)

    h = h_ref[...]
    u = jnp.dot(h, wu_ref[...], preferred_element_type=F32)
    v = jnp.dot(h, wv_ref[...], preferred_element_type=F32)

    ubuf_ref[0:8, :] = halo_ref[j]
    ubuf_ref[8:, :] = u
    halo_ref[j] = u[tm - 8:, :]
    conv = (cw_ref[0:1, :] * ubuf_ref[6:6 + tm, :] + cw_ref[1:2, :] * ubuf_ref[7:7 + tm, :]
            + cw_ref[2:3, :] * u + cw_ref[3:4, :])
    act = (jax.nn.gelu(conv) * v).astype(BF16)
    o_ref[...] += jnp.dot(act, wd_ref[...], preferred_element_type=F32)

    if final_norm:
        @pl.when(j == nj - 1)
        def _():
            _rmsnorm_rows(o_ref, fg_ref, o_ref, tm)


def _ffn(x2, h2, w_up, cw, w_down, final_g, *, final_norm, tm=1024, tf=512):
    m = x2.shape[0]
    nj = D_FF // tf
    est = (3 * tm * D_MODEL * 4 + 2 * tm * D_MODEL * 2 + 2 * 2 * D_MODEL * tf * 2 + 2 * tf * D_MODEL * 2
           + (tm + 8) * tf * 4 + nj * 8 * tf * 4 + 4 * tm * tf * 4)
    return pl.pallas_call(
        functools.partial(_ffn_kernel, tm=tm, tf=tf, final_norm=final_norm),
        grid=(m // tm, nj),
        in_specs=[
            pl.BlockSpec((tm, D_MODEL), lambda i, j: (i, 0), pipeline_mode=pl.Buffered(1)),
            pl.BlockSpec((tm, D_MODEL), lambda i, j: (i, 0)),
            pl.BlockSpec((D_MODEL, tf), lambda i, j: (0, j)),
            pl.BlockSpec((D_MODEL, tf), lambda i, j: (0, nj + j)),
            pl.BlockSpec((8, tf), lambda i, j: (0, j)),
            pl.BlockSpec((tf, D_MODEL), lambda i, j: (j, 0)),
            pl.BlockSpec((1, D_MODEL), lambda i, j: (0, 0)),
        ],
        out_specs=pl.BlockSpec((tm, D_MODEL), lambda i, j: (i, 0)),
        out_shape=jax.ShapeDtypeStruct((m, D_MODEL), F32),
        scratch_shapes=[
            pltpu.VMEM((tm + 8, tf), F32),
            pltpu.VMEM((nj, 8, tf), F32),
        ],
        compiler_params=_params(("arbitrary", "arbitrary"), est),
        name="conv_ffn",
    )(x2, h2, w_up, w_up, cw, w_down, final_g)


def _t5_bucket(rel):
    nb = NUM_BUCKETS // 2
    max_exact = nb // 2
    base = jnp.where(rel > 0, nb, 0)
    n = jnp.abs(rel)
    nf = jnp.maximum(n, 1).astype(F32)
    large = max_exact + (jnp.log(nf / max_exact) / math.log(MAX_DISTANCE / max_exact)
                         * (nb - max_exact)).astype(jnp.int32)
    large = jnp.minimum(large, nb - 1)
    return base + jnp.where(n < max_exact, n, large)


def _bucket_tiles(t):
    key = jnp.arange(t, dtype=jnp.int32)[:, None]
    qry = jnp.arange(t, dtype=jnp.int32)[None, :]
    return jnp.stack([_t5_bucket(key - qry - d * t) for d in range(3)])


def _rope_tables():
    half = ROPE_DIM // 2
    inv = ROPE_THETA ** (-jnp.arange(half, dtype=F32) / half)
    ang = jnp.arange(SEQ, dtype=jnp.int32).astype(F32)[:, None] * inv[None, :]
    cos, sin = jnp.cos(ang), jnp.sin(ang)
    pad = jnp.zeros((SEQ, LANE - ROPE_DIM), F32)
    return jnp.concatenate([cos, cos, pad], axis=1), jnp.concatenate([-sin, sin, pad], axis=1)


def _swap_halves(w):
    half = w.shape[-1] // 2
    return jnp.concatenate([w[..., half:], w[..., :half]], axis=-1)


def _pack_w_in_kernel(w_ref, o_ref):
    off = np.cumsum((0,) + IN_SIZES)
    seg = lambda k: w_ref[off[k]:off[k + 1], :]
    qa, ka, va, qi, ki, wi, qb, kb, vb, fl, cq, ckv, kr, gl = [seg(k) for k in range(len(IN_SIZES))]
    qa, qb = qa * Q_SCALE, qb * Q_SCALE
    z = lambda n: jnp.zeros((n, w_ref.shape[1]), F32)
    half = ROPE_DIM // 2
    groups = [
        (OFF_QI, [qi]), (OFF_QB, [qb]), (OFF_KB, [kb]), (OFF_VB, [vb]), (OFF_GL, [gl]), (OFF_QA, [qa]),
        (OFF_CQ, [cq, z(CQ_PAD - Q_LORA)]), (OFF_KA, [ka]), (OFF_VA, [va]), (OFF_KI_LO, [ki, z(64)]),
        (OFF_CKV, [ckv]), (OFF_KR, [kr, z(64)]), (OFF_KRS, [kr[half:], kr[:half], z(64)]),
        (OFF_MISC, [wi, fl, z(LANE - IDX_HEADS - B_HEADS)]), (OFF_KI_HI, [z(64), ki]),
    ]
    for start, pieces in groups:
        block = pieces[0] if len(pieces) == 1 else jnp.concatenate(pieces, axis=0)
        o_ref[start:start + block.shape[0], :] = block.astype(o_ref.dtype)


def _pack_w_in(w_in, l, *, cols=256):
    w_t = jnp.swapaxes(w_in, 1, 2)
    _, n, d = w_t.shape
    return pl.pallas_call(
        _pack_w_in_kernel,
        grid=(d // cols,),
        in_specs=[pl.BlockSpec((None, n, cols), lambda i: (l, 0, i))],
        out_specs=pl.BlockSpec((N_PACK, cols), lambda i: (0, i)),
        out_shape=jax.ShapeDtypeStruct((N_PACK, d), BF16),
        compiler_params=_params(("arbitrary",), 2 * cols * (n * 4 + N_PACK * 2)),
        name="pack_w_in",
    )(w_t)


def _pack_w_uq(w):
    w = jnp.pad(w, ((0, CQ_PAD - Q_LORA), (0, 0))).reshape(CQ_PAD, C_HEADS, NOPE_DIM + ROPE_DIM)
    z = jnp.zeros((CQ_PAD, C_HEADS, LANE - ROPE_DIM), w.dtype)
    rope = w[..., NOPE_DIM:]
    wq1 = jnp.concatenate([w, z], axis=-1).reshape(CQ_PAD, -1)
    wq2 = jnp.concatenate([_swap_halves(rope), z], axis=-1).reshape(CQ_PAD, -1)
    return wq1.astype(BF16), wq2.astype(BF16)


def kernel(x, norm_mix_g, w_in, b_forget, g_cq, g_ckv, w_uq, w_ukv, w_branch_a, w_branch_b, w_branch_c,
           w_o, norm_ffn_g, w_up, conv_w, conv_b, w_down, t5_bias, final_g):
    batch, seq, d = x.shape
    assert (seq, d) == (SEQ, D_MODEL)
    x2 = x.reshape(batch * seq, d)

    bias_tiles = _t5_tiles(_bucket_tiles(ATT_T), t5_bias)
    cos_t, sin_t = _rope_tables()
    final_row = final_g.reshape(1, D_MODEL)

    w_in_packed = _pack_w_in(w_in, 0)
    for l in range(DEPTH):
        casts = [(w_up, True, 11), (w_down, False, 11), (w_o, True, 8),
                 (w_branch_a, True, 8), (w_branch_b, True, 8), (w_branch_c, True, 8)]
        proj, misc, w_up_l, w_down_l, w_o_l, w_a_l, w_b_l, w_c_l = _inproj(
            x2, norm_mix_g[l].reshape(1, D_MODEL), w_in_packed, casts, l)

        fbias = jnp.zeros((1, LANE), F32).at[0, MISC_FL:MISC_FL + B_HEADS].set(b_forget[l])
        ccol, crow = _forget_cumsum(misc, fbias, batch)

        wq1, wq2 = _pack_w_uq(w_uq[l])
        gq = jnp.pad(g_cq[l], (0, CQ_PAD - Q_LORA)).reshape(1, CQ_PAD)
        qc, kc, vc = _mla_prep(proj, gq, g_ckv[l].reshape(1, KV_LORA), wq1, wq2,
                               w_ukv[l].astype(BF16), cos_t, sin_t)

        dsa_out = _dsa_attention(proj, misc, bias_tiles, batch, w_in, l + 1 if l + 1 < DEPTH else None)
        ya = dsa_out[0]
        if l + 1 < DEPTH:
            w_in_packed = dsa_out[1]
        yb = _fox_attention(proj, ccol, crow, batch)
        yc = _mla_attention(qc, kc, vc, batch)

        x2, h2 = _merge(x2, ya, yb, yc, proj, w_a_l, w_b_l, w_c_l, w_o_l,
                        norm_ffn_g[l].reshape(1, D_MODEL))

        cw = jnp.concatenate([conv_w[l], conv_b[l][None, :], jnp.zeros((4, D_FF), F32)], axis=0)
        x2 = _ffn(x2, h2, w_up_l, cw, w_down_l, final_row, final_norm=(l == DEPTH - 1))

    return x2.reshape(batch, seq, d)
```

```python
import functools
import math

import jax
import jax.numpy as jnp
import numpy as np
from jax import lax
from jax.experimental import pallas as pl
from jax.experimental.pallas import tpu as pltpu

F32 = jnp.float32
BF16 = jnp.bfloat16

D_MODEL = 2048
SEQ = 2048
DEPTH = 2
CHUNK = 64
HEAD_DIM = 128
EPS = 1e-6
NEG_INF = -1e30

A_HEADS = 4
IDX_HEADS = 16
IDX_DIM = 64
TOPK_MAX = 256
NUM_BUCKETS = 32
MAX_DISTANCE = 128
B_HEADS = 8
C_HEADS = 4
Q_LORA = 448
KV_LORA = 128
NOPE_DIM = 128
ROPE_DIM = 64
V_DIM = 128
ROPE_THETA = 10000.0
D_FF = 5632

IN_SIZES = (
    A_HEADS * HEAD_DIM, HEAD_DIM, HEAD_DIM,
    IDX_HEADS * IDX_DIM, IDX_DIM, IDX_HEADS,
    B_HEADS * HEAD_DIM, B_HEADS * HEAD_DIM, B_HEADS * HEAD_DIM, B_HEADS,
    Q_LORA, KV_LORA, ROPE_DIM,
    3 * D_MODEL,
)

LOG2E = math.log2(math.e)
Q_SCALE = HEAD_DIM ** -0.5 * LOG2E
Q_SCALE_MLA = (NOPE_DIM + ROPE_DIM) ** -0.5 * LOG2E
LANE = 128
V7X_VMEM_BYTES = 64 * 1024 * 1024

OFF_QI = 0
OFF_QB = 1024
OFF_KB = 2048
OFF_VB = 3072
OFF_GL = 4096
OFF_QA = 10240
OFF_CQ = 10752
OFF_KA = 11264
OFF_VA = 11392
OFF_KI_LO = 11520
OFF_CKV = 11648
OFF_KR = 11776
OFF_KRS = 11904
OFF_MISC = 12032
OFF_KI_HI = 12160
N_PACK = 12288
CQ_PAD = 512
MISC_WI = 0
MISC_FL = IDX_HEADS

ATT_T = 256
SELECT_MIN = -1e29
BISECT_MAX_ITERS = 512
SEARCH_STEPS_PER_CHECK = 4
DSA_ONES_ROWS = 16


def _vmem_limit(estimate_bytes):
    limit = V7X_VMEM_BYTES - (6 << 20)
    assert estimate_bytes <= limit, estimate_bytes
    return limit


def _params(semantics, vmem_estimate):
    return pltpu.CompilerParams(dimension_semantics=semantics,
                                vmem_limit_bytes=_vmem_limit(vmem_estimate))


def _rmsnorm_rows(x_ref, g_ref, out_ref, rows):
    def body(c, carry):
        r = pl.multiple_of(c * 128, 128)
        x = x_ref[pl.ds(r, 128), :]
        ms = jnp.mean(x * x, axis=-1, keepdims=True)
        out_ref[pl.ds(r, 128), :] = (x * lax.rsqrt(ms + EPS) * g_ref[...]).astype(out_ref.dtype)
        return carry
    lax.fori_loop(0, rows // 128, body, 0)


def _inproj_kernel(*refs, tm, tn, n_cast):
    x_ref, g_ref, w_ref = refs[:3]
    cast_in = refs[3:3 + n_cast]
    o_ref, misc_ref = refs[3 + n_cast:5 + n_cast]
    cast_out = refs[5 + n_cast:5 + 2 * n_cast]
    h_ref = refs[5 + 2 * n_cast]
    j = pl.program_id(1)

    @pl.when(j == 0)
    def _():
        _rmsnorm_rows(x_ref, g_ref, h_ref, tm)

    acc = _nt_dot(h_ref[...], w_ref[...])
    o_ref[...] = acc.astype(o_ref.dtype)

    for src, dst in zip(cast_in, cast_out):
        dst[...] = src[...].astype(dst.dtype)

    @pl.when(j == OFF_MISC // tn)
    def _():
        lo = OFF_MISC % tn
        misc_ref[...] = acc[:, lo:lo + LANE]


def _cast_specs(w_stack, l, ni, nj, rows_follow_i, n_chunks):
    _, r, c = w_stack.shape
    assert n_chunks <= nj
    chunk = lambda j: jnp.minimum(j, n_chunks - 1)
    if rows_follow_i:
        block = (r // ni, c // n_chunks)
        idx = lambda i, j: (i, chunk(j))
    else:
        block = (r // n_chunks, c // ni)
        idx = lambda i, j: (chunk(j), i)
    assert block[0] % 16 == 0 and block[1] % LANE == 0, block
    in_spec = pl.BlockSpec((None,) + block, lambda i, j: (l,) + idx(i, j))
    out_spec = pl.BlockSpec(block, idx)
    return in_spec, out_spec, jax.ShapeDtypeStruct((r, c), BF16), block[0] * block[1]


def _inproj(x2, g, w_pack, casts, l, *, tm=1024, tn=1024):
    m = x2.shape[0]
    ni, nj = m // tm, N_PACK // tn
    plans = [_cast_specs(w, l, ni, nj, rows_i, n) for w, rows_i, n in casts]
    est = (2 * tm * D_MODEL * 4 + tm * D_MODEL * 2 + 2 * D_MODEL * tn * 2 + 2 * tm * tn * 2 + tm * tn * 4
           + sum(2 * 6 * p[3] for p in plans))
    return pl.pallas_call(
        functools.partial(_inproj_kernel, tm=tm, tn=tn, n_cast=len(casts)),
        grid=(ni, nj),
        in_specs=[
            pl.BlockSpec((tm, D_MODEL), lambda i, j: (i, 0)),
            pl.BlockSpec((1, D_MODEL), lambda i, j: (0, 0)),
            pl.BlockSpec((tn, D_MODEL), lambda i, j: (j, 0)),
        ] + [p[0] for p in plans],
        out_specs=[
            pl.BlockSpec((tm, tn), lambda i, j: (i, j)),
            pl.BlockSpec((tm, LANE), lambda i, j: (i, 0)),
        ] + [p[1] for p in plans],
        out_shape=[
            jax.ShapeDtypeStruct((m, N_PACK), BF16),
            jax.ShapeDtypeStruct((m, LANE), F32),
        ] + [p[2] for p in plans],
        scratch_shapes=[pltpu.VMEM((tm, D_MODEL), BF16)],
        compiler_params=_params(("arbitrary", "arbitrary"), est),
        name="inproj",
    )(x2, g, w_pack, *[w for w, _, _ in casts])


def _forget_cumsum_kernel(misc_ref, bias_ref, ccol_ref, crow_ref, *, blk):
    rows = lax.broadcasted_iota(jnp.int32, (blk, blk), 0)
    cols = lax.broadcasted_iota(jnp.int32, (blk, blk), 1)
    tri = jnp.where(rows >= cols, 1.0, 0.0).astype(BF16)
    carry = jnp.zeros((1, LANE), F32)
    for c in range(SEQ // blk):
        z = misc_ref[c * blk:(c + 1) * blk, :] + bias_ref[...]
        lf = jnp.minimum(z, 0.0) - jnp.log1p(jnp.exp(-jnp.abs(z)))
        p0 = lf.astype(BF16)
        r1 = lf - p0.astype(F32)
        p1 = r1.astype(BF16)
        p2 = (r1 - p1.astype(F32)).astype(BF16)
        cs = (jnp.dot(tri, p0, preferred_element_type=F32)
              + jnp.dot(tri, p1, preferred_element_type=F32)
              + jnp.dot(tri, p2, preferred_element_type=F32)) + carry
        ccol_ref[c * blk:(c + 1) * blk, :] = cs
        crow_ref[:, c * blk:(c + 1) * blk] = cs.T[MISC_FL:MISC_FL + B_HEADS, :]
        carry = cs[blk - 1:blk, :]


def _forget_cumsum(misc, bias_row, batch, *, blk=256):
    est = 4 * SEQ * LANE * 4 + 2 * 8 * SEQ * 4
    return pl.pallas_call(
        functools.partial(_forget_cumsum_kernel, blk=blk),
        grid=(batch,),
        in_specs=[
            pl.BlockSpec((SEQ, LANE), lambda b: (b, 0)),
            pl.BlockSpec((1, LANE), lambda b: (0, 0)),
        ],
        out_specs=[
            pl.BlockSpec((SEQ, LANE), lambda b: (b, 0)),
            pl.BlockSpec((None, B_HEADS, SEQ), lambda b: (b, 0, 0)),
        ],
        out_shape=[
            jax.ShapeDtypeStruct((batch * SEQ, LANE), F32),
            jax.ShapeDtypeStruct((batch, B_HEADS, SEQ), F32),
        ],
        compiler_params=_params(("arbitrary",), est),
        name="forget_cumsum",
    )(misc, bias_row)


def _mla_prep_kernel(cq_ref, ckv_ref, kr_ref, krs_ref, gq_ref, gkv_ref, wq1_ref, wq2_ref, wkv_ref,
                     cos_ref, sin_ref, qc_ref, kc_ref, vc_ref):
    cq = cq_ref[...].astype(F32)
    ms = jnp.sum(cq * cq, axis=-1, keepdims=True) * (1.0 / Q_LORA)
    cqn = (cq * lax.rsqrt(ms + EPS) * gq_ref[...]).astype(BF16)
    ckv = ckv_ref[...].astype(F32)
    ms2 = jnp.mean(ckv * ckv, axis=-1, keepdims=True)
    ckvn = (ckv * lax.rsqrt(ms2 + EPS) * gkv_ref[...]).astype(BF16)

    q1 = jnp.dot(cqn, wq1_ref[...], preferred_element_type=F32) * Q_SCALE_MLA
    q2 = jnp.dot(cqn, wq2_ref[...], preferred_element_type=F32) * Q_SCALE_MLA
    kv = jnp.dot(ckvn, wkv_ref[...], preferred_element_type=F32)
    cos = cos_ref[...]
    sin = sin_ref[...]
    k_rope = (kr_ref[...].astype(F32) * cos + krs_ref[...].astype(F32) * sin).astype(BF16)
    for h in range(C_HEADS):
        qw = NOPE_DIM + LANE
        qc_ref[:, h * qw:h * qw + NOPE_DIM] = q1[:, h * qw:h * qw + NOPE_DIM].astype(BF16)
        q_rope = q1[:, h * qw + NOPE_DIM:(h + 1) * qw] * cos + q2[:, h * LANE:(h + 1) * LANE] * sin
        qc_ref[:, h * qw + NOPE_DIM:(h + 1) * qw] = q_rope.astype(BF16)
        kw = NOPE_DIM + V_DIM
        kc_ref[:, h * qw:h * qw + NOPE_DIM] = kv[:, h * kw:h * kw + NOPE_DIM].astype(BF16)
        kc_ref[:, h * qw + NOPE_DIM:(h + 1) * qw] = k_rope
        vc_ref[:, h * V_DIM:(h + 1) * V_DIM] = kv[:, h * kw + NOPE_DIM:(h + 1) * kw].astype(BF16)


def _mla_prep(proj, gq, gkv, wq1, wq2, wkv, cos_t, sin_t, *, tm=1024):
    m = proj.shape[0]
    nseq = SEQ // tm
    qw = C_HEADS * (NOPE_DIM + LANE)
    est = 2 * (tm * 1024 * 2) + 2 * (CQ_PAD * qw * 2 + CQ_PAD * 512 * 2 + 128 * 1024 * 2) \
        + 2 * (2 * tm * qw * 2 + tm * 512 * 2) + 3 * tm * qw * 4
    const = lambda i: (0, 0)
    return pl.pallas_call(
        _mla_prep_kernel,
        grid=(m // tm,),
        in_specs=[
            pl.BlockSpec((tm, CQ_PAD), lambda i: (i, OFF_CQ // CQ_PAD)),
            pl.BlockSpec((tm, LANE), lambda i: (i, OFF_CKV // LANE)),
            pl.BlockSpec((tm, LANE), lambda i: (i, OFF_KR // LANE)),
            pl.BlockSpec((tm, LANE), lambda i: (i, OFF_KRS // LANE)),
            pl.BlockSpec((1, CQ_PAD), const),
            pl.BlockSpec((1, KV_LORA), const),
            pl.BlockSpec((CQ_PAD, qw), const),
            pl.BlockSpec((CQ_PAD, C_HEADS * LANE), const),
            pl.BlockSpec((KV_LORA, C_HEADS * (NOPE_DIM + V_DIM)), const),
            pl.BlockSpec((tm, LANE), lambda i: (i % nseq, 0)),
            pl.BlockSpec((tm, LANE), lambda i: (i % nseq, 0)),
        ],
        out_specs=[
            pl.BlockSpec((tm, qw), lambda i: (i, 0)),
            pl.BlockSpec((tm, qw), lambda i: (i, 0)),
            pl.BlockSpec((tm, C_HEADS * V_DIM), lambda i: (i, 0)),
        ],
        out_shape=[
            jax.ShapeDtypeStruct((m, qw), BF16),
            jax.ShapeDtypeStruct((m, qw), BF16),
            jax.ShapeDtypeStruct((m, C_HEADS * V_DIM), BF16),
        ],
        compiler_params=_params(("arbitrary",), est),
        name="mla_prep",
    )(proj, proj, proj, proj, gq, gkv, wq1, wq2, wkv, cos_t, sin_t)


def _paired_loop(n, body, init):
    def pair(p, carry):
        return body(2 * p + 1, body(2 * p, carry))
    carry = lax.fori_loop(0, n // 2, pair, init)
    return lax.fori_loop(2 * (n // 2), n, body, carry)


def _nt_dot(a, b):
    return lax.dot_general(a, b, (((1,), (1,)), ((), ())), preferred_element_type=F32)


def _two_pass_attention(i, *, n_heads, tq, tk, dv, logits_fn, diag_mask, v_fn, store_fn,
                        s_ref, mx_ref, acc_ref):
    assert tk == 2 * tq
    n_wide = (i * tq) // tk
    has_narrow = (i * tq) % tk != 0
    narrow_ks = pl.multiple_of(n_wide * tk, tk)
    diag_ks = pl.multiple_of(i * tq, tq)

    mx_ref[...] = jnp.full(mx_ref.shape, NEG_INF, F32)

    def pass1(ks, width, mask=None):
        for h in range(n_heads):
            s = logits_fn(h, ks, width)
            if mask is not None:
                s = jnp.where(mask, s, NEG_INF)
            s_ref[h, :, pl.ds(ks, width)] = s
            mx = mx_ref[h]
            for c in range(width // LANE):
                mx = jnp.maximum(mx, s[:, c * LANE:(c + 1) * LANE])
            mx_ref[h] = mx

    def pass1_wide(kb, carry):
        pass1(pl.multiple_of(kb * tk, tk), tk)
        return carry

    _paired_loop(n_wide, pass1_wide, 0)

    @pl.when(has_narrow)
    def _():
        pass1(narrow_ks, tq)

    pass1(diag_ks, tq, diag_mask)

    for h in range(n_heads):
        mx_ref[h] = jnp.broadcast_to(jnp.max(mx_ref[h], axis=-1, keepdims=True), (tq, LANE))
    acc_ref[...] = jnp.zeros(acc_ref.shape, F32)

    def pass2(ks, width):
        ones = jnp.ones((width, LANE), BF16)
        for h in range(n_heads):
            m = mx_ref[h]
            p = jnp.concatenate(
                [jnp.exp2(s_ref[h, :, pl.ds(ks + c * LANE, LANE)] - m).astype(BF16)
                 for c in range(width // LANE)], axis=1)
            v1 = jnp.concatenate([v_fn(h, ks, width), ones], axis=1)
            acc_ref[h] += jnp.dot(p, v1, preferred_element_type=F32)

    def pass2_wide(kb, carry):
        pass2(pl.multiple_of(kb * tk, tk), tk)
        return carry

    _paired_loop(n_wide, pass2_wide, 0)

    @pl.when(has_narrow)
    def _():
        pass2(narrow_ks, tq)

    pass2(diag_ks, tq)

    for h in range(n_heads):
        acc = acc_ref[h]
        store_fn(h, acc[:, :dv] / acc[:, dv:])


def _tile_iotas(tq, tk):
    return (lax.broadcasted_iota(jnp.int32, (tq, tk), 0), lax.broadcasted_iota(jnp.int32, (tq, tk), 1))


def _fox_kernel(q_ref, k_ref, v_ref, ccol_ref, crow_ref, o_ref,
                s_ref, mx_ref, acc_ref, cq_ref, *, tq, tk):
    i = pl.program_id(1)
    head = lambda h: slice(h * HEAD_DIM, (h + 1) * HEAD_DIM)
    for h in range(B_HEADS):
        cq_ref[h] = jnp.broadcast_to(ccol_ref[:, MISC_FL + h:MISC_FL + h + 1] * LOG2E, (tq, LANE))

    def logits_fn(h, ks, width):
        s = _nt_dot(q_ref[:, head(h)], k_ref[pl.ds(ks, width), head(h)])
        cq = jnp.concatenate([cq_ref[h]] * (width // LANE), axis=1)
        return s + cq - crow_ref[h:h + 1, pl.ds(ks, width)] * LOG2E

    def store_fn(h, y):
        o_ref[:, head(h)] = y.astype(o_ref.dtype)

    rows, cols = _tile_iotas(tq, tq)
    _two_pass_attention(i, n_heads=B_HEADS, tq=tq, tk=tk, dv=HEAD_DIM, logits_fn=logits_fn,
                        diag_mask=cols <= rows,
                        v_fn=lambda h, ks, width: v_ref[pl.ds(ks, width), head(h)],
                        store_fn=store_fn, s_ref=s_ref, mx_ref=mx_ref, acc_ref=acc_ref)


def _fox_attention(proj, ccol, crow, batch, *, tq=256, tk=512):
    m = proj.shape[0]
    nq = SEQ // tq
    w = B_HEADS * HEAD_DIM
    state = B_HEADS * tq * LANE * 4
    est = (4 * tq * w * 2 + 4 * SEQ * w * 2 + 2 * tq * LANE * 4 + 2 * 8 * SEQ * 4
           + B_HEADS * tq * SEQ * 4 + 4 * state + 6 * tq * tk * 4)
    return pl.pallas_call(
        functools.partial(_fox_kernel, tq=tq, tk=tk),
        grid=(batch, nq),
        in_specs=[
            pl.BlockSpec((tq, w), lambda b, i: (b * nq + i, OFF_QB // w)),
            pl.BlockSpec((SEQ, w), lambda b, i: (b, OFF_KB // w)),
            pl.BlockSpec((SEQ, w), lambda b, i: (b, OFF_VB // w)),
            pl.BlockSpec((tq, LANE), lambda b, i: (b * nq + i, 0)),
            pl.BlockSpec((None, B_HEADS, SEQ), lambda b, i: (b, 0, 0)),
        ],
        out_specs=pl.BlockSpec((tq, w), lambda b, i: (b * nq + i, 0)),
        out_shape=jax.ShapeDtypeStruct((m, w), BF16),
        scratch_shapes=[
            pltpu.VMEM((B_HEADS, tq, SEQ), F32),
            pltpu.VMEM((B_HEADS, tq, LANE), F32),
            pltpu.VMEM((B_HEADS, tq, HEAD_DIM + LANE), F32),
            pltpu.VMEM((B_HEADS, tq, LANE), F32),
        ],
        compiler_params=_params(("arbitrary", "arbitrary"), est),
        name="fox_attention",
    )(proj, proj, proj, ccol, crow)


def _mla_kernel(q_ref, k_ref, v_ref, o_ref, s_ref, mx_ref, acc_ref, *, tq, tk):
    i = pl.program_id(1)
    qw = NOPE_DIM + LANE
    qhead = lambda h: slice(h * qw, (h + 1) * qw)
    vhead = lambda h: slice(h * V_DIM, (h + 1) * V_DIM)

    def logits_fn(h, ks, width):
        return _nt_dot(q_ref[:, qhead(h)], k_ref[pl.ds(ks, width), qhead(h)])

    def store_fn(h, y):
        o_ref[:, vhead(h)] = y.astype(o_ref.dtype)

    rows, cols = _tile_iotas(tq, tq)
    _two_pass_attention(i, n_heads=C_HEADS, tq=tq, tk=tk, dv=V_DIM, logits_fn=logits_fn,
                        diag_mask=cols // CHUNK <= rows // CHUNK,
                        v_fn=lambda h, ks, width: v_ref[pl.ds(ks, width), vhead(h)],
                        store_fn=store_fn, s_ref=s_ref, mx_ref=mx_ref, acc_ref=acc_ref)


def _mla_attention(qc, kc, vc, batch, *, tq=512, tk=1024):
    m = qc.shape[0]
    nq = SEQ // tq
    qw = C_HEADS * (NOPE_DIM + LANE)
    vw = C_HEADS * V_DIM
    state = C_HEADS * tq * LANE * 4
    est = (2 * tq * qw * 2 + 2 * SEQ * qw * 2 + 2 * SEQ * vw * 2 + 2 * tq * vw * 2
           + C_HEADS * tq * SEQ * 4 + 3 * state + 6 * tq * tk * 4)
    return pl.pallas_call(
        functools.partial(_mla_kernel, tq=tq, tk=tk),
        grid=(batch, nq),
        in_specs=[
            pl.BlockSpec((tq, qw), lambda b, i: (b * nq + i, 0)),
            pl.BlockSpec((SEQ, qw), lambda b, i: (b, 0)),
            pl.BlockSpec((SEQ, vw), lambda b, i: (b, 0)),
        ],
        out_specs=pl.BlockSpec((tq, vw), lambda b, i: (b * nq + i, 0)),
        out_shape=jax.ShapeDtypeStruct((m, vw), BF16),
        scratch_shapes=[
            pltpu.VMEM((C_HEADS, tq, SEQ), F32),
            pltpu.VMEM((C_HEADS, tq, LANE), F32),
            pltpu.VMEM((C_HEADS, tq, V_DIM + LANE), F32),
        ],
        compiler_params=_params(("arbitrary", "arbitrary"), est),
        name="mla_attention",
    )(qc, kc, vc)


def _t5_tiles_kernel(bucket_ref, table_ref, o_ref):
    for d in range(3):
        bucket = bucket_ref[d]
        for h in range(A_HEADS):
            acc = jnp.zeros(bucket.shape, F32)
            for nb in range(NUM_BUCKETS):
                acc = jnp.where(bucket == nb, table_ref[nb, h] * LOG2E, acc)
            o_ref[d, h] = acc


def _t5_tiles(bucket_tiles, t5_bias, *, t=ATT_T):
    return pl.pallas_call(
        _t5_tiles_kernel,
        in_specs=[
            pl.BlockSpec(memory_space=pltpu.VMEM),
            pl.BlockSpec(memory_space=pltpu.SMEM),
        ],
        out_specs=pl.BlockSpec(memory_space=pltpu.VMEM),
        out_shape=jax.ShapeDtypeStruct((3, A_HEADS, t, t), F32),
        name="t5_tiles",
    )(bucket_tiles, t5_bias)


def _dsa_kernel(qi_ref, klo_ref, khi_ref, misc_ref, qa_ref, ka_ref, va_ref, bias_ref, *rest, t):
    i = pl.program_id(1)
    if len(rest) == 6:
        w_next_ref, o_ref, w_next_packed_ref, sc_ref, lg_ref, acc_ref = rest

        @pl.when(i % 2 == 0)
        def _():
            _pack_w_in_kernel(w_next_ref, w_next_packed_ref)
    else:
        o_ref, sc_ref, lg_ref, acc_ref = rest
    groups = t // 8
    key = lax.broadcasted_iota(jnp.int32, (t, t), 0)
    qry = lax.broadcasted_iota(jnp.int32, (t, t), 1)
    admissible = (key // CHUNK) <= (qry // CHUNK)

    def fold(x, op):
        return op(x.reshape(groups, 8, t), axis=0)

    def over_keys(x8, op):
        return jnp.broadcast_to(op(x8, axis=0, keepdims=True), (8, t))

    w_t = misc_ref[...].T[MISC_WI:MISC_WI + IDX_HEADS, :] * (IDX_HEADS ** -0.5 * IDX_DIM ** -0.5)

    def block_scores(kb):
        ks = pl.multiple_of(kb * t, t)
        klo = klo_ref[pl.ds(ks, t), :]
        khi = khi_ref[pl.ds(ks, t), :]
        acc = jnp.zeros((t, t), F32)
        for pair in range(IDX_HEADS // 2):
            qp = qi_ref[:, pair * LANE:(pair + 1) * LANE]
            for sub, kk in enumerate((klo, khi)):
                h = 2 * pair + sub
                acc = acc + jnp.maximum(_nt_dot(kk, qp), 0.0) * w_t[h:h + 1, :]
        return ks, acc

    def score_body(kb, carry):
        mn, mx = carry
        ks, acc = block_scores(kb)
        sc_ref[pl.ds(ks, t), :] = acc
        return jnp.minimum(mn, fold(acc, jnp.min)), jnp.maximum(mx, fold(acc, jnp.max))

    mn, mx = _paired_loop(i, score_body,
                          (jnp.full((8, t), -SELECT_MIN, F32), jnp.full((8, t), SELECT_MIN, F32)))
    ks, acc = block_scores(i)
    sc_ref[pl.ds(ks, t), :] = jnp.where(admissible, acc, NEG_INF)
    mn = jnp.minimum(mn, fold(jnp.where(admissible, acc, -SELECT_MIN), jnp.min))
    mx = jnp.maximum(mx, fold(jnp.where(admissible, acc, SELECT_MIN), jnp.max))

    def count_ge(thr):
        def body(kb, cnt):
            ks = pl.multiple_of(kb * t, t)
            hit = jnp.where(sc_ref[pl.ds(ks, t), :].reshape(groups, 8, t) >= thr[None], 1.0, 0.0)
            return cnt + jnp.sum(hit, axis=0)
        return over_keys(_paired_loop(i + 1, body, jnp.zeros((8, t), F32)), jnp.sum)

    k_sel = float(TOPK_MAX)
    search = i * t >= TOPK_MAX
    lo0 = jnp.where(search, over_keys(mn, jnp.min), SELECT_MIN)
    hi0 = jnp.where(search, over_keys(mx, jnp.max), SELECT_MIN)

    def midpoint(lo, hi):
        return 0.5 * lo + 0.5 * hi

    def any_active(lo, hi):
        mid = midpoint(lo, hi)
        return jnp.max(jnp.where((mid > lo) & (mid < hi), 1.0, 0.0)) > 0.5

    def search_cond(state):
        return jnp.logical_and(state[0] < BISECT_MAX_ITERS, state[1])

    def search_body(state):
        it, _, lo, hi, c_lo = state
        for _ in range(SEARCH_STEPS_PER_CHECK):
            mid = midpoint(lo, hi)
            cnt = count_ge(mid)
            ge = cnt >= k_sel
            lo = jnp.where(ge, mid, lo)
            c_lo = jnp.where(ge, cnt, c_lo)
            hi = jnp.where(cnt > k_sel, hi, mid)
        return it + SEARCH_STEPS_PER_CHECK, any_active(lo, hi), lo, hi, c_lo

    n_adm = (((i * t + qry[:8]) // CHUNK + 1) * CHUNK).astype(F32)
    _, _, lo, hi, c_lo = lax.while_loop(search_cond, search_body,
                                        (jnp.int32(0), any_active(lo0, hi0), lo0, hi0, n_adm))
    c_hi = count_ge(hi)
    thr = jnp.where(c_hi >= k_sel, hi, lo)
    c_thr = jnp.where(c_hi >= k_sel, c_hi, c_lo)

    @pl.when(jnp.max(jnp.where(c_thr > k_sel, 1.0, 0.0)) > 0.5)
    def _():
        key_in_block = (lax.broadcasted_iota(jnp.int32, (groups, 8, t), 0) * 8
                        + lax.broadcasted_iota(jnp.int32, (groups, 8, t), 1))

        def count_where(pred):
            def body(kb, cnt):
                ks = pl.multiple_of(kb * t, t)
                blk = sc_ref[pl.ds(ks, t), :].reshape(groups, 8, t)
                return cnt + jnp.sum(jnp.where(pred(blk, ks), 1.0, 0.0), axis=0)
            return over_keys(lax.fori_loop(0, i + 1, body, jnp.zeros((8, t), F32)), jnp.sum)

        def tied_before(bound):
            return lambda blk, ks: (blk == thr[None]) & ((key_in_block + ks).astype(F32) < bound[None])

        keep = k_sel - count_where(lambda blk, ks: blk > thr[None])

        def cut_body(_, bounds):
            below, above = bounds
            mid = jnp.floor(0.5 * (below + above))
            enough = count_where(tied_before(mid)) >= keep
            return jnp.where(enough, below, mid), jnp.where(enough, mid, above)

        _, cut = lax.fori_loop(0, SEQ.bit_length(), cut_body,
                               (jnp.zeros((8, t), F32), jnp.full((8, t), float(SEQ), F32)))

        def drop_body(kb, carry):
            ks = pl.multiple_of(kb * t, t)
            blk = sc_ref[pl.ds(ks, t), :].reshape(groups, 8, t)
            drop = (blk == thr[None]) & ((key_in_block + ks).astype(F32) >= cut[None])
            sc_ref[pl.ds(ks, t), :] = jnp.where(drop, NEG_INF, blk).reshape(t, t)
            return carry

        lax.fori_loop(0, i + 1, drop_body, 0)

    head = lambda h: slice(h * HEAD_DIM, (h + 1) * HEAD_DIM)
    acc_ref[...] = jnp.zeros(acc_ref.shape, F32)

    def logits_pass(kb, mx):
        ks = pl.multiple_of(kb * t, t)
        dist = jnp.minimum(i - kb, 2)
        k_blk = ka_ref[pl.ds(ks, t), :]
        sel = sc_ref[pl.ds(ks, t), :] >= thr[0:1, :]
        out = []
        for h in range(A_HEADS):
            s = _nt_dot(k_blk, qa_ref[:, head(h)]) + bias_ref[dist, h]
            s = jnp.where(sel, s, NEG_INF)
            lg_ref[h, pl.ds(ks, t), :] = s
            out.append(jnp.maximum(mx[h], fold(s, jnp.max)))
        return tuple(out)

    mx = _paired_loop(i + 1, logits_pass,
                      tuple(jnp.full((8, t), NEG_INF, F32) for _ in range(A_HEADS)))
    m = [over_keys(mx[h], jnp.max)[0:1, :] for h in range(A_HEADS)]

    ones_rows = jnp.ones((DSA_ONES_ROWS, t), BF16)

    def value_pass(kb, carry):
        ks = pl.multiple_of(kb * t, t)
        v_t = va_ref[pl.ds(ks, t), :].astype(F32).T.astype(BF16)
        v1_t = jnp.concatenate([v_t, ones_rows], axis=0)
        for h in range(A_HEADS):
            p = jnp.exp2(lg_ref[h, pl.ds(ks, t), :] - m[h])
            acc_ref[h] += jnp.dot(v1_t, p.astype(BF16), preferred_element_type=F32)
        return carry

    _paired_loop(i + 1, value_pass, 0)
    for h in range(A_HEADS):
        acc = acc_ref[h]
        o_ref[:, head(h)] = (acc[:HEAD_DIM] / acc[HEAD_DIM:HEAD_DIM + 1]).T.astype(o_ref.dtype)


def _dsa_attention(proj, misc, bias_tiles, batch, w_in=None, pack_layer=None, *, t=ATT_T):
    m = proj.shape[0]
    nq = SEQ // t
    qiw = IDX_HEADS * IDX_DIM
    qaw = A_HEADS * HEAD_DIM
    est = (2 * t * qiw * 2 + 2 * 4 * SEQ * LANE * 2 + 2 * t * LANE * 4 + 2 * t * qaw * 2
           + 2 * 3 * A_HEADS * t * t * 4 + 2 * t * qaw * 2
           + (1 + A_HEADS) * t * SEQ * 4 + A_HEADS * HEAD_DIM * t * 4 + 12 * t * t * 4)
    kblock = lambda off: pl.BlockSpec((SEQ, LANE), lambda b, i: (b, off // LANE))
    extra_in, extra_out, extra_shape, extra_args = [], [], [], []
    if pack_layer is not None:
        w_t = jnp.swapaxes(w_in, 1, 2)
        _, n, d = w_t.shape
        cols = 2 * d // (batch * nq)
        col_block = lambda b, i: (b * nq + i) // 2
        extra_in = [pl.BlockSpec((None, n, cols), lambda b, i: (pack_layer, 0, col_block(b, i)))]
        extra_out = [pl.BlockSpec((N_PACK, cols), lambda b, i: (0, col_block(b, i)))]
        extra_shape = [jax.ShapeDtypeStruct((N_PACK, d), BF16)]
        extra_args = [w_t]
        est += 2 * cols * (n * 4 + N_PACK * 2)
    return pl.pallas_call(
        functools.partial(_dsa_kernel, t=t),
        grid=(batch, nq),
        in_specs=[
            pl.BlockSpec((t, qiw), lambda b, i: (b * nq + i, OFF_QI // qiw)),
            kblock(OFF_KI_LO),
            kblock(OFF_KI_HI),
            pl.BlockSpec((t, LANE), lambda b, i: (b * nq + i, 0)),
            pl.BlockSpec((t, qaw), lambda b, i: (b * nq + i, OFF_QA // qaw)),
            kblock(OFF_KA),
            kblock(OFF_VA),
            pl.BlockSpec((3, A_HEADS, t, t), lambda b, i: (0, 0, 0, 0)),
        ] + extra_in,
        out_specs=[pl.BlockSpec((t, qaw), lambda b, i: (b * nq + i, 0))] + extra_out,
        out_shape=[jax.ShapeDtypeStruct((m, qaw), BF16)] + extra_shape,
        scratch_shapes=[
            pltpu.VMEM((SEQ, t), F32),
            pltpu.VMEM((A_HEADS, SEQ, t), F32),
            pltpu.VMEM((A_HEADS, HEAD_DIM + DSA_ONES_ROWS, t), F32),
        ],
        compiler_params=_params(("arbitrary", "arbitrary"), est),
        name="dsa_attention",
    )(proj, proj, proj, misc, proj, proj, proj, bias_tiles, *extra_args)


def _merge_kernel(x_ref, ya_ref, yb_ref, yc_ref, ga_ref, gb_ref, gc_ref,
                  wa_ref, wb_ref, wc_ref, wo_ref, gn_ref, o_ref, hn_ref):
    def branch(y_ref, w_ref, g_ref):
        y = jnp.dot(y_ref[...], w_ref[...], preferred_element_type=F32)
        return jax.nn.sigmoid(g_ref[...].astype(F32)) * y

    merged = branch(ya_ref, wa_ref, ga_ref) + branch(yb_ref, wb_ref, gb_ref) + branch(yc_ref, wc_ref, gc_ref)
    x_new = x_ref[...] + jnp.dot(merged.astype(BF16), wo_ref[...], preferred_element_type=F32)
    o_ref[...] = x_new
    ms = jnp.mean(x_new * x_new, axis=-1, keepdims=True)
    hn_ref[...] = (x_new * lax.rsqrt(ms + EPS) * gn_ref[...]).astype(hn_ref.dtype)


def _merge(x2, ya, yb, yc, proj, wa, wb, wc, wo, g_next, *, tm=256):
    m = x2.shape[0]
    wbytes = (wa.size + wb.size + wc.size + wo.size) * 2
    est = 2 * wbytes + 4 * tm * D_MODEL * 4 + 2 * tm * 2048 * 2 + 8 * tm * D_MODEL * 2 + 4 * tm * D_MODEL * 4
    const = lambda i: (0, 0)
    gate = lambda k: pl.BlockSpec((tm, D_MODEL), lambda i: (i, OFF_GL // D_MODEL + k))
    return pl.pallas_call(
        _merge_kernel,
        grid=(m // tm,),
        in_specs=[
            pl.BlockSpec((tm, D_MODEL), lambda i: (i, 0)),
            pl.BlockSpec((tm, ya.shape[1]), lambda i: (i, 0)),
            pl.BlockSpec((tm, yb.shape[1]), lambda i: (i, 0)),
            pl.BlockSpec((tm, yc.shape[1]), lambda i: (i, 0)),
            gate(0), gate(1), gate(2),
            pl.BlockSpec(wa.shape, const),
            pl.BlockSpec(wb.shape, const),
            pl.BlockSpec(wc.shape, const),
            pl.BlockSpec(wo.shape, const),
            pl.BlockSpec((1, D_MODEL), const),
        ],
        out_specs=[
            pl.BlockSpec((tm, D_MODEL), lambda i: (i, 0)),
            pl.BlockSpec((tm, D_MODEL), lambda i: (i, 0)),
        ],
        out_shape=[
            jax.ShapeDtypeStruct((m, D_MODEL), F32),
            jax.ShapeDtypeStruct((m, D_MODEL), BF16),
        ],
        compiler_params=_params(("arbitrary",), est),
        name="merge",
    )(x2, ya, yb, yc, proj, proj, proj, wa, wb, wc, wo, g_next)


def _ffn_kernel(x_ref, h_ref, wu_ref, wv_ref, cw_ref, wd_ref, fg_ref, o_ref,
                ubuf_ref, halo_ref, *, tm, tf, final_norm):
    i = pl.program_id(0)
    j = pl.program_id(1)
    nj = pl.num_programs(1)

    @pl.when(j == 0)
    def _():
        o_ref[...] = x_ref[...]

        @pl.when((i % (SEQ // tm)) == 0)
        def _():
            halo_ref[...] = jnp.zeros(halo_ref.shape, F32)

    h = h_ref[...]
    u = jnp.dot(h, wu_ref[...], preferred_element_type=F32)
    v = jnp.dot(h, wv_ref[...], preferred_element_type=F32)

    ubuf_ref[0:8, :] = halo_ref[j]
    ubuf_ref[8:, :] = u
    halo_ref[j] = u[tm - 8:, :]
    conv = (cw_ref[0:1, :] * ubuf_ref[6:6 + tm, :] + cw_ref[1:2, :] * ubuf_ref[7:7 + tm, :]
            + cw_ref[2:3, :] * u + cw_ref[3:4, :])
    act = (jax.nn.gelu(conv) * v).astype(BF16)
    o_ref[...] += jnp.dot(act, wd_ref[...], preferred_element_type=F32)

    if final_norm:
        @pl.when(j == nj - 1)
        def _():
            _rmsnorm_rows(o_ref, fg_ref, o_ref, tm)


def _ffn(x2, h2, w_up, cw, w_down, final_g, *, final_norm, tm=1024, tf=512):
    m = x2.shape[0]
    nj = D_FF // tf
    est = (3 * tm * D_MODEL * 4 + 2 * tm * D_MODEL * 2 + 2 * 2 * D_MODEL * tf * 2 + 2 * tf * D_MODEL * 2
           + (tm + 8) * tf * 4 + nj * 8 * tf * 4 + 4 * tm * tf * 4)
    return pl.pallas_call(
        functools.partial(_ffn_kernel, tm=tm, tf=tf, final_norm=final_norm),
        grid=(m // tm, nj),
        in_specs=[
            pl.BlockSpec((tm, D_MODEL), lambda i, j: (i, 0), pipeline_mode=pl.Buffered(1)),
            pl.BlockSpec((tm, D_MODEL), lambda i, j: (i, 0)),
            pl.BlockSpec((D_MODEL, tf), lambda i, j: (0, j)),
            pl.BlockSpec((D_MODEL, tf), lambda i, j: (0, nj + j)),
            pl.BlockSpec((8, tf), lambda i, j: (0, j)),
            pl.BlockSpec((tf, D_MODEL), lambda i, j: (j, 0)),
            pl.BlockSpec((1, D_MODEL), lambda i, j: (0, 0)),
        ],
        out_specs=pl.BlockSpec((tm, D_MODEL), lambda i, j: (i, 0)),
        out_shape=jax.ShapeDtypeStruct((m, D_MODEL), F32),
        scratch_shapes=[
            pltpu.VMEM((tm + 8, tf), F32),
            pltpu.VMEM((nj, 8, tf), F32),
        ],
        compiler_params=_params(("arbitrary", "arbitrary"), est),
        name="conv_ffn",
    )(x2, h2, w_up, w_up, cw, w_down, final_g)


def _t5_bucket(rel):
    nb = NUM_BUCKETS // 2
    max_exact = nb // 2
    base = jnp.where(rel > 0, nb, 0)
    n = jnp.abs(rel)
    nf = jnp.maximum(n, 1).astype(F32)
    large = max_exact + (jnp.log(nf / max_exact) / math.log(MAX_DISTANCE / max_exact)
                         * (nb - max_exact)).astype(jnp.int32)
    large = jnp.minimum(large, nb - 1)
    return base + jnp.where(n < max_exact, n, large)


def _bucket_tiles(t):
    key = jnp.arange(t, dtype=jnp.int32)[:, None]
    qry = jnp.arange(t, dtype=jnp.int32)[None, :]
    return jnp.stack([_t5_bucket(key - qry - d * t) for d in range(3)])


def _rope_tables():
    half = ROPE_DIM // 2
    inv = ROPE_THETA ** (-jnp.arange(half, dtype=F32) / half)
    ang = jnp.arange(SEQ, dtype=jnp.int32).astype(F32)[:, None] * inv[None, :]
    cos, sin = jnp.cos(ang), jnp.sin(ang)
    pad = jnp.zeros((SEQ, LANE - ROPE_DIM), F32)
    return jnp.concatenate([cos, cos, pad], axis=1), jnp.concatenate([-sin, sin, pad], axis=1)


def _swap_halves(w):
    half = w.shape[-1] // 2
    return jnp.concatenate([w[..., half:], w[..., :half]], axis=-1)


def _pack_w_in_kernel(w_ref, o_ref):
    off = np.cumsum((0,) + IN_SIZES)
    seg = lambda k: w_ref[off[k]:off[k + 1], :]
    qa, ka, va, qi, ki, wi, qb, kb, vb, fl, cq, ckv, kr, gl = [seg(k) for k in range(len(IN_SIZES))]
    qa, qb = qa * Q_SCALE, qb * Q_SCALE
    z = lambda n: jnp.zeros((n, w_ref.shape[1]), F32)
    half = ROPE_DIM // 2
    groups = [
        (OFF_QI, [qi]), (OFF_QB, [qb]), (OFF_KB, [kb]), (OFF_VB, [vb]), (OFF_GL, [gl]), (OFF_QA, [qa]),
        (OFF_CQ, [cq, z(CQ_PAD - Q_LORA)]), (OFF_KA, [ka]), (OFF_VA, [va]), (OFF_KI_LO, [ki, z(64)]),
        (OFF_CKV, [ckv]), (OFF_KR, [kr, z(64)]), (OFF_KRS, [kr[half:], kr[:half], z(64)]),
        (OFF_MISC, [wi, fl, z(LANE - IDX_HEADS - B_HEADS)]), (OFF_KI_HI, [z(64), ki]),
    ]
    for start, pieces in groups:
        block = pieces[0] if len(pieces) == 1 else jnp.concatenate(pieces, axis=0)
        o_ref[start:start + block.shape[0], :] = block.astype(o_ref.dtype)


def _pack_w_in(w_in, l, *, cols=256):
    w_t = jnp.swapaxes(w_in, 1, 2)
    _, n, d = w_t.shape
    return pl.pallas_call(
        _pack_w_in_kernel,
        grid=(d // cols,),
        in_specs=[pl.BlockSpec((None, n, cols), lambda i: (l, 0, i))],
        out_specs=pl.BlockSpec((N_PACK, cols), lambda i: (0, i)),
        out_shape=jax.ShapeDtypeStruct((N_PACK, d), BF16),
        compiler_params=_params(("arbitrary",), 2 * cols * (n * 4 + N_PACK * 2)),
        name="pack_w_in",
    )(w_t)


def _pack_w_uq(w):
    w = jnp.pad(w, ((0, CQ_PAD - Q_LORA), (0, 0))).reshape(CQ_PAD, C_HEADS, NOPE_DIM + ROPE_DIM)
    z = jnp.zeros((CQ_PAD, C_HEADS, LANE - ROPE_DIM), w.dtype)
    rope = w[..., NOPE_DIM:]
    wq1 = jnp.concatenate([w, z], axis=-1).reshape(CQ_PAD, -1)
    wq2 = jnp.concatenate([_swap_halves(rope), z], axis=-1).reshape(CQ_PAD, -1)
    return wq1.astype(BF16), wq2.astype(BF16)


def kernel(x, norm_mix_g, w_in, b_forget, g_cq, g_ckv, w_uq, w_ukv, w_branch_a, w_branch_b, w_branch_c,
           w_o, norm_ffn_g, w_up, conv_w, conv_b, w_down, t5_bias, final_g):
    batch, seq, d = x.shape
    assert (seq, d) == (SEQ, D_MODEL)
    x2 = x.reshape(batch * seq, d)

    bias_tiles = _t5_tiles(_bucket_tiles(ATT_T), t5_bias)
    cos_t, sin_t = _rope_tables()
    final_row = final_g.reshape(1, D_MODEL)

    w_in_packed = _pack_w_in(w_in, 0)
    for l in range(DEPTH):
        casts = [(w_up, True, 11), (w_down, False, 11), (w_o, True, 8),
                 (w_branch_a, True, 8), (w_branch_b, True, 8), (w_branch_c, True, 8)]
        proj, misc, w_up_l, w_down_l, w_o_l, w_a_l, w_b_l, w_c_l = _inproj(
            x2, norm_mix_g[l].reshape(1, D_MODEL), w_in_packed, casts, l)

        fbias = jnp.zeros((1, LANE), F32).at[0, MISC_FL:MISC_FL + B_HEADS].set(b_forget[l])
        ccol, crow = _forget_cumsum(misc, fbias, batch)

        wq1, wq2 = _pack_w_uq(w_uq[l])
        gq = jnp.pad(g_cq[l], (0, CQ_PAD - Q_LORA)).reshape(1, CQ_PAD)
        qc, kc, vc = _mla_prep(proj, gq, g_ckv[l].reshape(1, KV_LORA), wq1, wq2,
                               w_ukv[l].astype(BF16), cos_t, sin_t)

        dsa_out = _dsa_attention(proj, misc, bias_tiles, batch, w_in, l + 1 if l + 1 < DEPTH else None)
        ya = dsa_out[0]
        if l + 1 < DEPTH:
            w_in_packed = dsa_out[1]
        yb = _fox_attention(proj, ccol, crow, batch)
        yc = _mla_attention(qc, kc, vc, batch)

        x2, h2 = _merge(x2, ya, yb, yc, proj, w_a_l, w_b_l, w_c_l, w_o_l,
                        norm_ffn_g[l].reshape(1, D_MODEL))

        cw = jnp.concatenate([conv_w[l], conv_b[l][None, :], jnp.zeros((4, D_FF), F32)], axis=0)
        x2 = _ffn(x2, h2, w_up_l, cw, w_down_l, final_row, final_norm=(l == DEPTH - 1))

    return x2.reshape(batch, seq, d)
```
